```python
import jax, jax.numpy as jnp
from jax import lax
import numpy as np

D_MODEL = 2048
BATCH = 4
SEQ = 2048
DEPTH = 1

D_MIX = D_MODEL
W_POOL = D_MIX // 2
W_LRU = D_MIX - W_POOL
POOL_WINDOWS = (2, 4, 8, 16)
N_POOL_GROUPS = len(POOL_WINDOWS)
POOL_GROUP = W_POOL // N_POOL_GROUPS
N_LRU_HEADS = 4
LRU_HEAD = W_LRU // N_LRU_HEADS
N_DIR = 2
CONV_WIDTH = 4
LRU_C = 8.0
EPS = 1e-6

kernel_name = "bidir_hybrid_pool_rglru_block"


def rms_norm(x, g):
    xf = x.astype(jnp.float32)
    y = xf * lax.rsqrt(jnp.mean(xf * xf, axis=-1, keepdims=True) + EPS)
    return (y * g.astype(jnp.float32)).astype(x.dtype)


def pool_mixer(u, w_pool, b_pool, pool_scale):
    B, S, _ = u.shape
    uf = u.astype(jnp.float32)
    cs = jnp.concatenate([jnp.zeros((B, 1, W_POOL), jnp.float32), jnp.cumsum(uf, axis=1)], axis=1)
    t = jnp.arange(S)
    outs = []
    for g, w in enumerate(POOL_WINDOWS):
        lo = w // 2
        hi = w - lo - 1
        start = jnp.maximum(t - lo, 0)
        end = jnp.minimum(t + hi, S - 1) + 1
        csg = cs[..., g * POOL_GROUP:(g + 1) * POOL_GROUP]
        total = csg[:, end, :] - csg[:, start, :]
        cnt = (end - start).astype(jnp.float32)[None, :, None]
        outs.append(total / cnt - uf[..., g * POOL_GROUP:(g + 1) * POOL_GROUP])
    pooled = jnp.stack(outs, axis=2).astype(u.dtype)
    mixed = jnp.einsum('bsgp,gpq->bsgq', pooled, w_pool) + b_pool
    return mixed.reshape(B, S, W_POOL) * pool_scale


def centred_dwconv(u, conv_w, conv_b):
    S = u.shape[1]
    left = CONV_WIDTH // 2
    up = jnp.pad(u, ((0, 0), (left, CONV_WIDTH - left - 1), (0, 0)))
    out = conv_b
    for k in range(CONV_WIDTH):
        out = out + up[:, k:k + S, :] * conv_w[k]
    return out


def linear_scan(a, b, reverse):
    def combine(l, r):
        a_l, b_l = l
        a_r, b_r = r
        return a_l * a_r, a_r * b_l + b_r
    _, h = lax.associative_scan(combine, (a, b), axis=1, reverse=reverse)
    return h


def bidir_rg_lru(u, w_gate, b_gate, lru_lambda):
    B, S, _ = u.shape
    uh = u.reshape(B, S, N_LRU_HEADS, LRU_HEAD)
    gates = jnp.einsum('bshc,nhce->nbshe', uh, w_gate) + b_gate[:, None, None]
    gates = jax.nn.sigmoid(gates.astype(jnp.float32))
    r = gates[..., :LRU_HEAD].reshape(N_DIR, B, S, W_LRU)
    i = gates[..., LRU_HEAD:].reshape(N_DIR, B, S, W_LRU)
    log_a = -LRU_C * r * jax.nn.softplus(-lru_lambda.astype(jnp.float32))[:, None, None, :]
    a = jnp.exp(log_a)
    mult = jnp.sqrt(-jnp.expm1(2.0 * log_a))
    bx = mult * i * u.astype(jnp.float32)[None]
    h_f = linear_scan(a[0], bx[0], reverse=False)
    h_b = linear_scan(a[1], bx[1], reverse=True)
    return (h_f + h_b).astype(u.dtype)


def setup_inputs(seed: int = 0) -> dict:
    key = jax.random.key(seed)
    ks = jax.random.split(key, 24)
    f32 = jnp.float32
    nrm = lambda k, shape, s: jax.random.normal(k, shape, f32) * s
    a0 = jax.random.uniform(ks[14], (DEPTH, N_DIR, W_LRU), f32, 0.9, 0.999)
    p = a0 ** (1.0 / LRU_C)
    lru_lambda = jnp.log(p) - jnp.log1p(-p)
    return {
        "x": nrm(ks[0], (BATCH, SEQ, D_MODEL), 1.0),
        "c": nrm(ks[1], (BATCH, D_MODEL), 1.0),
        "norm_g": 1.0 + nrm(ks[2], (DEPTH, D_MODEL), 0.02),
        "w_ada": nrm(ks[3], (DEPTH, D_MODEL, 3 * D_MODEL), 0.5 * D_MODEL ** -0.5),
        "b_ada": nrm(ks[4], (DEPTH, 3 * D_MODEL), 0.02),
        "w_in": nrm(ks[5], (DEPTH, D_MODEL, 2 * D_MIX), D_MODEL ** -0.5),
        "b_in": nrm(ks[6], (DEPTH, 2 * D_MIX), 0.02),
        "w_pool": nrm(ks[7], (DEPTH, N_POOL_GROUPS, POOL_GROUP, POOL_GROUP), POOL_GROUP ** -0.5),
        "b_pool": nrm(ks[8], (DEPTH, N_POOL_GROUPS, POOL_GROUP), 0.02),
        "pool_scale": 1.0 + nrm(ks[9], (DEPTH, W_POOL), 0.1),
        "conv_w": nrm(ks[10], (DEPTH, CONV_WIDTH, W_LRU), CONV_WIDTH ** -0.5),
        "conv_b": nrm(ks[11], (DEPTH, W_LRU), 0.02),
        "w_gate": nrm(ks[12], (DEPTH, N_DIR, N_LRU_HEADS, LRU_HEAD, 2 * LRU_HEAD), LRU_HEAD ** -0.5),
        "b_gate": nrm(ks[13], (DEPTH, N_DIR, N_LRU_HEADS, 2 * LRU_HEAD), 0.02),
        "lru_lambda": lru_lambda,
        "out_norm_pool_g": 1.0 + nrm(ks[15], (DEPTH, W_POOL), 0.02),
        "out_norm_lru_g": 1.0 + nrm(ks[16], (DEPTH, W_LRU), 0.02),
        "w_out": nrm(ks[17], (DEPTH, D_MIX, D_MODEL), D_MIX ** -0.5),
        "b_out": nrm(ks[18], (DEPTH, D_MODEL), 0.02),
        "final_norm_g": 1.0 + nrm(ks[19], (D_MODEL,), 0.02),
    }


def reference(x, c, norm_g, w_ada, b_ada, w_in, b_in, w_pool, b_pool, pool_scale,
              conv_w, conv_b, w_gate, b_gate, lru_lambda, out_norm_pool_g, out_norm_lru_g,
              w_out, b_out, final_norm_g):
    c_act = jax.nn.silu(c)
    for l in range(DEPTH):
        mod = c_act @ w_ada[l] + b_ada[l]
        shift, scale, gate = jnp.split(mod, 3, axis=-1)
        h = rms_norm(x, norm_g[l]) * (1.0 + scale[:, None, :]) + shift[:, None, :]
        z = h @ w_in[l] + b_in[l]
        u_pool = z[..., :W_POOL]
        u_lru = z[..., W_POOL:D_MIX]
        g_pool = z[..., D_MIX:D_MIX + W_POOL]
        g_lru = z[..., D_MIX + W_POOL:]
        y_pool = pool_mixer(u_pool, w_pool[l], b_pool[l], pool_scale[l])
        y_lru = bidir_rg_lru(centred_dwconv(u_lru, conv_w[l], conv_b[l]),
                             w_gate[l], b_gate[l], lru_lambda[l])
        y_pool = rms_norm(y_pool, out_norm_pool_g[l]) * jax.nn.silu(g_pool)
        y_lru = rms_norm(y_lru, out_norm_lru_g[l]) * jax.nn.silu(g_lru)
        y = jnp.concatenate([y_pool, y_lru], axis=-1) @ w_out[l] + b_out[l]
        x = x + gate[:, None, :] * y
    return rms_norm(x, final_norm_g)
```

```python
import functools

import jax
import jax.numpy as jnp
from jax import lax
from jax.experimental import pallas as pl
from jax.experimental.pallas import tpu as pltpu

EPS = 1e-6
LRU_C = 8.0
POOL_WINDOWS = (2, 4, 8, 16)
N_LRU_HEADS = 4
CONV_WIDTH = 4
HALO = 8
SUBLANES = 8
V7X_VMEM_LIMIT_BYTES = 56 * 1024 * 1024

F32 = jnp.float32
BF16 = jnp.bfloat16


def _params(semantics):
    return pltpu.CompilerParams(dimension_semantics=semantics,
                                vmem_limit_bytes=V7X_VMEM_LIMIT_BYTES)


def _silu(v):
    return v * jax.nn.sigmoid(v)


def _shift_rows(v, k):
    n = v.shape[0]
    return pltpu.roll(v, k % n, 0)


def _adaln_kernel(c_ref, w_ref, b_ref, o_ref):
    ca = _silu(c_ref[...]).astype(BF16)
    o_ref[...] = jnp.dot(ca, w_ref[...].astype(BF16), preferred_element_type=F32) + b_ref[...]


def _adaln_mod(c8, w_ada, b_ada, *, tn):
    d, n = w_ada.shape
    return pl.pallas_call(
        _adaln_kernel,
        grid=(n // tn,),
        in_specs=[pl.BlockSpec((c8.shape[0], d), lambda j: (0, 0)),
                  pl.BlockSpec((d, tn), lambda j: (0, j)),
                  pl.BlockSpec((1, tn), lambda j: (0, j))],
        out_specs=pl.BlockSpec((c8.shape[0], tn), lambda j: (0, j)),
        out_shape=jax.ShapeDtypeStruct((c8.shape[0], n), F32),
        compiler_params=_params(("arbitrary",)),
        name="adaln_mod",
    )(c8, w_ada, b_ada)


def _in_proj_kernel(x_ref, mod_ref, g_ref, w_ref, b_ref, z_ref, *, n_chunk):
    x = x_ref[...]
    r = lax.rsqrt(jnp.mean(x * x, axis=-1, keepdims=True) + EPS)
    mod = mod_ref[0]
    h = (x * r) * (g_ref[...] * (1.0 + mod[1:2])) + mod[0:1]
    hb = h.astype(BF16)
    n = w_ref.shape[1]
    for j in range(n // n_chunk):
        sl = slice(j * n_chunk, (j + 1) * n_chunk)
        z_ref[:, sl] = jnp.dot(hb, w_ref[:, sl], preferred_element_type=F32) + b_ref[:, sl]


def _in_proj(x2, mod3, norm_g, w_in_b, b_in, *, batch, seq, tm):
    d = x2.shape[1]
    n = w_in_b.shape[1]
    nt = seq // tm
    return pl.pallas_call(
        functools.partial(_in_proj_kernel, n_chunk=1024),
        grid=(batch, nt),
        in_specs=[pl.BlockSpec((tm, d), lambda b, i: (b * nt + i, 0)),
                  pl.BlockSpec((1, 3, d), lambda b, i: (b, 0, 0)),
                  pl.BlockSpec((1, d), lambda b, i: (0, 0)),
                  pl.BlockSpec((d, n), lambda b, i: (0, 0)),
                  pl.BlockSpec((1, n), lambda b, i: (0, 0))],
        out_specs=pl.BlockSpec((tm, n), lambda b, i: (b * nt + i, 0)),
        out_shape=jax.ShapeDtypeStruct((batch * seq, n), F32),
        compiler_params=_params(("arbitrary", "arbitrary")),
        name="in_proj",
    )(x2, mod3, norm_g, w_in_b, b_in)


def _halo_specs(width, col_block, *, seq, ts, n_tiles, total_rows, tile_of):
    per_seq = seq // HALO
    per_tile = ts // HALO
    last = total_rows // HALO - 1

    def prev_map(*idx):
        b, i = idx[0], tile_of(*idx)
        return (jnp.maximum(b * per_seq + i * per_tile - 1, 0), col_block)

    def next_map(*idx):
        b, i = idx[0], tile_of(*idx)
        return (jnp.minimum(b * per_seq + (i + 1) * per_tile, last), col_block)

    return (pl.BlockSpec((HALO, width), prev_map), pl.BlockSpec((HALO, width), next_map))


def _with_halo(prev_ref, cur_ref, next_ref, tile, n_tiles):
    prev = jnp.where(tile > 0, prev_ref[...], 0.0)
    nxt = jnp.where(tile < n_tiles - 1, next_ref[...], 0.0)
    return jnp.concatenate([prev, cur_ref[...], nxt], axis=0)


def _pool_kernel(prev_ref, u_ref, next_ref, gate_ref, w_ref, b_ref, scale_ref, ng_ref, o_ref,
                 *, seq, ts, n_tiles):
    i = pl.program_id(1)
    ue = _with_halo(prev_ref, u_ref, next_ref, i, n_tiles)
    width = ue.shape[1]
    grp = width // len(POOL_WINDOWS)
    t = i * ts + lax.broadcasted_iota(jnp.int32, (ts, 1), 0)
    mixed = []
    for g, w in enumerate(POOL_WINDOWS):
        cols = slice(g * grp, (g + 1) * grp)
        e = ue[:, cols]
        lo = w // 2
        hi = w - lo - 1
        s = e + _shift_rows(e, 1)
        half = 1
        while 2 * half < w:
            s = _shift_rows(s, half) + _shift_rows(s, -half)
            half *= 2
        total = s[HALO:HALO + ts]
        cnt = (jnp.minimum(t + hi, seq - 1) - jnp.maximum(t - lo, 0) + 1).astype(F32)
        pooled = total / cnt - e[HALO:HALO + ts]
        m = jnp.dot(pooled.astype(BF16), w_ref[g], preferred_element_type=F32)
        mixed.append((m + b_ref[g:g + 1, :]) * scale_ref[:, cols])
    y = jnp.concatenate(mixed, axis=-1)
    r = lax.rsqrt(jnp.mean(y * y, axis=-1, keepdims=True) + EPS)
    o_ref[...] = ((y * r) * ng_ref[...] * _silu(gate_ref[...])).astype(o_ref.dtype)


def _pool_mixer(z, w_pool_b, b_pool, pool_scale, out_norm_g, *, batch, seq, ts):
    width = pool_scale.shape[1]
    n_tiles = seq // ts
    prev_spec, next_spec = _halo_specs(width, 0, seq=seq, ts=ts, n_tiles=n_tiles,
                                       total_rows=batch * seq, tile_of=lambda b, i: i)
    n_grp, grp, _ = w_pool_b.shape
    return pl.pallas_call(
        functools.partial(_pool_kernel, seq=seq, ts=ts, n_tiles=n_tiles),
        grid=(batch, n_tiles),
        in_specs=[prev_spec,
                  pl.BlockSpec((ts, width), lambda b, i: (b * n_tiles + i, 0)),
                  next_spec,
                  pl.BlockSpec((ts, width), lambda b, i: (b * n_tiles + i, 2)),
                  pl.BlockSpec((n_grp, grp, grp), lambda b, i: (0, 0, 0)),
                  pl.BlockSpec((n_grp, grp), lambda b, i: (0, 0)),
                  pl.BlockSpec((1, width), lambda b, i: (0, 0)),
                  pl.BlockSpec((1, width), lambda b, i: (0, 0))],
        out_specs=pl.BlockSpec((ts, width), lambda b, i: (b * n_tiles + i, 0)),
        out_shape=jax.ShapeDtypeStruct((batch * seq, width), BF16),
        compiler_params=_params(("arbitrary", "arbitrary")),
        name="pool_mixer",
    )(z, z, z, z, w_pool_b, b_pool, pool_scale, out_norm_g)


def _softplus(v):
    return jnp.maximum(v, 0.0) + jnp.log1p(jnp.exp(-jnp.abs(v)))


def _scan_within_groups(a, b, reverse):
    rows, width = a.shape
    groups = rows // SUBLANES
    a3 = a.reshape(groups, SUBLANES, width)
    b3 = b.reshape(groups, SUBLANES, width)
    sub = lax.broadcasted_iota(jnp.int32, a3.shape, 1)
    for d in (1, 2, 4):
        if reverse:
            valid = sub < SUBLANES - d
            shift = SUBLANES - d
        else:
            valid = sub >= d
            shift = d
        a_nb = jnp.where(valid, pltpu.roll(a3, shift, 1), 1.0)
        b_nb = jnp.where(valid, pltpu.roll(b3, shift, 1), 0.0)
        b3 = a3 * b_nb + b3
        a3 = a3 * a_nb
    return a3.reshape(rows, width), b3.reshape(rows, width)


def _lru_kernel(prev_ref, u_ref, next_ref, gate_ref, cw_ref, cb_ref, wg_ref, bg_ref, lam_ref, ng_ref,
                o_ref, hf_ref, acum_ref, resp_ref, hb_ref, carry_ref, *, tc, n_chunks):
    p = pl.program_id(1)
    c = pl.program_id(2)
    chunk = c + p * (n_chunks - 1 - 2 * c)
    groups = tc // SUBLANES
    width = u_ref.shape[1]
    head = width // N_LRU_HEADS

    @pl.when(c == 0)
    def _():
        carry_ref[...] = jnp.zeros_like(carry_ref)

    ue = _with_halo(prev_ref, u_ref, next_ref, chunk, n_chunks)
    left = CONV_WIDTH // 2
    xc = cb_ref[...]
    for k in range(CONV_WIDTH):
        xc = xc + _shift_rows(ue, left - k)[HALO:HALO + tc] * cw_ref[k:k + 1, :]

    decay_rate = LRU_C * _softplus(-lam_ref[0])

    def gates_and_local_scan(reverse):
        for h in range(N_LRU_HEADS):
            cols = slice(h * head, (h + 1) * head)
            xh = xc[:, cols]
            gates = jnp.dot(xh.astype(BF16), wg_ref[0, h], preferred_element_type=F32) + bg_ref[0, h:h + 1, :]
            r = jax.nn.sigmoid(gates[:, :head])
            i = jax.nn.sigmoid(gates[:, head:])
            log_a = -(r * decay_rate[:, cols])
            a = jnp.exp(log_a)
            mult = jnp.sqrt(-jnp.tanh(log_a) * (a * a + 1.0))
            acum, resp = _scan_within_groups(a, mult * i * xh, reverse)
            acum_ref[:, cols] = acum
            resp_ref[:, cols] = resp

    @pl.when(p == 0)
    def _():
        gates_and_local_scan(False)
        row0 = pl.multiple_of(chunk * tc, SUBLANES)

        def body(k, h):
            r0 = pl.multiple_of(k * SUBLANES, SUBLANES)
            hk = resp_ref[pl.ds(r0, SUBLANES), :] + acum_ref[pl.ds(r0, SUBLANES), :] * h
            hf_ref[pl.ds(row0 + r0, SUBLANES), :] = hk
            return hk[SUBLANES - 1:SUBLANES, :]

        carry_ref[...] = lax.fori_loop(0, groups, body, carry_ref[...])

    @pl.when(p == 1)
    def _():
        gates_and_local_scan(True)
        row0 = pl.multiple_of(chunk * tc, SUBLANES)

        def body(j, h):
            k = groups - 1 - j
            r0 = pl.multiple_of(k * SUBLANES, SUBLANES)
            hk = resp_ref[pl.ds(r0, SUBLANES), :] + acum_ref[pl.ds(r0, SUBLANES), :] * h
            hb_ref[pl.ds(r0, SUBLANES), :] = hk
            return hk[0:1, :]

        carry_ref[...] = lax.fori_loop(0, groups, body, carry_ref[...])
        y = hf_ref[pl.ds(row0, tc), :] + hb_ref[...]
        r = lax.rsqrt(jnp.mean(y * y, axis=-1, keepdims=True) + EPS)
        o_ref[...] = ((y * r) * ng_ref[...] * _silu(gate_ref[...])).astype(o_ref.dtype)


def _lru_mixer(z, conv_w, conv_b, w_gate_b, b_gate, lru_lambda3, out_norm_g, *, batch, seq, tc):
    width = conv_b.shape[1]
    n_chunks = seq // tc

    def chunk_of(b, p, c):
        return c + p * (n_chunks - 1 - 2 * c)

    def finish_chunk_of(b, p, c):
        return n_chunks - 1 - p * c

    prev_spec, next_spec = _halo_specs(width, 1, seq=seq, ts=tc, n_tiles=n_chunks,
                                       total_rows=batch * seq, tile_of=chunk_of)
    n_dir, n_head, head, two_head = w_gate_b.shape
    return pl.pallas_call(
        functools.partial(_lru_kernel, tc=tc, n_chunks=n_chunks),
        grid=(batch, n_dir, n_chunks),
        in_specs=[prev_spec,
                  pl.BlockSpec((tc, width), lambda b, p, c: (b * n_chunks + chunk_of(b, p, c), 1)),
                  next_spec,
                  pl.BlockSpec((tc, width), lambda b, p, c: (b * n_chunks + finish_chunk_of(b, p, c), 3)),
                  pl.BlockSpec((CONV_WIDTH, width), lambda b, p, c: (0, 0)),
                  pl.BlockSpec((1, width), lambda b, p, c: (0, 0)),
                  pl.BlockSpec((1, n_head, head, two_head), lambda b, p, c: (p, 0, 0, 0)),
                  pl.BlockSpec((1, n_head, two_head), lambda b, p, c: (p, 0, 0)),
                  pl.BlockSpec((1, 1, width), lambda b, p, c: (p, 0, 0)),
                  pl.BlockSpec((1, width), lambda b, p, c: (0, 0))],
        out_specs=pl.BlockSpec((tc, width), lambda b, p, c: (b * n_chunks + finish_chunk_of(b, p, c), 0)),
        out_shape=jax.ShapeDtypeStruct((batch * seq, width), BF16),
        scratch_shapes=[pltpu.VMEM((seq, width), F32),
                        pltpu.VMEM((tc, width), F32),
                        pltpu.VMEM((tc, width), F32),
                        pltpu.VMEM((tc, width), F32),
                        pltpu.VMEM((1, width), F32)],
        compiler_params=_params(("arbitrary", "arbitrary", "arbitrary")),
        name="lru_mixer",
    )(z, z, z, z, conv_w, conv_b, w_gate_b, b_gate, lru_lambda3, out_norm_g)


def _out_proj_kernel(yp_ref, yl_ref, x_ref, mod_ref, w_ref, b_ref, g_ref, o_ref):
    kp = yp_ref.shape[1]
    y = jnp.dot(yp_ref[...], w_ref[:kp, :], preferred_element_type=F32)
    y = y + jnp.dot(yl_ref[...], w_ref[kp:, :], preferred_element_type=F32)
    y = y + b_ref[...]
    xn = x_ref[...] + mod_ref[0][2:3] * y
    r = lax.rsqrt(jnp.mean(xn * xn, axis=-1, keepdims=True) + EPS)
    o_ref[...] = (xn * r) * g_ref[...]


def _out_proj(yp, yl, x2, mod3, w_out_b, b_out, final_g, *, batch, seq, tm):
    d = x2.shape[1]
    kp, kl = yp.shape[1], yl.shape[1]
    nt = seq // tm
    return pl.pallas_call(
        _out_proj_kernel,
        grid=(batch, nt),
        in_specs=[pl.BlockSpec((tm, kp), lambda b, i: (b * nt + i, 0)),
                  pl.BlockSpec((tm, kl), lambda b, i: (b * nt + i, 0)),
                  pl.BlockSpec((tm, d), lambda b, i: (b * nt + i, 0)),
                  pl.BlockSpec((1, 3, d), lambda b, i: (b, 0, 0)),
                  pl.BlockSpec((kp + kl, d), lambda b, i: (0, 0)),
                  pl.BlockSpec((1, d), lambda b, i: (0, 0)),
                  pl.BlockSpec((1, d), lambda b, i: (0, 0))],
        out_specs=pl.BlockSpec((tm, d), lambda b, i: (b * nt + i, 0)),
        out_shape=jax.ShapeDtypeStruct((batch * seq, d), F32),
        compiler_params=_params(("arbitrary", "arbitrary")),
        name="out_proj",
    )(yp, yl, x2, mod3, w_out_b, b_out, final_g)


def kernel(x, c, norm_g, w_ada, b_ada, w_in, b_in, w_pool, b_pool, pool_scale, conv_w, conv_b, w_gate,
           b_gate, lru_lambda, out_norm_pool_g, out_norm_lru_g, w_out, b_out, final_norm_g):
    batch, seq, d = x.shape
    assert w_in.shape[0] == 1, "single-layer block only"
    xs = x.reshape(batch * seq, d)
    c8 = jnp.pad(c, ((0, -batch % SUBLANES), (0, 0)))
    mod = _adaln_mod(c8, w_ada[0], b_ada[0][None, :], tn=768)
    mod3 = mod[:batch].reshape(batch, 3, d)
    z = _in_proj(xs, mod3, norm_g[0][None, :], w_in[0].astype(BF16), b_in[0][None, :],
                 batch=batch, seq=seq, tm=256)
    yp = _pool_mixer(z, w_pool[0].astype(BF16), b_pool[0], pool_scale[0][None, :],
                     out_norm_pool_g[0][None, :], batch=batch, seq=seq, ts=512)
    yl = _lru_mixer(z, conv_w[0], conv_b[0][None, :], w_gate[0].astype(BF16), b_gate[0],
                    lru_lambda[0][:, None, :], out_norm_lru_g[0][None, :],
                    batch=batch, seq=seq, tc=256)
    out = _out_proj(yp, yl, xs, mod3, w_out[0].astype(BF16), b_out[0][None, :], final_norm_g[None, :],
                    batch=batch, seq=seq, tm=256)
    return out.reshape(batch, seq, d)
```

```python
import functools
import math

import jax
import jax.numpy as jnp
from jax import lax
from jax.experimental import pallas as pl
from jax.experimental.pallas import tpu as pltpu

EPS = 1e-6
LRU_C = 8.0
POOL_WINDOWS = (2, 4, 8, 16)
N_LRU_HEADS = 4
CONV_WIDTH = 4
HALO = 8
SUBLANES = 8
V7X_VMEM_LIMIT_BYTES = 56 * 1024 * 1024
LOG2E = math.log2(math.e)
LN2 = math.log(2.0)

F32 = jnp.float32
BF16 = jnp.bfloat16


def _params(semantics):
    return pltpu.CompilerParams(dimension_semantics=semantics,
                                vmem_limit_bytes=V7X_VMEM_LIMIT_BYTES)


def _sigmoid(v):
    return 0.5 * jnp.tanh(0.5 * v) + 0.5


def _silu(v):
    return v * _sigmoid(v)


def _shift_rows(v, k):
    rows, width = v.shape
    v3 = v.reshape(rows // SUBLANES, SUBLANES, width)
    sub = lax.broadcasted_iota(jnp.int32, v3.shape, 1)
    if k > 0:
        rot = pltpu.roll(v3, k, 1)
        other = jnp.concatenate([rot[-1:], rot[:-1]], axis=0)
        out = jnp.where(sub >= k, rot, other)
    else:
        rot = pltpu.roll(v3, SUBLANES + k, 1)
        other = jnp.concatenate([rot[1:], rot[:1]], axis=0)
        out = jnp.where(sub < SUBLANES + k, rot, other)
    return out.reshape(rows, width)


def _adaln_kernel(c_ref, w_ref, b_ref, o_ref):
    ca = _silu(c_ref[...]).astype(BF16)
    o_ref[...] = jnp.dot(ca, w_ref[...].astype(BF16), preferred_element_type=F32) + b_ref[...]


def _adaln_mod(c8, w_ada, b_ada, *, tn):
    d, n = w_ada.shape
    return pl.pallas_call(
        _adaln_kernel,
        grid=(n // tn,),
        in_specs=[pl.BlockSpec((c8.shape[0], d), lambda j: (0, 0)),
                  pl.BlockSpec((d, tn), lambda j: (0, j)),
                  pl.BlockSpec((1, tn), lambda j: (0, j))],
        out_specs=pl.BlockSpec((c8.shape[0], tn), lambda j: (0, j)),
        out_shape=jax.ShapeDtypeStruct((c8.shape[0], n), F32),
        compiler_params=_params(("arbitrary",)),
        name="adaln_mod",
    )(c8, w_ada, b_ada)


def _in_proj_kernel(x_ref, mod_ref, g_ref, w_ref, b_ref, z_ref, *, n_chunk):
    x = x_ref[...]
    r = lax.rsqrt(jnp.mean(x * x, axis=-1, keepdims=True) + EPS)
    mod = mod_ref[0]
    h = (x * r) * (g_ref[...] * (1.0 + mod[1:2])) + mod[0:1]
    hb = h.astype(BF16)
    n = w_ref.shape[1]
    for j in range(n // n_chunk):
        sl = slice(j * n_chunk, (j + 1) * n_chunk)
        z_ref[:, sl] = jnp.dot(hb, w_ref[:, sl], preferred_element_type=F32) + b_ref[:, sl]


def _in_proj(x2, mod3, norm_g, w_in_b, b_in, *, batch, seq, tm):
    d = x2.shape[1]
    n = w_in_b.shape[1]
    nt = seq // tm
    return pl.pallas_call(
        functools.partial(_in_proj_kernel, n_chunk=1024),
        grid=(batch, nt),
        in_specs=[pl.BlockSpec((tm, d), lambda b, i: (b * nt + i, 0)),
                  pl.BlockSpec((1, 3, d), lambda b, i: (b, 0, 0)),
                  pl.BlockSpec((1, d), lambda b, i: (0, 0)),
                  pl.BlockSpec((d, n), lambda b, i: (0, 0)),
                  pl.BlockSpec((1, n), lambda b, i: (0, 0))],
        out_specs=pl.BlockSpec((tm, n), lambda b, i: (b * nt + i, 0)),
        out_shape=jax.ShapeDtypeStruct((batch * seq, n), F32),
        compiler_params=_params(("arbitrary", "arbitrary")),
        name="in_proj",
    )(x2, mod3, norm_g, w_in_b, b_in)


def _halo_specs(width, col_block, *, seq, ts, total_rows, tile_of):
    per_seq = seq // HALO
    per_tile = ts // HALO
    last = total_rows // HALO - 1

    def prev_map(*idx):
        b, i = idx[0], tile_of(*idx)
        return (jnp.maximum(b * per_seq + i * per_tile - 1, 0), col_block)

    def next_map(*idx):
        b, i = idx[0], tile_of(*idx)
        return (jnp.minimum(b * per_seq + (i + 1) * per_tile, last), col_block)

    return (pl.BlockSpec((HALO, width), prev_map), pl.BlockSpec((HALO, width), next_map))


def _with_halo(prev_ref, cur_ref, next_ref, tile, n_tiles):
    prev = jnp.where(tile > 0, prev_ref[...], 0.0)
    nxt = jnp.where(tile < n_tiles - 1, next_ref[...], 0.0)
    return jnp.concatenate([prev, cur_ref[...], nxt], axis=0)


def _pool_kernel(prev_ref, u_ref, next_ref, gate_ref, w_ref, b_ref, scale_ref, ng_ref, o_ref,
                 *, seq, ts, n_tiles):
    i = pl.program_id(1)
    ue = _with_halo(prev_ref, u_ref, next_ref, i, n_tiles)
    width = ue.shape[1]
    grp = width // len(POOL_WINDOWS)
    t = i * ts + lax.broadcasted_iota(jnp.int32, (ts, 1), 0)
    mixed = []
    for g, w in enumerate(POOL_WINDOWS):
        cols = slice(g * grp, (g + 1) * grp)
        e = ue[:, cols]
        lo = w // 2
        hi = w - lo - 1
        s = e + _shift_rows(e, 1)
        half = 1
        while 2 * half < w:
            s = _shift_rows(s, half) + _shift_rows(s, -half)
            half *= 2
        total = s[HALO:HALO + ts]
        cnt = (jnp.minimum(t + hi, seq - 1) - jnp.maximum(t - lo, 0) + 1).astype(F32)
        pooled = total / cnt - e[HALO:HALO + ts]
        m = jnp.dot(pooled.astype(BF16), w_ref[g], preferred_element_type=F32)
        mixed.append((m + b_ref[g:g + 1, :]) * scale_ref[:, cols])
    y = jnp.concatenate(mixed, axis=-1)
    r = lax.rsqrt(jnp.mean(y * y, axis=-1, keepdims=True) + EPS)
    o_ref[...] = ((y * r) * ng_ref[...] * _silu(gate_ref[...])).astype(o_ref.dtype)


def _pool_mixer(z, w_pool_b, b_pool, pool_scale, out_norm_g, *, batch, seq, ts):
    width = pool_scale.shape[1]
    n_tiles = seq // ts
    prev_spec, next_spec = _halo_specs(width, 0, seq=seq, ts=ts, total_rows=batch * seq,
                                       tile_of=lambda b, i: i)
    n_grp, grp, _ = w_pool_b.shape
    return pl.pallas_call(
        functools.partial(_pool_kernel, seq=seq, ts=ts, n_tiles=n_tiles),
        grid=(batch, n_tiles),
        in_specs=[prev_spec,
                  pl.BlockSpec((ts, width), lambda b, i: (b * n_tiles + i, 0)),
                  next_spec,
                  pl.BlockSpec((ts, width), lambda b, i: (b * n_tiles + i, 2)),
                  pl.BlockSpec((n_grp, grp, grp), lambda b, i: (0, 0, 0)),
                  pl.BlockSpec((n_grp, grp), lambda b, i: (0, 0)),
                  pl.BlockSpec((1, width), lambda b, i: (0, 0)),
                  pl.BlockSpec((1, width), lambda b, i: (0, 0))],
        out_specs=pl.BlockSpec((ts, width), lambda b, i: (b * n_tiles + i, 0)),
        out_shape=jax.ShapeDtypeStruct((batch * seq, width), BF16),
        compiler_params=_params(("arbitrary", "arbitrary")),
        name="pool_mixer",
    )(z, z, z, z, w_pool_b, b_pool, pool_scale, out_norm_g)


def _softplus(v):
    return jnp.maximum(v, 0.0) + jnp.log1p(jnp.exp(-jnp.abs(v)))


def _lru_direction(xc_half, wg_ref, bg_half, rate2, carry, h_ref, *, reverse):
    tc, width = xc_half.shape
    head = width // N_LRU_HEADS
    groups = tc // SUBLANES
    entry = SUBLANES - 1 if reverse else 0
    sub = lax.broadcasted_iota(jnp.int32, (groups, SUBLANES, head), 1)
    is_entry = sub == entry

    def back(v, d, axis):
        return pltpu.roll(v, SUBLANES - d if reverse else d, axis)

    carries = []
    for h in range(N_LRU_HEADS):
        cols = slice(h * head, (h + 1) * head)
        xh = xc_half[:, cols]
        g = jnp.dot(xh.astype(BF16), wg_ref[h], preferred_element_type=F32) + bg_half[h:h + 1, :]
        tr = jnp.tanh(g[:, :head])
        ti = jnp.tanh(g[:, head:])
        log2a = tr * rate2[:, cols] + rate2[:, cols]
        a = jnp.exp2(log2a)
        one_minus_a2 = (a * a + 1.0) * jnp.tanh(log2a * (-LN2))
        mult = jnp.where(one_minus_a2 > 0.0, one_minus_a2 * lax.rsqrt(one_minus_a2), 0.0)
        b = mult * (ti + 1.0) * xh

        a3 = a.reshape(groups, SUBLANES, head)
        b3 = b.reshape(groups, SUBLANES, head)
        a_in = jnp.where(is_entry, a3, 0.0)
        a1 = jnp.where(is_entry, 0.0, a3)
        a2 = a1 * back(a1, 1, 1)
        a4 = a2 * back(a2, 2, 1)
        state = jnp.broadcast_to(carry[:, cols], (SUBLANES, head))
        order = range(groups - 1, -1, -1) if reverse else range(groups)
        for gi in order:
            s = b3[gi] + a_in[gi] * back(state, 1, 0)
            s = s + a1[gi] * back(s, 1, 0)
            s = s + a2[gi] * back(s, 2, 0)
            state = s + a4[gi] * back(s, 4, 0)
            h_ref[gi * SUBLANES:(gi + 1) * SUBLANES, cols] = state
        exit_row = SUBLANES - 1 - entry
        carries.append(state[exit_row:exit_row + 1, :])
    return jnp.concatenate(carries, axis=1)


def _lru_kernel(prev_ref, u_ref, next_ref, gate_ref, cw_ref, cb_ref, wg_ref, bg_ref, lam_ref, ng_ref,
                o_ref, hf_ref, xc_ref, hdir_ref, carry_ref, *, tc, n_chunks):
    p = pl.program_id(1)
    c = pl.program_id(2)
    chunk = c + p * (n_chunks - 1 - 2 * c)
    row0 = pl.multiple_of(chunk * tc, SUBLANES)

    @pl.when(c == 0)
    def _():
        carry_ref[...] = jnp.zeros_like(carry_ref)

    rate2 = (-0.5 * LRU_C * LOG2E) * _softplus(-lam_ref[0])
    bg_half = 0.5 * bg_ref[0]

    @pl.when(p == 0)
    def _():
        ue = _with_halo(prev_ref, u_ref, next_ref, chunk, n_chunks)
        left = CONV_WIDTH // 2
        xc_half = 0.5 * cb_ref[...]
        for k in range(CONV_WIDTH):
            tap = ue if k == left else _shift_rows(ue, left - k)
            xc_half = xc_half + tap[HALO:HALO + tc] * (0.5 * cw_ref[k:k + 1, :])
        xc_ref[pl.ds(row0, tc), :] = xc_half
        carry_ref[...] = _lru_direction(xc_half, wg_ref.at[0], bg_half, rate2, carry_ref[...], hdir_ref,
                                        reverse=False)
        hf_ref[pl.ds(row0, tc), :] = hdir_ref[...]

    @pl.when(p == 1)
    def _():
        xc_half = xc_ref[pl.ds(row0, tc), :]
        carry_ref[...] = _lru_direction(xc_half, wg_ref.at[0], bg_half, rate2, carry_ref[...], hdir_ref,
                                        reverse=True)
        y = hf_ref[pl.ds(row0, tc), :] + hdir_ref[...]
        r = lax.rsqrt(jnp.mean(y * y, axis=-1, keepdims=True) + EPS)
        o_ref[...] = ((y * r) * ng_ref[...] * _silu(gate_ref[...])).astype(o_ref.dtype)


def _lru_mixer(z, conv_w, conv_b, w_gate_b, b_gate, lru_lambda3, out_norm_g, *, batch, seq, tc):
    width = conv_b.shape[1]
    n_chunks = seq // tc

    def chunk_of(b, p, c):
        return (1 - p) * c + p * (n_chunks - 1)

    def finish_chunk_of(b, p, c):
        return n_chunks - 1 - p * c

    prev_spec, next_spec = _halo_specs(width, 1, seq=seq, ts=tc, total_rows=batch * seq, tile_of=chunk_of)
    n_dir, n_head, head, two_head = w_gate_b.shape
    return pl.pallas_call(
        functools.partial(_lru_kernel, tc=tc, n_chunks=n_chunks),
        grid=(batch, n_dir, n_chunks),
        in_specs=[prev_spec,
                  pl.BlockSpec((tc, width), lambda b, p, c: (b * n_chunks + chunk_of(b, p, c), 1)),
                  next_spec,
                  pl.BlockSpec((tc, width), lambda b, p, c: (b * n_chunks + finish_chunk_of(b, p, c), 3)),
                  pl.BlockSpec((CONV_WIDTH, width), lambda b, p, c: (0, 0)),
                  pl.BlockSpec((1, width), lambda b, p, c: (0, 0)),
                  pl.BlockSpec((1, n_head, head, two_head), lambda b, p, c: (p, 0, 0, 0)),
                  pl.BlockSpec((1, n_head, two_head), lambda b, p, c: (p, 0, 0)),
                  pl.BlockSpec((1, 1, width), lambda b, p, c: (p, 0, 0)),
                  pl.BlockSpec((1, width), lambda b, p, c: (0, 0))],
        out_specs=pl.BlockSpec((tc, width), lambda b, p, c: (b * n_chunks + finish_chunk_of(b, p, c), 0)),
        out_shape=jax.ShapeDtypeStruct((batch * seq, width), BF16),
        scratch_shapes=[pltpu.VMEM((seq, width), F32),
                        pltpu.VMEM((seq, width), F32),
                        pltpu.VMEM((tc, width), F32),
                        pltpu.VMEM((1, width), F32)],
        compiler_params=_params(("arbitrary", "arbitrary", "arbitrary")),
        name="lru_mixer",
    )(z, z, z, z, conv_w, conv_b, w_gate_b, b_gate, lru_lambda3, out_norm_g)


def _out_proj_kernel(yp_ref, yl_ref, x_ref, mod_ref, w_ref, b_ref, g_ref, o_ref):
    kp = yp_ref.shape[1]
    y = jnp.dot(yp_ref[...], w_ref[:kp, :], preferred_element_type=F32)
    y = y + jnp.dot(yl_ref[...], w_ref[kp:, :], preferred_element_type=F32)
    y = y + b_ref[...]
    xn = x_ref[...] + mod_ref[0][2:3] * y
    r = lax.rsqrt(jnp.mean(xn * xn, axis=-1, keepdims=True) + EPS)
    o_ref[...] = (xn * r) * g_ref[...]


def _out_proj(yp, yl, x2, mod3, w_out_b, b_out, final_g, *, batch, seq, tm):
    d = x2.shape[1]
    kp, kl = yp.shape[1], yl.shape[1]
    nt = seq // tm
    return pl.pallas_call(
        _out_proj_kernel,
        grid=(batch, nt),
        in_specs=[pl.BlockSpec((tm, kp), lambda b, i: (b * nt + i, 0)),
                  pl.BlockSpec((tm, kl), lambda b, i: (b * nt + i, 0)),
                  pl.BlockSpec((tm, d), lambda b, i: (b * nt + i, 0)),
                  pl.BlockSpec((1, 3, d), lambda b, i: (b, 0, 0)),
                  pl.BlockSpec((kp + kl, d), lambda b, i: (0, 0)),
                  pl.BlockSpec((1, d), lambda b, i: (0, 0)),
                  pl.BlockSpec((1, d), lambda b, i: (0, 0))],
        out_specs=pl.BlockSpec((tm, d), lambda b, i: (b * nt + i, 0)),
        out_shape=jax.ShapeDtypeStruct((batch * seq, d), F32),
        compiler_params=_params(("arbitrary", "arbitrary")),
        name="out_proj",
    )(yp, yl, x2, mod3, w_out_b, b_out, final_g)


def kernel(x, c, norm_g, w_ada, b_ada, w_in, b_in, w_pool, b_pool, pool_scale, conv_w, conv_b, w_gate,
           b_gate, lru_lambda, out_norm_pool_g, out_norm_lru_g, w_out, b_out, final_norm_g):
    batch, seq, d = x.shape
    assert w_in.shape[0] == 1, "single-layer block only"
    xs = x.reshape(batch * seq, d)
    c8 = jnp.pad(c, ((0, -batch % SUBLANES), (0, 0)))
    mod = _adaln_mod(c8, w_ada[0], b_ada[0][None, :], tn=768)
    mod3 = mod[:batch].reshape(batch, 3, d)
    z = _in_proj(xs, mod3, norm_g[0][None, :], w_in[0].astype(BF16), b_in[0][None, :],
                 batch=batch, seq=seq, tm=256)
    yp = _pool_mixer(z, w_pool[0].astype(BF16), b_pool[0], pool_scale[0][None, :],
                     out_norm_pool_g[0][None, :], batch=batch, seq=seq, ts=512)
    yl = _lru_mixer(z, conv_w[0], conv_b[0][None, :], w_gate[0].astype(BF16), b_gate[0],
                    lru_lambda[0][:, None, :], out_norm_lru_g[0][None, :],
                    batch=batch, seq=seq, tc=256)
    out = _out_proj(yp, yl, xs, mod3, w_out[0].astype(BF16), b_out[0][None, :], final_norm_g[None, :],
                    batch=batch, seq=seq, tm=256)
    return out.reshape(batch, seq, d)
```

```python
import functools
import math

import jax
import jax.numpy as jnp
from jax import lax
from jax.experimental import pallas as pl
from jax.experimental.pallas import tpu as pltpu

EPS = 1e-6
LRU_C = 8.0
POOL_WINDOWS = (2, 4, 8, 16)
N_LRU_HEADS = 4
CONV_WIDTH = 4
HALO = 8
SUBLANES = 8
V7X_VMEM_LIMIT_BYTES = 56 * 1024 * 1024
LOG2E = math.log2(math.e)
LN2 = math.log(2.0)

F32 = jnp.float32
BF16 = jnp.bfloat16


def _params(semantics):
    return pltpu.CompilerParams(dimension_semantics=semantics,
                                vmem_limit_bytes=V7X_VMEM_LIMIT_BYTES)


def _sigmoid(v):
    return 0.5 * jnp.tanh(0.5 * v) + 0.5


def _silu(v):
    return v * _sigmoid(v)


def _shift_rows(v, k):
    rows, width = v.shape
    v3 = v.reshape(rows // SUBLANES, SUBLANES, width)
    sub = lax.broadcasted_iota(jnp.int32, v3.shape, 1)
    if k > 0:
        rot = pltpu.roll(v3, k, 1)
        other = jnp.concatenate([rot[-1:], rot[:-1]], axis=0)
        out = jnp.where(sub >= k, rot, other)
    else:
        rot = pltpu.roll(v3, SUBLANES + k, 1)
        other = jnp.concatenate([rot[1:], rot[:1]], axis=0)
        out = jnp.where(sub < SUBLANES + k, rot, other)
    return out.reshape(rows, width)


def _adaln_kernel(c_ref, w_ref, b_ref, o_ref):
    ca = _silu(c_ref[...]).astype(BF16)
    o_ref[...] = jnp.dot(ca, w_ref[...].astype(BF16), preferred_element_type=F32) + b_ref[...]


def _adaln_mod(c8, w_ada, b_ada, *, tn):
    d, n = w_ada.shape
    return pl.pallas_call(
        _adaln_kernel,
        grid=(n // tn,),
        in_specs=[pl.BlockSpec((c8.shape[0], d), lambda j: (0, 0)),
                  pl.BlockSpec((d, tn), lambda j: (0, j)),
                  pl.BlockSpec((1, tn), lambda j: (0, j))],
        out_specs=pl.BlockSpec((c8.shape[0], tn), lambda j: (0, j)),
        out_shape=jax.ShapeDtypeStruct((c8.shape[0], n), F32),
        compiler_params=_params(("arbitrary",)),
        name="adaln_mod",
    )(c8, w_ada, b_ada)


def _in_proj_kernel(x_ref, mod_ref, g_ref, w_ref, b_ref, z_ref, *, n_chunk):
    x = x_ref[...]
    r = lax.rsqrt(jnp.mean(x * x, axis=-1, keepdims=True) + EPS)
    mod = mod_ref[0]
    h = (x * r) * (g_ref[...] * (1.0 + mod[1:2])) + mod[0:1]
    hb = h.astype(BF16)
    n = w_ref.shape[1]
    for j in range(n // n_chunk):
        sl = slice(j * n_chunk, (j + 1) * n_chunk)
        z_ref[:, sl] = jnp.dot(hb, w_ref[:, sl], preferred_element_type=F32) + b_ref[:, sl]


def _in_proj(x2, mod3, norm_g, w_in_b, b_in, *, batch, seq, tm):
    d = x2.shape[1]
    n = w_in_b.shape[1]
    nt = seq // tm
    return pl.pallas_call(
        functools.partial(_in_proj_kernel, n_chunk=1024),
        grid=(batch, nt),
        in_specs=[pl.BlockSpec((tm, d), lambda b, i: (b * nt + i, 0)),
                  pl.BlockSpec((1, 3, d), lambda b, i: (b, 0, 0)),
                  pl.BlockSpec((1, d), lambda b, i: (0, 0)),
                  pl.BlockSpec((d, n), lambda b, i: (0, 0)),
                  pl.BlockSpec((1, n), lambda b, i: (0, 0))],
        out_specs=pl.BlockSpec((tm, n), lambda b, i: (b * nt + i, 0)),
        out_shape=jax.ShapeDtypeStruct((batch * seq, n), F32),
        compiler_params=_params(("arbitrary", "arbitrary")),
        name="in_proj",
    )(x2, mod3, norm_g, w_in_b, b_in)


def _halo_specs(width, col_block, *, seq, ts, total_rows, tile_of):
    per_seq = seq // HALO
    per_tile = ts // HALO
    last = total_rows // HALO - 1

    def prev_map(*idx):
        b, i = idx[0], tile_of(*idx)
        return (jnp.maximum(b * per_seq + i * per_tile - 1, 0), col_block)

    def next_map(*idx):
        b, i = idx[0], tile_of(*idx)
        return (jnp.minimum(b * per_seq + (i + 1) * per_tile, last), col_block)

    return (pl.BlockSpec((HALO, width), prev_map), pl.BlockSpec((HALO, width), next_map))


def _with_halo(prev_ref, cur_ref, next_ref, tile, n_tiles):
    prev = jnp.where(tile > 0, prev_ref[...], 0.0)
    nxt = jnp.where(tile < n_tiles - 1, next_ref[...], 0.0)
    return jnp.concatenate([prev, cur_ref[...], nxt], axis=0)


def _pool_kernel(prev_ref, u_ref, next_ref, gate_ref, w_ref, b_ref, scale_ref, ng_ref, o_ref,
                 *, seq, ts, n_tiles):
    i = pl.program_id(1)
    ue = _with_halo(prev_ref, u_ref, next_ref, i, n_tiles)
    width = ue.shape[1]
    grp = width // len(POOL_WINDOWS)
    t = i * ts + lax.broadcasted_iota(jnp.int32, (ts, 1), 0)
    mixed = []
    for g, w in enumerate(POOL_WINDOWS):
        cols = slice(g * grp, (g + 1) * grp)
        e = ue[:, cols]
        lo = w // 2
        hi = w - lo - 1
        s = e + _shift_rows(e, 1)
        half = 1
        while 2 * half < w:
            s = _shift_rows(s, half) + _shift_rows(s, -half)
            half *= 2
        total = s[HALO:HALO + ts]
        cnt = (jnp.minimum(t + hi, seq - 1) - jnp.maximum(t - lo, 0) + 1).astype(F32)
        pooled = total / cnt - e[HALO:HALO + ts]
        m = jnp.dot(pooled.astype(BF16), w_ref[g], preferred_element_type=F32)
        mixed.append((m + b_ref[g:g + 1, :]) * scale_ref[:, cols])
    y = jnp.concatenate(mixed, axis=-1)
    r = lax.rsqrt(jnp.mean(y * y, axis=-1, keepdims=True) + EPS)
    o_ref[...] = ((y * r) * ng_ref[...] * _silu(gate_ref[...])).astype(o_ref.dtype)


def _pool_mixer(z, w_pool_b, b_pool, pool_scale, out_norm_g, *, batch, seq, ts):
    width = pool_scale.shape[1]
    n_tiles = seq // ts
    prev_spec, next_spec = _halo_specs(width, 0, seq=seq, ts=ts, total_rows=batch * seq,
                                       tile_of=lambda b, i: i)
    n_grp, grp, _ = w_pool_b.shape
    return pl.pallas_call(
        functools.partial(_pool_kernel, seq=seq, ts=ts, n_tiles=n_tiles),
        grid=(batch, n_tiles),
        in_specs=[prev_spec,
                  pl.BlockSpec((ts, width), lambda b, i: (b * n_tiles + i, 0)),
                  next_spec,
                  pl.BlockSpec((ts, width), lambda b, i: (b * n_tiles + i, 2)),
                  pl.BlockSpec((n_grp, grp, grp), lambda b, i: (0, 0, 0)),
                  pl.BlockSpec((n_grp, grp), lambda b, i: (0, 0)),
                  pl.BlockSpec((1, width), lambda b, i: (0, 0)),
                  pl.BlockSpec((1, width), lambda b, i: (0, 0))],
        out_specs=pl.BlockSpec((ts, width), lambda b, i: (b * n_tiles + i, 0)),
        out_shape=jax.ShapeDtypeStruct((batch * seq, width), BF16),
        compiler_params=_params(("arbitrary", "arbitrary")),
        name="pool_mixer",
    )(z, z, z, z, w_pool_b, b_pool, pool_scale, out_norm_g)


def _softplus(v):
    return jnp.maximum(v, 0.0) + jnp.log1p(jnp.exp(-jnp.abs(v)))


def _lru_direction(xc_half, wg_ref, bg_half, rate2, carry, h_ref, *, reverse, after_gates=None):
    tc, width = xc_half.shape
    head = width // N_LRU_HEADS
    groups = tc // SUBLANES
    entry = SUBLANES - 1 if reverse else 0
    sub = lax.broadcasted_iota(jnp.int32, (groups, SUBLANES, head), 1)
    is_entry = sub == entry

    def back(v, d, axis):
        return pltpu.roll(v, SUBLANES - d if reverse else d, axis)

    carries = []
    for h in range(N_LRU_HEADS):
        cols = slice(h * head, (h + 1) * head)
        xh = xc_half[:, cols]
        g = jnp.dot(xh.astype(BF16), wg_ref[h], preferred_element_type=F32) + bg_half[h:h + 1, :]
        if after_gates is not None:
            after_gates(h)
        tr = jnp.tanh(g[:, :head])
        ti = jnp.tanh(g[:, head:])
        log2a = tr * rate2[:, cols] + rate2[:, cols]
        a = jnp.exp2(log2a)
        one_minus_a2 = (a * a + 1.0) * jnp.tanh(log2a * (-LN2))
        mult = jnp.where(one_minus_a2 > 0.0, one_minus_a2 * lax.rsqrt(one_minus_a2), 0.0)
        b = mult * (ti + 1.0) * xh

        a3 = a.reshape(groups, SUBLANES, head)
        b3 = b.reshape(groups, SUBLANES, head)
        a_in = jnp.where(is_entry, a3, 0.0)
        a1 = jnp.where(is_entry, 0.0, a3)
        a2 = a1 * back(a1, 1, 1)
        a4 = a2 * back(a2, 2, 1)
        state = jnp.broadcast_to(carry[:, cols], (SUBLANES, head))
        order = range(groups - 1, -1, -1) if reverse else range(groups)
        for gi in order:
            s = b3[gi] + a_in[gi] * back(state, 1, 0)
            s = s + a1[gi] * back(s, 1, 0)
            s = s + a2[gi] * back(s, 2, 0)
            state = s + a4[gi] * back(s, 4, 0)
            h_ref[gi * SUBLANES:(gi + 1) * SUBLANES, cols] = state
        exit_row = SUBLANES - 1 - entry
        carries.append(state[exit_row:exit_row + 1, :])
    return jnp.concatenate(carries, axis=1)


def _lru_rates(lam_ref, bg_ref):
    rate2 = (-0.5 * LRU_C * LOG2E) * _softplus(-lam_ref[0])
    return rate2, 0.5 * bg_ref[0]


def _lru_fwd_kernel(prev_ref, u_ref, next_ref, cw_ref, cb_ref, wg_ref, bg_ref, lam_ref,
                    xc_ref, hf_ref, carry_ref, *, tc, n_chunks):
    c = pl.program_id(1)

    @pl.when(c == 0)
    def _():
        carry_ref[...] = jnp.zeros_like(carry_ref)

    rate2, bg_half = _lru_rates(lam_ref, bg_ref)
    ue = _with_halo(prev_ref, u_ref, next_ref, c, n_chunks)
    left = CONV_WIDTH // 2
    xc_half = 0.5 * cb_ref[...]
    for k in range(CONV_WIDTH):
        tap = ue if k == left else _shift_rows(ue, left - k)
        xc_half = xc_half + tap[HALO:HALO + tc] * (0.5 * cw_ref[k:k + 1, :])
    xc_ref[...] = xc_half
    carry_ref[...] = _lru_direction(xc_half, wg_ref.at[0], bg_half, rate2, carry_ref[...], hf_ref,
                                    reverse=False)


def _lru_fwd(z, conv_w, conv_b, w_gate_b, b_gate, lru_lambda3, *, batch, seq, tc):
    width = conv_b.shape[1]
    n_chunks = seq // tc
    prev_spec, next_spec = _halo_specs(width, 1, seq=seq, ts=tc, total_rows=batch * seq,
                                       tile_of=lambda b, c: c)
    _, n_head, head, two_head = w_gate_b.shape
    tile = pl.BlockSpec((tc, width), lambda b, c: (b * n_chunks + c, 0))
    return pl.pallas_call(
        functools.partial(_lru_fwd_kernel, tc=tc, n_chunks=n_chunks),
        grid=(batch, n_chunks),
        in_specs=[prev_spec,
                  pl.BlockSpec((tc, width), lambda b, c: (b * n_chunks + c, 1)),
                  next_spec,
                  pl.BlockSpec((CONV_WIDTH, width), lambda b, c: (0, 0)),
                  pl.BlockSpec((1, width), lambda b, c: (0, 0)),
                  pl.BlockSpec((1, n_head, head, two_head), lambda b, c: (0, 0, 0, 0)),
                  pl.BlockSpec((1, n_head, two_head), lambda b, c: (0, 0, 0)),
                  pl.BlockSpec((1, 1, width), lambda b, c: (0, 0, 0))],
        out_specs=[tile, tile],
        out_shape=[jax.ShapeDtypeStruct((batch * seq, width), F32)] * 2,
        scratch_shapes=[pltpu.VMEM((1, width), F32)],
        compiler_params=_params(("arbitrary", "arbitrary")),
        name="lru_fwd",
    )(z, z, z, conv_w, conv_b, w_gate_b, b_gate, lru_lambda3)


def _lru_bwd_out_kernel(xc_ref, hf_ref, gate_ref, wg_ref, bg_ref, lam_ref, ng_ref,
                        yp_ref, x_ref, mod_ref, w_ref, b_ref, fg_ref,
                        o_ref, hb_ref, yl_ref, y_ref, carry_ref, *, n_tiles, tiles_per_seq):
    s = pl.program_id(0)
    j = jnp.maximum(n_tiles - 1 - s, 0)

    @pl.when(s == 0)
    def _():
        carry_ref[...] = jnp.zeros_like(carry_ref)
        yl_ref[...] = jnp.zeros_like(yl_ref)

    rate2, bg_half = _lru_rates(lam_ref, bg_ref)
    kp = yp_ref.shape[1]
    n_chunk = o_ref.shape[1] // N_LRU_HEADS

    def project_chunk(h):
        cols = slice(h * n_chunk, (h + 1) * n_chunk)
        y = jnp.dot(yp_ref[...], w_ref[:kp, cols], preferred_element_type=F32)
        y = y + jnp.dot(yl_ref[...], w_ref[kp:, cols], preferred_element_type=F32)
        y_ref[:, cols] = y + b_ref[:, cols]

    carry_in = jnp.where(j % tiles_per_seq == tiles_per_seq - 1, 0.0, carry_ref[...])
    carry_ref[...] = _lru_direction(xc_ref[...], wg_ref.at[0], bg_half, rate2, carry_in, hb_ref, reverse=True,
                                    after_gates=project_chunk)

    xn = x_ref[...] + mod_ref[0][2:3] * y_ref[...]
    r = lax.rsqrt(jnp.mean(xn * xn, axis=-1, keepdims=True) + EPS)
    o_ref[...] = (xn * r) * fg_ref[...]

    yl = hf_ref[...] + hb_ref[...]
    rl = lax.rsqrt(jnp.mean(yl * yl, axis=-1, keepdims=True) + EPS)
    yl_ref[...] = ((yl * rl) * ng_ref[...] * _silu(gate_ref[...])).astype(yl_ref.dtype)


def _lru_bwd_out_proj(xc, hf, z, w_gate_b, b_gate, lru_lambda3, out_norm_g, yp, x2, mod3, w_out_b, b_out,
                      final_g, *, batch, seq, tm):
    d = x2.shape[1]
    width = xc.shape[1]
    tiles_per_seq = seq // tm
    n_tiles = batch * tiles_per_seq
    _, n_head, head, two_head = w_gate_b.shape

    def scan_tile(s):
        return jnp.maximum(n_tiles - 1 - s, 0)

    def proj_tile(s):
        return jnp.minimum(n_tiles - s, n_tiles - 1)

    return pl.pallas_call(
        functools.partial(_lru_bwd_out_kernel, n_tiles=n_tiles, tiles_per_seq=tiles_per_seq),
        grid=(n_tiles + 1,),
        in_specs=[pl.BlockSpec((tm, width), lambda s: (scan_tile(s), 0)),
                  pl.BlockSpec((tm, width), lambda s: (scan_tile(s), 0)),
                  pl.BlockSpec((tm, width), lambda s: (scan_tile(s), 3)),
                  pl.BlockSpec((1, n_head, head, two_head), lambda s: (1, 0, 0, 0)),
                  pl.BlockSpec((1, n_head, two_head), lambda s: (1, 0, 0)),
                  pl.BlockSpec((1, 1, width), lambda s: (1, 0, 0)),
                  pl.BlockSpec((1, width), lambda s: (0, 0)),
                  pl.BlockSpec((tm, yp.shape[1]), lambda s: (proj_tile(s), 0)),
                  pl.BlockSpec((tm, d), lambda s: (proj_tile(s), 0)),
                  pl.BlockSpec((1, 3, d), lambda s: (proj_tile(s) // tiles_per_seq, 0, 0)),
                  pl.BlockSpec(w_out_b.shape, lambda s: (0, 0)),
                  pl.BlockSpec((1, d), lambda s: (0, 0)),
                  pl.BlockSpec((1, d), lambda s: (0, 0))],
        out_specs=pl.BlockSpec((tm, d), lambda s: (proj_tile(s), 0)),
        out_shape=jax.ShapeDtypeStruct((batch * seq, d), F32),
        scratch_shapes=[pltpu.VMEM((tm, width), F32),
                        pltpu.VMEM((tm, width), BF16),
                        pltpu.VMEM((tm, d), F32),
                        pltpu.VMEM((1, width), F32)],
        compiler_params=_params(("arbitrary",)),
        name="lru_bwd_out_proj",
    )(xc, hf, z, w_gate_b, b_gate, lru_lambda3, out_norm_g, yp, x2, mod3, w_out_b, b_out, final_g)


def kernel(x, c, norm_g, w_ada, b_ada, w_in, b_in, w_pool, b_pool, pool_scale, conv_w, conv_b, w_gate,
           b_gate, lru_lambda, out_norm_pool_g, out_norm_lru_g, w_out, b_out, final_norm_g):
    batch, seq, d = x.shape
    assert w_in.shape[0] == 1, "single-layer block only"
    xs = x.reshape(batch * seq, d)
    c8 = jnp.pad(c, ((0, -batch % SUBLANES), (0, 0)))
    mod = _adaln_mod(c8, w_ada[0], b_ada[0][None, :], tn=768)
    mod3 = mod[:batch].reshape(batch, 3, d)
    z = _in_proj(xs, mod3, norm_g[0][None, :], w_in[0].astype(BF16), b_in[0][None, :],
                 batch=batch, seq=seq, tm=256)
    yp = _pool_mixer(z, w_pool[0].astype(BF16), b_pool[0], pool_scale[0][None, :],
                     out_norm_pool_g[0][None, :], batch=batch, seq=seq, ts=512)
    w_gate_b = w_gate[0].astype(BF16)
    lam3 = lru_lambda[0][:, None, :]
    xc, hf = _lru_fwd(z, conv_w[0], conv_b[0][None, :], w_gate_b, b_gate[0], lam3,
                      batch=batch, seq=seq, tc=256)
    out = _lru_bwd_out_proj(xc, hf, z, w_gate_b, b_gate[0], lam3, out_norm_lru_g[0][None, :], yp, xs, mod3,
                            w_out[0].astype(BF16), b_out[0][None, :], final_norm_g[None, :],
                            batch=batch, seq=seq, tm=256)
    return out.reshape(batch, seq, d)
```

```python
import functools
import math

import jax
import jax.numpy as jnp
from jax import lax
from jax.experimental import pallas as pl
from jax.experimental.pallas import tpu as pltpu

EPS = 1e-6
LRU_C = 8.0
POOL_WINDOWS = (2, 4, 8, 16)
N_LRU_HEADS = 4
CONV_WIDTH = 4
HALO = 8
SUBLANES = 8
V7X_VMEM_LIMIT_BYTES = 56 * 1024 * 1024
LOG2E = math.log2(math.e)
LN2 = math.log(2.0)

F32 = jnp.float32
BF16 = jnp.bfloat16


def _params(semantics):
    return pltpu.CompilerParams(dimension_semantics=semantics,
                                vmem_limit_bytes=V7X_VMEM_LIMIT_BYTES)


def _sigmoid(v):
    return 0.5 * jnp.tanh(0.5 * v) + 0.5


def _silu(v):
    return v * _sigmoid(v)


def _shift_rows(v, k):
    rows, width = v.shape
    v3 = v.reshape(rows // SUBLANES, SUBLANES, width)
    sub = lax.broadcasted_iota(jnp.int32, v3.shape, 1)
    if k > 0:
        rot = pltpu.roll(v3, k, 1)
        other = jnp.concatenate([rot[-1:], rot[:-1]], axis=0)
        out = jnp.where(sub >= k, rot, other)
    else:
        rot = pltpu.roll(v3, SUBLANES + k, 1)
        other = jnp.concatenate([rot[1:], rot[:1]], axis=0)
        out = jnp.where(sub < SUBLANES + k, rot, other)
    return out.reshape(rows, width)


def _adaln_kernel(c_ref, w_ref, b_ref, o_ref):
    ca = _silu(c_ref[...]).astype(BF16)
    o_ref[...] = jnp.dot(ca, w_ref[...].astype(BF16), preferred_element_type=F32) + b_ref[...]


def _adaln_mod(c8, w_ada, b_ada, *, tn):
    d, n = w_ada.shape
    return pl.pallas_call(
        _adaln_kernel,
        grid=(n // tn,),
        in_specs=[pl.BlockSpec((c8.shape[0], d), lambda j: (0, 0)),
                  pl.BlockSpec((d, tn), lambda j: (0, j)),
                  pl.BlockSpec((1, tn), lambda j: (0, j))],
        out_specs=pl.BlockSpec((c8.shape[0], tn), lambda j: (0, j)),
        out_shape=jax.ShapeDtypeStruct((c8.shape[0], n), F32),
        compiler_params=_params(("arbitrary",)),
        name="adaln_mod",
    )(c8, w_ada, b_ada)


def _in_proj_kernel(x_ref, mod_ref, g_ref, w_ref, b_ref, z_ref, *, n_chunk):
    x = x_ref[...]
    r = lax.rsqrt(jnp.mean(x * x, axis=-1, keepdims=True) + EPS)
    mod = mod_ref[0]
    h = (x * r) * (g_ref[...] * (1.0 + mod[1:2])) + mod[0:1]
    hb = h.astype(BF16)
    n = w_ref.shape[1]
    for j in range(n // n_chunk):
        sl = slice(j * n_chunk, (j + 1) * n_chunk)
        z_ref[:, sl] = jnp.dot(hb, w_ref[:, sl], preferred_element_type=F32) + b_ref[:, sl]


def _in_proj(x2, mod3, norm_g, w_in_b, b_in, *, batch, seq, tm):
    d = x2.shape[1]
    n = w_in_b.shape[1]
    nt = seq // tm
    return pl.pallas_call(
        functools.partial(_in_proj_kernel, n_chunk=1024),
        grid=(batch, nt),
        in_specs=[pl.BlockSpec((tm, d), lambda b, i: (b * nt + i, 0)),
                  pl.BlockSpec((1, 3, d), lambda b, i: (b, 0, 0)),
                  pl.BlockSpec((1, d), lambda b, i: (0, 0)),
                  pl.BlockSpec((d, n), lambda b, i: (0, 0)),
                  pl.BlockSpec((1, n), lambda b, i: (0, 0))],
        out_specs=pl.BlockSpec((tm, n), lambda b, i: (b * nt + i, 0)),
        out_shape=jax.ShapeDtypeStruct((batch * seq, n), F32),
        compiler_params=_params(("arbitrary", "arbitrary")),
        name="in_proj",
    )(x2, mod3, norm_g, w_in_b, b_in)


def _halo_specs(width, col_block, *, seq, ts, total_rows, tile_of):
    per_seq = seq // HALO
    per_tile = ts // HALO
    last = total_rows // HALO - 1

    def prev_map(*idx):
        b, i = idx[0], tile_of(*idx)
        return (jnp.maximum(b * per_seq + i * per_tile - 1, 0), col_block)

    def next_map(*idx):
        b, i = idx[0], tile_of(*idx)
        return (jnp.minimum(b * per_seq + (i + 1) * per_tile, last), col_block)

    return (pl.BlockSpec((HALO, width), prev_map), pl.BlockSpec((HALO, width), next_map))


def _with_halo(prev_ref, cur_ref, next_ref, tile, n_tiles):
    prev = jnp.where(tile > 0, prev_ref[...], 0.0)
    nxt = jnp.where(tile < n_tiles - 1, next_ref[...], 0.0)
    return jnp.concatenate([prev, cur_ref[...], nxt], axis=0)


def _pool_kernel(prev_ref, u_ref, next_ref, gate_ref, w_ref, b_ref, scale_ref, ng_ref, o_ref,
                 *, seq, ts, n_tiles):
    i = pl.program_id(1)
    ue = _with_halo(prev_ref, u_ref, next_ref, i, n_tiles)
    width = ue.shape[1]
    grp = width // len(POOL_WINDOWS)
    t = i * ts + lax.broadcasted_iota(jnp.int32, (ts, 1), 0)
    mixed = []
    for g, w in enumerate(POOL_WINDOWS):
        cols = slice(g * grp, (g + 1) * grp)
        e = ue[:, cols]
        lo = w // 2
        hi = w - lo - 1
        s = e + _shift_rows(e, 1)
        half = 1
        while 2 * half < w:
            s = _shift_rows(s, half) + _shift_rows(s, -half)
            half *= 2
        total = s[HALO:HALO + ts]
        cnt = (jnp.minimum(t + hi, seq - 1) - jnp.maximum(t - lo, 0) + 1).astype(F32)
        pooled = total / cnt - e[HALO:HALO + ts]
        m = jnp.dot(pooled.astype(BF16), w_ref[g], preferred_element_type=F32)
        mixed.append((m + b_ref[g:g + 1, :]) * scale_ref[:, cols])
    y = jnp.concatenate(mixed, axis=-1)
    r = lax.rsqrt(jnp.mean(y * y, axis=-1, keepdims=True) + EPS)
    o_ref[...] = ((y * r) * ng_ref[...] * _silu(gate_ref[...])).astype(o_ref.dtype)


def _pool_mixer(z, w_pool_b, b_pool, pool_scale, out_norm_g, *, batch, seq, ts):
    width = pool_scale.shape[1]
    n_tiles = seq // ts
    prev_spec, next_spec = _halo_specs(width, 0, seq=seq, ts=ts, total_rows=batch * seq,
                                       tile_of=lambda b, i: i)
    n_grp, grp, _ = w_pool_b.shape
    return pl.pallas_call(
        functools.partial(_pool_kernel, seq=seq, ts=ts, n_tiles=n_tiles),
        grid=(batch, n_tiles),
        in_specs=[prev_spec,
                  pl.BlockSpec((ts, width), lambda b, i: (b * n_tiles + i, 0)),
                  next_spec,
                  pl.BlockSpec((ts, width), lambda b, i: (b * n_tiles + i, 2)),
                  pl.BlockSpec((n_grp, grp, grp), lambda b, i: (0, 0, 0)),
                  pl.BlockSpec((n_grp, grp), lambda b, i: (0, 0)),
                  pl.BlockSpec((1, width), lambda b, i: (0, 0)),
                  pl.BlockSpec((1, width), lambda b, i: (0, 0))],
        out_specs=pl.BlockSpec((ts, width), lambda b, i: (b * n_tiles + i, 0)),
        out_shape=jax.ShapeDtypeStruct((batch * seq, width), BF16),
        compiler_params=_params(("arbitrary", "arbitrary")),
        name="pool_mixer",
    )(z, z, z, z, w_pool_b, b_pool, pool_scale, out_norm_g)


def _softplus(v):
    return jnp.maximum(v, 0.0) + jnp.log1p(jnp.exp(-jnp.abs(v)))


def _lru_direction(xc_half, wg_ref, bg_half, rate2, carry, h_ref, *, reverse, after_gates=None):
    tc, width = xc_half.shape
    head = width // N_LRU_HEADS
    groups = tc // SUBLANES
    entry = SUBLANES - 1 if reverse else 0
    sub = lax.broadcasted_iota(jnp.int32, (groups, SUBLANES, head), 1)
    is_entry = sub == entry

    def back(v, d, axis):
        return pltpu.roll(v, SUBLANES - d if reverse else d, axis)

    carries = []
    for h in range(N_LRU_HEADS):
        cols = slice(h * head, (h + 1) * head)
        xh = xc_half[:, cols]
        g = jnp.dot(xh.astype(BF16), wg_ref[h], preferred_element_type=F32) + bg_half[h:h + 1, :]
        if after_gates is not None:
            after_gates(h)
        tr = jnp.tanh(g[:, :head])
        ti = jnp.tanh(g[:, head:])
        log2a = tr * rate2[:, cols] + rate2[:, cols]
        a = jnp.exp2(log2a)
        one_minus_a2 = (a * a + 1.0) * jnp.tanh(log2a * (-LN2))
        mult = jnp.where(one_minus_a2 > 0.0, one_minus_a2 * lax.rsqrt(one_minus_a2), 0.0)
        b = mult * (ti + 1.0) * xh

        a3 = a.reshape(groups, SUBLANES, head)
        b3 = b.reshape(groups, SUBLANES, head)
        a_in = jnp.where(is_entry, a3, 0.0)
        a1 = jnp.where(is_entry, 0.0, a3)
        a2 = a1 * back(a1, 1, 1)
        a4 = a2 * back(a2, 2, 1)
        state = jnp.broadcast_to(carry[:, cols], (SUBLANES, head))
        order = range(groups - 1, -1, -1) if reverse else range(groups)
        for gi in order:
            s = b3[gi] + a_in[gi] * back(state, 1, 0)
            s = s + a1[gi] * back(s, 1, 0)
            s = s + a2[gi] * back(s, 2, 0)
            state = s + a4[gi] * back(s, 4, 0)
            h_ref[gi * SUBLANES:(gi + 1) * SUBLANES, cols] = state
        exit_row = SUBLANES - 1 - entry
        carries.append(state[exit_row:exit_row + 1, :])
    return jnp.concatenate(carries, axis=1)


def _lru_rates(lam_ref, bg_ref):
    rate2 = (-0.5 * LRU_C * LOG2E) * _softplus(-lam_ref[0])
    return rate2, 0.5 * bg_ref[0]


def _lru_fwd_kernel(prev_ref, u_ref, next_ref, cw_ref, cb_ref, wg_ref, bg_ref, lam_ref,
                    xc_ref, hf_ref, carry_ref, *, tc, n_chunks):
    c = pl.program_id(1)

    @pl.when(c == 0)
    def _():
        carry_ref[...] = jnp.zeros_like(carry_ref)

    rate2, bg_half = _lru_rates(lam_ref, bg_ref)
    ue = _with_halo(prev_ref, u_ref, next_ref, c, n_chunks)
    left = CONV_WIDTH // 2
    xc_half = 0.5 * cb_ref[...]
    for k in range(CONV_WIDTH):
        tap = ue if k == left else _shift_rows(ue, left - k)
        xc_half = xc_half + tap[HALO:HALO + tc] * (0.5 * cw_ref[k:k + 1, :])
    xc_ref[...] = xc_half
    carry_ref[...] = _lru_direction(xc_half, wg_ref.at[0], bg_half, rate2, carry_ref[...], hf_ref,
                                    reverse=False)


def _lru_fwd(z, conv_w, conv_b, w_gate_b, b_gate, lru_lambda3, *, batch, seq, tc):
    width = conv_b.shape[1]
    n_chunks = seq // tc
    prev_spec, next_spec = _halo_specs(width, 1, seq=seq, ts=tc, total_rows=batch * seq,
                                       tile_of=lambda b, c: c)
    _, n_head, head, two_head = w_gate_b.shape
    tile = pl.BlockSpec((tc, width), lambda b, c: (b * n_chunks + c, 0))
    return pl.pallas_call(
        functools.partial(_lru_fwd_kernel, tc=tc, n_chunks=n_chunks),
        grid=(batch, n_chunks),
        in_specs=[prev_spec,
                  pl.BlockSpec((tc, width), lambda b, c: (b * n_chunks + c, 1)),
                  next_spec,
                  pl.BlockSpec((CONV_WIDTH, width), lambda b, c: (0, 0)),
                  pl.BlockSpec((1, width), lambda b, c: (0, 0)),
                  pl.BlockSpec((1, n_head, head, two_head), lambda b, c: (0, 0, 0, 0)),
                  pl.BlockSpec((1, n_head, two_head), lambda b, c: (0, 0, 0)),
                  pl.BlockSpec((1, 1, width), lambda b, c: (0, 0, 0))],
        out_specs=[tile, tile],
        out_shape=[jax.ShapeDtypeStruct((batch * seq, width), F32)] * 2,
        scratch_shapes=[pltpu.VMEM((1, width), F32)],
        compiler_params=_params(("arbitrary", "arbitrary")),
        name="lru_fwd",
    )(z, z, z, conv_w, conv_b, w_gate_b, b_gate, lru_lambda3)


def _lru_bwd_out_kernel(xc_ref, hf_ref, gate_ref, wg_ref, bg_ref, lam_ref, ng_ref,
                        yp_ref, x_ref, mod_ref, w_ref, b_ref, fg_ref,
                        o_ref, hb_ref, yl_ref, y_ref, carry_ref, *, n_tiles, tiles_per_seq):
    s = pl.program_id(0)
    j = jnp.maximum(n_tiles - 1 - s, 0)

    @pl.when(s == 0)
    def _():
        carry_ref[...] = jnp.zeros_like(carry_ref)
        yl_ref[...] = jnp.zeros_like(yl_ref)

    rate2, bg_half = _lru_rates(lam_ref, bg_ref)
    kp = yp_ref.shape[1]
    n_chunk = o_ref.shape[1] // N_LRU_HEADS

    def project_chunk(h):
        cols = slice(h * n_chunk, (h + 1) * n_chunk)
        y = jnp.dot(yp_ref[...], w_ref[:kp, cols], preferred_element_type=F32)
        y = y + jnp.dot(yl_ref[...], w_ref[kp:, cols], preferred_element_type=F32)
        y_ref[:, cols] = y + b_ref[:, cols]

    carry_in = jnp.where(j % tiles_per_seq == tiles_per_seq - 1, 0.0, carry_ref[...])
    carry_ref[...] = _lru_direction(xc_ref[...], wg_ref.at[0], bg_half, rate2, carry_in, hb_ref, reverse=True,
                                    after_gates=project_chunk)

    xn = x_ref[...] + mod_ref[0][2:3] * y_ref[...]
    r = lax.rsqrt(jnp.mean(xn * xn, axis=-1, keepdims=True) + EPS)
    o_ref[...] = (xn * r) * fg_ref[...]

    yl = hf_ref[...] + hb_ref[...]
    rl = lax.rsqrt(jnp.mean(yl * yl, axis=-1, keepdims=True) + EPS)
    yl_ref[...] = ((yl * rl) * ng_ref[...] * _silu(gate_ref[...])).astype(yl_ref.dtype)


def _lru_bwd_out_proj(xc, hf, z, w_gate_b, b_gate, lru_lambda3, out_norm_g, yp, x2, mod3, w_out_b, b_out,
                      final_g, *, batch, seq, tm):
    d = x2.shape[1]
    width = xc.shape[1]
    tiles_per_seq = seq // tm
    n_tiles = batch * tiles_per_seq
    _, n_head, head, two_head = w_gate_b.shape

    def scan_tile(s):
        return jnp.maximum(n_tiles - 1 - s, 0)

    def proj_tile(s):
        return jnp.minimum(n_tiles - s, n_tiles - 1)

    return pl.pallas_call(
        functools.partial(_lru_bwd_out_kernel, n_tiles=n_tiles, tiles_per_seq=tiles_per_seq),
        grid=(n_tiles + 1,),
        in_specs=[pl.BlockSpec((tm, width), lambda s: (scan_tile(s), 0)),
                  pl.BlockSpec((tm, width), lambda s: (scan_tile(s), 0)),
                  pl.BlockSpec((tm, width), lambda s: (scan_tile(s), 3)),
                  pl.BlockSpec((1, n_head, head, two_head), lambda s: (1, 0, 0, 0)),
                  pl.BlockSpec((1, n_head, two_head), lambda s: (1, 0, 0)),
                  pl.BlockSpec((1, 1, width), lambda s: (1, 0, 0)),
                  pl.BlockSpec((1, width), lambda s: (0, 0)),
                  pl.BlockSpec((tm, yp.shape[1]), lambda s: (proj_tile(s), 0)),
                  pl.BlockSpec((tm, d), lambda s: (proj_tile(s), 0)),
                  pl.BlockSpec((1, 3, d), lambda s: (proj_tile(s) // tiles_per_seq, 0, 0)),
                  pl.BlockSpec(w_out_b.shape, lambda s: (0, 0)),
                  pl.BlockSpec((1, d), lambda s: (0, 0)),
                  pl.BlockSpec((1, d), lambda s: (0, 0))],
        out_specs=pl.BlockSpec((tm, d), lambda s: (proj_tile(s), 0)),
        out_shape=jax.ShapeDtypeStruct((batch * seq, d), F32),
        scratch_shapes=[pltpu.VMEM((tm, width), F32),
                        pltpu.VMEM((tm, width), BF16),
                        pltpu.VMEM((tm, d), F32),
                        pltpu.VMEM((1, width), F32)],
        compiler_params=_params(("arbitrary",)),
        name="lru_bwd_out_proj",
    )(xc, hf, z, w_gate_b, b_gate, lru_lambda3, out_norm_g, yp, x2, mod3, w_out_b, b_out, final_g)


def kernel(x, c, norm_g, w_ada, b_ada, w_in, b_in, w_pool, b_pool, pool_scale, conv_w, conv_b, w_gate,
           b_gate, lru_lambda, out_norm_pool_g, out_norm_lru_g, w_out, b_out, final_norm_g):
    batch, seq, d = x.shape
    assert w_in.shape[0] == 1, "single-layer block only"
    xs = x.reshape(batch * seq, d)
    c8 = jnp.pad(c, ((0, -batch % SUBLANES), (0, 0)))
    mod = _adaln_mod(c8, w_ada[0], b_ada[0][None, :], tn=768)
    mod3 = mod[:batch].reshape(batch, 3, d)
    z = _in_proj(xs, mod3, norm_g[0][None, :], w_in[0].astype(BF16), b_in[0][None, :],
                 batch=batch, seq=seq, tm=512)
    yp = _pool_mixer(z, w_pool[0].astype(BF16), b_pool[0], pool_scale[0][None, :],
                     out_norm_pool_g[0][None, :], batch=batch, seq=seq, ts=512)
    w_gate_b = w_gate[0].astype(BF16)
    lam3 = lru_lambda[0][:, None, :]
    xc, hf = _lru_fwd(z, conv_w[0], conv_b[0][None, :], w_gate_b, b_gate[0], lam3,
                      batch=batch, seq=seq, tc=512)
    out = _lru_bwd_out_proj(xc, hf, z, w_gate_b, b_gate[0], lam3, out_norm_lru_g[0][None, :], yp, xs, mod3,
                            w_out[0].astype(BF16), b_out[0][None, :], final_norm_g[None, :],
                            batch=batch, seq=seq, tm=512)
    return out.reshape(batch, seq, d)
```

```python
import functools
import math

import jax
import jax.numpy as jnp
from jax import lax
from jax.experimental import pallas as pl
from jax.experimental.pallas import tpu as pltpu

EPS = 1e-6
LRU_C = 8.0
POOL_WINDOWS = (2, 4, 8, 16)
N_LRU_HEADS = 4
CONV_WIDTH = 4
HALO = 8
SUBLANES = 8
V7X_VMEM_LIMIT_BYTES = 56 * 1024 * 1024
LOG2E = math.log2(math.e)
LN2 = math.log(2.0)

F32 = jnp.float32
BF16 = jnp.bfloat16


def _params(semantics):
    return pltpu.CompilerParams(dimension_semantics=semantics,
                                vmem_limit_bytes=V7X_VMEM_LIMIT_BYTES)


def _sigmoid(v):
    return 0.5 * jnp.tanh(0.5 * v) + 0.5


def _silu(v):
    return v * _sigmoid(v)


def _shift_rows(v, k):
    rows, width = v.shape
    v3 = v.reshape(rows // SUBLANES, SUBLANES, width)
    sub = lax.broadcasted_iota(jnp.int32, v3.shape, 1)
    if k > 0:
        rot = pltpu.roll(v3, k, 1)
        other = jnp.concatenate([rot[-1:], rot[:-1]], axis=0)
        out = jnp.where(sub >= k, rot, other)
    else:
        rot = pltpu.roll(v3, SUBLANES + k, 1)
        other = jnp.concatenate([rot[1:], rot[:1]], axis=0)
        out = jnp.where(sub < SUBLANES + k, rot, other)
    return out.reshape(rows, width)


WEIGHT_STAGE_ROWS = 256


def _load_weight_bf16(w_hbm, w_vmem, stage, sem):
    rows = stage.shape[1]
    n_chunks = w_hbm.shape[0] // rows

    def copy(c):
        return pltpu.make_async_copy(w_hbm.at[pl.ds(c * rows, rows), :], stage.at[c % 2], sem.at[c % 2])

    copy(0).start()
    for c in range(n_chunks):
        if c + 1 < n_chunks:
            copy(c + 1).start()
        copy(c).wait()
        w_vmem[c * rows:(c + 1) * rows, :] = stage[c % 2].astype(BF16)


def _weight_scratch(w):
    k, n = w.shape
    assert k % WEIGHT_STAGE_ROWS == 0
    return [pltpu.VMEM((k, n), BF16), pltpu.VMEM((2, WEIGHT_STAGE_ROWS, n), F32), pltpu.SemaphoreType.DMA((2,))]


def _adaln_kernel(c_ref, w_ref, b_ref, o_ref):
    ca = _silu(c_ref[...]).astype(BF16)
    o_ref[...] = jnp.dot(ca, w_ref[...].astype(BF16), preferred_element_type=F32) + b_ref[...]


def _adaln_mod(c8, w_ada, b_ada, *, tn):
    d, n = w_ada.shape
    return pl.pallas_call(
        _adaln_kernel,
        grid=(n // tn,),
        in_specs=[pl.BlockSpec((c8.shape[0], d), lambda j: (0, 0)),
                  pl.BlockSpec((d, tn), lambda j: (0, j)),
                  pl.BlockSpec((1, tn), lambda j: (0, j))],
        out_specs=pl.BlockSpec((c8.shape[0], tn), lambda j: (0, j)),
        out_shape=jax.ShapeDtypeStruct((c8.shape[0], n), F32),
        compiler_params=_params(("arbitrary",)),
        name="adaln_mod",
    )(c8, w_ada, b_ada)


def _in_proj_kernel(x_ref, mod_ref, g_ref, w_hbm, b_ref, z_ref, w_ref, stage, sem, *, n_chunk):
    @pl.when((pl.program_id(0) == 0) & (pl.program_id(1) == 0))
    def _():
        _load_weight_bf16(w_hbm, w_ref, stage, sem)

    x = x_ref[...]
    r = lax.rsqrt(jnp.mean(x * x, axis=-1, keepdims=True) + EPS)
    mod = mod_ref[0]
    h = (x * r) * (g_ref[...] * (1.0 + mod[1:2])) + mod[0:1]
    hb = h.astype(BF16)
    n = w_ref.shape[1]
    for j in range(n // n_chunk):
        sl = slice(j * n_chunk, (j + 1) * n_chunk)
        z_ref[:, sl] = jnp.dot(hb, w_ref[:, sl], preferred_element_type=F32) + b_ref[:, sl]


def _in_proj(x2, mod3, norm_g, w_in, b_in, *, batch, seq, tm):
    d = x2.shape[1]
    n = w_in.shape[1]
    nt = seq // tm
    return pl.pallas_call(
        functools.partial(_in_proj_kernel, n_chunk=1024),
        grid=(batch, nt),
        in_specs=[pl.BlockSpec((tm, d), lambda b, i: (b * nt + i, 0)),
                  pl.BlockSpec((1, 3, d), lambda b, i: (b, 0, 0)),
                  pl.BlockSpec((1, d), lambda b, i: (0, 0)),
                  pl.BlockSpec(memory_space=pl.ANY),
                  pl.BlockSpec((1, n), lambda b, i: (0, 0))],
        out_specs=pl.BlockSpec((tm, n), lambda b, i: (b * nt + i, 0)),
        out_shape=jax.ShapeDtypeStruct((batch * seq, n), F32),
        scratch_shapes=_weight_scratch(w_in),
        compiler_params=_params(("arbitrary", "arbitrary")),
        name="in_proj",
    )(x2, mod3, norm_g, w_in, b_in)


def _halo_specs(width, col_block, *, seq, ts, total_rows, tile_of):
    per_seq = seq // HALO
    per_tile = ts // HALO
    last = total_rows // HALO - 1

    def prev_map(*idx):
        b, i = idx[0], tile_of(*idx)
        return (jnp.maximum(b * per_seq + i * per_tile - 1, 0), col_block)

    def next_map(*idx):
        b, i = idx[0], tile_of(*idx)
        return (jnp.minimum(b * per_seq + (i + 1) * per_tile, last), col_block)

    return (pl.BlockSpec((HALO, width), prev_map), pl.BlockSpec((HALO, width), next_map))


def _with_halo(prev_ref, cur_ref, next_ref, tile, n_tiles):
    prev = jnp.where(tile > 0, prev_ref[...], 0.0)
    nxt = jnp.where(tile < n_tiles - 1, next_ref[...], 0.0)
    return jnp.concatenate([prev, cur_ref[...], nxt], axis=0)


def _pool_kernel(prev_ref, u_ref, next_ref, gate_ref, w_ref, b_ref, scale_ref, ng_ref, o_ref,
                 *, seq, ts, n_tiles):
    i = pl.program_id(1)
    ue = _with_halo(prev_ref, u_ref, next_ref, i, n_tiles)
    width = ue.shape[1]
    grp = width // len(POOL_WINDOWS)
    t = i * ts + lax.broadcasted_iota(jnp.int32, (ts, 1), 0)
    mixed = []
    for g, w in enumerate(POOL_WINDOWS):
        cols = slice(g * grp, (g + 1) * grp)
        e = ue[:, cols]
        lo = w // 2
        hi = w - lo - 1
        s = e + _shift_rows(e, 1)
        half = 1
        while 2 * half < w:
            s = _shift_rows(s, half) + _shift_rows(s, -half)
            half *= 2
        total = s[HALO:HALO + ts]
        cnt = (jnp.minimum(t + hi, seq - 1) - jnp.maximum(t - lo, 0) + 1).astype(F32)
        pooled = total / cnt - e[HALO:HALO + ts]
        m = jnp.dot(pooled.astype(BF16), w_ref[g].astype(BF16), preferred_element_type=F32)
        mixed.append((m + b_ref[g:g + 1, :]) * scale_ref[:, cols])
    y = jnp.concatenate(mixed, axis=-1)
    r = lax.rsqrt(jnp.mean(y * y, axis=-1, keepdims=True) + EPS)
    o_ref[...] = ((y * r) * ng_ref[...] * _silu(gate_ref[...])).astype(o_ref.dtype)


def _pool_mixer(z, w_pool, b_pool, pool_scale, out_norm_g, *, batch, seq, ts):
    width = pool_scale.shape[1]
    n_tiles = seq // ts
    prev_spec, next_spec = _halo_specs(width, 0, seq=seq, ts=ts, total_rows=batch * seq,
                                       tile_of=lambda b, i: i)
    n_grp, grp, _ = w_pool.shape
    return pl.pallas_call(
        functools.partial(_pool_kernel, seq=seq, ts=ts, n_tiles=n_tiles),
        grid=(batch, n_tiles),
        in_specs=[prev_spec,
                  pl.BlockSpec((ts, width), lambda b, i: (b * n_tiles + i, 0)),
                  next_spec,
                  pl.BlockSpec((ts, width), lambda b, i: (b * n_tiles + i, 2)),
                  pl.BlockSpec((n_grp, grp, grp), lambda b, i: (0, 0, 0)),
                  pl.BlockSpec((n_grp, grp), lambda b, i: (0, 0)),
                  pl.BlockSpec((1, width), lambda b, i: (0, 0)),
                  pl.BlockSpec((1, width), lambda b, i: (0, 0))],
        out_specs=pl.BlockSpec((ts, width), lambda b, i: (b * n_tiles + i, 0)),
        out_shape=jax.ShapeDtypeStruct((batch * seq, width), BF16),
        compiler_params=_params(("arbitrary", "arbitrary")),
        name="pool_mixer",
    )(z, z, z, z, w_pool, b_pool, pool_scale, out_norm_g)


def _softplus(v):
    return jnp.maximum(v, 0.0) + jnp.log1p(jnp.exp(-jnp.abs(v)))


def _lru_direction(xc_half, wg_ref, bg_half, rate2, carry, h_ref, *, reverse, after_gates=None):
    tc, width = xc_half.shape
    head = width // N_LRU_HEADS
    groups = tc // SUBLANES
    entry = SUBLANES - 1 if reverse else 0
    sub = lax.broadcasted_iota(jnp.int32, (groups, SUBLANES, head), 1)
    is_entry = sub == entry

    def back(v, d, axis):
        return pltpu.roll(v, SUBLANES - d if reverse else d, axis)

    carries = []
    for h in range(N_LRU_HEADS):
        cols = slice(h * head, (h + 1) * head)
        xh = xc_half[:, cols]
        g = jnp.dot(xh.astype(BF16), wg_ref[h].astype(BF16), preferred_element_type=F32) + bg_half[h:h + 1, :]
        if after_gates is not None:
            after_gates(h)
        tr = jnp.tanh(g[:, :head])
        ti = jnp.tanh(g[:, head:])
        log2a = tr * rate2[:, cols] + rate2[:, cols]
        a = jnp.exp2(log2a)
        one_minus_a2 = (a * a + 1.0) * jnp.tanh(log2a * (-LN2))
        mult = jnp.where(one_minus_a2 > 0.0, one_minus_a2 * lax.rsqrt(one_minus_a2), 0.0)
        b = mult * (ti + 1.0) * xh

        a3 = a.reshape(groups, SUBLANES, head)
        b3 = b.reshape(groups, SUBLANES, head)
        a_in = jnp.where(is_entry, a3, 0.0)
        a1 = jnp.where(is_entry, 0.0, a3)
        a2 = a1 * back(a1, 1, 1)
        a4 = a2 * back(a2, 2, 1)
        state = jnp.broadcast_to(carry[:, cols], (SUBLANES, head))
        order = range(groups - 1, -1, -1) if reverse else range(groups)
        for gi in order:
            s = b3[gi] + a_in[gi] * back(state, 1, 0)
            s = s + a1[gi] * back(s, 1, 0)
            s = s + a2[gi] * back(s, 2, 0)
            state = s + a4[gi] * back(s, 4, 0)
            h_ref[gi * SUBLANES:(gi + 1) * SUBLANES, cols] = state
        exit_row = SUBLANES - 1 - entry
        carries.append(state[exit_row:exit_row + 1, :])
    return jnp.concatenate(carries, axis=1)


def _lru_rates(lam_ref, bg_ref):
    rate2 = (-0.5 * LRU_C * LOG2E) * _softplus(-lam_ref[0])
    return rate2, 0.5 * bg_ref[0]


def _lru_fwd_kernel(prev_ref, u_ref, next_ref, cw_ref, cb_ref, wg_ref, bg_ref, lam_ref,
                    xc_ref, hf_ref, carry_ref, *, tc, n_chunks):
    c = pl.program_id(1)

    @pl.when(c == 0)
    def _():
        carry_ref[...] = jnp.zeros_like(carry_ref)

    rate2, bg_half = _lru_rates(lam_ref, bg_ref)
    ue = _with_halo(prev_ref, u_ref, next_ref, c, n_chunks)
    left = CONV_WIDTH // 2
    xc_half = 0.5 * cb_ref[...]
    for k in range(CONV_WIDTH):
        tap = ue if k == left else _shift_rows(ue, left - k)
        xc_half = xc_half + tap[HALO:HALO + tc] * (0.5 * cw_ref[k:k + 1, :])
    xc_ref[...] = xc_half
    carry_ref[...] = _lru_direction(xc_half, wg_ref.at[0], bg_half, rate2, carry_ref[...], hf_ref,
                                    reverse=False)


def _lru_fwd(z, conv_w, conv_b, w_gate, b_gate, lru_lambda3, *, batch, seq, tc):
    width = conv_b.shape[1]
    n_chunks = seq // tc
    prev_spec, next_spec = _halo_specs(width, 1, seq=seq, ts=tc, total_rows=batch * seq,
                                       tile_of=lambda b, c: c)
    _, n_head, head, two_head = w_gate.shape
    tile = pl.BlockSpec((tc, width), lambda b, c: (b * n_chunks + c, 0))
    return pl.pallas_call(
        functools.partial(_lru_fwd_kernel, tc=tc, n_chunks=n_chunks),
        grid=(batch, n_chunks),
        in_specs=[prev_spec,
                  pl.BlockSpec((tc, width), lambda b, c: (b * n_chunks + c, 1)),
                  next_spec,
                  pl.BlockSpec((CONV_WIDTH, width), lambda b, c: (0, 0)),
                  pl.BlockSpec((1, width), lambda b, c: (0, 0)),
                  pl.BlockSpec((1, n_head, head, two_head), lambda b, c: (0, 0, 0, 0)),
                  pl.BlockSpec((1, n_head, two_head), lambda b, c: (0, 0, 0)),
                  pl.BlockSpec((1, 1, width), lambda b, c: (0, 0, 0))],
        out_specs=[tile, tile],
        out_shape=[jax.ShapeDtypeStruct((batch * seq, width), F32)] * 2,
        scratch_shapes=[pltpu.VMEM((1, width), F32)],
        compiler_params=_params(("arbitrary", "arbitrary")),
        name="lru_fwd",
    )(z, z, z, conv_w, conv_b, w_gate, b_gate, lru_lambda3)


def _lru_bwd_out_kernel(xc_ref, hf_ref, gate_ref, wg_ref, bg_ref, lam_ref, ng_ref,
                        yp_ref, x_ref, mod_ref, w_hbm, b_ref, fg_ref,
                        o_ref, hb_ref, yl_ref, y_ref, carry_ref, w_ref, stage, sem, *, n_tiles, tiles_per_seq):
    s = pl.program_id(0)
    j = jnp.maximum(n_tiles - 1 - s, 0)

    @pl.when(s == 0)
    def _():
        carry_ref[...] = jnp.zeros_like(carry_ref)
        yl_ref[...] = jnp.zeros_like(yl_ref)
        _load_weight_bf16(w_hbm, w_ref, stage, sem)

    rate2, bg_half = _lru_rates(lam_ref, bg_ref)
    kp = yp_ref.shape[1]
    n_chunk = o_ref.shape[1] // N_LRU_HEADS

    def project_chunk(h):
        cols = slice(h * n_chunk, (h + 1) * n_chunk)
        y = jnp.dot(yp_ref[...], w_ref[:kp, cols], preferred_element_type=F32)
        y = y + jnp.dot(yl_ref[...], w_ref[kp:, cols], preferred_element_type=F32)
        y_ref[:, cols] = y + b_ref[:, cols]

    carry_in = jnp.where(j % tiles_per_seq == tiles_per_seq - 1, 0.0, carry_ref[...])
    carry_ref[...] = _lru_direction(xc_ref[...], wg_ref.at[0], bg_half, rate2, carry_in, hb_ref, reverse=True,
                                    after_gates=project_chunk)

    xn = x_ref[...] + mod_ref[0][2:3] * y_ref[...]
    r = lax.rsqrt(jnp.mean(xn * xn, axis=-1, keepdims=True) + EPS)
    o_ref[...] = (xn * r) * fg_ref[...]

    yl = hf_ref[...] + hb_ref[...]
    rl = lax.rsqrt(jnp.mean(yl * yl, axis=-1, keepdims=True) + EPS)
    yl_ref[...] = ((yl * rl) * ng_ref[...] * _silu(gate_ref[...])).astype(yl_ref.dtype)


def _lru_bwd_out_proj(xc, hf, z, w_gate, b_gate, lru_lambda3, out_norm_g, yp, x2, mod3, w_out, b_out,
                      final_g, *, batch, seq, tm):
    d = x2.shape[1]
    width = xc.shape[1]
    tiles_per_seq = seq // tm
    n_tiles = batch * tiles_per_seq
    _, n_head, head, two_head = w_gate.shape

    def scan_tile(s):
        return jnp.maximum(n_tiles - 1 - s, 0)

    def proj_tile(s):
        return jnp.minimum(n_tiles - s, n_tiles - 1)

    return pl.pallas_call(
        functools.partial(_lru_bwd_out_kernel, n_tiles=n_tiles, tiles_per_seq=tiles_per_seq),
        grid=(n_tiles + 1,),
        in_specs=[pl.BlockSpec((tm, width), lambda s: (scan_tile(s), 0)),
                  pl.BlockSpec((tm, width), lambda s: (scan_tile(s), 0)),
                  pl.BlockSpec((tm, width), lambda s: (scan_tile(s), 3)),
                  pl.BlockSpec((1, n_head, head, two_head), lambda s: (1, 0, 0, 0)),
                  pl.BlockSpec((1, n_head, two_head), lambda s: (1, 0, 0)),
                  pl.BlockSpec((1, 1, width), lambda s: (1, 0, 0)),
                  pl.BlockSpec((1, width), lambda s: (0, 0)),
                  pl.BlockSpec((tm, yp.shape[1]), lambda s: (proj_tile(s), 0)),
                  pl.BlockSpec((tm, d), lambda s: (proj_tile(s), 0)),
                  pl.BlockSpec((1, 3, d), lambda s: (proj_tile(s) // tiles_per_seq, 0, 0)),
                  pl.BlockSpec(memory_space=pl.ANY),
                  pl.BlockSpec((1, d), lambda s: (0, 0)),
                  pl.BlockSpec((1, d), lambda s: (0, 0))],
        out_specs=pl.BlockSpec((tm, d), lambda s: (proj_tile(s), 0)),
        out_shape=jax.ShapeDtypeStruct((batch * seq, d), F32),
        scratch_shapes=[pltpu.VMEM((tm, width), F32),
                        pltpu.VMEM((tm, width), BF16),
                        pltpu.VMEM((tm, d), F32),
                        pltpu.VMEM((1, width), F32),
                        *_weight_scratch(w_out)],
        compiler_params=_params(("arbitrary",)),
        name="lru_bwd_out_proj",
    )(xc, hf, z, w_gate, b_gate, lru_lambda3, out_norm_g, yp, x2, mod3, w_out, b_out, final_g)


def kernel(x, c, norm_g, w_ada, b_ada, w_in, b_in, w_pool, b_pool, pool_scale, conv_w, conv_b, w_gate,
           b_gate, lru_lambda, out_norm_pool_g, out_norm_lru_g, w_out, b_out, final_norm_g):
    batch, seq, d = x.shape
    assert w_in.shape[0] == 1, "single-layer block only"
    xs = x.reshape(batch * seq, d)
    c8 = jnp.pad(c, ((0, -batch % SUBLANES), (0, 0)))
    mod = _adaln_mod(c8, w_ada[0], b_ada[0][None, :], tn=768)
    mod3 = mod[:batch].reshape(batch, 3, d)
    z = _in_proj(xs, mod3, norm_g[0][None, :], w_in[0], b_in[0][None, :], batch=batch, seq=seq, tm=512)
    yp = _pool_mixer(z, w_pool[0], b_pool[0], pool_scale[0][None, :], out_norm_pool_g[0][None, :],
                     batch=batch, seq=seq, ts=512)
    lam3 = lru_lambda[0][:, None, :]
    xc, hf = _lru_fwd(z, conv_w[0], conv_b[0][None, :], w_gate[0], b_gate[0], lam3,
                      batch=batch, seq=seq, tc=512)
    out = _lru_bwd_out_proj(xc, hf, z, w_gate[0], b_gate[0], lam3, out_norm_lru_g[0][None, :], yp, xs, mod3,
                            w_out[0], b_out[0][None, :], final_norm_g[None, :],
                            batch=batch, seq=seq, tm=512)
    return out.reshape(batch, seq, d)
```

```python
import functools
import math

import jax
import jax.numpy as jnp
from jax import lax
from jax.experimental import pallas as pl
from jax.experimental.pallas import tpu as pltpu

EPS = 1e-6
LRU_C = 8.0
POOL_WINDOWS = (2, 4, 8, 16)
N_LRU_HEADS = 4
CONV_WIDTH = 4
HALO = 8
SUBLANES = 8
BF16_ROW_TILE = 16
V7X_VMEM_LIMIT_BYTES = 56 * 1024 * 1024
WEIGHT_STAGE_ROWS = 256
LOG2E = math.log2(math.e)
LN2 = math.log(2.0)

F32 = jnp.float32
BF16 = jnp.bfloat16


def _params(semantics):
    return pltpu.CompilerParams(dimension_semantics=semantics,
                                vmem_limit_bytes=V7X_VMEM_LIMIT_BYTES)


def _sigmoid(v):
    return 0.5 * jnp.tanh(0.5 * v) + 0.5


def _silu(v):
    return v * _sigmoid(v)


def _shift_rows(v, k):
    rows, width = v.shape
    v3 = v.reshape(rows // SUBLANES, SUBLANES, width)
    if k % SUBLANES == 0:
        g = (k // SUBLANES) % v3.shape[0]
        return jnp.concatenate([v3[-g:], v3[:-g]], axis=0).reshape(rows, width) if g else v
    sub = lax.broadcasted_iota(jnp.int32, v3.shape, 1)
    if k > 0:
        rot = pltpu.roll(v3, k, 1)
        other = jnp.concatenate([rot[-1:], rot[:-1]], axis=0)
        out = jnp.where(sub >= k, rot, other)
    else:
        rot = pltpu.roll(v3, SUBLANES + k, 1)
        other = jnp.concatenate([rot[1:], rot[:1]], axis=0)
        out = jnp.where(sub < SUBLANES + k, rot, other)
    return out.reshape(rows, width)


def _load_weight_bf16(w_hbm, w_vmem, stage, sem):
    rows = stage.shape[1]
    n_chunks = w_hbm.shape[0] // rows

    def copy(c):
        return pltpu.make_async_copy(w_hbm.at[pl.ds(c * rows, rows), :], stage.at[c % 2], sem.at[c % 2])

    copy(0).start()
    for c in range(n_chunks):
        if c + 1 < n_chunks:
            copy(c + 1).start()
        copy(c).wait()
        w_vmem[c * rows:(c + 1) * rows, :] = stage[c % 2].astype(BF16)


def _weight_scratch(w):
    k, n = w.shape
    assert k % WEIGHT_STAGE_ROWS == 0
    return [pltpu.VMEM((k, n), BF16), pltpu.VMEM((2, WEIGHT_STAGE_ROWS, n), F32), pltpu.SemaphoreType.DMA((2,))]


def _adaln_kernel(c_ref, w_ref, b_ref, o_ref):
    batch = c_ref.shape[0]
    ca = _silu(c_ref[...]).astype(BF16)
    ca = jnp.concatenate([ca, jnp.zeros((-batch % BF16_ROW_TILE, ca.shape[1]), BF16)], axis=0)
    res = jnp.dot(ca, w_ref[...].astype(BF16), preferred_element_type=F32)
    o_ref[0] = res[:batch] + b_ref[...]


def _adaln_mod(c, w_ada, b_ada, *, tn):
    batch, d = c.shape
    per_part = d // tn
    return pl.pallas_call(
        _adaln_kernel,
        grid=(3 * per_part,),
        in_specs=[pl.BlockSpec((batch, d), lambda j: (0, 0)),
                  pl.BlockSpec((None, d, tn), lambda j: (0, 0, j)),
                  pl.BlockSpec((1, tn), lambda j: (0, j))],
        out_specs=pl.BlockSpec((1, batch, tn), lambda j: (j // per_part, 0, j % per_part)),
        out_shape=jax.ShapeDtypeStruct((3, batch, d), F32),
        compiler_params=_params(("arbitrary",)),
        name="adaln_mod",
    )(c, w_ada, b_ada)


def _in_proj_kernel(x_ref, mod_ref, g_ref, w_hbm, b_ref, z_ref, w_ref, stage, sem, *, n_chunk):
    @pl.when((pl.program_id(0) == 0) & (pl.program_id(1) == 0))
    def _():
        _load_weight_bf16(w_hbm, w_ref, stage, sem)

    x = x_ref[...]
    r = lax.rsqrt(jnp.mean(x * x, axis=-1, keepdims=True) + EPS)
    b = pl.program_id(0)
    shift = mod_ref[0, pl.ds(b, 1), :]
    scale = mod_ref[1, pl.ds(b, 1), :]
    h = (x * r) * (g_ref[...] * (1.0 + scale)) + shift
    hb = h.astype(BF16)
    n = w_ref.shape[1]
    for j in range(n // n_chunk):
        sl = slice(j * n_chunk, (j + 1) * n_chunk)
        z_ref[:, sl] = jnp.dot(hb, w_ref[:, sl], preferred_element_type=F32) + b_ref[:, sl]


def _in_proj(x2, mod3, norm_g, w_in, b_in, *, batch, seq, tm):
    d = x2.shape[1]
    n = w_in.shape[1]
    nt = seq // tm
    return pl.pallas_call(
        functools.partial(_in_proj_kernel, n_chunk=1024),
        grid=(batch, nt),
        in_specs=[pl.BlockSpec((tm, d), lambda b, i: (b * nt + i, 0)),
                  pl.BlockSpec(mod3.shape, lambda b, i: (0, 0, 0)),
                  pl.BlockSpec((1, d), lambda b, i: (0, 0)),
                  pl.BlockSpec(memory_space=pl.ANY),
                  pl.BlockSpec((1, n), lambda b, i: (0, 0))],
        out_specs=pl.BlockSpec((tm, n), lambda b, i: (b * nt + i, 0)),
        out_shape=jax.ShapeDtypeStruct((batch * seq, n), F32),
        scratch_shapes=_weight_scratch(w_in),
        compiler_params=_params(("arbitrary", "arbitrary")),
        name="in_proj",
    )(x2, mod3, norm_g, w_in, b_in)


def _halo_specs(width, col_block, *, seq, ts, total_rows, tile_of):
    per_seq = seq // HALO
    per_tile = ts // HALO
    last = total_rows // HALO - 1

    def prev_map(*idx):
        b, i = idx[0], tile_of(*idx)
        return (jnp.maximum(b * per_seq + i * per_tile - 1, 0), col_block)

    def next_map(*idx):
        b, i = idx[0], tile_of(*idx)
        return (jnp.minimum(b * per_seq + (i + 1) * per_tile, last), col_block)

    return (pl.BlockSpec((HALO, width), prev_map), pl.BlockSpec((HALO, width), next_map))


def _with_halo(prev_ref, cur_ref, next_ref, tile, n_tiles):
    prev = jnp.where(tile > 0, prev_ref[...], 0.0)
    nxt = jnp.where(tile < n_tiles - 1, next_ref[...], 0.0)
    return jnp.concatenate([prev, cur_ref[...], nxt], axis=0)


def _pool_kernel(prev_ref, u_ref, next_ref, gate_ref, w_ref, b_ref, scale_ref, ng_ref, o_ref,
                 *, seq, ts, n_tiles):
    i = pl.program_id(1)
    ue = _with_halo(prev_ref, u_ref, next_ref, i, n_tiles)
    width = ue.shape[1]
    grp = width // len(POOL_WINDOWS)
    t = i * ts + lax.broadcasted_iota(jnp.int32, (ts, 1), 0)
    mixed = []
    for g, w in enumerate(POOL_WINDOWS):
        cols = slice(g * grp, (g + 1) * grp)
        e = ue[:, cols]
        lo = w // 2
        hi = w - lo - 1
        s, span = e, 1
        while span < w:
            s = s + _shift_rows(s, span)
            span *= 2
        total = (_shift_rows(s, -hi) if hi else s)[HALO:HALO + ts]
        cnt = (jnp.minimum(t + hi, seq - 1) - jnp.maximum(t - lo, 0) + 1).astype(F32)
        pooled = total / cnt - e[HALO:HALO + ts]
        m = jnp.dot(pooled.astype(BF16), w_ref[g].astype(BF16), preferred_element_type=F32)
        mixed.append((m + b_ref[g:g + 1, :]) * scale_ref[:, cols])
    y = jnp.concatenate(mixed, axis=-1)
    r = lax.rsqrt(jnp.mean(y * y, axis=-1, keepdims=True) + EPS)
    o_ref[...] = ((y * r) * ng_ref[...] * _silu(gate_ref[...])).astype(o_ref.dtype)


def _pool_mixer(z, w_pool, b_pool, pool_scale, out_norm_g, *, batch, seq, ts):
    width = pool_scale.shape[1]
    n_tiles = seq // ts
    prev_spec, next_spec = _halo_specs(width, 0, seq=seq, ts=ts, total_rows=batch * seq,
                                       tile_of=lambda b, i: i)
    n_grp, grp, _ = w_pool.shape
    return pl.pallas_call(
        functools.partial(_pool_kernel, seq=seq, ts=ts, n_tiles=n_tiles),
        grid=(batch, n_tiles),
        in_specs=[prev_spec,
                  pl.BlockSpec((ts, width), lambda b, i: (b * n_tiles + i, 0)),
                  next_spec,
                  pl.BlockSpec((ts, width), lambda b, i: (b * n_tiles + i, 2)),
                  pl.BlockSpec((n_grp, grp, grp), lambda b, i: (0, 0, 0)),
                  pl.BlockSpec((n_grp, grp), lambda b, i: (0, 0)),
                  pl.BlockSpec((1, width), lambda b, i: (0, 0)),
                  pl.BlockSpec((1, width), lambda b, i: (0, 0))],
        out_specs=pl.BlockSpec((ts, width), lambda b, i: (b * n_tiles + i, 0)),
        out_shape=jax.ShapeDtypeStruct((batch * seq, width), BF16),
        compiler_params=_params(("arbitrary", "arbitrary")),
        name="pool_mixer",
    )(z, z, z, z, w_pool, b_pool, pool_scale, out_norm_g)


def _softplus(v):
    return jnp.maximum(v, 0.0) + jnp.log1p(jnp.exp(-jnp.abs(v)))


def _lru_direction(xc_half, wg_ref, bg_half, rate2, carry, h_ref, *, reverse, after_gates=None):
    tc, width = xc_half.shape
    head = width // N_LRU_HEADS
    groups = tc // SUBLANES
    entry = SUBLANES - 1 if reverse else 0
    sub = lax.broadcasted_iota(jnp.int32, (groups, SUBLANES, head), 1)
    is_entry = sub == entry

    def back(v, d, axis):
        return pltpu.roll(v, SUBLANES - d if reverse else d, axis)

    carries = []
    for h in range(N_LRU_HEADS):
        cols = slice(h * head, (h + 1) * head)
        xh = xc_half[:, cols]
        g = jnp.dot(xh.astype(BF16), wg_ref[h], preferred_element_type=F32) + bg_half[h:h + 1, :]
        if after_gates is not None:
            after_gates(h)
        tr = jnp.tanh(g[:, :head])
        ti = jnp.tanh(g[:, head:])
        log2a = tr * rate2[:, cols] + rate2[:, cols]
        a = jnp.exp2(log2a)
        one_minus_a2 = (a * a + 1.0) * jnp.tanh(log2a * (-LN2))
        mult = jnp.where(one_minus_a2 > 0.0, one_minus_a2 * lax.rsqrt(one_minus_a2), 0.0)
        b = mult * (ti + 1.0) * xh

        a3 = a.reshape(groups, SUBLANES, head)
        b3 = b.reshape(groups, SUBLANES, head)
        a_in = jnp.where(is_entry, a3, 0.0)
        a1 = jnp.where(is_entry, 0.0, a3)
        a2 = a1 * back(a1, 1, 1)
        a4 = a2 * back(a2, 2, 1)
        state = jnp.broadcast_to(carry[:, cols], (SUBLANES, head))
        order = range(groups - 1, -1, -1) if reverse else range(groups)
        for gi in order:
            s = b3[gi] + a_in[gi] * back(state, 1, 0)
            s = s + a1[gi] * back(s, 1, 0)
            s = s + a2[gi] * back(s, 2, 0)
            state = s + a4[gi] * back(s, 4, 0)
            h_ref[gi * SUBLANES:(gi + 1) * SUBLANES, cols] = state
        exit_row = SUBLANES - 1 - entry
        carries.append(state[exit_row:exit_row + 1, :])
    return jnp.concatenate(carries, axis=1)


def _lru_rates(lam_row, bg_ref):
    rate2 = (-0.5 * LRU_C * LOG2E) * _softplus(-lam_row)
    return rate2, 0.5 * bg_ref[0]


def _cast_gate_weights(wg_ref, wgb_ref):
    for h in range(N_LRU_HEADS):
        wgb_ref[h] = wg_ref[0, h].astype(BF16)


def _lru_fwd_kernel(prev_ref, u_ref, next_ref, cw_ref, cb_ref, wg_ref, bg_ref, lam_ref,
                    xc_ref, hf_ref, carry_ref, wgb_ref, *, tc, n_chunks):
    c = pl.program_id(1)

    @pl.when((pl.program_id(0) == 0) & (c == 0))
    def _():
        _cast_gate_weights(wg_ref, wgb_ref)

    @pl.when(c == 0)
    def _():
        carry_ref[...] = jnp.zeros_like(carry_ref)

    rate2, bg_half = _lru_rates(lam_ref[0:1, :], bg_ref)
    ue = _with_halo(prev_ref, u_ref, next_ref, c, n_chunks)
    left = CONV_WIDTH // 2
    xc_half = 0.5 * cb_ref[...]
    for k in range(CONV_WIDTH):
        tap = ue if k == left else _shift_rows(ue, left - k)
        xc_half = xc_half + tap[HALO:HALO + tc] * (0.5 * cw_ref[k:k + 1, :])
    xc_ref[...] = xc_half
    carry_ref[...] = _lru_direction(xc_half, wgb_ref, bg_half, rate2, carry_ref[...], hf_ref, reverse=False)


def _lru_fwd(z, conv_w, conv_b, w_gate, b_gate, lru_lambda, *, batch, seq, tc):
    width = conv_b.shape[1]
    n_chunks = seq // tc
    prev_spec, next_spec = _halo_specs(width, 1, seq=seq, ts=tc, total_rows=batch * seq,
                                       tile_of=lambda b, c: c)
    _, n_head, head, two_head = w_gate.shape
    tile = pl.BlockSpec((tc, width), lambda b, c: (b * n_chunks + c, 0))
    return pl.pallas_call(
        functools.partial(_lru_fwd_kernel, tc=tc, n_chunks=n_chunks),
        grid=(batch, n_chunks),
        in_specs=[prev_spec,
                  pl.BlockSpec((tc, width), lambda b, c: (b * n_chunks + c, 1)),
                  next_spec,
                  pl.BlockSpec((CONV_WIDTH, width), lambda b, c: (0, 0)),
                  pl.BlockSpec((1, width), lambda b, c: (0, 0)),
                  pl.BlockSpec((1, n_head, head, two_head), lambda b, c: (0, 0, 0, 0)),
                  pl.BlockSpec((1, n_head, two_head), lambda b, c: (0, 0, 0)),
                  pl.BlockSpec(lru_lambda.shape, lambda b, c: (0, 0))],
        out_specs=[tile, tile],
        out_shape=[jax.ShapeDtypeStruct((batch * seq, width), F32)] * 2,
        scratch_shapes=[pltpu.VMEM((1, width), F32),
                        pltpu.VMEM((n_head, head, two_head), BF16)],
        compiler_params=_params(("arbitrary", "arbitrary")),
        name="lru_fwd",
    )(z, z, z, conv_w, conv_b, w_gate, b_gate, lru_lambda)


def _lru_bwd_out_kernel(xc_ref, hf_ref, gate_ref, wg_ref, bg_ref, lam_ref, ng_ref,
                        yp_ref, x_ref, mod_ref, w_hbm, b_ref, fg_ref,
                        o_ref, hb_ref, yl_ref, y_ref, carry_ref, wgb_ref, w_ref, stage, sem, *, n_tiles,
                        tiles_per_seq):
    s = pl.program_id(0)
    j = jnp.maximum(n_tiles - 1 - s, 0)

    @pl.when(s == 0)
    def _():
        carry_ref[...] = jnp.zeros_like(carry_ref)
        yl_ref[...] = jnp.zeros_like(yl_ref)
        _cast_gate_weights(wg_ref, wgb_ref)
        _load_weight_bf16(w_hbm, w_ref, stage, sem)

    rate2, bg_half = _lru_rates(lam_ref[1:2, :], bg_ref)
    kp = yp_ref.shape[1]
    n_chunk = o_ref.shape[1] // N_LRU_HEADS

    def project_chunk(h):
        cols = slice(h * n_chunk, (h + 1) * n_chunk)
        y = jnp.dot(yp_ref[...], w_ref[:kp, cols], preferred_element_type=F32)
        y = y + jnp.dot(yl_ref[...], w_ref[kp:, cols], preferred_element_type=F32)
        y_ref[:, cols] = y + b_ref[:, cols]

    carry_in = jnp.where(j % tiles_per_seq == tiles_per_seq - 1, 0.0, carry_ref[...])
    carry_ref[...] = _lru_direction(xc_ref[...], wgb_ref, bg_half, rate2, carry_in, hb_ref, reverse=True,
                                    after_gates=project_chunk)

    gate = mod_ref[2, pl.ds(jnp.minimum(n_tiles - s, n_tiles - 1) // tiles_per_seq, 1), :]
    xn = x_ref[...] + gate * y_ref[...]
    r = lax.rsqrt(jnp.mean(xn * xn, axis=-1, keepdims=True) + EPS)
    o_ref[...] = (xn * r) * fg_ref[...]

    yl = hf_ref[...] + hb_ref[...]
    rl = lax.rsqrt(jnp.mean(yl * yl, axis=-1, keepdims=True) + EPS)
    yl_ref[...] = ((yl * rl) * ng_ref[...] * _silu(gate_ref[...])).astype(yl_ref.dtype)


def _lru_bwd_out_proj(xc, hf, z, w_gate, b_gate, lru_lambda, out_norm_g, yp, x2, mod3, w_out, b_out,
                      final_g, *, batch, seq, tm):
    d = x2.shape[1]
    width = xc.shape[1]
    tiles_per_seq = seq // tm
    n_tiles = batch * tiles_per_seq
    _, n_head, head, two_head = w_gate.shape

    def scan_tile(s):
        return jnp.maximum(n_tiles - 1 - s, 0)

    def proj_tile(s):
        return jnp.minimum(n_tiles - s, n_tiles - 1)

    return pl.pallas_call(
        functools.partial(_lru_bwd_out_kernel, n_tiles=n_tiles, tiles_per_seq=tiles_per_seq),
        grid=(n_tiles + 1,),
        in_specs=[pl.BlockSpec((tm, width), lambda s: (scan_tile(s), 0)),
                  pl.BlockSpec((tm, width), lambda s: (scan_tile(s), 0)),
                  pl.BlockSpec((tm, width), lambda s: (scan_tile(s), 3)),
                  pl.BlockSpec((1, n_head, head, two_head), lambda s: (1, 0, 0, 0)),
                  pl.BlockSpec((1, n_head, two_head), lambda s: (1, 0, 0)),
                  pl.BlockSpec(lru_lambda.shape, lambda s: (0, 0)),
                  pl.BlockSpec((1, width), lambda s: (0, 0)),
                  pl.BlockSpec((tm, yp.shape[1]), lambda s: (proj_tile(s), 0)),
                  pl.BlockSpec((tm, d), lambda s: (proj_tile(s), 0)),
                  pl.BlockSpec(mod3.shape, lambda s: (0, 0, 0)),
                  pl.BlockSpec(memory_space=pl.ANY),
                  pl.BlockSpec((1, d), lambda s: (0, 0)),
                  pl.BlockSpec((1, d), lambda s: (0, 0))],
        out_specs=pl.BlockSpec((tm, d), lambda s: (proj_tile(s), 0)),
        out_shape=jax.ShapeDtypeStruct((batch * seq, d), F32),
        scratch_shapes=[pltpu.VMEM((tm, width), F32),
                        pltpu.VMEM((tm, width), BF16),
                        pltpu.VMEM((tm, d), F32),
                        pltpu.VMEM((1, width), F32),
                        pltpu.VMEM((n_head, head, two_head), BF16),
                        *_weight_scratch(w_out)],
        compiler_params=_params(("arbitrary",)),
        name="lru_bwd_out_proj",
    )(xc, hf, z, w_gate, b_gate, lru_lambda, out_norm_g, yp, x2, mod3, w_out, b_out, final_g)


def kernel(x, c, norm_g, w_ada, b_ada, w_in, b_in, w_pool, b_pool, pool_scale, conv_w, conv_b, w_gate,
           b_gate, lru_lambda, out_norm_pool_g, out_norm_lru_g, w_out, b_out, final_norm_g):
    batch, seq, d = x.shape
    assert w_in.shape[0] == 1, "single-layer block only"
    xs = x.reshape(batch * seq, d)
    mod3 = _adaln_mod(c, w_ada, b_ada, tn=1024)
    z = _in_proj(xs, mod3, norm_g, w_in[0], b_in, batch=batch, seq=seq, tm=512)
    yp = _pool_mixer(z, w_pool[0], b_pool[0], pool_scale, out_norm_pool_g, batch=batch, seq=seq, ts=512)
    xc, hf = _lru_fwd(z, conv_w[0], conv_b, w_gate[0], b_gate[0], lru_lambda[0], batch=batch, seq=seq, tc=512)
    out = _lru_bwd_out_proj(xc, hf, z, w_gate[0], b_gate[0], lru_lambda[0], out_norm_lru_g, yp, xs, mod3,
                            w_out[0], b_out, final_norm_g[None, :], batch=batch, seq=seq, tm=512)
    return out.reshape(batch, seq, d)
```

```python
import functools
import math

import jax
import jax.numpy as jnp
from jax import lax
from jax.experimental import pallas as pl
from jax.experimental.pallas import tpu as pltpu

EPS = 1e-6
LRU_C = 8.0
POOL_WINDOWS = (2, 4, 8, 16)
N_LRU_HEADS = 4
CONV_WIDTH = 4
HALO = 8
SUBLANES = 8
BF16_ROW_TILE = 16
V7X_VMEM_LIMIT_BYTES = 56 * 1024 * 1024
WEIGHT_STAGE_ROWS = 256
LOG2E = math.log2(math.e)
LN2 = math.log(2.0)

F32 = jnp.float32
BF16 = jnp.bfloat16


def _params(semantics):
    return pltpu.CompilerParams(dimension_semantics=semantics,
                                vmem_limit_bytes=V7X_VMEM_LIMIT_BYTES)


def _sigmoid(v):
    return 0.5 * jnp.tanh(0.5 * v) + 0.5


def _silu(v):
    return v * _sigmoid(v)


def _shift_rows(v, k):
    rows, width = v.shape
    v3 = v.reshape(rows // SUBLANES, SUBLANES, width)
    if k % SUBLANES == 0:
        g = (k // SUBLANES) % v3.shape[0]
        return jnp.concatenate([v3[-g:], v3[:-g]], axis=0).reshape(rows, width) if g else v
    sub = lax.broadcasted_iota(jnp.int32, v3.shape, 1)
    if k > 0:
        rot = pltpu.roll(v3, k, 1)
        other = jnp.concatenate([rot[-1:], rot[:-1]], axis=0)
        out = jnp.where(sub >= k, rot, other)
    else:
        rot = pltpu.roll(v3, SUBLANES + k, 1)
        other = jnp.concatenate([rot[1:], rot[:1]], axis=0)
        out = jnp.where(sub < SUBLANES + k, rot, other)
    return out.reshape(rows, width)


def _load_weight_bf16(w_hbm, w_vmem, stage, sem):
    rows = stage.shape[1]
    n_chunks = w_hbm.shape[0] // rows

    def copy(c):
        return pltpu.make_async_copy(w_hbm.at[pl.ds(c * rows, rows), :], stage.at[c % 2], sem.at[c % 2])

    copy(0).start()
    for c in range(n_chunks):
        if c + 1 < n_chunks:
            copy(c + 1).start()
        copy(c).wait()
        w_vmem[c * rows:(c + 1) * rows, :] = stage[c % 2].astype(BF16)


def _weight_scratch(w):
    k, n = w.shape
    assert k % WEIGHT_STAGE_ROWS == 0
    return [pltpu.VMEM((k, n), BF16), pltpu.VMEM((2, WEIGHT_STAGE_ROWS, n), F32), pltpu.SemaphoreType.DMA((2,))]


def _adaln_kernel(c_ref, w_ref, b_ref, o_ref):
    batch = c_ref.shape[0]
    ca = _silu(c_ref[...]).astype(BF16)
    ca = jnp.concatenate([ca, jnp.zeros((-batch % BF16_ROW_TILE, ca.shape[1]), BF16)], axis=0)
    res = jnp.dot(ca, w_ref[...].astype(BF16), preferred_element_type=F32)
    o_ref[0] = res[:batch] + b_ref[...]


def _adaln_mod(c, w_ada, b_ada, *, tn):
    batch, d = c.shape
    per_part = d // tn
    return pl.pallas_call(
        _adaln_kernel,
        grid=(3 * per_part,),
        in_specs=[pl.BlockSpec((batch, d), lambda j: (0, 0)),
                  pl.BlockSpec((None, d, tn), lambda j: (0, 0, j)),
                  pl.BlockSpec((1, tn), lambda j: (0, j))],
        out_specs=pl.BlockSpec((1, batch, tn), lambda j: (j // per_part, 0, j % per_part)),
        out_shape=jax.ShapeDtypeStruct((3, batch, d), F32),
        compiler_params=_params(("arbitrary",)),
        name="adaln_mod",
    )(c, w_ada, b_ada)


def _in_proj_kernel(x_ref, mod_ref, g_ref, w_hbm, b_ref, u_ref, sg_ref, w_ref, stage, sem, *, n_chunk):
    @pl.when((pl.program_id(0) == 0) & (pl.program_id(1) == 0))
    def _():
        _load_weight_bf16(w_hbm, w_ref, stage, sem)

    x = x_ref[...]
    r = lax.rsqrt(jnp.mean(x * x, axis=-1, keepdims=True) + EPS)
    b = pl.program_id(0)
    shift = mod_ref[0, pl.ds(b, 1), :]
    scale = mod_ref[1, pl.ds(b, 1), :]
    h = (x * r) * (g_ref[...] * (1.0 + scale)) + shift
    hb = h.astype(BF16)
    n = w_ref.shape[1]
    n_mix = u_ref.shape[1]
    for j in range(n // n_chunk):
        sl = slice(j * n_chunk, (j + 1) * n_chunk)
        zj = jnp.dot(hb, w_ref[:, sl], preferred_element_type=F32) + b_ref[:, sl]
        if j * n_chunk < n_mix:
            u_ref[:, sl] = zj
        else:
            sg_ref[:, j * n_chunk - n_mix:(j + 1) * n_chunk - n_mix] = _silu(zj).astype(sg_ref.dtype)


def _in_proj(x2, mod3, norm_g, w_in, b_in, *, batch, seq, tm):
    d = x2.shape[1]
    n = w_in.shape[1]
    nt = seq // tm
    rows = batch * seq
    return pl.pallas_call(
        functools.partial(_in_proj_kernel, n_chunk=1024),
        grid=(batch, nt),
        in_specs=[pl.BlockSpec((tm, d), lambda b, i: (b * nt + i, 0)),
                  pl.BlockSpec(mod3.shape, lambda b, i: (0, 0, 0)),
                  pl.BlockSpec((1, d), lambda b, i: (0, 0)),
                  pl.BlockSpec(memory_space=pl.ANY),
                  pl.BlockSpec((1, n), lambda b, i: (0, 0))],
        out_specs=[pl.BlockSpec((tm, n // 2), lambda b, i: (b * nt + i, 0)),
                   pl.BlockSpec((tm, n // 2), lambda b, i: (b * nt + i, 0))],
        out_shape=[jax.ShapeDtypeStruct((rows, n // 2), F32), jax.ShapeDtypeStruct((rows, n // 2), BF16)],
        scratch_shapes=_weight_scratch(w_in),
        compiler_params=_params(("arbitrary", "arbitrary")),
        name="in_proj",
    )(x2, mod3, norm_g, w_in, b_in)


def _halo_specs(width, col_block, *, seq, ts, total_rows, tile_of):
    per_seq = seq // HALO
    per_tile = ts // HALO
    last = total_rows // HALO - 1

    def prev_map(*idx):
        b, i = idx[0], tile_of(*idx)
        return (jnp.maximum(b * per_seq + i * per_tile - 1, 0), col_block)

    def next_map(*idx):
        b, i = idx[0], tile_of(*idx)
        return (jnp.minimum(b * per_seq + (i + 1) * per_tile, last), col_block)

    return (pl.BlockSpec((HALO, width), prev_map), pl.BlockSpec((HALO, width), next_map))


def _with_halo(prev_ref, cur_ref, next_ref, tile, n_tiles):
    prev = jnp.where(tile > 0, prev_ref[...], 0.0)
    nxt = jnp.where(tile < n_tiles - 1, next_ref[...], 0.0)
    return jnp.concatenate([prev, cur_ref[...], nxt], axis=0)


def _pool_kernel(prev_ref, u_ref, next_ref, gate_ref, w_ref, b_ref, scale_ref, ng_ref, o_ref,
                 *, seq, ts, n_tiles):
    i = pl.program_id(1)
    ue = _with_halo(prev_ref, u_ref, next_ref, i, n_tiles)
    width = ue.shape[1]
    grp = width // len(POOL_WINDOWS)
    t = i * ts + lax.broadcasted_iota(jnp.int32, (ts, 1), 0)
    mixed = []
    for g, w in enumerate(POOL_WINDOWS):
        cols = slice(g * grp, (g + 1) * grp)
        e = ue[:, cols]
        lo = w // 2
        hi = w - lo - 1
        s, span = e, 1
        while span < w:
            s = s + _shift_rows(s, span)
            span *= 2
        total = (_shift_rows(s, -hi) if hi else s)[HALO:HALO + ts]
        cnt = (jnp.minimum(t + hi, seq - 1) - jnp.maximum(t - lo, 0) + 1).astype(F32)
        pooled = total / cnt - e[HALO:HALO + ts]
        m = jnp.dot(pooled.astype(BF16), w_ref[g].astype(BF16), preferred_element_type=F32)
        mixed.append((m + b_ref[g:g + 1, :]) * scale_ref[:, cols])
    y = jnp.concatenate(mixed, axis=-1)
    r = lax.rsqrt(jnp.mean(y * y, axis=-1, keepdims=True) + EPS)
    o_ref[...] = ((y * r) * ng_ref[...] * gate_ref[...].astype(F32)).astype(o_ref.dtype)


def _pool_mixer(z, sg, w_pool, b_pool, pool_scale, out_norm_g, *, batch, seq, ts):
    width = pool_scale.shape[1]
    n_tiles = seq // ts
    prev_spec, next_spec = _halo_specs(width, 0, seq=seq, ts=ts, total_rows=batch * seq,
                                       tile_of=lambda b, i: i)
    n_grp, grp, _ = w_pool.shape
    return pl.pallas_call(
        functools.partial(_pool_kernel, seq=seq, ts=ts, n_tiles=n_tiles),
        grid=(batch, n_tiles),
        in_specs=[prev_spec,
                  pl.BlockSpec((ts, width), lambda b, i: (b * n_tiles + i, 0)),
                  next_spec,
                  pl.BlockSpec((ts, width), lambda b, i: (b * n_tiles + i, 0)),
                  pl.BlockSpec((n_grp, grp, grp), lambda b, i: (0, 0, 0)),
                  pl.BlockSpec((n_grp, grp), lambda b, i: (0, 0)),
                  pl.BlockSpec((1, width), lambda b, i: (0, 0)),
                  pl.BlockSpec((1, width), lambda b, i: (0, 0))],
        out_specs=pl.BlockSpec((ts, width), lambda b, i: (b * n_tiles + i, 0)),
        out_shape=jax.ShapeDtypeStruct((batch * seq, width), BF16),
        compiler_params=_params(("arbitrary", "arbitrary")),
        name="pool_mixer",
    )(z, z, z, sg, w_pool, b_pool, pool_scale, out_norm_g)


def _softplus(v):
    return jnp.maximum(v, 0.0) + jnp.log1p(jnp.exp(-jnp.abs(v)))


def _lru_direction(xc_half, wg_ref, bg_half, rate2, carry, h_ref, *, reverse, after_gates=None):
    tc, width = xc_half.shape
    head = width // N_LRU_HEADS
    groups = tc // SUBLANES
    entry = SUBLANES - 1 if reverse else 0
    sub = lax.broadcasted_iota(jnp.int32, (groups, SUBLANES, head), 1)
    is_entry = sub == entry

    def back(v, d, axis):
        return pltpu.roll(v, SUBLANES - d if reverse else d, axis)

    carries = []
    for h in range(N_LRU_HEADS):
        cols = slice(h * head, (h + 1) * head)
        xh = xc_half[:, cols]
        g = jnp.dot(xh.astype(BF16), wg_ref[h], preferred_element_type=F32) + bg_half[h:h + 1, :]
        if after_gates is not None:
            after_gates(h)
        tr = jnp.tanh(g[:, :head])
        ti = jnp.tanh(g[:, head:])
        log2a = tr * rate2[:, cols] + rate2[:, cols]
        a = jnp.exp2(log2a)
        one_minus_a2 = (a * a + 1.0) * jnp.tanh(log2a * (-LN2))
        mult = jnp.where(one_minus_a2 > 0.0, one_minus_a2 * lax.rsqrt(one_minus_a2), 0.0)
        b = mult * (ti + 1.0) * xh

        a3 = a.reshape(groups, SUBLANES, head)
        b3 = b.reshape(groups, SUBLANES, head)
        a_in = jnp.where(is_entry, a3, 0.0)
        a1 = jnp.where(is_entry, 0.0, a3)
        a2 = a1 * back(a1, 1, 1)
        a4 = a2 * back(a2, 2, 1)
        state = jnp.broadcast_to(carry[:, cols], (SUBLANES, head))
        order = range(groups - 1, -1, -1) if reverse else range(groups)
        for gi in order:
            s = b3[gi] + a_in[gi] * back(state, 1, 0)
            s = s + a1[gi] * back(s, 1, 0)
            s = s + a2[gi] * back(s, 2, 0)
            state = s + a4[gi] * back(s, 4, 0)
            h_ref[gi * SUBLANES:(gi + 1) * SUBLANES, cols] = state
        exit_row = SUBLANES - 1 - entry
        carries.append(state[exit_row:exit_row + 1, :])
    return jnp.concatenate(carries, axis=1)


def _lru_rates(lam_row, bg_ref):
    rate2 = (-0.5 * LRU_C * LOG2E) * _softplus(-lam_row)
    return rate2, 0.5 * bg_ref[0]


def _cast_gate_weights(wg_ref, wgb_ref):
    for h in range(N_LRU_HEADS):
        wgb_ref[h] = wg_ref[0, h].astype(BF16)


def _lru_fwd_kernel(prev_ref, u_ref, next_ref, cw_ref, cb_ref, wg_ref, bg_ref, lam_ref,
                    xc_ref, hf_ref, carry_ref, wgb_ref, *, tc, n_chunks):
    c = pl.program_id(1)

    @pl.when((pl.program_id(0) == 0) & (c == 0))
    def _():
        _cast_gate_weights(wg_ref, wgb_ref)

    @pl.when(c == 0)
    def _():
        carry_ref[...] = jnp.zeros_like(carry_ref)

    rate2, bg_half = _lru_rates(lam_ref[0:1, :], bg_ref)
    ue = _with_halo(prev_ref, u_ref, next_ref, c, n_chunks)
    left = CONV_WIDTH // 2
    xc_half = 0.5 * cb_ref[...]
    for k in range(CONV_WIDTH):
        tap = ue if k == left else _shift_rows(ue, left - k)
        xc_half = xc_half + tap[HALO:HALO + tc] * (0.5 * cw_ref[k:k + 1, :])
    xc_ref[...] = xc_half
    carry_ref[...] = _lru_direction(xc_half, wgb_ref, bg_half, rate2, carry_ref[...], hf_ref, reverse=False)


def _lru_fwd(z, conv_w, conv_b, w_gate, b_gate, lru_lambda, *, batch, seq, tc):
    width = conv_b.shape[1]
    n_chunks = seq // tc
    prev_spec, next_spec = _halo_specs(width, 1, seq=seq, ts=tc, total_rows=batch * seq,
                                       tile_of=lambda b, c: c)
    _, n_head, head, two_head = w_gate.shape
    tile = pl.BlockSpec((tc, width), lambda b, c: (b * n_chunks + c, 0))
    return pl.pallas_call(
        functools.partial(_lru_fwd_kernel, tc=tc, n_chunks=n_chunks),
        grid=(batch, n_chunks),
        in_specs=[prev_spec,
                  pl.BlockSpec((tc, width), lambda b, c: (b * n_chunks + c, 1)),
                  next_spec,
                  pl.BlockSpec((CONV_WIDTH, width), lambda b, c: (0, 0)),
                  pl.BlockSpec((1, width), lambda b, c: (0, 0)),
                  pl.BlockSpec((1, n_head, head, two_head), lambda b, c: (0, 0, 0, 0)),
                  pl.BlockSpec((1, n_head, two_head), lambda b, c: (0, 0, 0)),
                  pl.BlockSpec(lru_lambda.shape, lambda b, c: (0, 0))],
        out_specs=[tile, tile],
        out_shape=[jax.ShapeDtypeStruct((batch * seq, width), F32)] * 2,
        scratch_shapes=[pltpu.VMEM((1, width), F32),
                        pltpu.VMEM((n_head, head, two_head), BF16)],
        compiler_params=_params(("arbitrary", "arbitrary")),
        name="lru_fwd",
    )(z, z, z, conv_w, conv_b, w_gate, b_gate, lru_lambda)


def _lru_bwd_out_kernel(xc_ref, hf_ref, gate_ref, wg_ref, bg_ref, lam_ref, ng_ref,
                        yp_ref, x_ref, mod_ref, w_hbm, b_ref, fg_ref,
                        o_ref, hb_ref, yl_ref, y_ref, carry_ref, wgb_ref, w_ref, stage, sem, *, n_tiles,
                        tiles_per_seq):
    s = pl.program_id(0)
    j = jnp.maximum(n_tiles - 1 - s, 0)

    @pl.when(s == 0)
    def _():
        carry_ref[...] = jnp.zeros_like(carry_ref)
        yl_ref[...] = jnp.zeros_like(yl_ref)
        _cast_gate_weights(wg_ref, wgb_ref)
        _load_weight_bf16(w_hbm, w_ref, stage, sem)

    rate2, bg_half = _lru_rates(lam_ref[1:2, :], bg_ref)
    kp = yp_ref.shape[1]
    n_chunk = o_ref.shape[1] // N_LRU_HEADS

    def project_chunk(h):
        cols = slice(h * n_chunk, (h + 1) * n_chunk)
        y = jnp.dot(yp_ref[...], w_ref[:kp, cols], preferred_element_type=F32)
        y = y + jnp.dot(yl_ref[...], w_ref[kp:, cols], preferred_element_type=F32)
        y_ref[:, cols] = y + b_ref[:, cols]

    carry_in = jnp.where(j % tiles_per_seq == tiles_per_seq - 1, 0.0, carry_ref[...])
    carry_ref[...] = _lru_direction(xc_ref[...], wgb_ref, bg_half, rate2, carry_in, hb_ref, reverse=True,
                                    after_gates=project_chunk)

    gate = mod_ref[2, pl.ds(jnp.minimum(n_tiles - s, n_tiles - 1) // tiles_per_seq, 1), :]
    xn = x_ref[...] + gate * y_ref[...]
    r = lax.rsqrt(jnp.mean(xn * xn, axis=-1, keepdims=True) + EPS)
    o_ref[...] = (xn * r) * fg_ref[...]

    yl = hf_ref[...] + hb_ref[...]
    rl = lax.rsqrt(jnp.mean(yl * yl, axis=-1, keepdims=True) + EPS)
    yl_ref[...] = ((yl * rl) * ng_ref[...] * gate_ref[...].astype(F32)).astype(yl_ref.dtype)


def _lru_bwd_out_proj(xc, hf, sg, w_gate, b_gate, lru_lambda, out_norm_g, yp, x2, mod3, w_out, b_out,
                      final_g, *, batch, seq, tm):
    d = x2.shape[1]
    width = xc.shape[1]
    tiles_per_seq = seq // tm
    n_tiles = batch * tiles_per_seq
    _, n_head, head, two_head = w_gate.shape

    def scan_tile(s):
        return jnp.maximum(n_tiles - 1 - s, 0)

    def proj_tile(s):
        return jnp.minimum(n_tiles - s, n_tiles - 1)

    return pl.pallas_call(
        functools.partial(_lru_bwd_out_kernel, n_tiles=n_tiles, tiles_per_seq=tiles_per_seq),
        grid=(n_tiles + 1,),
        in_specs=[pl.BlockSpec((tm, width), lambda s: (scan_tile(s), 0)),
                  pl.BlockSpec((tm, width), lambda s: (scan_tile(s), 0)),
                  pl.BlockSpec((tm, width), lambda s: (scan_tile(s), 1)),
                  pl.BlockSpec((1, n_head, head, two_head), lambda s: (1, 0, 0, 0)),
                  pl.BlockSpec((1, n_head, two_head), lambda s: (1, 0, 0)),
                  pl.BlockSpec(lru_lambda.shape, lambda s: (0, 0)),
                  pl.BlockSpec((1, width), lambda s: (0, 0)),
                  pl.BlockSpec((tm, yp.shape[1]), lambda s: (proj_tile(s), 0)),
                  pl.BlockSpec((tm, d), lambda s: (proj_tile(s), 0)),
                  pl.BlockSpec(mod3.shape, lambda s: (0, 0, 0)),
                  pl.BlockSpec(memory_space=pl.ANY),
                  pl.BlockSpec((1, d), lambda s: (0, 0)),
                  pl.BlockSpec((1, d), lambda s: (0, 0))],
        out_specs=pl.BlockSpec((tm, d), lambda s: (proj_tile(s), 0)),
        out_shape=jax.ShapeDtypeStruct((batch * seq, d), F32),
        scratch_shapes=[pltpu.VMEM((tm, width), F32),
                        pltpu.VMEM((tm, width), BF16),
                        pltpu.VMEM((tm, d), F32),
                        pltpu.VMEM((1, width), F32),
                        pltpu.VMEM((n_head, head, two_head), BF16),
                        *_weight_scratch(w_out)],
        compiler_params=_params(("arbitrary",)),
        name="lru_bwd_out_proj",
    )(xc, hf, sg, w_gate, b_gate, lru_lambda, out_norm_g, yp, x2, mod3, w_out, b_out, final_g)


def kernel(x, c, norm_g, w_ada, b_ada, w_in, b_in, w_pool, b_pool, pool_scale, conv_w, conv_b, w_gate,
           b_gate, lru_lambda, out_norm_pool_g, out_norm_lru_g, w_out, b_out, final_norm_g):
    batch, seq, d = x.shape
    assert w_in.shape[0] == 1, "single-layer block only"
    xs = x.reshape(batch * seq, d)
    mod3 = _adaln_mod(c, w_ada, b_ada, tn=1024)
    z, sg = _in_proj(xs, mod3, norm_g, w_in[0], b_in, batch=batch, seq=seq, tm=512)
    yp = _pool_mixer(z, sg, w_pool[0], b_pool[0], pool_scale, out_norm_pool_g, batch=batch, seq=seq, ts=512)
    xc, hf = _lru_fwd(z, conv_w[0], conv_b, w_gate[0], b_gate[0], lru_lambda[0], batch=batch, seq=seq, tc=512)
    out = _lru_bwd_out_proj(xc, hf, sg, w_gate[0], b_gate[0], lru_lambda[0], out_norm_lru_g, yp, xs, mod3,
                            w_out[0], b_out, final_norm_g[None, :], batch=batch, seq=seq, tm=512)
    return out.reshape(batch, seq, d)
```

```python
import functools
import math

import jax
import jax.numpy as jnp
from jax import lax
from jax.experimental import pallas as pl
from jax.experimental.pallas import tpu as pltpu

EPS = 1e-6
LRU_C = 8.0
POOL_WINDOWS = (2, 4, 8, 16)
N_LRU_HEADS = 4
CONV_WIDTH = 4
HALO = 8
SUBLANES = 8
BF16_ROW_TILE = 16
V7X_VMEM_LIMIT_BYTES = 56 * 1024 * 1024
WEIGHT_STAGE_ROWS = 256
LOG2E = math.log2(math.e)
LN2 = math.log(2.0)

F32 = jnp.float32
BF16 = jnp.bfloat16


def _params(semantics):
    return pltpu.CompilerParams(dimension_semantics=semantics,
                                vmem_limit_bytes=V7X_VMEM_LIMIT_BYTES)


def _sigmoid(v):
    return 0.5 * jnp.tanh(0.5 * v) + 0.5


def _silu(v):
    return v * _sigmoid(v)


def _shift_rows(v, k):
    rows, width = v.shape
    v3 = v.reshape(rows // SUBLANES, SUBLANES, width)
    if k % SUBLANES == 0:
        g = (k // SUBLANES) % v3.shape[0]
        return jnp.concatenate([v3[-g:], v3[:-g]], axis=0).reshape(rows, width) if g else v
    sub = lax.broadcasted_iota(jnp.int32, v3.shape, 1)
    if k > 0:
        rot = pltpu.roll(v3, k, 1)
        other = jnp.concatenate([rot[-1:], rot[:-1]], axis=0)
        out = jnp.where(sub >= k, rot, other)
    else:
        rot = pltpu.roll(v3, SUBLANES + k, 1)
        other = jnp.concatenate([rot[1:], rot[:1]], axis=0)
        out = jnp.where(sub < SUBLANES + k, rot, other)
    return out.reshape(rows, width)


def _load_weight_bf16(w_hbm, w_vmem, stage, sem):
    rows = stage.shape[1]
    n_chunks = w_hbm.shape[0] // rows

    def copy(c):
        return pltpu.make_async_copy(w_hbm.at[pl.ds(c * rows, rows), :], stage.at[c % 2], sem.at[c % 2])

    copy(0).start()
    for c in range(n_chunks):
        if c + 1 < n_chunks:
            copy(c + 1).start()
        copy(c).wait()
        w_vmem[c * rows:(c + 1) * rows, :] = stage[c % 2].astype(BF16)


def _weight_scratch(w):
    k, n = w.shape
    assert k % WEIGHT_STAGE_ROWS == 0
    return [pltpu.VMEM((k, n), BF16), pltpu.VMEM((2, WEIGHT_STAGE_ROWS, n), F32), pltpu.SemaphoreType.DMA((2,))]


ADALN_STAGE_SLOTS = 3


def _adaln_kernel(c_ref, w_hbm, b_ref, o_ref, stage, sem):
    batch, d = c_ref.shape
    slots, rows, n = stage.shape
    n_chunks = d // rows
    ca = _silu(c_ref[...]).astype(BF16)
    ca = jnp.concatenate([ca, jnp.zeros((-batch % BF16_ROW_TILE, d), BF16)], axis=0)

    def copy(k):
        return pltpu.make_async_copy(w_hbm.at[0, pl.ds(k * rows, rows), :], stage.at[k % slots], sem.at[k % slots])

    for k in range(min(slots - 1, n_chunks)):
        copy(k).start()
    acc = jnp.zeros((ca.shape[0], n), F32)
    for k in range(n_chunks):
        if k + slots - 1 < n_chunks:
            copy(k + slots - 1).start()
        copy(k).wait()
        acc = acc + jnp.dot(ca[:, k * rows:(k + 1) * rows], stage[k % slots].astype(BF16),
                            preferred_element_type=F32)
    res = acc[:batch] + b_ref[...]
    for part in range(o_ref.shape[0]):
        o_ref[part] = res[:, part * d:(part + 1) * d]


def _adaln_mod(c, w_ada, b_ada):
    batch, d = c.shape
    n = w_ada.shape[2]
    assert d % WEIGHT_STAGE_ROWS == 0 and n == 3 * d
    return pl.pallas_call(
        _adaln_kernel,
        in_specs=[pl.BlockSpec(memory_space=pltpu.VMEM),
                  pl.BlockSpec(memory_space=pl.ANY),
                  pl.BlockSpec(memory_space=pltpu.VMEM)],
        out_specs=pl.BlockSpec(memory_space=pltpu.VMEM),
        out_shape=jax.ShapeDtypeStruct((3, batch, d), F32),
        scratch_shapes=[pltpu.VMEM((ADALN_STAGE_SLOTS, WEIGHT_STAGE_ROWS, n), F32),
                        pltpu.SemaphoreType.DMA((ADALN_STAGE_SLOTS,))],
        compiler_params=pltpu.CompilerParams(vmem_limit_bytes=V7X_VMEM_LIMIT_BYTES),
        name="adaln_mod",
    )(c, w_ada, b_ada)


def _in_proj_kernel(x_ref, mod_ref, g_ref, w_hbm, b_ref, u_ref, sg_ref, w_ref, stage, sem, *, n_chunk):
    @pl.when((pl.program_id(0) == 0) & (pl.program_id(1) == 0))
    def _():
        _load_weight_bf16(w_hbm, w_ref, stage, sem)

    x = x_ref[...]
    r = lax.rsqrt(jnp.mean(x * x, axis=-1, keepdims=True) + EPS)
    b = pl.program_id(0)
    shift = mod_ref[0, pl.ds(b, 1), :]
    scale = mod_ref[1, pl.ds(b, 1), :]
    h = (x * r) * (g_ref[...] * (1.0 + scale)) + shift
    hb = h.astype(BF16)
    n = w_ref.shape[1]
    n_mix = u_ref.shape[1]
    for j in range(n // n_chunk):
        sl = slice(j * n_chunk, (j + 1) * n_chunk)
        zj = jnp.dot(hb, w_ref[:, sl], preferred_element_type=F32) + b_ref[:, sl]
        if j * n_chunk < n_mix:
            u_ref[:, sl] = zj
        else:
            sg_ref[:, j * n_chunk - n_mix:(j + 1) * n_chunk - n_mix] = _silu(zj).astype(sg_ref.dtype)


def _in_proj(x2, mod3, norm_g, w_in, b_in, *, batch, seq, tm):
    d = x2.shape[1]
    n = w_in.shape[1]
    nt = seq // tm
    rows = batch * seq
    return pl.pallas_call(
        functools.partial(_in_proj_kernel, n_chunk=1024),
        grid=(batch, nt),
        in_specs=[pl.BlockSpec((tm, d), lambda b, i: (b * nt + i, 0)),
                  pl.BlockSpec(mod3.shape, lambda b, i: (0, 0, 0)),
                  pl.BlockSpec((1, d), lambda b, i: (0, 0)),
                  pl.BlockSpec(memory_space=pl.ANY),
                  pl.BlockSpec((1, n), lambda b, i: (0, 0))],
        out_specs=[pl.BlockSpec((tm, n // 2), lambda b, i: (b * nt + i, 0)),
                   pl.BlockSpec((tm, n // 2), lambda b, i: (b * nt + i, 0))],
        out_shape=[jax.ShapeDtypeStruct((rows, n // 2), F32), jax.ShapeDtypeStruct((rows, n // 2), BF16)],
        scratch_shapes=_weight_scratch(w_in),
        compiler_params=_params(("arbitrary", "arbitrary")),
        name="in_proj",
    )(x2, mod3, norm_g, w_in, b_in)


def _halo_specs(width, col_block, *, seq, ts, total_rows, tile_of):
    per_seq = seq // HALO
    per_tile = ts // HALO
    last = total_rows // HALO - 1

    def prev_map(*idx):
        b, i = idx[0], tile_of(*idx)
        return (jnp.maximum(b * per_seq + i * per_tile - 1, 0), col_block)

    def next_map(*idx):
        b, i = idx[0], tile_of(*idx)
        return (jnp.minimum(b * per_seq + (i + 1) * per_tile, last), col_block)

    return (pl.BlockSpec((HALO, width), prev_map), pl.BlockSpec((HALO, width), next_map))


def _with_halo(prev_ref, cur_ref, next_ref, tile, n_tiles):
    prev = jnp.where(tile > 0, prev_ref[...], 0.0)
    nxt = jnp.where(tile < n_tiles - 1, next_ref[...], 0.0)
    return jnp.concatenate([prev, cur_ref[...], nxt], axis=0)


def _pool_kernel(prev_ref, u_ref, next_ref, gate_ref, w_ref, b_ref, scale_ref, ng_ref, o_ref,
                 *, seq, ts, n_tiles):
    i = pl.program_id(1)
    ue = _with_halo(prev_ref, u_ref, next_ref, i, n_tiles)
    width = ue.shape[1]
    grp = width // len(POOL_WINDOWS)
    t = i * ts + lax.broadcasted_iota(jnp.int32, (ts, 1), 0)
    mixed = []
    for g, w in enumerate(POOL_WINDOWS):
        cols = slice(g * grp, (g + 1) * grp)
        e = ue[:, cols]
        lo = w // 2
        hi = w - lo - 1
        s, span = e, 1
        while span < w:
            s = s + _shift_rows(s, span)
            span *= 2
        total = (_shift_rows(s, -hi) if hi else s)[HALO:HALO + ts]
        cnt = (jnp.minimum(t + hi, seq - 1) - jnp.maximum(t - lo, 0) + 1).astype(F32)
        pooled = total / cnt - e[HALO:HALO + ts]
        m = jnp.dot(pooled.astype(BF16), w_ref[g].astype(BF16), preferred_element_type=F32)
        mixed.append((m + b_ref[g:g + 1, :]) * scale_ref[:, cols])
    y = jnp.concatenate(mixed, axis=-1)
    r = lax.rsqrt(jnp.mean(y * y, axis=-1, keepdims=True) + EPS)
    o_ref[...] = ((y * r) * ng_ref[...] * gate_ref[...].astype(F32)).astype(o_ref.dtype)


def _pool_mixer(z, sg, w_pool, b_pool, pool_scale, out_norm_g, *, batch, seq, ts):
    width = pool_scale.shape[1]
    n_tiles = seq // ts
    prev_spec, next_spec = _halo_specs(width, 0, seq=seq, ts=ts, total_rows=batch * seq,
                                       tile_of=lambda b, i: i)
    n_grp, grp, _ = w_pool.shape
    return pl.pallas_call(
        functools.partial(_pool_kernel, seq=seq, ts=ts, n_tiles=n_tiles),
        grid=(batch, n_tiles),
        in_specs=[prev_spec,
                  pl.BlockSpec((ts, width), lambda b, i: (b * n_tiles + i, 0)),
                  next_spec,
                  pl.BlockSpec((ts, width), lambda b, i: (b * n_tiles + i, 0)),
                  pl.BlockSpec((n_grp, grp, grp), lambda b, i: (0, 0, 0)),
                  pl.BlockSpec((n_grp, grp), lambda b, i: (0, 0)),
                  pl.BlockSpec((1, width), lambda b, i: (0, 0)),
                  pl.BlockSpec((1, width), lambda b, i: (0, 0))],
        out_specs=pl.BlockSpec((ts, width), lambda b, i: (b * n_tiles + i, 0)),
        out_shape=jax.ShapeDtypeStruct((batch * seq, width), BF16),
        compiler_params=_params(("arbitrary", "arbitrary")),
        name="pool_mixer",
    )(z, z, z, sg, w_pool, b_pool, pool_scale, out_norm_g)


def _softplus(v):
    return jnp.maximum(v, 0.0) + jnp.log1p(jnp.exp(-jnp.abs(v)))


def _lru_direction(xc_half, wg_ref, bg_half, rate2, carry, h_ref, *, reverse, after_gates=None):
    tc, width = xc_half.shape
    head = width // N_LRU_HEADS
    groups = tc // SUBLANES
    entry = SUBLANES - 1 if reverse else 0
    sub = lax.broadcasted_iota(jnp.int32, (groups, SUBLANES, head), 1)
    is_entry = sub == entry

    def back(v, d, axis):
        return pltpu.roll(v, SUBLANES - d if reverse else d, axis)

    carries = []
    for h in range(N_LRU_HEADS):
        cols = slice(h * head, (h + 1) * head)
        xh = xc_half[:, cols]
        g = jnp.dot(xh.astype(BF16), wg_ref[h], preferred_element_type=F32) + bg_half[h:h + 1, :]
        if after_gates is not None:
            after_gates(h)
        tr = jnp.tanh(g[:, :head])
        ti = jnp.tanh(g[:, head:])
        log2a = tr * rate2[:, cols] + rate2[:, cols]
        a = jnp.exp2(log2a)
        one_minus_a2 = (a * a + 1.0) * jnp.tanh(log2a * (-LN2))
        mult = jnp.where(one_minus_a2 > 0.0, one_minus_a2 * lax.rsqrt(one_minus_a2), 0.0)
        b = mult * (ti + 1.0) * xh

        a3 = a.reshape(groups, SUBLANES, head)
        b3 = b.reshape(groups, SUBLANES, head)
        a_in = jnp.where(is_entry, a3, 0.0)
        a1 = jnp.where(is_entry, 0.0, a3)
        a2 = a1 * back(a1, 1, 1)
        a4 = a2 * back(a2, 2, 1)
        state = jnp.broadcast_to(carry[:, cols], (SUBLANES, head))
        order = range(groups - 1, -1, -1) if reverse else range(groups)
        for gi in order:
            s = b3[gi] + a_in[gi] * back(state, 1, 0)
            s = s + a1[gi] * back(s, 1, 0)
            s = s + a2[gi] * back(s, 2, 0)
            state = s + a4[gi] * back(s, 4, 0)
            h_ref[gi * SUBLANES:(gi + 1) * SUBLANES, cols] = state
        exit_row = SUBLANES - 1 - entry
        carries.append(state[exit_row:exit_row + 1, :])
    return jnp.concatenate(carries, axis=1)


def _lru_rates(lam_row, bg_ref):
    rate2 = (-0.5 * LRU_C * LOG2E) * _softplus(-lam_row)
    return rate2, 0.5 * bg_ref[0]


def _cast_gate_weights(wg_ref, wgb_ref):
    for h in range(N_LRU_HEADS):
        wgb_ref[h] = wg_ref[0, h].astype(BF16)


def _lru_fwd_kernel(prev_ref, u_ref, next_ref, cw_ref, cb_ref, wg_ref, bg_ref, lam_ref,
                    xc_ref, hf_ref, carry_ref, wgb_ref, *, tc, n_chunks):
    c = pl.program_id(1)

    @pl.when((pl.program_id(0) == 0) & (c == 0))
    def _():
        _cast_gate_weights(wg_ref, wgb_ref)

    @pl.when(c == 0)
    def _():
        carry_ref[...] = jnp.zeros_like(carry_ref)

    rate2, bg_half = _lru_rates(lam_ref[0:1, :], bg_ref)
    ue = _with_halo(prev_ref, u_ref, next_ref, c, n_chunks)
    left = CONV_WIDTH // 2
    xc_half = 0.5 * cb_ref[...]
    for k in range(CONV_WIDTH):
        tap = ue if k == left else _shift_rows(ue, left - k)
        xc_half = xc_half + tap[HALO:HALO + tc] * (0.5 * cw_ref[k:k + 1, :])
    xc_ref[...] = xc_half
    carry_ref[...] = _lru_direction(xc_half, wgb_ref, bg_half, rate2, carry_ref[...], hf_ref, reverse=False)


def _lru_fwd(z, conv_w, conv_b, w_gate, b_gate, lru_lambda, *, batch, seq, tc):
    width = conv_b.shape[1]
    n_chunks = seq // tc
    prev_spec, next_spec = _halo_specs(width, 1, seq=seq, ts=tc, total_rows=batch * seq,
                                       tile_of=lambda b, c: c)
    _, n_head, head, two_head = w_gate.shape
    tile = pl.BlockSpec((tc, width), lambda b, c: (b * n_chunks + c, 0))
    return pl.pallas_call(
        functools.partial(_lru_fwd_kernel, tc=tc, n_chunks=n_chunks),
        grid=(batch, n_chunks),
        in_specs=[prev_spec,
                  pl.BlockSpec((tc, width), lambda b, c: (b * n_chunks + c, 1)),
                  next_spec,
                  pl.BlockSpec((CONV_WIDTH, width), lambda b, c: (0, 0)),
                  pl.BlockSpec((1, width), lambda b, c: (0, 0)),
                  pl.BlockSpec((1, n_head, head, two_head), lambda b, c: (0, 0, 0, 0)),
                  pl.BlockSpec((1, n_head, two_head), lambda b, c: (0, 0, 0)),
                  pl.BlockSpec(lru_lambda.shape, lambda b, c: (0, 0))],
        out_specs=[tile, tile],
        out_shape=[jax.ShapeDtypeStruct((batch * seq, width), F32)] * 2,
        scratch_shapes=[pltpu.VMEM((1, width), F32),
                        pltpu.VMEM((n_head, head, two_head), BF16)],
        compiler_params=_params(("arbitrary", "arbitrary")),
        name="lru_fwd",
    )(z, z, z, conv_w, conv_b, w_gate, b_gate, lru_lambda)


def _lru_bwd_out_kernel(xc_ref, hf_ref, gate_ref, wg_ref, bg_ref, lam_ref, ng_ref,
                        yp_ref, x_ref, mod_ref, w_hbm, b_ref, fg_ref,
                        o_ref, hb_ref, yl_ref, y_ref, carry_ref, wgb_ref, w_ref, stage, sem, *, n_tiles,
                        tiles_per_seq):
    s = pl.program_id(0)
    j = jnp.maximum(n_tiles - 1 - s, 0)

    @pl.when(s == 0)
    def _():
        carry_ref[...] = jnp.zeros_like(carry_ref)
        yl_ref[...] = jnp.zeros_like(yl_ref)
        _cast_gate_weights(wg_ref, wgb_ref)
        _load_weight_bf16(w_hbm, w_ref, stage, sem)

    rate2, bg_half = _lru_rates(lam_ref[1:2, :], bg_ref)
    kp = yp_ref.shape[1]
    n_chunk = o_ref.shape[1] // N_LRU_HEADS

    def project_chunk(h):
        cols = slice(h * n_chunk, (h + 1) * n_chunk)
        y = jnp.dot(yp_ref[...], w_ref[:kp, cols], preferred_element_type=F32)
        y = y + jnp.dot(yl_ref[...], w_ref[kp:, cols], preferred_element_type=F32)
        y_ref[:, cols] = y + b_ref[:, cols]

    carry_in = jnp.where(j % tiles_per_seq == tiles_per_seq - 1, 0.0, carry_ref[...])
    carry_ref[...] = _lru_direction(xc_ref[...], wgb_ref, bg_half, rate2, carry_in, hb_ref, reverse=True,
                                    after_gates=project_chunk)

    gate = mod_ref[2, pl.ds(jnp.minimum(n_tiles - s, n_tiles - 1) // tiles_per_seq, 1), :]
    xn = x_ref[...] + gate * y_ref[...]
    r = lax.rsqrt(jnp.mean(xn * xn, axis=-1, keepdims=True) + EPS)
    o_ref[...] = (xn * r) * fg_ref[...]

    yl = hf_ref[...] + hb_ref[...]
    rl = lax.rsqrt(jnp.mean(yl * yl, axis=-1, keepdims=True) + EPS)
    yl_ref[...] = ((yl * rl) * ng_ref[...] * gate_ref[...].astype(F32)).astype(yl_ref.dtype)


def _lru_bwd_out_proj(xc, hf, sg, w_gate, b_gate, lru_lambda, out_norm_g, yp, x2, mod3, w_out, b_out,
                      final_g, *, batch, seq, tm):
    d = x2.shape[1]
    width = xc.shape[1]
    tiles_per_seq = seq // tm
    n_tiles = batch * tiles_per_seq
    _, n_head, head, two_head = w_gate.shape

    def scan_tile(s):
        return jnp.maximum(n_tiles - 1 - s, 0)

    def proj_tile(s):
        return jnp.minimum(n_tiles - s, n_tiles - 1)

    return pl.pallas_call(
        functools.partial(_lru_bwd_out_kernel, n_tiles=n_tiles, tiles_per_seq=tiles_per_seq),
        grid=(n_tiles + 1,),
        in_specs=[pl.BlockSpec((tm, width), lambda s: (scan_tile(s), 0)),
                  pl.BlockSpec((tm, width), lambda s: (scan_tile(s), 0)),
                  pl.BlockSpec((tm, width), lambda s: (scan_tile(s), 1)),
                  pl.BlockSpec((1, n_head, head, two_head), lambda s: (1, 0, 0, 0)),
                  pl.BlockSpec((1, n_head, two_head), lambda s: (1, 0, 0)),
                  pl.BlockSpec(lru_lambda.shape, lambda s: (0, 0)),
                  pl.BlockSpec((1, width), lambda s: (0, 0)),
                  pl.BlockSpec((tm, yp.shape[1]), lambda s: (proj_tile(s), 0)),
                  pl.BlockSpec((tm, d), lambda s: (proj_tile(s), 0)),
                  pl.BlockSpec(mod3.shape, lambda s: (0, 0, 0)),
                  pl.BlockSpec(memory_space=pl.ANY),
                  pl.BlockSpec((1, d), lambda s: (0, 0)),
                  pl.BlockSpec((1, d), lambda s: (0, 0))],
        out_specs=pl.BlockSpec((tm, d), lambda s: (proj_tile(s), 0)),
        out_shape=jax.ShapeDtypeStruct((batch * seq, d), F32),
        scratch_shapes=[pltpu.VMEM((tm, width), F32),
                        pltpu.VMEM((tm, width), BF16),
                        pltpu.VMEM((tm, d), F32),
                        pltpu.VMEM((1, width), F32),
                        pltpu.VMEM((n_head, head, two_head), BF16),
                        *_weight_scratch(w_out)],
        compiler_params=_params(("arbitrary",)),
        name="lru_bwd_out_proj",
    )(xc, hf, sg, w_gate, b_gate, lru_lambda, out_norm_g, yp, x2, mod3, w_out, b_out, final_g)


def kernel(x, c, norm_g, w_ada, b_ada, w_in, b_in, w_pool, b_pool, pool_scale, conv_w, conv_b, w_gate,
           b_gate, lru_lambda, out_norm_pool_g, out_norm_lru_g, w_out, b_out, final_norm_g):
    batch, seq, d = x.shape
    assert w_in.shape[0] == 1, "single-layer block only"
    xs = x.reshape(batch * seq, d)
    mod3 = _adaln_mod(c, w_ada, b_ada)
    z, sg = _in_proj(xs, mod3, norm_g, w_in[0], b_in, batch=batch, seq=seq, tm=512)
    yp = _pool_mixer(z, sg, w_pool[0], b_pool[0], pool_scale, out_norm_pool_g, batch=batch, seq=seq, ts=512)
    xc, hf = _lru_fwd(z, conv_w[0], conv_b, w_gate[0], b_gate[0], lru_lambda[0], batch=batch, seq=seq, tc=512)
    out = _lru_bwd_out_proj(xc, hf, sg, w_gate[0], b_gate[0], lru_lambda[0], out_norm_lru_g, yp, xs, mod3,
                            w_out[0], b_out, final_norm_g[None, :], batch=batch, seq=seq, tm=512)
    return out.reshape(batch, seq, d)
```

```python
import functools
import math

import jax
import jax.numpy as jnp
from jax import lax
from jax.experimental import pallas as pl
from jax.experimental.pallas import tpu as pltpu

EPS = 1e-6
LRU_C = 8.0
POOL_WINDOWS = (2, 4, 8, 16)
N_LRU_HEADS = 4
CONV_WIDTH = 4
HALO = 8
SUBLANES = 8
BF16_ROW_TILE = 16
V7X_VMEM_LIMIT_BYTES = 56 * 1024 * 1024
WEIGHT_STAGE_ROWS = 256
LOG2E = math.log2(math.e)
LN2 = math.log(2.0)

F32 = jnp.float32
BF16 = jnp.bfloat16


def _params(semantics):
    return pltpu.CompilerParams(dimension_semantics=semantics,
                                vmem_limit_bytes=V7X_VMEM_LIMIT_BYTES)


def _sigmoid(v):
    return 0.5 * jnp.tanh(0.5 * v) + 0.5


def _silu(v):
    return v * _sigmoid(v)


def _shift_rows(v, k):
    rows, width = v.shape
    v3 = v.reshape(rows // SUBLANES, SUBLANES, width)
    if k % SUBLANES == 0:
        g = (k // SUBLANES) % v3.shape[0]
        return jnp.concatenate([v3[-g:], v3[:-g]], axis=0).reshape(rows, width) if g else v
    sub = lax.broadcasted_iota(jnp.int32, v3.shape, 1)
    if k > 0:
        rot = pltpu.roll(v3, k, 1)
        other = jnp.concatenate([rot[-1:], rot[:-1]], axis=0)
        out = jnp.where(sub >= k, rot, other)
    else:
        rot = pltpu.roll(v3, SUBLANES + k, 1)
        other = jnp.concatenate([rot[1:], rot[:1]], axis=0)
        out = jnp.where(sub < SUBLANES + k, rot, other)
    return out.reshape(rows, width)


def _load_weight_bf16(w_hbm, w_vmem, stage, sem):
    rows = stage.shape[1]
    n_chunks = w_hbm.shape[0] // rows

    def copy(c):
        return pltpu.make_async_copy(w_hbm.at[pl.ds(c * rows, rows), :], stage.at[c % 2], sem.at[c % 2])

    copy(0).start()
    for c in range(n_chunks):
        if c + 1 < n_chunks:
            copy(c + 1).start()
        copy(c).wait()
        w_vmem[c * rows:(c + 1) * rows, :] = stage[c % 2].astype(BF16)


def _weight_scratch(w):
    k, n = w.shape
    assert k % WEIGHT_STAGE_ROWS == 0
    return [pltpu.VMEM((k, n), BF16), pltpu.VMEM((2, WEIGHT_STAGE_ROWS, n), F32), pltpu.SemaphoreType.DMA((2,))]


ADALN_STAGE_SLOTS = 3


def _adaln_kernel(c_ref, w_hbm, b_ref, o_ref, stage, sem):
    batch, d = c_ref.shape
    slots, rows, n = stage.shape
    n_chunks = d // rows
    ca = _silu(c_ref[...]).astype(BF16)
    ca = jnp.concatenate([ca, jnp.zeros((-batch % BF16_ROW_TILE, d), BF16)], axis=0)

    def copy(k):
        return pltpu.make_async_copy(w_hbm.at[0, pl.ds(k * rows, rows), :], stage.at[k % slots], sem.at[k % slots])

    for k in range(min(slots - 1, n_chunks)):
        copy(k).start()
    acc = jnp.zeros((ca.shape[0], n), F32)
    for k in range(n_chunks):
        if k + slots - 1 < n_chunks:
            copy(k + slots - 1).start()
        copy(k).wait()
        acc = acc + jnp.dot(ca[:, k * rows:(k + 1) * rows], stage[k % slots].astype(BF16),
                            preferred_element_type=F32)
    res = acc[:batch] + b_ref[...]
    for part in range(o_ref.shape[0]):
        o_ref[part] = res[:, part * d:(part + 1) * d]


def _adaln_mod(c, w_ada, b_ada):
    batch, d = c.shape
    n = w_ada.shape[2]
    assert d % WEIGHT_STAGE_ROWS == 0 and n == 3 * d
    return pl.pallas_call(
        _adaln_kernel,
        in_specs=[pl.BlockSpec(memory_space=pltpu.VMEM),
                  pl.BlockSpec(memory_space=pl.ANY),
                  pl.BlockSpec(memory_space=pltpu.VMEM)],
        out_specs=pl.BlockSpec(memory_space=pltpu.VMEM),
        out_shape=jax.ShapeDtypeStruct((3, batch, d), F32),
        scratch_shapes=[pltpu.VMEM((ADALN_STAGE_SLOTS, WEIGHT_STAGE_ROWS, n), F32),
                        pltpu.SemaphoreType.DMA((ADALN_STAGE_SLOTS,))],
        compiler_params=pltpu.CompilerParams(vmem_limit_bytes=V7X_VMEM_LIMIT_BYTES),
        name="adaln_mod",
    )(c, w_ada, b_ada)


def _halo_specs(width, col_block, *, seq, ts, total_rows, tile_of):
    per_seq = seq // HALO
    per_tile = ts // HALO
    last = total_rows // HALO - 1

    def prev_map(*idx):
        b, i = idx[0], tile_of(*idx)
        return (jnp.maximum(b * per_seq + i * per_tile - 1, 0), col_block)

    def next_map(*idx):
        b, i = idx[0], tile_of(*idx)
        return (jnp.minimum(b * per_seq + (i + 1) * per_tile, last), col_block)

    return (pl.BlockSpec((HALO, width), prev_map), pl.BlockSpec((HALO, width), next_map))


def _with_halo(prev_ref, cur_ref, next_ref, tile, n_tiles):
    prev = jnp.where(tile > 0, prev_ref[...], 0.0)
    nxt = jnp.where(tile < n_tiles - 1, next_ref[...], 0.0)
    return jnp.concatenate([prev, cur_ref[...], nxt], axis=0)


def _pool_kernel(prev_ref, u_ref, next_ref, gate_ref, w_ref, b_ref, scale_ref, ng_ref, o_ref,
                 *, seq, ts, n_tiles):
    i = pl.program_id(1)
    ue = _with_halo(prev_ref, u_ref, next_ref, i, n_tiles)
    width = ue.shape[1]
    grp = width // len(POOL_WINDOWS)
    t = i * ts + lax.broadcasted_iota(jnp.int32, (ts, 1), 0)
    mixed = []
    for g, w in enumerate(POOL_WINDOWS):
        cols = slice(g * grp, (g + 1) * grp)
        e = ue[:, cols]
        lo = w // 2
        hi = w - lo - 1
        s, span = e, 1
        while span < w:
            s = s + _shift_rows(s, span)
            span *= 2
        total = (_shift_rows(s, -hi) if hi else s)[HALO:HALO + ts]
        cnt = (jnp.minimum(t + hi, seq - 1) - jnp.maximum(t - lo, 0) + 1).astype(F32)
        pooled = total / cnt - e[HALO:HALO + ts]
        m = jnp.dot(pooled.astype(BF16), w_ref[g].astype(BF16), preferred_element_type=F32)
        mixed.append((m + b_ref[g:g + 1, :]) * scale_ref[:, cols])
    y = jnp.concatenate(mixed, axis=-1)
    r = lax.rsqrt(jnp.mean(y * y, axis=-1, keepdims=True) + EPS)
    o_ref[...] = ((y * r) * ng_ref[...] * gate_ref[...].astype(F32)).astype(o_ref.dtype)


def _pool_mixer(z, sg, w_pool, b_pool, pool_scale, out_norm_g, *, batch, seq, ts):
    width = pool_scale.shape[1]
    n_tiles = seq // ts
    prev_spec, next_spec = _halo_specs(width, 0, seq=seq, ts=ts, total_rows=batch * seq,
                                       tile_of=lambda b, i: i)
    n_grp, grp, _ = w_pool.shape
    return pl.pallas_call(
        functools.partial(_pool_kernel, seq=seq, ts=ts, n_tiles=n_tiles),
        grid=(batch, n_tiles),
        in_specs=[prev_spec,
                  pl.BlockSpec((ts, width), lambda b, i: (b * n_tiles + i, 0)),
                  next_spec,
                  pl.BlockSpec((ts, width), lambda b, i: (b * n_tiles + i, 0)),
                  pl.BlockSpec((n_grp, grp, grp), lambda b, i: (0, 0, 0)),
                  pl.BlockSpec((n_grp, grp), lambda b, i: (0, 0)),
                  pl.BlockSpec((1, width), lambda b, i: (0, 0)),
                  pl.BlockSpec((1, width), lambda b, i: (0, 0))],
        out_specs=pl.BlockSpec((ts, width), lambda b, i: (b * n_tiles + i, 0)),
        out_shape=jax.ShapeDtypeStruct((batch * seq, width), BF16),
        compiler_params=_params(("arbitrary", "arbitrary")),
        name="pool_mixer",
    )(z, z, z, sg, w_pool, b_pool, pool_scale, out_norm_g)


def _softplus(v):
    return jnp.maximum(v, 0.0) + jnp.log1p(jnp.exp(-jnp.abs(v)))


def _lru_direction(xc_half, wg_ref, bg_half, rate2, carry, h_ref, *, reverse, after_gates=None):
    tc, width = xc_half.shape
    head = width // N_LRU_HEADS
    groups = tc // SUBLANES
    entry = SUBLANES - 1 if reverse else 0
    sub = lax.broadcasted_iota(jnp.int32, (groups, SUBLANES, head), 1)
    is_entry = sub == entry

    def back(v, d, axis):
        return pltpu.roll(v, SUBLANES - d if reverse else d, axis)

    carries = []
    for h in range(N_LRU_HEADS):
        cols = slice(h * head, (h + 1) * head)
        xh = xc_half[:, cols]
        g = jnp.dot(xh.astype(BF16), wg_ref[h], preferred_element_type=F32) + bg_half[h:h + 1, :]
        if after_gates is not None:
            after_gates(h)
        tr = jnp.tanh(g[:, :head])
        ti = jnp.tanh(g[:, head:])
        log2a = tr * rate2[:, cols] + rate2[:, cols]
        a = jnp.exp2(log2a)
        one_minus_a2 = (a * a + 1.0) * jnp.tanh(log2a * (-LN2))
        mult = jnp.where(one_minus_a2 > 0.0, one_minus_a2 * lax.rsqrt(one_minus_a2), 0.0)
        b = mult * (ti + 1.0) * xh

        a3 = a.reshape(groups, SUBLANES, head)
        b3 = b.reshape(groups, SUBLANES, head)
        a_in = jnp.where(is_entry, a3, 0.0)
        a1 = jnp.where(is_entry, 0.0, a3)
        a2 = a1 * back(a1, 1, 1)
        a4 = a2 * back(a2, 2, 1)
        state = jnp.broadcast_to(carry[:, cols], (SUBLANES, head))
        order = range(groups - 1, -1, -1) if reverse else range(groups)
        for gi in order:
            s = b3[gi] + a_in[gi] * back(state, 1, 0)
            s = s + a1[gi] * back(s, 1, 0)
            s = s + a2[gi] * back(s, 2, 0)
            state = s + a4[gi] * back(s, 4, 0)
            h_ref[gi * SUBLANES:(gi + 1) * SUBLANES, cols] = state
        exit_row = SUBLANES - 1 - entry
        carries.append(state[exit_row:exit_row + 1, :])
    return jnp.concatenate(carries, axis=1)


def _lru_rates(lam_row, bg_ref):
    rate2 = (-0.5 * LRU_C * LOG2E) * _softplus(-lam_row)
    return rate2, 0.5 * bg_ref[0]


def _cast_gate_weights(wg_ref, wgb_ref):
    for h in range(N_LRU_HEADS):
        wgb_ref[h] = wg_ref[0, h].astype(BF16)


def _in_proj_kernel(x_ref, mod_ref, g_ref, w_hbm, b_ref, cw_ref, cb_ref, wg_ref, bg_ref, lam_ref,
                    up_ref, sg_ref, xc_ref, hf_ref,
                    w_ref, stage, sem, wgb_ref, zprev_ref, ztail_ref, carry_ref, *, n_tiles, tiles_per_seq):
    i = pl.program_id(0)
    tm, width = zprev_ref.shape

    @pl.when(i == 0)
    def _():
        _load_weight_bf16(w_hbm, w_ref, stage, sem)
        _cast_gate_weights(wg_ref, wgb_ref)
        zprev_ref[...] = jnp.zeros_like(zprev_ref)
        ztail_ref[...] = jnp.zeros_like(ztail_ref)
        carry_ref[...] = jnp.zeros_like(carry_ref)

    bsel = jnp.minimum(i, n_tiles - 1) // tiles_per_seq
    x = x_ref[...]
    r = lax.rsqrt(jnp.mean(x * x, axis=-1, keepdims=True) + EPS)
    shift = mod_ref[0, pl.ds(bsel, 1), :]
    scale = mod_ref[1, pl.ds(bsel, 1), :]
    hb = ((x * r) * (g_ref[...] * (1.0 + scale)) + shift).astype(BF16)

    def project(sl):
        return jnp.dot(hb, w_ref[:, sl], preferred_element_type=F32) + b_ref[:, sl]

    u = project(slice(width, 2 * width))
    up_ref[...] = project(slice(0, width))

    mt = (i + tiles_per_seq - 1) % tiles_per_seq
    prev = jnp.where(mt > 0, ztail_ref[...], 0.0)
    nxt = jnp.where(mt < tiles_per_seq - 1, u[0:HALO], 0.0)
    ue = jnp.concatenate([prev, zprev_ref[...], nxt], axis=0)
    left = CONV_WIDTH // 2
    xc_half = 0.5 * cb_ref[...]
    for k in range(CONV_WIDTH):
        tap = ue if k == left else _shift_rows(ue, left - k)
        xc_half = xc_half + tap[HALO:HALO + tm] * (0.5 * cw_ref[k:k + 1, :])
    xc_ref[...] = xc_half

    quarter = 2 * width // N_LRU_HEADS

    def gate_columns(h):
        zg = project(slice(2 * width + h * quarter, 2 * width + (h + 1) * quarter))
        sg_ref[:, h * quarter:(h + 1) * quarter] = _silu(zg).astype(sg_ref.dtype)

    rate2, bg_half = _lru_rates(lam_ref[0:1, :], bg_ref)
    carry_in = jnp.where(mt == 0, 0.0, carry_ref[...])
    carry_ref[...] = _lru_direction(xc_half, wgb_ref, bg_half, rate2, carry_in, hf_ref, reverse=False,
                                    after_gates=gate_columns)
    ztail_ref[...] = zprev_ref[tm - HALO:tm, :]
    zprev_ref[...] = u


def _in_proj(x2, mod3, norm_g, w_in, b_in, conv_w, conv_b, w_gate, b_gate, lru_lambda, *, batch, seq, tm):
    d = x2.shape[1]
    n = w_in.shape[1]
    width = n // 4
    tiles_per_seq = seq // tm
    n_tiles = batch * tiles_per_seq
    _, n_head, head, two_head = w_gate.shape
    rows = batch * seq

    def proj_tile(i):
        return jnp.minimum(i, n_tiles - 1)

    def lru_tile(i):
        return jnp.maximum(i - 1, 0)

    lagged = pl.BlockSpec((tm, width), lambda i: (lru_tile(i), 0))
    return pl.pallas_call(
        functools.partial(_in_proj_kernel, n_tiles=n_tiles, tiles_per_seq=tiles_per_seq),
        grid=(n_tiles + 1,),
        in_specs=[pl.BlockSpec((tm, d), lambda i: (proj_tile(i), 0)),
                  pl.BlockSpec(mod3.shape, lambda i: (0, 0, 0)),
                  pl.BlockSpec((1, d), lambda i: (0, 0)),
                  pl.BlockSpec(memory_space=pl.ANY),
                  pl.BlockSpec((1, n), lambda i: (0, 0)),
                  pl.BlockSpec((CONV_WIDTH, width), lambda i: (0, 0)),
                  pl.BlockSpec((1, width), lambda i: (0, 0)),
                  pl.BlockSpec((1, n_head, head, two_head), lambda i: (0, 0, 0, 0)),
                  pl.BlockSpec((1, n_head, two_head), lambda i: (0, 0, 0)),
                  pl.BlockSpec(lru_lambda.shape, lambda i: (0, 0))],
        out_specs=[pl.BlockSpec((tm, width), lambda i: (proj_tile(i), 0)),
                   pl.BlockSpec((tm, 2 * width), lambda i: (proj_tile(i), 0)),
                   lagged, lagged],
        out_shape=[jax.ShapeDtypeStruct((rows, width), F32), jax.ShapeDtypeStruct((rows, 2 * width), BF16),
                   jax.ShapeDtypeStruct((rows, width), F32), jax.ShapeDtypeStruct((rows, width), F32)],
        scratch_shapes=[*_weight_scratch(w_in),
                        pltpu.VMEM((n_head, head, two_head), BF16),
                        pltpu.VMEM((tm, width), F32),
                        pltpu.VMEM((HALO, width), F32),
                        pltpu.VMEM((1, width), F32)],
        compiler_params=_params(("arbitrary",)),
        name="in_proj_lru_fwd",
    )(x2, mod3, norm_g, w_in, b_in, conv_w, conv_b, w_gate, b_gate, lru_lambda)


def _lru_bwd_out_kernel(xc_ref, hf_ref, gate_ref, wg_ref, bg_ref, lam_ref, ng_ref,
                        yp_ref, x_ref, mod_ref, w_hbm, b_ref, fg_ref,
                        o_ref, hb_ref, yl_ref, y_ref, carry_ref, wgb_ref, w_ref, stage, sem, *, n_tiles,
                        tiles_per_seq):
    s = pl.program_id(0)
    j = jnp.maximum(n_tiles - 1 - s, 0)

    @pl.when(s == 0)
    def _():
        carry_ref[...] = jnp.zeros_like(carry_ref)
        yl_ref[...] = jnp.zeros_like(yl_ref)
        _cast_gate_weights(wg_ref, wgb_ref)
        _load_weight_bf16(w_hbm, w_ref, stage, sem)

    rate2, bg_half = _lru_rates(lam_ref[1:2, :], bg_ref)
    kp = yp_ref.shape[1]
    n_chunk = o_ref.shape[1] // N_LRU_HEADS

    def project_chunk(h):
        cols = slice(h * n_chunk, (h + 1) * n_chunk)
        y = jnp.dot(yp_ref[...], w_ref[:kp, cols], preferred_element_type=F32)
        y = y + jnp.dot(yl_ref[...], w_ref[kp:, cols], preferred_element_type=F32)
        y_ref[:, cols] = y + b_ref[:, cols]

    carry_in = jnp.where(j % tiles_per_seq == tiles_per_seq - 1, 0.0, carry_ref[...])
    carry_ref[...] = _lru_direction(xc_ref[...], wgb_ref, bg_half, rate2, carry_in, hb_ref, reverse=True,
                                    after_gates=project_chunk)

    gate = mod_ref[2, pl.ds(jnp.minimum(n_tiles - s, n_tiles - 1) // tiles_per_seq, 1), :]
    xn = x_ref[...] + gate * y_ref[...]
    r = lax.rsqrt(jnp.mean(xn * xn, axis=-1, keepdims=True) + EPS)
    o_ref[...] = (xn * r) * fg_ref[...]

    yl = hf_ref[...] + hb_ref[...]
    rl = lax.rsqrt(jnp.mean(yl * yl, axis=-1, keepdims=True) + EPS)
    yl_ref[...] = ((yl * rl) * ng_ref[...] * gate_ref[...].astype(F32)).astype(yl_ref.dtype)


def _lru_bwd_out_proj(xc, hf, sg, w_gate, b_gate, lru_lambda, out_norm_g, yp, x2, mod3, w_out, b_out,
                      final_g, *, batch, seq, tm):
    d = x2.shape[1]
    width = xc.shape[1]
    tiles_per_seq = seq // tm
    n_tiles = batch * tiles_per_seq
    _, n_head, head, two_head = w_gate.shape

    def scan_tile(s):
        return jnp.maximum(n_tiles - 1 - s, 0)

    def proj_tile(s):
        return jnp.minimum(n_tiles - s, n_tiles - 1)

    return pl.pallas_call(
        functools.partial(_lru_bwd_out_kernel, n_tiles=n_tiles, tiles_per_seq=tiles_per_seq),
        grid=(n_tiles + 1,),
        in_specs=[pl.BlockSpec((tm, width), lambda s: (scan_tile(s), 0)),
                  pl.BlockSpec((tm, width), lambda s: (scan_tile(s), 0)),
                  pl.BlockSpec((tm, width), lambda s: (scan_tile(s), 1)),
                  pl.BlockSpec((1, n_head, head, two_head), lambda s: (1, 0, 0, 0)),
                  pl.BlockSpec((1, n_head, two_head), lambda s: (1, 0, 0)),
                  pl.BlockSpec(lru_lambda.shape, lambda s: (0, 0)),
                  pl.BlockSpec((1, width), lambda s: (0, 0)),
                  pl.BlockSpec((tm, yp.shape[1]), lambda s: (proj_tile(s), 0)),
                  pl.BlockSpec((tm, d), lambda s: (proj_tile(s), 0)),
                  pl.BlockSpec(mod3.shape, lambda s: (0, 0, 0)),
                  pl.BlockSpec(memory_space=pl.ANY),
                  pl.BlockSpec((1, d), lambda s: (0, 0)),
                  pl.BlockSpec((1, d), lambda s: (0, 0))],
        out_specs=pl.BlockSpec((tm, d), lambda s: (proj_tile(s), 0)),
        out_shape=jax.ShapeDtypeStruct((batch * seq, d), F32),
        scratch_shapes=[pltpu.VMEM((tm, width), F32),
                        pltpu.VMEM((tm, width), BF16),
                        pltpu.VMEM((tm, d), F32),
                        pltpu.VMEM((1, width), F32),
                        pltpu.VMEM((n_head, head, two_head), BF16),
                        *_weight_scratch(w_out)],
        compiler_params=_params(("arbitrary",)),
        name="lru_bwd_out_proj",
    )(xc, hf, sg, w_gate, b_gate, lru_lambda, out_norm_g, yp, x2, mod3, w_out, b_out, final_g)


def kernel(x, c, norm_g, w_ada, b_ada, w_in, b_in, w_pool, b_pool, pool_scale, conv_w, conv_b, w_gate,
           b_gate, lru_lambda, out_norm_pool_g, out_norm_lru_g, w_out, b_out, final_norm_g):
    batch, seq, d = x.shape
    assert w_in.shape[0] == 1, "single-layer block only"
    xs = x.reshape(batch * seq, d)
    mod3 = _adaln_mod(c, w_ada, b_ada)
    up, sg, xc, hf = _in_proj(xs, mod3, norm_g, w_in[0], b_in, conv_w[0], conv_b, w_gate[0], b_gate[0],
                              lru_lambda[0], batch=batch, seq=seq, tm=256)
    yp = _pool_mixer(up, sg, w_pool[0], b_pool[0], pool_scale, out_norm_pool_g, batch=batch, seq=seq, ts=512)
    out = _lru_bwd_out_proj(xc, hf, sg, w_gate[0], b_gate[0], lru_lambda[0], out_norm_lru_g, yp, xs, mod3,
                            w_out[0], b_out, final_norm_g[None, :], batch=batch, seq=seq, tm=512)
    return out.reshape(batch, seq, d)
```

```python
import functools
import math

import jax
import jax.numpy as jnp
from jax import lax
from jax.experimental import pallas as pl
from jax.experimental.pallas import tpu as pltpu

EPS = 1e-6
LRU_C = 8.0
POOL_WINDOWS = (2, 4, 8, 16)
N_LRU_HEADS = 4
CONV_WIDTH = 4
HALO = 8
SUBLANES = 8
BF16_ROW_TILE = 16
V7X_VMEM_LIMIT_BYTES = 56 * 1024 * 1024
W_IN_STAGE_COLS = 512
WEIGHT_STAGE_ROWS = 256
LOG2E = math.log2(math.e)
LN2 = math.log(2.0)

F32 = jnp.float32
BF16 = jnp.bfloat16


def _params(semantics):
    return pltpu.CompilerParams(dimension_semantics=semantics,
                                vmem_limit_bytes=V7X_VMEM_LIMIT_BYTES)


def _sigmoid(v):
    return 0.5 * jnp.tanh(0.5 * v) + 0.5


def _silu(v):
    return v * _sigmoid(v)


def _shift_rows(v, k):
    rows, width = v.shape
    v3 = v.reshape(rows // SUBLANES, SUBLANES, width)
    if k % SUBLANES == 0:
        g = (k // SUBLANES) % v3.shape[0]
        return jnp.concatenate([v3[-g:], v3[:-g]], axis=0).reshape(rows, width) if g else v
    sub = lax.broadcasted_iota(jnp.int32, v3.shape, 1)
    if k > 0:
        rot = pltpu.roll(v3, k, 1)
        other = jnp.concatenate([rot[-1:], rot[:-1]], axis=0)
        out = jnp.where(sub >= k, rot, other)
    else:
        rot = pltpu.roll(v3, SUBLANES + k, 1)
        other = jnp.concatenate([rot[1:], rot[:1]], axis=0)
        out = jnp.where(sub < SUBLANES + k, rot, other)
    return out.reshape(rows, width)


def _load_weight_bf16(w_hbm, w_vmem, stage, sem):
    rows = stage.shape[1]
    n_chunks = w_hbm.shape[0] // rows

    def copy(c):
        return pltpu.make_async_copy(w_hbm.at[pl.ds(c * rows, rows), :], stage.at[c % 2], sem.at[c % 2])

    copy(0).start()
    for c in range(n_chunks):
        if c + 1 < n_chunks:
            copy(c + 1).start()
        copy(c).wait()
        w_vmem[c * rows:(c + 1) * rows, :] = stage[c % 2].astype(BF16)


def _weight_scratch(w):
    k, n = w.shape
    assert k % WEIGHT_STAGE_ROWS == 0
    return [pltpu.VMEM((k, n), BF16), pltpu.VMEM((2, WEIGHT_STAGE_ROWS, n), F32), pltpu.SemaphoreType.DMA((2,))]


ADALN_STAGE_SLOTS = 3


def _adaln_kernel(c_ref, w_hbm, b_ref, o_ref, stage, sem):
    batch, d = c_ref.shape
    slots, rows, n = stage.shape
    n_chunks = d // rows
    ca = _silu(c_ref[...]).astype(BF16)
    ca = jnp.concatenate([ca, jnp.zeros((-batch % BF16_ROW_TILE, d), BF16)], axis=0)

    def copy(k):
        return pltpu.make_async_copy(w_hbm.at[0, pl.ds(k * rows, rows), :], stage.at[k % slots], sem.at[k % slots])

    for k in range(min(slots - 1, n_chunks)):
        copy(k).start()
    acc = jnp.zeros((ca.shape[0], n), F32)
    for k in range(n_chunks):
        if k + slots - 1 < n_chunks:
            copy(k + slots - 1).start()
        copy(k).wait()
        acc = acc + jnp.dot(ca[:, k * rows:(k + 1) * rows], stage[k % slots].astype(BF16),
                            preferred_element_type=F32)
    res = acc[:batch] + b_ref[...]
    for part in range(o_ref.shape[0]):
        o_ref[part] = res[:, part * d:(part + 1) * d]


def _adaln_mod(c, w_ada, b_ada):
    batch, d = c.shape
    n = w_ada.shape[2]
    assert d % WEIGHT_STAGE_ROWS == 0 and n == 3 * d
    return pl.pallas_call(
        _adaln_kernel,
        in_specs=[pl.BlockSpec(memory_space=pltpu.VMEM),
                  pl.BlockSpec(memory_space=pl.ANY),
                  pl.BlockSpec(memory_space=pltpu.VMEM)],
        out_specs=pl.BlockSpec(memory_space=pltpu.VMEM),
        out_shape=jax.ShapeDtypeStruct((3, batch, d), F32),
        scratch_shapes=[pltpu.VMEM((ADALN_STAGE_SLOTS, WEIGHT_STAGE_ROWS, n), F32),
                        pltpu.SemaphoreType.DMA((ADALN_STAGE_SLOTS,))],
        compiler_params=pltpu.CompilerParams(vmem_limit_bytes=V7X_VMEM_LIMIT_BYTES),
        name="adaln_mod",
    )(c, w_ada, b_ada)


def _in_proj_kernel(x_ref, mod_ref, g_ref, w_hbm, b_ref, u_ref, sg_ref, w_ref, stage, sem, *, n_chunk):
    first = (pl.program_id(0) == 0) & (pl.program_id(1) == 0)
    n = w_ref.shape[1]
    n_mix = u_ref.shape[1]

    def normalised_input():
        x = x_ref[...]
        r = lax.rsqrt(jnp.mean(x * x, axis=-1, keepdims=True) + EPS)
        b = pl.program_id(0)
        shift = mod_ref[0, pl.ds(b, 1), :]
        scale = mod_ref[1, pl.ds(b, 1), :]
        return ((x * r) * (g_ref[...] * (1.0 + scale)) + shift).astype(BF16)

    def project(hb, sl):
        zj = jnp.dot(hb, w_ref[:, sl], preferred_element_type=F32) + b_ref[:, sl]
        if sl.start < n_mix:
            u_ref[:, sl] = zj
        else:
            sg_ref[:, sl.start - n_mix:sl.stop - n_mix] = _silu(zj).astype(sg_ref.dtype)

    @pl.when(first)
    def _():
        cols = stage.shape[2]

        def copy(c):
            return pltpu.make_async_copy(w_hbm.at[:, pl.ds(c * cols, cols)], stage.at[c % 2], sem.at[c % 2])

        copy(0).start()
        hb = normalised_input()
        for c in range(n // cols):
            if c + 1 < n // cols:
                copy(c + 1).start()
            copy(c).wait()
            sl = slice(c * cols, (c + 1) * cols)
            w_ref[:, sl] = stage[c % 2].astype(BF16)
            project(hb, sl)

    @pl.when(jnp.logical_not(first))
    def _():
        hb = normalised_input()
        for j in range(n // n_chunk):
            project(hb, slice(j * n_chunk, (j + 1) * n_chunk))


def _in_proj(x2, mod3, norm_g, w_in, b_in, *, batch, seq, tm):
    d = x2.shape[1]
    n = w_in.shape[1]
    nt = seq // tm
    rows = batch * seq
    return pl.pallas_call(
        functools.partial(_in_proj_kernel, n_chunk=1024),
        grid=(batch, nt),
        in_specs=[pl.BlockSpec((tm, d), lambda b, i: (b * nt + i, 0)),
                  pl.BlockSpec(mod3.shape, lambda b, i: (0, 0, 0)),
                  pl.BlockSpec((1, d), lambda b, i: (0, 0)),
                  pl.BlockSpec(memory_space=pl.ANY),
                  pl.BlockSpec((1, n), lambda b, i: (0, 0))],
        out_specs=[pl.BlockSpec((tm, n // 2), lambda b, i: (b * nt + i, 0)),
                   pl.BlockSpec((tm, n // 2), lambda b, i: (b * nt + i, 0))],
        out_shape=[jax.ShapeDtypeStruct((rows, n // 2), F32), jax.ShapeDtypeStruct((rows, n // 2), BF16)],
        scratch_shapes=[pltpu.VMEM(w_in.shape, BF16),
                        pltpu.VMEM((2, d, W_IN_STAGE_COLS), F32),
                        pltpu.SemaphoreType.DMA((2,))],
        compiler_params=_params(("arbitrary", "arbitrary")),
        name="in_proj",
    )(x2, mod3, norm_g, w_in, b_in)


def _halo_specs(width, col_block, *, seq, ts, total_rows, tile_of):
    per_seq = seq // HALO
    per_tile = ts // HALO
    last = total_rows // HALO - 1

    def prev_map(*idx):
        b, i = idx[0], tile_of(*idx)
        return (jnp.maximum(b * per_seq + i * per_tile - 1, 0), col_block)

    def next_map(*idx):
        b, i = idx[0], tile_of(*idx)
        return (jnp.minimum(b * per_seq + (i + 1) * per_tile, last), col_block)

    return (pl.BlockSpec((HALO, width), prev_map), pl.BlockSpec((HALO, width), next_map))


def _with_halo(prev_ref, cur_ref, next_ref, tile, n_tiles):
    prev = jnp.where(tile > 0, prev_ref[...], 0.0)
    nxt = jnp.where(tile < n_tiles - 1, next_ref[...], 0.0)
    return jnp.concatenate([prev, cur_ref[...], nxt], axis=0)


def _pool_kernel(prev_ref, u_ref, next_ref, gate_ref, w_ref, b_ref, scale_ref, ng_ref, o_ref,
                 *, seq, ts, n_tiles):
    i = pl.program_id(1)
    ue = _with_halo(prev_ref, u_ref, next_ref, i, n_tiles)
    width = ue.shape[1]
    grp = width // len(POOL_WINDOWS)
    t = i * ts + lax.broadcasted_iota(jnp.int32, (ts, 1), 0)
    mixed = []
    for g, w in enumerate(POOL_WINDOWS):
        cols = slice(g * grp, (g + 1) * grp)
        e = ue[:, cols]
        lo = w // 2
        hi = w - lo - 1
        s, span = e, 1
        while span < w:
            s = s + _shift_rows(s, span)
            span *= 2
        total = (_shift_rows(s, -hi) if hi else s)[HALO:HALO + ts]
        cnt = (jnp.minimum(t + hi, seq - 1) - jnp.maximum(t - lo, 0) + 1).astype(F32)
        pooled = total / cnt - e[HALO:HALO + ts]
        m = jnp.dot(pooled.astype(BF16), w_ref[g].astype(BF16), preferred_element_type=F32)
        mixed.append((m + b_ref[g:g + 1, :]) * scale_ref[:, cols])
    y = jnp.concatenate(mixed, axis=-1)
    r = lax.rsqrt(jnp.mean(y * y, axis=-1, keepdims=True) + EPS)
    o_ref[...] = ((y * r) * ng_ref[...] * gate_ref[...].astype(F32)).astype(o_ref.dtype)


def _pool_mixer(z, sg, w_pool, b_pool, pool_scale, out_norm_g, *, batch, seq, ts):
    width = pool_scale.shape[1]
    n_tiles = seq // ts
    prev_spec, next_spec = _halo_specs(width, 0, seq=seq, ts=ts, total_rows=batch * seq,
                                       tile_of=lambda b, i: i)
    n_grp, grp, _ = w_pool.shape
    return pl.pallas_call(
        functools.partial(_pool_kernel, seq=seq, ts=ts, n_tiles=n_tiles),
        grid=(batch, n_tiles),
        in_specs=[prev_spec,
                  pl.BlockSpec((ts, width), lambda b, i: (b * n_tiles + i, 0)),
                  next_spec,
                  pl.BlockSpec((ts, width), lambda b, i: (b * n_tiles + i, 0)),
                  pl.BlockSpec((n_grp, grp, grp), lambda b, i: (0, 0, 0)),
                  pl.BlockSpec((n_grp, grp), lambda b, i: (0, 0)),
                  pl.BlockSpec((1, width), lambda b, i: (0, 0)),
                  pl.BlockSpec((1, width), lambda b, i: (0, 0))],
        out_specs=pl.BlockSpec((ts, width), lambda b, i: (b * n_tiles + i, 0)),
        out_shape=jax.ShapeDtypeStruct((batch * seq, width), BF16),
        compiler_params=_params(("arbitrary", "arbitrary")),
        name="pool_mixer",
    )(z, z, z, sg, w_pool, b_pool, pool_scale, out_norm_g)


def _softplus(v):
    return jnp.maximum(v, 0.0) + jnp.log1p(jnp.exp(-jnp.abs(v)))


def _lru_direction(xc_half, wg_ref, bg_half, rate2, carry, h_ref, *, reverse, after_gates=None):
    tc, width = xc_half.shape
    head = width // N_LRU_HEADS
    groups = tc // SUBLANES
    entry = SUBLANES - 1 if reverse else 0
    sub = lax.broadcasted_iota(jnp.int32, (groups, SUBLANES, head), 1)
    is_entry = sub == entry

    def back(v, d, axis):
        return pltpu.roll(v, SUBLANES - d if reverse else d, axis)

    carries = []
    for h in range(N_LRU_HEADS):
        cols = slice(h * head, (h + 1) * head)
        xh = xc_half[:, cols]
        g = jnp.dot(xh.astype(BF16), wg_ref[h], preferred_element_type=F32) + bg_half[h:h + 1, :]
        if after_gates is not None:
            after_gates(h)
        tr = jnp.tanh(g[:, :head])
        ti = jnp.tanh(g[:, head:])
        log2a = tr * rate2[:, cols] + rate2[:, cols]
        a = jnp.exp2(log2a)
        one_minus_a2 = (a * a + 1.0) * jnp.tanh(log2a * (-LN2))
        mult = jnp.where(one_minus_a2 > 0.0, one_minus_a2 * lax.rsqrt(one_minus_a2), 0.0)
        b = mult * (ti + 1.0) * xh

        a3 = a.reshape(groups, SUBLANES, head)
        b3 = b.reshape(groups, SUBLANES, head)
        a_in = jnp.where(is_entry, a3, 0.0)
        a1 = jnp.where(is_entry, 0.0, a3)
        a2 = a1 * back(a1, 1, 1)
        a4 = a2 * back(a2, 2, 1)
        state = jnp.broadcast_to(carry[:, cols], (SUBLANES, head))
        order = range(groups - 1, -1, -1) if reverse else range(groups)
        for gi in order:
            s = b3[gi] + a_in[gi] * back(state, 1, 0)
            s = s + a1[gi] * back(s, 1, 0)
            s = s + a2[gi] * back(s, 2, 0)
            state = s + a4[gi] * back(s, 4, 0)
            h_ref[gi * SUBLANES:(gi + 1) * SUBLANES, cols] = state
        exit_row = SUBLANES - 1 - entry
        carries.append(state[exit_row:exit_row + 1, :])
    return jnp.concatenate(carries, axis=1)


def _lru_rates(lam_row, bg_ref):
    rate2 = (-0.5 * LRU_C * LOG2E) * _softplus(-lam_row)
    return rate2, 0.5 * bg_ref[0]


def _cast_gate_weights(wg_ref, wgb_ref):
    for h in range(N_LRU_HEADS):
        wgb_ref[h] = wg_ref[0, h].astype(BF16)


def _lru_fwd_kernel(prev_ref, u_ref, next_ref, cw_ref, cb_ref, wg_ref, bg_ref, lam_ref,
                    xc_ref, hf_ref, carry_ref, wgb_ref, *, tc, n_chunks):
    c = pl.program_id(1)

    @pl.when((pl.program_id(0) == 0) & (c == 0))
    def _():
        _cast_gate_weights(wg_ref, wgb_ref)

    @pl.when(c == 0)
    def _():
        carry_ref[...] = jnp.zeros_like(carry_ref)

    rate2, bg_half = _lru_rates(lam_ref[0:1, :], bg_ref)
    ue = _with_halo(prev_ref, u_ref, next_ref, c, n_chunks)
    left = CONV_WIDTH // 2
    xc_half = 0.5 * cb_ref[...]
    for k in range(CONV_WIDTH):
        tap = ue if k == left else _shift_rows(ue, left - k)
        xc_half = xc_half + tap[HALO:HALO + tc] * (0.5 * cw_ref[k:k + 1, :])
    xc_ref[...] = xc_half
    carry_ref[...] = _lru_direction(xc_half, wgb_ref, bg_half, rate2, carry_ref[...], hf_ref, reverse=False)


def _lru_fwd(z, conv_w, conv_b, w_gate, b_gate, lru_lambda, *, batch, seq, tc):
    width = conv_b.shape[1]
    n_chunks = seq // tc
    prev_spec, next_spec = _halo_specs(width, 1, seq=seq, ts=tc, total_rows=batch * seq,
                                       tile_of=lambda b, c: c)
    _, n_head, head, two_head = w_gate.shape
    tile = pl.BlockSpec((tc, width), lambda b, c: (b * n_chunks + c, 0))
    return pl.pallas_call(
        functools.partial(_lru_fwd_kernel, tc=tc, n_chunks=n_chunks),
        grid=(batch, n_chunks),
        in_specs=[prev_spec,
                  pl.BlockSpec((tc, width), lambda b, c: (b * n_chunks + c, 1)),
                  next_spec,
                  pl.BlockSpec((CONV_WIDTH, width), lambda b, c: (0, 0)),
                  pl.BlockSpec((1, width), lambda b, c: (0, 0)),
                  pl.BlockSpec((1, n_head, head, two_head), lambda b, c: (0, 0, 0, 0)),
                  pl.BlockSpec((1, n_head, two_head), lambda b, c: (0, 0, 0)),
                  pl.BlockSpec(lru_lambda.shape, lambda b, c: (0, 0))],
        out_specs=[tile, tile],
        out_shape=[jax.ShapeDtypeStruct((batch * seq, width), F32)] * 2,
        scratch_shapes=[pltpu.VMEM((1, width), F32),
                        pltpu.VMEM((n_head, head, two_head), BF16)],
        compiler_params=_params(("arbitrary", "arbitrary")),
        name="lru_fwd",
    )(z, z, z, conv_w, conv_b, w_gate, b_gate, lru_lambda)


def _lru_bwd_out_kernel(xc_ref, hf_ref, gate_ref, wg_ref, bg_ref, lam_ref, ng_ref,
                        yp_ref, x_ref, mod_ref, w_hbm, b_ref, fg_ref,
                        o_ref, hb_ref, yl_ref, y_ref, carry_ref, wgb_ref, w_ref, stage, sem, *, n_tiles,
                        tiles_per_seq):
    s = pl.program_id(0)
    j = jnp.maximum(n_tiles - 1 - s, 0)

    @pl.when(s == 0)
    def _():
        carry_ref[...] = jnp.zeros_like(carry_ref)
        yl_ref[...] = jnp.zeros_like(yl_ref)
        _cast_gate_weights(wg_ref, wgb_ref)
        _load_weight_bf16(w_hbm, w_ref, stage, sem)

    rate2, bg_half = _lru_rates(lam_ref[1:2, :], bg_ref)
    kp = yp_ref.shape[1]
    n_chunk = o_ref.shape[1] // N_LRU_HEADS

    def project_chunk(h):
        cols = slice(h * n_chunk, (h + 1) * n_chunk)
        y = jnp.dot(yp_ref[...], w_ref[:kp, cols], preferred_element_type=F32)
        y = y + jnp.dot(yl_ref[...], w_ref[kp:, cols], preferred_element_type=F32)
        y_ref[:, cols] = y + b_ref[:, cols]

    carry_in = jnp.where(j % tiles_per_seq == tiles_per_seq - 1, 0.0, carry_ref[...])
    carry_ref[...] = _lru_direction(xc_ref[...], wgb_ref, bg_half, rate2, carry_in, hb_ref, reverse=True,
                                    after_gates=project_chunk)

    gate = mod_ref[2, pl.ds(jnp.minimum(n_tiles - s, n_tiles - 1) // tiles_per_seq, 1), :]
    xn = x_ref[...] + gate * y_ref[...]
    r = lax.rsqrt(jnp.mean(xn * xn, axis=-1, keepdims=True) + EPS)
    o_ref[...] = (xn * r) * fg_ref[...]

    yl = hf_ref[...] + hb_ref[...]
    rl = lax.rsqrt(jnp.mean(yl * yl, axis=-1, keepdims=True) + EPS)
    yl_ref[...] = ((yl * rl) * ng_ref[...] * gate_ref[...].astype(F32)).astype(yl_ref.dtype)


def _lru_bwd_out_proj(xc, hf, sg, w_gate, b_gate, lru_lambda, out_norm_g, yp, x2, mod3, w_out, b_out,
                      final_g, *, batch, seq, tm):
    d = x2.shape[1]
    width = xc.shape[1]
    tiles_per_seq = seq // tm
    n_tiles = batch * tiles_per_seq
    _, n_head, head, two_head = w_gate.shape

    def scan_tile(s):
        return jnp.maximum(n_tiles - 1 - s, 0)

    def proj_tile(s):
        return jnp.minimum(n_tiles - s, n_tiles - 1)

    return pl.pallas_call(
        functools.partial(_lru_bwd_out_kernel, n_tiles=n_tiles, tiles_per_seq=tiles_per_seq),
        grid=(n_tiles + 1,),
        in_specs=[pl.BlockSpec((tm, width), lambda s: (scan_tile(s), 0)),
                  pl.BlockSpec((tm, width), lambda s: (scan_tile(s), 0)),
                  pl.BlockSpec((tm, width), lambda s: (scan_tile(s), 1)),
                  pl.BlockSpec((1, n_head, head, two_head), lambda s: (1, 0, 0, 0)),
                  pl.BlockSpec((1, n_head, two_head), lambda s: (1, 0, 0)),
                  pl.BlockSpec(lru_lambda.shape, lambda s: (0, 0)),
                  pl.BlockSpec((1, width), lambda s: (0, 0)),
                  pl.BlockSpec((tm, yp.shape[1]), lambda s: (proj_tile(s), 0)),
                  pl.BlockSpec((tm, d), lambda s: (proj_tile(s), 0)),
                  pl.BlockSpec(mod3.shape, lambda s: (0, 0, 0)),
                  pl.BlockSpec(memory_space=pl.ANY),
                  pl.BlockSpec((1, d), lambda s: (0, 0)),
                  pl.BlockSpec((1, d), lambda s: (0, 0))],
        out_specs=pl.BlockSpec((tm, d), lambda s: (proj_tile(s), 0)),
        out_shape=jax.ShapeDtypeStruct((batch * seq, d), F32),
        scratch_shapes=[pltpu.VMEM((tm, width), F32),
                        pltpu.VMEM((tm, width), BF16),
                        pltpu.VMEM((tm, d), F32),
                        pltpu.VMEM((1, width), F32),
                        pltpu.VMEM((n_head, head, two_head), BF16),
                        *_weight_scratch(w_out)],
        compiler_params=_params(("arbitrary",)),
        name="lru_bwd_out_proj",
    )(xc, hf, sg, w_gate, b_gate, lru_lambda, out_norm_g, yp, x2, mod3, w_out, b_out, final_g)


def kernel(x, c, norm_g, w_ada, b_ada, w_in, b_in, w_pool, b_pool, pool_scale, conv_w, conv_b, w_gate,
           b_gate, lru_lambda, out_norm_pool_g, out_norm_lru_g, w_out, b_out, final_norm_g):
    batch, seq, d = x.shape
    assert w_in.shape[0] == 1, "single-layer block only"
    xs = x.reshape(batch * seq, d)
    mod3 = _adaln_mod(c, w_ada, b_ada)
    z, sg = _in_proj(xs, mod3, norm_g, w_in[0], b_in, batch=batch, seq=seq, tm=512)
    yp = _pool_mixer(z, sg, w_pool[0], b_pool[0], pool_scale, out_norm_pool_g, batch=batch, seq=seq, ts=512)
    xc, hf = _lru_fwd(z, conv_w[0], conv_b, w_gate[0], b_gate[0], lru_lambda[0], batch=batch, seq=seq, tc=512)
    out = _lru_bwd_out_proj(xc, hf, sg, w_gate[0], b_gate[0], lru_lambda[0], out_norm_lru_g, yp, xs, mod3,
                            w_out[0], b_out, final_norm_g[None, :], batch=batch, seq=seq, tm=512)
    return out.reshape(batch, seq, d)
```

```python
import functools
import math

import jax
import jax.numpy as jnp
from jax import lax
from jax.experimental import pallas as pl
from jax.experimental.pallas import tpu as pltpu

EPS = 1e-6
LRU_C = 8.0
POOL_WINDOWS = (2, 4, 8, 16)
N_LRU_HEADS = 4
CONV_WIDTH = 4
HALO = 8
SUBLANES = 8
BF16_ROW_TILE = 16
V7X_VMEM_LIMIT_BYTES = 56 * 1024 * 1024
W_IN_STAGE_COLS = 512
WEIGHT_STAGE_ROWS = 256
LOG2E = math.log2(math.e)
LN2 = math.log(2.0)

F32 = jnp.float32
BF16 = jnp.bfloat16


def _params(semantics):
    return pltpu.CompilerParams(dimension_semantics=semantics,
                                vmem_limit_bytes=V7X_VMEM_LIMIT_BYTES)


def _sigmoid(v):
    return 0.5 * jnp.tanh(0.5 * v) + 0.5


def _silu(v):
    return v * _sigmoid(v)


def _shift_rows(v, k):
    rows, width = v.shape
    v3 = v.reshape(rows // SUBLANES, SUBLANES, width)
    if k % SUBLANES == 0:
        g = (k // SUBLANES) % v3.shape[0]
        return jnp.concatenate([v3[-g:], v3[:-g]], axis=0).reshape(rows, width) if g else v
    sub = lax.broadcasted_iota(jnp.int32, v3.shape, 1)
    if k > 0:
        rot = pltpu.roll(v3, k, 1)
        other = jnp.concatenate([rot[-1:], rot[:-1]], axis=0)
        out = jnp.where(sub >= k, rot, other)
    else:
        rot = pltpu.roll(v3, SUBLANES + k, 1)
        other = jnp.concatenate([rot[1:], rot[:1]], axis=0)
        out = jnp.where(sub < SUBLANES + k, rot, other)
    return out.reshape(rows, width)


def _weight_stream(w_hbm, w_vmem, stage, sem):
    slots, rows, _ = stage.shape
    n_chunks = w_hbm.shape[0] // rows

    def copy(c):
        return pltpu.make_async_copy(w_hbm.at[pl.ds(c * rows, rows), :], stage.at[c % slots], sem.at[c % slots])

    def begin():
        for c in range(min(slots, n_chunks)):
            copy(c).start()

    def advance(lo, hi):
        for c in range(lo, hi):
            copy(c).wait()
            w_vmem[c * rows:(c + 1) * rows, :] = stage[c % slots].astype(BF16)
            if c + slots < n_chunks:
                copy(c + slots).start()

    return begin, advance, n_chunks


def _weight_scratch(w):
    k, n = w.shape
    assert k % WEIGHT_STAGE_ROWS == 0
    return [pltpu.VMEM((k, n), BF16), pltpu.VMEM((2, WEIGHT_STAGE_ROWS, n), F32), pltpu.SemaphoreType.DMA((2,))]


ADALN_STAGE_SLOTS = 3


def _adaln_kernel(c_ref, w_hbm, b_ref, o_ref, stage, sem):
    batch, d = c_ref.shape
    slots, rows, n = stage.shape
    n_chunks = d // rows
    ca = _silu(c_ref[...]).astype(BF16)
    ca = jnp.concatenate([ca, jnp.zeros((-batch % BF16_ROW_TILE, d), BF16)], axis=0)

    def copy(k):
        return pltpu.make_async_copy(w_hbm.at[0, pl.ds(k * rows, rows), :], stage.at[k % slots], sem.at[k % slots])

    for k in range(min(slots - 1, n_chunks)):
        copy(k).start()
    acc = jnp.zeros((ca.shape[0], n), F32)
    for k in range(n_chunks):
        if k + slots - 1 < n_chunks:
            copy(k + slots - 1).start()
        copy(k).wait()
        acc = acc + jnp.dot(ca[:, k * rows:(k + 1) * rows], stage[k % slots].astype(BF16),
                            preferred_element_type=F32)
    res = acc[:batch] + b_ref[...]
    for part in range(o_ref.shape[0]):
        o_ref[part] = res[:, part * d:(part + 1) * d]


def _adaln_mod(c, w_ada, b_ada):
    batch, d = c.shape
    n = w_ada.shape[2]
    assert d % WEIGHT_STAGE_ROWS == 0 and n == 3 * d
    return pl.pallas_call(
        _adaln_kernel,
        in_specs=[pl.BlockSpec(memory_space=pltpu.VMEM),
                  pl.BlockSpec(memory_space=pl.ANY),
                  pl.BlockSpec(memory_space=pltpu.VMEM)],
        out_specs=pl.BlockSpec(memory_space=pltpu.VMEM),
        out_shape=jax.ShapeDtypeStruct((3, batch, d), F32),
        scratch_shapes=[pltpu.VMEM((ADALN_STAGE_SLOTS, WEIGHT_STAGE_ROWS, n), F32),
                        pltpu.SemaphoreType.DMA((ADALN_STAGE_SLOTS,))],
        compiler_params=pltpu.CompilerParams(vmem_limit_bytes=V7X_VMEM_LIMIT_BYTES),
        name="adaln_mod",
    )(c, w_ada, b_ada)


def _in_proj_kernel(x_ref, mod_ref, g_ref, w_hbm, b_ref, u_ref, sg_ref, w_ref, stage, sem, *, n_chunk):
    first = (pl.program_id(0) == 0) & (pl.program_id(1) == 0)
    n = w_ref.shape[1]
    n_mix = u_ref.shape[1]

    def normalised_input():
        x = x_ref[...]
        r = lax.rsqrt(jnp.mean(x * x, axis=-1, keepdims=True) + EPS)
        b = pl.program_id(0)
        shift = mod_ref[0, pl.ds(b, 1), :]
        scale = mod_ref[1, pl.ds(b, 1), :]
        return ((x * r) * (g_ref[...] * (1.0 + scale)) + shift).astype(BF16)

    def project(hb, sl):
        zj = jnp.dot(hb, w_ref[:, sl], preferred_element_type=F32) + b_ref[:, sl]
        if sl.start < n_mix:
            u_ref[:, sl] = zj
        else:
            sg_ref[:, sl.start - n_mix:sl.stop - n_mix] = _silu(zj).astype(sg_ref.dtype)

    @pl.when(first)
    def _():
        cols = stage.shape[2]

        def copy(c):
            return pltpu.make_async_copy(w_hbm.at[:, pl.ds(c * cols, cols)], stage.at[c % 2], sem.at[c % 2])

        copy(0).start()
        hb = normalised_input()
        for c in range(n // cols):
            if c + 1 < n // cols:
                copy(c + 1).start()
            copy(c).wait()
            sl = slice(c * cols, (c + 1) * cols)
            w_ref[:, sl] = stage[c % 2].astype(BF16)
            project(hb, sl)

    @pl.when(jnp.logical_not(first))
    def _():
        hb = normalised_input()
        for j in range(n // n_chunk):
            project(hb, slice(j * n_chunk, (j + 1) * n_chunk))


def _in_proj(x2, mod3, norm_g, w_in, b_in, *, batch, seq, tm):
    d = x2.shape[1]
    n = w_in.shape[1]
    nt = seq // tm
    rows = batch * seq
    return pl.pallas_call(
        functools.partial(_in_proj_kernel, n_chunk=1024),
        grid=(batch, nt),
        in_specs=[pl.BlockSpec((tm, d), lambda b, i: (b * nt + i, 0)),
                  pl.BlockSpec(mod3.shape, lambda b, i: (0, 0, 0)),
                  pl.BlockSpec((1, d), lambda b, i: (0, 0)),
                  pl.BlockSpec(memory_space=pl.ANY),
                  pl.BlockSpec((1, n), lambda b, i: (0, 0))],
        out_specs=[pl.BlockSpec((tm, n // 2), lambda b, i: (b * nt + i, 0)),
                   pl.BlockSpec((tm, n // 2), lambda b, i: (b * nt + i, 0))],
        out_shape=[jax.ShapeDtypeStruct((rows, n // 2), F32), jax.ShapeDtypeStruct((rows, n // 2), BF16)],
        scratch_shapes=[pltpu.VMEM(w_in.shape, BF16),
                        pltpu.VMEM((2, d, W_IN_STAGE_COLS), F32),
                        pltpu.SemaphoreType.DMA((2,))],
        compiler_params=_params(("arbitrary", "arbitrary")),
        name="in_proj",
    )(x2, mod3, norm_g, w_in, b_in)


def _halo_specs(width, col_block, *, seq, ts, total_rows, tile_of):
    per_seq = seq // HALO
    per_tile = ts // HALO
    last = total_rows // HALO - 1

    def prev_map(*idx):
        b, i = idx[0], tile_of(*idx)
        return (jnp.maximum(b * per_seq + i * per_tile - 1, 0), col_block)

    def next_map(*idx):
        b, i = idx[0], tile_of(*idx)
        return (jnp.minimum(b * per_seq + (i + 1) * per_tile, last), col_block)

    return (pl.BlockSpec((HALO, width), prev_map), pl.BlockSpec((HALO, width), next_map))


def _with_halo(prev_ref, cur_ref, next_ref, tile, n_tiles):
    prev = jnp.where(tile > 0, prev_ref[...], 0.0)
    nxt = jnp.where(tile < n_tiles - 1, next_ref[...], 0.0)
    return jnp.concatenate([prev, cur_ref[...], nxt], axis=0)


def _pool_kernel(prev_ref, u_ref, next_ref, gate_ref, w_ref, b_ref, scale_ref, ng_ref, o_ref,
                 *, seq, ts, n_tiles):
    i = pl.program_id(1)
    ue = _with_halo(prev_ref, u_ref, next_ref, i, n_tiles)
    width = ue.shape[1]
    grp = width // len(POOL_WINDOWS)
    t = i * ts + lax.broadcasted_iota(jnp.int32, (ts, 1), 0)
    mixed = []
    for g, w in enumerate(POOL_WINDOWS):
        cols = slice(g * grp, (g + 1) * grp)
        e = ue[:, cols]
        lo = w // 2
        hi = w - lo - 1
        s, span = e, 1
        while span < w:
            s = s + _shift_rows(s, span)
            span *= 2
        total = (_shift_rows(s, -hi) if hi else s)[HALO:HALO + ts]
        cnt = (jnp.minimum(t + hi, seq - 1) - jnp.maximum(t - lo, 0) + 1).astype(F32)
        pooled = total / cnt - e[HALO:HALO + ts]
        m = jnp.dot(pooled.astype(BF16), w_ref[g].astype(BF16), preferred_element_type=F32)
        mixed.append((m + b_ref[g:g + 1, :]) * scale_ref[:, cols])
    y = jnp.concatenate(mixed, axis=-1)
    r = lax.rsqrt(jnp.mean(y * y, axis=-1, keepdims=True) + EPS)
    o_ref[...] = ((y * r) * ng_ref[...] * gate_ref[...].astype(F32)).astype(o_ref.dtype)


def _pool_mixer(z, sg, w_pool, b_pool, pool_scale, out_norm_g, *, batch, seq, ts):
    width = pool_scale.shape[1]
    n_tiles = seq // ts
    prev_spec, next_spec = _halo_specs(width, 0, seq=seq, ts=ts, total_rows=batch * seq,
                                       tile_of=lambda b, i: i)
    n_grp, grp, _ = w_pool.shape
    return pl.pallas_call(
        functools.partial(_pool_kernel, seq=seq, ts=ts, n_tiles=n_tiles),
        grid=(batch, n_tiles),
        in_specs=[prev_spec,
                  pl.BlockSpec((ts, width), lambda b, i: (b * n_tiles + i, 0)),
                  next_spec,
                  pl.BlockSpec((ts, width), lambda b, i: (b * n_tiles + i, 0)),
                  pl.BlockSpec((n_grp, grp, grp), lambda b, i: (0, 0, 0)),
                  pl.BlockSpec((n_grp, grp), lambda b, i: (0, 0)),
                  pl.BlockSpec((1, width), lambda b, i: (0, 0)),
                  pl.BlockSpec((1, width), lambda b, i: (0, 0))],
        out_specs=pl.BlockSpec((ts, width), lambda b, i: (b * n_tiles + i, 0)),
        out_shape=jax.ShapeDtypeStruct((batch * seq, width), BF16),
        compiler_params=_params(("arbitrary", "arbitrary")),
        name="pool_mixer",
    )(z, z, z, sg, w_pool, b_pool, pool_scale, out_norm_g)


def _softplus(v):
    return jnp.maximum(v, 0.0) + jnp.log1p(jnp.exp(-jnp.abs(v)))


def _lru_direction(xc_half, wg_ref, bg_half, rate2, carry, h_ref, *, reverse, after_gates=None):
    tc, width = xc_half.shape
    head = width // N_LRU_HEADS
    groups = tc // SUBLANES
    entry = SUBLANES - 1 if reverse else 0
    sub = lax.broadcasted_iota(jnp.int32, (groups, SUBLANES, head), 1)
    is_entry = sub == entry

    def back(v, d, axis):
        return pltpu.roll(v, SUBLANES - d if reverse else d, axis)

    carries = []
    for h in range(N_LRU_HEADS):
        cols = slice(h * head, (h + 1) * head)
        xh = xc_half[:, cols]
        g = jnp.dot(xh.astype(BF16), wg_ref[h], preferred_element_type=F32) + bg_half[h:h + 1, :]
        if after_gates is not None:
            after_gates(h)
        tr = jnp.tanh(g[:, :head])
        ti = jnp.tanh(g[:, head:])
        log2a = tr * rate2[:, cols] + rate2[:, cols]
        a = jnp.exp2(log2a)
        one_minus_a2 = (a * a + 1.0) * jnp.tanh(log2a * (-LN2))
        mult = jnp.where(one_minus_a2 > 0.0, one_minus_a2 * lax.rsqrt(one_minus_a2), 0.0)
        b = mult * (ti + 1.0) * xh

        a3 = a.reshape(groups, SUBLANES, head)
        b3 = b.reshape(groups, SUBLANES, head)
        a_in = jnp.where(is_entry, a3, 0.0)
        a1 = jnp.where(is_entry, 0.0, a3)
        a2 = a1 * back(a1, 1, 1)
        a4 = a2 * back(a2, 2, 1)
        state = jnp.broadcast_to(carry[:, cols], (SUBLANES, head))
        order = range(groups - 1, -1, -1) if reverse else range(groups)
        for gi in order:
            s = b3[gi] + a_in[gi] * back(state, 1, 0)
            s = s + a1[gi] * back(s, 1, 0)
            s = s + a2[gi] * back(s, 2, 0)
            state = s + a4[gi] * back(s, 4, 0)
            h_ref[gi * SUBLANES:(gi + 1) * SUBLANES, cols] = state
        exit_row = SUBLANES - 1 - entry
        carries.append(state[exit_row:exit_row + 1, :])
    return jnp.concatenate(carries, axis=1)


def _lru_rates(lam_row, bg_ref):
    rate2 = (-0.5 * LRU_C * LOG2E) * _softplus(-lam_row)
    return rate2, 0.5 * bg_ref[0]


def _cast_gate_weights(wg_ref, wgb_ref):
    for h in range(N_LRU_HEADS):
        wgb_ref[h] = wg_ref[0, h].astype(BF16)


def _lru_fwd_kernel(prev_ref, u_ref, next_ref, cw_ref, cb_ref, wg_ref, bg_ref, lam_ref,
                    xc_ref, hf_ref, carry_ref, wgb_ref, *, tc, n_chunks):
    c = pl.program_id(1)

    @pl.when((pl.program_id(0) == 0) & (c == 0))
    def _():
        _cast_gate_weights(wg_ref, wgb_ref)

    @pl.when(c == 0)
    def _():
        carry_ref[...] = jnp.zeros_like(carry_ref)

    rate2, bg_half = _lru_rates(lam_ref[0:1, :], bg_ref)
    ue = _with_halo(prev_ref, u_ref, next_ref, c, n_chunks)
    left = CONV_WIDTH // 2
    xc_half = 0.5 * cb_ref[...]
    for k in range(CONV_WIDTH):
        tap = ue if k == left else _shift_rows(ue, left - k)
        xc_half = xc_half + tap[HALO:HALO + tc] * (0.5 * cw_ref[k:k + 1, :])
    xc_ref[...] = xc_half
    carry_ref[...] = _lru_direction(xc_half, wgb_ref, bg_half, rate2, carry_ref[...], hf_ref, reverse=False)


def _lru_fwd(z, conv_w, conv_b, w_gate, b_gate, lru_lambda, *, batch, seq, tc):
    width = conv_b.shape[1]
    n_chunks = seq // tc
    prev_spec, next_spec = _halo_specs(width, 1, seq=seq, ts=tc, total_rows=batch * seq,
                                       tile_of=lambda b, c: c)
    _, n_head, head, two_head = w_gate.shape
    tile = pl.BlockSpec((tc, width), lambda b, c: (b * n_chunks + c, 0))
    return pl.pallas_call(
        functools.partial(_lru_fwd_kernel, tc=tc, n_chunks=n_chunks),
        grid=(batch, n_chunks),
        in_specs=[prev_spec,
                  pl.BlockSpec((tc, width), lambda b, c: (b * n_chunks + c, 1)),
                  next_spec,
                  pl.BlockSpec((CONV_WIDTH, width), lambda b, c: (0, 0)),
                  pl.BlockSpec((1, width), lambda b, c: (0, 0)),
                  pl.BlockSpec((1, n_head, head, two_head), lambda b, c: (0, 0, 0, 0)),
                  pl.BlockSpec((1, n_head, two_head), lambda b, c: (0, 0, 0)),
                  pl.BlockSpec(lru_lambda.shape, lambda b, c: (0, 0))],
        out_specs=[tile, tile],
        out_shape=[jax.ShapeDtypeStruct((batch * seq, width), F32)] * 2,
        scratch_shapes=[pltpu.VMEM((1, width), F32),
                        pltpu.VMEM((n_head, head, two_head), BF16)],
        compiler_params=_params(("arbitrary", "arbitrary")),
        name="lru_fwd",
    )(z, z, z, conv_w, conv_b, w_gate, b_gate, lru_lambda)


def _lru_bwd_out_kernel(xc_ref, hf_ref, gate_ref, wg_ref, bg_ref, lam_ref, ng_ref,
                        yp_ref, x_ref, mod_ref, w_hbm, b_ref, fg_ref,
                        o_ref, hb_ref, yl_ref, y_ref, carry_ref, wgb_ref, w_ref, stage, sem, *, n_tiles,
                        tiles_per_seq):
    s = pl.program_id(0)
    kp = yp_ref.shape[1]
    n_chunk = o_ref.shape[1] // N_LRU_HEADS

    def project_chunk(h):
        cols = slice(h * n_chunk, (h + 1) * n_chunk)
        y = jnp.dot(yp_ref[...], w_ref[:kp, cols], preferred_element_type=F32)
        y = y + jnp.dot(yl_ref[...], w_ref[kp:, cols], preferred_element_type=F32)
        y_ref[:, cols] = y + b_ref[:, cols]

    def scan(carry_in, after_gates):
        rate2, bg_half = _lru_rates(lam_ref[1:2, :], bg_ref)
        carry_ref[...] = _lru_direction(xc_ref[...], wgb_ref, bg_half, rate2, carry_in, hb_ref, reverse=True,
                                        after_gates=after_gates)

    def finish_scan():
        yl = hf_ref[...] + hb_ref[...]
        rl = lax.rsqrt(jnp.mean(yl * yl, axis=-1, keepdims=True) + EPS)
        yl_ref[...] = ((yl * rl) * ng_ref[...] * gate_ref[...].astype(F32)).astype(yl_ref.dtype)

    def finish_projection():
        gate = mod_ref[2, pl.ds((n_tiles - s) // tiles_per_seq, 1), :]
        xn = x_ref[...] + gate * y_ref[...]
        r = lax.rsqrt(jnp.mean(xn * xn, axis=-1, keepdims=True) + EPS)
        o_ref[...] = (xn * r) * fg_ref[...]

    @pl.when(s == 0)
    def _():
        _cast_gate_weights(wg_ref, wgb_ref)
        begin, advance, n_chunks = _weight_stream(w_hbm, w_ref, stage, sem)
        per_head = n_chunks // N_LRU_HEADS
        begin()
        scan(jnp.zeros_like(carry_ref), lambda h: advance(h * per_head, (h + 1) * per_head))
        advance(N_LRU_HEADS * per_head, n_chunks)
        finish_scan()

    @pl.when((s > 0) & (s < n_tiles))
    def _():
        j = n_tiles - 1 - s
        scan(jnp.where(j % tiles_per_seq == tiles_per_seq - 1, 0.0, carry_ref[...]), project_chunk)
        finish_projection()
        finish_scan()

    @pl.when(s == n_tiles)
    def _():
        for h in range(N_LRU_HEADS):
            project_chunk(h)
        finish_projection()


def _lru_bwd_out_proj(xc, hf, sg, w_gate, b_gate, lru_lambda, out_norm_g, yp, x2, mod3, w_out, b_out,
                      final_g, *, batch, seq, tm):
    d = x2.shape[1]
    width = xc.shape[1]
    tiles_per_seq = seq // tm
    n_tiles = batch * tiles_per_seq
    _, n_head, head, two_head = w_gate.shape

    def scan_tile(s):
        return jnp.maximum(n_tiles - 1 - s, 0)

    def proj_tile(s):
        return jnp.minimum(n_tiles - s, n_tiles - 1)

    return pl.pallas_call(
        functools.partial(_lru_bwd_out_kernel, n_tiles=n_tiles, tiles_per_seq=tiles_per_seq),
        grid=(n_tiles + 1,),
        in_specs=[pl.BlockSpec((tm, width), lambda s: (scan_tile(s), 0)),
                  pl.BlockSpec((tm, width), lambda s: (scan_tile(s), 0)),
                  pl.BlockSpec((tm, width), lambda s: (scan_tile(s), 1)),
                  pl.BlockSpec((1, n_head, head, two_head), lambda s: (1, 0, 0, 0)),
                  pl.BlockSpec((1, n_head, two_head), lambda s: (1, 0, 0)),
                  pl.BlockSpec(lru_lambda.shape, lambda s: (0, 0)),
                  pl.BlockSpec((1, width), lambda s: (0, 0)),
                  pl.BlockSpec((tm, yp.shape[1]), lambda s: (proj_tile(s), 0)),
                  pl.BlockSpec((tm, d), lambda s: (proj_tile(s), 0)),
                  pl.BlockSpec(mod3.shape, lambda s: (0, 0, 0)),
                  pl.BlockSpec(memory_space=pl.ANY),
                  pl.BlockSpec((1, d), lambda s: (0, 0)),
                  pl.BlockSpec((1, d), lambda s: (0, 0))],
        out_specs=pl.BlockSpec((tm, d), lambda s: (proj_tile(s), 0)),
        out_shape=jax.ShapeDtypeStruct((batch * seq, d), F32),
        scratch_shapes=[pltpu.VMEM((tm, width), F32),
                        pltpu.VMEM((tm, width), BF16),
                        pltpu.VMEM((tm, d), F32),
                        pltpu.VMEM((1, width), F32),
                        pltpu.VMEM((n_head, head, two_head), BF16),
                        *_weight_scratch(w_out)],
        compiler_params=_params(("arbitrary",)),
        name="lru_bwd_out_proj",
    )(xc, hf, sg, w_gate, b_gate, lru_lambda, out_norm_g, yp, x2, mod3, w_out, b_out, final_g)


def kernel(x, c, norm_g, w_ada, b_ada, w_in, b_in, w_pool, b_pool, pool_scale, conv_w, conv_b, w_gate,
           b_gate, lru_lambda, out_norm_pool_g, out_norm_lru_g, w_out, b_out, final_norm_g):
    batch, seq, d = x.shape
    assert w_in.shape[0] == 1, "single-layer block only"
    xs = x.reshape(batch * seq, d)
    mod3 = _adaln_mod(c, w_ada, b_ada)
    z, sg = _in_proj(xs, mod3, norm_g, w_in[0], b_in, batch=batch, seq=seq, tm=512)
    yp = _pool_mixer(z, sg, w_pool[0], b_pool[0], pool_scale, out_norm_pool_g, batch=batch, seq=seq, ts=512)
    xc, hf = _lru_fwd(z, conv_w[0], conv_b, w_gate[0], b_gate[0], lru_lambda[0], batch=batch, seq=seq, tc=512)
    out = _lru_bwd_out_proj(xc, hf, sg, w_gate[0], b_gate[0], lru_lambda[0], out_norm_lru_g, yp, xs, mod3,
                            w_out[0], b_out, final_norm_g[None, :], batch=batch, seq=seq, tm=512)
    return out.reshape(batch, seq, d)
```

```python
import functools
import math

import jax
import jax.numpy as jnp
from jax import lax
from jax.experimental import pallas as pl
from jax.experimental.pallas import tpu as pltpu

EPS = 1e-6
LRU_C = 8.0
POOL_WINDOWS = (2, 4, 8, 16)
N_LRU_HEADS = 4
CONV_WIDTH = 4
HALO = 8
SUBLANES = 8
BF16_ROW_TILE = 16
V7X_VMEM_LIMIT_BYTES = 56 * 1024 * 1024
TIME_TILE_ROWS = 512
N_Z_BLOCKS = 4
W_IN_STAGE_COLS = 512
WEIGHT_STAGE_ROWS = 256
LOG2E = math.log2(math.e)
LN2 = math.log(2.0)

F32 = jnp.float32
BF16 = jnp.bfloat16


def _params(semantics):
    return pltpu.CompilerParams(dimension_semantics=semantics,
                                vmem_limit_bytes=V7X_VMEM_LIMIT_BYTES)


def _sigmoid(v):
    return 0.5 * jnp.tanh(0.5 * v) + 0.5


def _silu(v):
    return v * _sigmoid(v)


def _shift_rows(v, k):
    rows, width = v.shape
    v3 = v.reshape(rows // SUBLANES, SUBLANES, width)
    if k % SUBLANES == 0:
        g = (k // SUBLANES) % v3.shape[0]
        return jnp.concatenate([v3[-g:], v3[:-g]], axis=0).reshape(rows, width) if g else v
    sub = lax.broadcasted_iota(jnp.int32, v3.shape, 1)
    if k > 0:
        rot = pltpu.roll(v3, k, 1)
        other = jnp.concatenate([rot[-1:], rot[:-1]], axis=0)
        out = jnp.where(sub >= k, rot, other)
    else:
        rot = pltpu.roll(v3, SUBLANES + k, 1)
        other = jnp.concatenate([rot[1:], rot[:1]], axis=0)
        out = jnp.where(sub < SUBLANES + k, rot, other)
    return out.reshape(rows, width)


def _weight_stream(w_hbm, w_vmem, stage, sem):
    slots, rows, _ = stage.shape
    n_chunks = w_hbm.shape[0] // rows

    def copy(c):
        return pltpu.make_async_copy(w_hbm.at[pl.ds(c * rows, rows), :], stage.at[c % slots], sem.at[c % slots])

    def begin():
        for c in range(min(slots, n_chunks)):
            copy(c).start()

    def advance(lo, hi):
        for c in range(lo, hi):
            copy(c).wait()
            w_vmem[c * rows:(c + 1) * rows, :] = stage[c % slots].astype(BF16)
            if c + slots < n_chunks:
                copy(c + slots).start()

    return begin, advance, n_chunks


def _weight_scratch(w):
    k, n = w.shape
    assert k % WEIGHT_STAGE_ROWS == 0
    return [pltpu.VMEM((k, n), BF16), pltpu.VMEM((2, WEIGHT_STAGE_ROWS, n), F32), pltpu.SemaphoreType.DMA((2,))]


ADALN_STAGE_SLOTS = 4


def _adaln_kernel(c_ref, w_hbm, b_ref, o_ref, stage, sem):
    batch, d = c_ref.shape
    slots, rows, n = stage.shape
    n_chunks = d // rows
    ca = _silu(c_ref[...]).astype(BF16)
    ca = jnp.concatenate([ca, jnp.zeros((-batch % BF16_ROW_TILE, d), BF16)], axis=0)

    def copy(k):
        return pltpu.make_async_copy(w_hbm.at[0, pl.ds(k * rows, rows), :], stage.at[k % slots], sem.at[k % slots])

    for k in range(min(slots - 1, n_chunks)):
        copy(k).start()
    acc = jnp.zeros((ca.shape[0], n), F32)
    for k in range(n_chunks):
        if k + slots - 1 < n_chunks:
            copy(k + slots - 1).start()
        copy(k).wait()
        acc = acc + jnp.dot(ca[:, k * rows:(k + 1) * rows], stage[k % slots].astype(BF16),
                            preferred_element_type=F32)
    res = acc[:batch] + b_ref[...]
    for part in range(o_ref.shape[0]):
        o_ref[part] = res[:, part * d:(part + 1) * d]


def _adaln_mod(c, w_ada, b_ada):
    batch, d = c.shape
    n = w_ada.shape[2]
    assert d % WEIGHT_STAGE_ROWS == 0 and n == 3 * d
    return pl.pallas_call(
        _adaln_kernel,
        in_specs=[pl.BlockSpec(memory_space=pltpu.VMEM),
                  pl.BlockSpec(memory_space=pl.ANY),
                  pl.BlockSpec(memory_space=pltpu.VMEM)],
        out_specs=pl.BlockSpec(memory_space=pltpu.VMEM),
        out_shape=jax.ShapeDtypeStruct((3, batch, d), F32),
        scratch_shapes=[pltpu.VMEM((ADALN_STAGE_SLOTS, WEIGHT_STAGE_ROWS, n), F32),
                        pltpu.SemaphoreType.DMA((ADALN_STAGE_SLOTS,))],
        compiler_params=pltpu.CompilerParams(vmem_limit_bytes=V7X_VMEM_LIMIT_BYTES),
        name="adaln_mod",
    )(c, w_ada, b_ada)


def _in_proj_kernel(x_ref, mod_ref, g_ref, w_hbm, b_ref, u_ref, sg_ref, w_ref, stage, sem, *, n_chunk):
    first = (pl.program_id(0) == 0) & (pl.program_id(1) == 0)
    n = w_ref.shape[1]
    n_mix = u_ref.shape[1]

    def normalised_input():
        x = x_ref[...]
        r = lax.rsqrt(jnp.mean(x * x, axis=-1, keepdims=True) + EPS)
        b = pl.program_id(0)
        shift = mod_ref[0, pl.ds(b, 1), :]
        scale = mod_ref[1, pl.ds(b, 1), :]
        return ((x * r) * (g_ref[...] * (1.0 + scale)) + shift).astype(BF16)

    def project(hb, sl):
        zj = jnp.dot(hb, w_ref[:, sl], preferred_element_type=F32) + b_ref[:, sl]
        if sl.start < n_mix:
            u_ref[:, sl] = zj
        else:
            sg_ref[:, sl.start - n_mix:sl.stop - n_mix] = _silu(zj).astype(sg_ref.dtype)

    @pl.when(first)
    def _():
        cols = stage.shape[2]

        def copy(c):
            return pltpu.make_async_copy(w_hbm.at[:, pl.ds(c * cols, cols)], stage.at[c % 2], sem.at[c % 2])

        copy(0).start()
        hb = normalised_input()
        for c in range(n // cols):
            if c + 1 < n // cols:
                copy(c + 1).start()
            copy(c).wait()
            sl = slice(c * cols, (c + 1) * cols)
            w_ref[:, sl] = stage[c % 2].astype(BF16)
            project(hb, sl)

    @pl.when(jnp.logical_not(first))
    def _():
        hb = normalised_input()
        for j in range(n // n_chunk):
            project(hb, slice(j * n_chunk, (j + 1) * n_chunk))


def _in_proj(x2, mod3, norm_g, w_in, b_in, *, batch, seq, tm):
    d = x2.shape[1]
    n = w_in.shape[1]
    nt = seq // tm
    rows = batch * seq
    return pl.pallas_call(
        functools.partial(_in_proj_kernel, n_chunk=n // N_Z_BLOCKS),
        grid=(batch, nt),
        in_specs=[pl.BlockSpec((tm, d), lambda b, i: (b * nt + i, 0)),
                  pl.BlockSpec(mod3.shape, lambda b, i: (0, 0, 0)),
                  pl.BlockSpec((1, d), lambda b, i: (0, 0)),
                  pl.BlockSpec(memory_space=pl.ANY),
                  pl.BlockSpec((1, n), lambda b, i: (0, 0))],
        out_specs=[pl.BlockSpec((tm, n // 2), lambda b, i: (b * nt + i, 0)),
                   pl.BlockSpec((tm, n // 2), lambda b, i: (b * nt + i, 0))],
        out_shape=[jax.ShapeDtypeStruct((rows, n // 2), F32), jax.ShapeDtypeStruct((rows, n // 2), BF16)],
        scratch_shapes=[pltpu.VMEM(w_in.shape, BF16),
                        pltpu.VMEM((2, d, W_IN_STAGE_COLS), F32),
                        pltpu.SemaphoreType.DMA((2,))],
        compiler_params=_params(("arbitrary", "arbitrary")),
        name="in_proj",
    )(x2, mod3, norm_g, w_in, b_in)


def _halo_specs(width, col_block, *, seq, ts, total_rows, tile_of):
    per_seq = seq // HALO
    per_tile = ts // HALO
    last = total_rows // HALO - 1

    def prev_map(*idx):
        b, i = idx[0], tile_of(*idx)
        return (jnp.maximum(b * per_seq + i * per_tile - 1, 0), col_block)

    def next_map(*idx):
        b, i = idx[0], tile_of(*idx)
        return (jnp.minimum(b * per_seq + (i + 1) * per_tile, last), col_block)

    return (pl.BlockSpec((HALO, width), prev_map), pl.BlockSpec((HALO, width), next_map))


def _with_halo(prev_ref, cur_ref, next_ref, tile, n_tiles):
    prev = jnp.where(tile > 0, prev_ref[...], 0.0)
    nxt = jnp.where(tile < n_tiles - 1, next_ref[...], 0.0)
    return jnp.concatenate([prev, cur_ref[...], nxt], axis=0)


def _pool_kernel(prev_ref, u_ref, next_ref, gate_ref, w_ref, b_ref, scale_ref, ng_ref, o_ref,
                 *, seq, ts, n_tiles):
    i = pl.program_id(1)
    ue = _with_halo(prev_ref, u_ref, next_ref, i, n_tiles)
    width = ue.shape[1]
    grp = width // len(POOL_WINDOWS)
    t = i * ts + lax.broadcasted_iota(jnp.int32, (ts, 1), 0)
    mixed = []
    for g, w in enumerate(POOL_WINDOWS):
        cols = slice(g * grp, (g + 1) * grp)
        e = ue[:, cols]
        lo = w // 2
        hi = w - lo - 1
        s, span = e, 1
        while span < w:
            s = s + _shift_rows(s, span)
            span *= 2
        total = (_shift_rows(s, -hi) if hi else s)[HALO:HALO + ts]
        cnt = (jnp.minimum(t + hi, seq - 1) - jnp.maximum(t - lo, 0) + 1).astype(F32)
        pooled = total / cnt - e[HALO:HALO + ts]
        m = jnp.dot(pooled.astype(BF16), w_ref[g].astype(BF16), preferred_element_type=F32)
        mixed.append((m + b_ref[g:g + 1, :]) * scale_ref[:, cols])
    y = jnp.concatenate(mixed, axis=-1)
    r = lax.rsqrt(jnp.mean(y * y, axis=-1, keepdims=True) + EPS)
    o_ref[...] = ((y * r) * ng_ref[...] * gate_ref[...].astype(F32)).astype(o_ref.dtype)


def _pool_mixer(z, sg, w_pool, b_pool, pool_scale, out_norm_g, *, batch, seq, ts):
    width = pool_scale.shape[1]
    n_tiles = seq // ts
    prev_spec, next_spec = _halo_specs(width, 0, seq=seq, ts=ts, total_rows=batch * seq,
                                       tile_of=lambda b, i: i)
    n_grp, grp, _ = w_pool.shape
    return pl.pallas_call(
        functools.partial(_pool_kernel, seq=seq, ts=ts, n_tiles=n_tiles),
        grid=(batch, n_tiles),
        in_specs=[prev_spec,
                  pl.BlockSpec((ts, width), lambda b, i: (b * n_tiles + i, 0)),
                  next_spec,
                  pl.BlockSpec((ts, width), lambda b, i: (b * n_tiles + i, 0)),
                  pl.BlockSpec((n_grp, grp, grp), lambda b, i: (0, 0, 0)),
                  pl.BlockSpec((n_grp, grp), lambda b, i: (0, 0)),
                  pl.BlockSpec((1, width), lambda b, i: (0, 0)),
                  pl.BlockSpec((1, width), lambda b, i: (0, 0))],
        out_specs=pl.BlockSpec((ts, width), lambda b, i: (b * n_tiles + i, 0)),
        out_shape=jax.ShapeDtypeStruct((batch * seq, width), BF16),
        compiler_params=_params(("arbitrary", "arbitrary")),
        name="pool_mixer",
    )(z, z, z, sg, w_pool, b_pool, pool_scale, out_norm_g)


def _softplus(v):
    return jnp.maximum(v, 0.0) + jnp.log1p(jnp.exp(-jnp.abs(v)))


def _lru_direction(xc_half, wg_ref, bg_half, rate2, carry, h_ref, *, reverse, after_gates=None):
    tc, width = xc_half.shape
    head = width // N_LRU_HEADS
    groups = tc // SUBLANES
    entry = SUBLANES - 1 if reverse else 0
    sub = lax.broadcasted_iota(jnp.int32, (groups, SUBLANES, head), 1)
    is_entry = sub == entry

    def back(v, d, axis):
        return pltpu.roll(v, SUBLANES - d if reverse else d, axis)

    carries = []
    for h in range(N_LRU_HEADS):
        cols = slice(h * head, (h + 1) * head)
        xh = xc_half[:, cols]
        g = jnp.dot(xh.astype(BF16), wg_ref[h], preferred_element_type=F32) + bg_half[h:h + 1, :]
        if after_gates is not None:
            after_gates(h)
        tr = jnp.tanh(g[:, :head])
        ti = jnp.tanh(g[:, head:])
        log2a = tr * rate2[:, cols] + rate2[:, cols]
        a = jnp.exp2(log2a)
        one_minus_a2 = (a * a + 1.0) * jnp.tanh(log2a * (-LN2))
        mult = jnp.where(one_minus_a2 > 0.0, one_minus_a2 * lax.rsqrt(one_minus_a2), 0.0)
        b = mult * (ti + 1.0) * xh

        a3 = a.reshape(groups, SUBLANES, head)
        b3 = b.reshape(groups, SUBLANES, head)
        a_in = jnp.where(is_entry, a3, 0.0)
        a1 = jnp.where(is_entry, 0.0, a3)
        a2 = a1 * back(a1, 1, 1)
        a4 = a2 * back(a2, 2, 1)
        state = jnp.broadcast_to(carry[:, cols], (SUBLANES, head))
        order = range(groups - 1, -1, -1) if reverse else range(groups)
        for gi in order:
            s = b3[gi] + a_in[gi] * back(state, 1, 0)
            s = s + a1[gi] * back(s, 1, 0)
            s = s + a2[gi] * back(s, 2, 0)
            state = s + a4[gi] * back(s, 4, 0)
            h_ref[gi * SUBLANES:(gi + 1) * SUBLANES, cols] = state
        exit_row = SUBLANES - 1 - entry
        carries.append(state[exit_row:exit_row + 1, :])
    return jnp.concatenate(carries, axis=1)


def _lru_rates(lam_row, bg_ref):
    rate2 = (-0.5 * LRU_C * LOG2E) * _softplus(-lam_row)
    return rate2, 0.5 * bg_ref[0]


def _cast_gate_weights(wg_ref, wgb_ref):
    for h in range(N_LRU_HEADS):
        wgb_ref[h] = wg_ref[0, h].astype(BF16)


def _lru_fwd_kernel(prev_ref, u_ref, next_ref, cw_ref, cb_ref, wg_ref, bg_ref, lam_ref,
                    xc_ref, hf_ref, carry_ref, wgb_ref, *, tc, n_chunks):
    c = pl.program_id(1)

    @pl.when((pl.program_id(0) == 0) & (c == 0))
    def _():
        _cast_gate_weights(wg_ref, wgb_ref)

    @pl.when(c == 0)
    def _():
        carry_ref[...] = jnp.zeros_like(carry_ref)

    rate2, bg_half = _lru_rates(lam_ref[0:1, :], bg_ref)
    ue = _with_halo(prev_ref, u_ref, next_ref, c, n_chunks)
    left = CONV_WIDTH // 2
    xc_half = 0.5 * cb_ref[...]
    for k in range(CONV_WIDTH):
        tap = ue if k == left else _shift_rows(ue, left - k)
        xc_half = xc_half + tap[HALO:HALO + tc] * (0.5 * cw_ref[k:k + 1, :])
    xc_ref[...] = xc_half
    carry_ref[...] = _lru_direction(xc_half, wgb_ref, bg_half, rate2, carry_ref[...], hf_ref, reverse=False)


def _lru_fwd(z, conv_w, conv_b, w_gate, b_gate, lru_lambda, *, batch, seq, tc):
    width = conv_b.shape[1]
    n_chunks = seq // tc
    prev_spec, next_spec = _halo_specs(width, 1, seq=seq, ts=tc, total_rows=batch * seq,
                                       tile_of=lambda b, c: c)
    _, n_head, head, two_head = w_gate.shape
    tile = pl.BlockSpec((tc, width), lambda b, c: (b * n_chunks + c, 0))
    return pl.pallas_call(
        functools.partial(_lru_fwd_kernel, tc=tc, n_chunks=n_chunks),
        grid=(batch, n_chunks),
        in_specs=[prev_spec,
                  pl.BlockSpec((tc, width), lambda b, c: (b * n_chunks + c, 1)),
                  next_spec,
                  pl.BlockSpec((CONV_WIDTH, width), lambda b, c: (0, 0)),
                  pl.BlockSpec((1, width), lambda b, c: (0, 0)),
                  pl.BlockSpec((1, n_head, head, two_head), lambda b, c: (0, 0, 0, 0)),
                  pl.BlockSpec((1, n_head, two_head), lambda b, c: (0, 0, 0)),
                  pl.BlockSpec(lru_lambda.shape, lambda b, c: (0, 0))],
        out_specs=[tile, tile],
        out_shape=[jax.ShapeDtypeStruct((batch * seq, width), F32)] * 2,
        scratch_shapes=[pltpu.VMEM((1, width), F32),
                        pltpu.VMEM((n_head, head, two_head), BF16)],
        compiler_params=_params(("arbitrary", "arbitrary")),
        name="lru_fwd",
    )(z, z, z, conv_w, conv_b, w_gate, b_gate, lru_lambda)


def _lru_bwd_out_kernel(xc_ref, hf_ref, gate_ref, wg_ref, bg_ref, lam_ref, ng_ref,
                        yp_ref, x_ref, mod_ref, w_hbm, b_ref, fg_ref,
                        o_ref, hb_ref, yl_ref, y_ref, carry_ref, wgb_ref, w_ref, stage, sem, *, n_tiles,
                        tiles_per_seq):
    s = pl.program_id(0)
    kp = yp_ref.shape[1]
    n_chunk = o_ref.shape[1] // N_LRU_HEADS

    def project_chunk(h):
        cols = slice(h * n_chunk, (h + 1) * n_chunk)
        y = jnp.dot(yp_ref[...], w_ref[:kp, cols], preferred_element_type=F32)
        y = y + jnp.dot(yl_ref[...], w_ref[kp:, cols], preferred_element_type=F32)
        y_ref[:, cols] = y + b_ref[:, cols]

    def scan(carry_in, after_gates):
        rate2, bg_half = _lru_rates(lam_ref[1:2, :], bg_ref)
        carry_ref[...] = _lru_direction(xc_ref[...], wgb_ref, bg_half, rate2, carry_in, hb_ref, reverse=True,
                                        after_gates=after_gates)

    def finish_scan():
        yl = hf_ref[...] + hb_ref[...]
        rl = lax.rsqrt(jnp.mean(yl * yl, axis=-1, keepdims=True) + EPS)
        yl_ref[...] = ((yl * rl) * ng_ref[...] * gate_ref[...].astype(F32)).astype(yl_ref.dtype)

    def finish_projection():
        gate = mod_ref[2, pl.ds((n_tiles - s) // tiles_per_seq, 1), :]
        xn = x_ref[...] + gate * y_ref[...]
        r = lax.rsqrt(jnp.mean(xn * xn, axis=-1, keepdims=True) + EPS)
        o_ref[...] = (xn * r) * fg_ref[...]

    @pl.when(s == 0)
    def _():
        _cast_gate_weights(wg_ref, wgb_ref)
        begin, advance, n_chunks = _weight_stream(w_hbm, w_ref, stage, sem)
        per_head = n_chunks // N_LRU_HEADS
        begin()
        scan(jnp.zeros_like(carry_ref), lambda h: advance(h * per_head, (h + 1) * per_head))
        advance(N_LRU_HEADS * per_head, n_chunks)
        finish_scan()

    @pl.when((s > 0) & (s < n_tiles))
    def _():
        j = n_tiles - 1 - s
        scan(jnp.where(j % tiles_per_seq == tiles_per_seq - 1, 0.0, carry_ref[...]), project_chunk)
        finish_projection()
        finish_scan()

    @pl.when(s == n_tiles)
    def _():
        for h in range(N_LRU_HEADS):
            project_chunk(h)
        finish_projection()


def _lru_bwd_out_proj(xc, hf, sg, w_gate, b_gate, lru_lambda, out_norm_g, yp, x2, mod3, w_out, b_out,
                      final_g, *, batch, seq, tm):
    d = x2.shape[1]
    width = xc.shape[1]
    tiles_per_seq = seq // tm
    n_tiles = batch * tiles_per_seq
    _, n_head, head, two_head = w_gate.shape

    def scan_tile(s):
        return jnp.maximum(n_tiles - 1 - s, 0)

    def proj_tile(s):
        return jnp.minimum(n_tiles - s, n_tiles - 1)

    return pl.pallas_call(
        functools.partial(_lru_bwd_out_kernel, n_tiles=n_tiles, tiles_per_seq=tiles_per_seq),
        grid=(n_tiles + 1,),
        in_specs=[pl.BlockSpec((tm, width), lambda s: (scan_tile(s), 0)),
                  pl.BlockSpec((tm, width), lambda s: (scan_tile(s), 0)),
                  pl.BlockSpec((tm, width), lambda s: (scan_tile(s), 1)),
                  pl.BlockSpec((1, n_head, head, two_head), lambda s: (1, 0, 0, 0)),
                  pl.BlockSpec((1, n_head, two_head), lambda s: (1, 0, 0)),
                  pl.BlockSpec(lru_lambda.shape, lambda s: (0, 0)),
                  pl.BlockSpec((1, width), lambda s: (0, 0)),
                  pl.BlockSpec((tm, yp.shape[1]), lambda s: (proj_tile(s), 0)),
                  pl.BlockSpec((tm, d), lambda s: (proj_tile(s), 0)),
                  pl.BlockSpec(mod3.shape, lambda s: (0, 0, 0)),
                  pl.BlockSpec(memory_space=pl.ANY),
                  pl.BlockSpec((1, d), lambda s: (0, 0)),
                  pl.BlockSpec((1, d), lambda s: (0, 0))],
        out_specs=pl.BlockSpec((tm, d), lambda s: (proj_tile(s), 0)),
        out_shape=jax.ShapeDtypeStruct((batch * seq, d), F32),
        scratch_shapes=[pltpu.VMEM((tm, width), F32),
                        pltpu.VMEM((tm, width), BF16),
                        pltpu.VMEM((tm, d), F32),
                        pltpu.VMEM((1, width), F32),
                        pltpu.VMEM((n_head, head, two_head), BF16),
                        *_weight_scratch(w_out)],
        compiler_params=_params(("arbitrary",)),
        name="lru_bwd_out_proj",
    )(xc, hf, sg, w_gate, b_gate, lru_lambda, out_norm_g, yp, x2, mod3, w_out, b_out, final_g)


def kernel(x, c, norm_g, w_ada, b_ada, w_in, b_in, w_pool, b_pool, pool_scale, conv_w, conv_b, w_gate,
           b_gate, lru_lambda, out_norm_pool_g, out_norm_lru_g, w_out, b_out, final_norm_g):
    batch, seq, d = x.shape
    assert w_in.shape[0] == 1, "single-layer block only"
    xs = x.reshape(batch * seq, d)
    mod3 = _adaln_mod(c, w_ada, b_ada)
    tile = TIME_TILE_ROWS
    z, sg = _in_proj(xs, mod3, norm_g, w_in[0], b_in, batch=batch, seq=seq, tm=tile)
    yp = _pool_mixer(z, sg, w_pool[0], b_pool[0], pool_scale, out_norm_pool_g, batch=batch, seq=seq, ts=tile)
    xc, hf = _lru_fwd(z, conv_w[0], conv_b, w_gate[0], b_gate[0], lru_lambda[0], batch=batch, seq=seq, tc=tile)
    out = _lru_bwd_out_proj(xc, hf, sg, w_gate[0], b_gate[0], lru_lambda[0], out_norm_lru_g, yp, xs, mod3,
                            w_out[0], b_out, final_norm_g[None, :], batch=batch, seq=seq, tm=tile)
    return out.reshape(batch, seq, d)
```

```python
import functools
import math

import jax
import jax.numpy as jnp
from jax import lax
from jax.experimental import pallas as pl
from jax.experimental.pallas import tpu as pltpu

EPS = 1e-6
LRU_C = 8.0
POOL_WINDOWS = (2, 4, 8, 16)
N_LRU_HEADS = 4
CONV_WIDTH = 4
HALO = 8
SUBLANES = 8
BF16_ROW_TILE = 16
V7X_VMEM_LIMIT_BYTES = 56 * 1024 * 1024
TIME_TILE_ROWS = 512
POOL_TILE_ROWS = 1024
N_Z_BLOCKS = 4
W_IN_STAGE_COLS = 512
WEIGHT_STAGE_ROWS = 256
LOG2E = math.log2(math.e)
LN2 = math.log(2.0)

F32 = jnp.float32
BF16 = jnp.bfloat16


def _params(semantics):
    return pltpu.CompilerParams(dimension_semantics=semantics,
                                vmem_limit_bytes=V7X_VMEM_LIMIT_BYTES)


def _sigmoid(v):
    return 0.5 * jnp.tanh(0.5 * v) + 0.5


def _silu(v):
    return v * _sigmoid(v)


def _shift_rows(v, k):
    rows, width = v.shape
    v3 = v.reshape(rows // SUBLANES, SUBLANES, width)
    if k % SUBLANES == 0:
        g = (k // SUBLANES) % v3.shape[0]
        return jnp.concatenate([v3[-g:], v3[:-g]], axis=0).reshape(rows, width) if g else v
    sub = lax.broadcasted_iota(jnp.int32, v3.shape, 1)
    if k > 0:
        rot = pltpu.roll(v3, k, 1)
        other = jnp.concatenate([rot[-1:], rot[:-1]], axis=0)
        out = jnp.where(sub >= k, rot, other)
    else:
        rot = pltpu.roll(v3, SUBLANES + k, 1)
        other = jnp.concatenate([rot[1:], rot[:1]], axis=0)
        out = jnp.where(sub < SUBLANES + k, rot, other)
    return out.reshape(rows, width)


def _weight_stream(w_hbm, w_vmem, stage, sem):
    slots, rows, _ = stage.shape
    n_chunks = w_hbm.shape[0] // rows

    def copy(c):
        return pltpu.make_async_copy(w_hbm.at[pl.ds(c * rows, rows), :], stage.at[c % slots], sem.at[c % slots])

    def begin():
        for c in range(min(slots, n_chunks)):
            copy(c).start()

    def advance(lo, hi):
        for c in range(lo, hi):
            copy(c).wait()
            w_vmem[c * rows:(c + 1) * rows, :] = stage[c % slots].astype(BF16)
            if c + slots < n_chunks:
                copy(c + slots).start()

    return begin, advance, n_chunks


def _weight_scratch(w):
    k, n = w.shape
    assert k % WEIGHT_STAGE_ROWS == 0
    return [pltpu.VMEM((k, n), BF16), pltpu.VMEM((2, WEIGHT_STAGE_ROWS, n), F32), pltpu.SemaphoreType.DMA((2,))]


ADALN_STAGE_SLOTS = 4


def _adaln_kernel(c_ref, w_hbm, b_ref, o_ref, stage, sem):
    batch, d = c_ref.shape
    slots, rows, n = stage.shape
    n_chunks = d // rows
    ca = _silu(c_ref[...]).astype(BF16)
    ca = jnp.concatenate([ca, jnp.zeros((-batch % BF16_ROW_TILE, d), BF16)], axis=0)

    def copy(k):
        return pltpu.make_async_copy(w_hbm.at[0, pl.ds(k * rows, rows), :], stage.at[k % slots], sem.at[k % slots])

    for k in range(min(slots - 1, n_chunks)):
        copy(k).start()
    acc = jnp.zeros((ca.shape[0], n), F32)
    for k in range(n_chunks):
        if k + slots - 1 < n_chunks:
            copy(k + slots - 1).start()
        copy(k).wait()
        acc = acc + jnp.dot(ca[:, k * rows:(k + 1) * rows], stage[k % slots].astype(BF16),
                            preferred_element_type=F32)
    res = acc[:batch] + b_ref[...]
    for part in range(o_ref.shape[0]):
        o_ref[part] = res[:, part * d:(part + 1) * d]


def _adaln_mod(c, w_ada, b_ada):
    batch, d = c.shape
    n = w_ada.shape[2]
    assert d % WEIGHT_STAGE_ROWS == 0 and n == 3 * d
    return pl.pallas_call(
        _adaln_kernel,
        in_specs=[pl.BlockSpec(memory_space=pltpu.VMEM),
                  pl.BlockSpec(memory_space=pl.ANY),
                  pl.BlockSpec(memory_space=pltpu.VMEM)],
        out_specs=pl.BlockSpec(memory_space=pltpu.VMEM),
        out_shape=jax.ShapeDtypeStruct((3, batch, d), F32),
        scratch_shapes=[pltpu.VMEM((ADALN_STAGE_SLOTS, WEIGHT_STAGE_ROWS, n), F32),
                        pltpu.SemaphoreType.DMA((ADALN_STAGE_SLOTS,))],
        compiler_params=pltpu.CompilerParams(vmem_limit_bytes=V7X_VMEM_LIMIT_BYTES),
        name="adaln_mod",
    )(c, w_ada, b_ada)


def _in_proj_kernel(x_ref, mod_ref, g_ref, w_hbm, b_ref, u_ref, sg_ref, w_ref, stage, sem, *, n_chunk):
    first = (pl.program_id(0) == 0) & (pl.program_id(1) == 0)
    n = w_ref.shape[1]
    n_mix = u_ref.shape[1]

    def normalised_input():
        x = x_ref[...]
        r = lax.rsqrt(jnp.mean(x * x, axis=-1, keepdims=True) + EPS)
        b = pl.program_id(0)
        shift = mod_ref[0, pl.ds(b, 1), :]
        scale = mod_ref[1, pl.ds(b, 1), :]
        return ((x * r) * (g_ref[...] * (1.0 + scale)) + shift).astype(BF16)

    def project(hb, sl):
        zj = jnp.dot(hb, w_ref[:, sl], preferred_element_type=F32) + b_ref[:, sl]
        if sl.start < n_mix:
            u_ref[:, sl] = zj
        else:
            sg_ref[:, sl.start - n_mix:sl.stop - n_mix] = _silu(zj).astype(sg_ref.dtype)

    @pl.when(first)
    def _():
        cols = stage.shape[2]

        def copy(c):
            return pltpu.make_async_copy(w_hbm.at[:, pl.ds(c * cols, cols)], stage.at[c % 2], sem.at[c % 2])

        copy(0).start()
        hb = normalised_input()
        for c in range(n // cols):
            if c + 1 < n // cols:
                copy(c + 1).start()
            copy(c).wait()
            sl = slice(c * cols, (c + 1) * cols)
            w_ref[:, sl] = stage[c % 2].astype(BF16)
            project(hb, sl)

    @pl.when(jnp.logical_not(first))
    def _():
        hb = normalised_input()
        for j in range(n // n_chunk):
            project(hb, slice(j * n_chunk, (j + 1) * n_chunk))


def _in_proj(x2, mod3, norm_g, w_in, b_in, *, batch, seq, tm):
    d = x2.shape[1]
    n = w_in.shape[1]
    nt = seq // tm
    rows = batch * seq
    return pl.pallas_call(
        functools.partial(_in_proj_kernel, n_chunk=n // N_Z_BLOCKS),
        grid=(batch, nt),
        in_specs=[pl.BlockSpec((tm, d), lambda b, i: (b * nt + i, 0)),
                  pl.BlockSpec(mod3.shape, lambda b, i: (0, 0, 0)),
                  pl.BlockSpec((1, d), lambda b, i: (0, 0)),
                  pl.BlockSpec(memory_space=pl.ANY),
                  pl.BlockSpec((1, n), lambda b, i: (0, 0))],
        out_specs=[pl.BlockSpec((tm, n // 2), lambda b, i: (b * nt + i, 0)),
                   pl.BlockSpec((tm, n // 2), lambda b, i: (b * nt + i, 0))],
        out_shape=[jax.ShapeDtypeStruct((rows, n // 2), F32), jax.ShapeDtypeStruct((rows, n // 2), BF16)],
        scratch_shapes=[pltpu.VMEM(w_in.shape, BF16),
                        pltpu.VMEM((2, d, W_IN_STAGE_COLS), F32),
                        pltpu.SemaphoreType.DMA((2,))],
        compiler_params=_params(("arbitrary", "arbitrary")),
        name="in_proj",
    )(x2, mod3, norm_g, w_in, b_in)


def _halo_specs(width, col_block, *, seq, ts, total_rows, tile_of):
    per_seq = seq // HALO
    per_tile = ts // HALO
    last = total_rows // HALO - 1

    def prev_map(*idx):
        b, i = idx[0], tile_of(*idx)
        return (jnp.maximum(b * per_seq + i * per_tile - 1, 0), col_block)

    def next_map(*idx):
        b, i = idx[0], tile_of(*idx)
        return (jnp.minimum(b * per_seq + (i + 1) * per_tile, last), col_block)

    return (pl.BlockSpec((HALO, width), prev_map), pl.BlockSpec((HALO, width), next_map))


def _with_halo(prev_ref, cur_ref, next_ref, tile, n_tiles):
    prev = jnp.where(tile > 0, prev_ref[...], 0.0)
    nxt = jnp.where(tile < n_tiles - 1, next_ref[...], 0.0)
    return jnp.concatenate([prev, cur_ref[...], nxt], axis=0)


def _pool_kernel(prev_ref, u_ref, next_ref, gate_ref, w_ref, b_ref, scale_ref, ng_ref, o_ref,
                 *, seq, ts, n_tiles):
    i = pl.program_id(1)
    ue = _with_halo(prev_ref, u_ref, next_ref, i, n_tiles)
    width = ue.shape[1]
    grp = width // len(POOL_WINDOWS)
    t = i * ts + lax.broadcasted_iota(jnp.int32, (ts, 1), 0)
    mixed = []
    for g, w in enumerate(POOL_WINDOWS):
        cols = slice(g * grp, (g + 1) * grp)
        e = ue[:, cols]
        lo = w // 2
        hi = w - lo - 1
        s, span = e, 1
        while span < w:
            s = s + _shift_rows(s, span)
            span *= 2
        total = (_shift_rows(s, -hi) if hi else s)[HALO:HALO + ts]
        cnt = (jnp.minimum(t + hi, seq - 1) - jnp.maximum(t - lo, 0) + 1).astype(F32)
        pooled = total / cnt - e[HALO:HALO + ts]
        m = jnp.dot(pooled.astype(BF16), w_ref[g].astype(BF16), preferred_element_type=F32)
        mixed.append((m + b_ref[g:g + 1, :]) * scale_ref[:, cols])
    y = jnp.concatenate(mixed, axis=-1)
    r = lax.rsqrt(jnp.mean(y * y, axis=-1, keepdims=True) + EPS)
    o_ref[...] = ((y * r) * ng_ref[...] * gate_ref[...].astype(F32)).astype(o_ref.dtype)


def _pool_mixer(z, sg, w_pool, b_pool, pool_scale, out_norm_g, *, batch, seq, ts):
    width = pool_scale.shape[1]
    n_tiles = seq // ts
    prev_spec, next_spec = _halo_specs(width, 0, seq=seq, ts=ts, total_rows=batch * seq,
                                       tile_of=lambda b, i: i)
    n_grp, grp, _ = w_pool.shape
    return pl.pallas_call(
        functools.partial(_pool_kernel, seq=seq, ts=ts, n_tiles=n_tiles),
        grid=(batch, n_tiles),
        in_specs=[prev_spec,
                  pl.BlockSpec((ts, width), lambda b, i: (b * n_tiles + i, 0)),
                  next_spec,
                  pl.BlockSpec((ts, width), lambda b, i: (b * n_tiles + i, 0)),
                  pl.BlockSpec((n_grp, grp, grp), lambda b, i: (0, 0, 0)),
                  pl.BlockSpec((n_grp, grp), lambda b, i: (0, 0)),
                  pl.BlockSpec((1, width), lambda b, i: (0, 0)),
                  pl.BlockSpec((1, width), lambda b, i: (0, 0))],
        out_specs=pl.BlockSpec((ts, width), lambda b, i: (b * n_tiles + i, 0)),
        out_shape=jax.ShapeDtypeStruct((batch * seq, width), BF16),
        compiler_params=_params(("arbitrary", "arbitrary")),
        name="pool_mixer",
    )(z, z, z, sg, w_pool, b_pool, pool_scale, out_norm_g)


def _softplus(v):
    return jnp.maximum(v, 0.0) + jnp.log1p(jnp.exp(-jnp.abs(v)))


def _lru_direction(xc_half, wg_ref, bg_half, rate2, carry, h_ref, *, reverse, after_gates=None):
    tc, width = xc_half.shape
    head = width // N_LRU_HEADS
    groups = tc // SUBLANES
    entry = SUBLANES - 1 if reverse else 0
    sub = lax.broadcasted_iota(jnp.int32, (groups, SUBLANES, head), 1)
    is_entry = sub == entry

    def back(v, d, axis):
        return pltpu.roll(v, SUBLANES - d if reverse else d, axis)

    carries = []
    for h in range(N_LRU_HEADS):
        cols = slice(h * head, (h + 1) * head)
        xh = xc_half[:, cols]
        g = jnp.dot(xh.astype(BF16), wg_ref[h], preferred_element_type=F32) + bg_half[h:h + 1, :]
        if after_gates is not None:
            after_gates(h)
        tr = jnp.tanh(g[:, :head])
        ti = jnp.tanh(g[:, head:])
        log2a = tr * rate2[:, cols] + rate2[:, cols]
        a = jnp.exp2(log2a)
        one_minus_a2 = (a * a + 1.0) * jnp.tanh(log2a * (-LN2))
        mult = jnp.where(one_minus_a2 > 0.0, one_minus_a2 * lax.rsqrt(one_minus_a2), 0.0)
        b = mult * (ti + 1.0) * xh

        a3 = a.reshape(groups, SUBLANES, head)
        b3 = b.reshape(groups, SUBLANES, head)
        a_in = jnp.where(is_entry, a3, 0.0)
        a1 = jnp.where(is_entry, 0.0, a3)
        a2 = a1 * back(a1, 1, 1)
        a4 = a2 * back(a2, 2, 1)
        state = jnp.broadcast_to(carry[:, cols], (SUBLANES, head))
        order = range(groups - 1, -1, -1) if reverse else range(groups)
        for gi in order:
            s = b3[gi] + a_in[gi] * back(state, 1, 0)
            s = s + a1[gi] * back(s, 1, 0)
            s = s + a2[gi] * back(s, 2, 0)
            state = s + a4[gi] * back(s, 4, 0)
            h_ref[gi * SUBLANES:(gi + 1) * SUBLANES, cols] = state
        exit_row = SUBLANES - 1 - entry
        carries.append(state[exit_row:exit_row + 1, :])
    return jnp.concatenate(carries, axis=1)


def _lru_rates(lam_row, bg_ref):
    rate2 = (-0.5 * LRU_C * LOG2E) * _softplus(-lam_row)
    return rate2, 0.5 * bg_ref[0]


def _cast_gate_weights(wg_ref, wgb_ref):
    for h in range(N_LRU_HEADS):
        wgb_ref[h] = wg_ref[0, h].astype(BF16)


def _lru_fwd_kernel(prev_ref, u_ref, next_ref, cw_ref, cb_ref, wg_ref, bg_ref, lam_ref,
                    xc_ref, hf_ref, carry_ref, wgb_ref, *, tc, n_chunks):
    c = pl.program_id(1)

    @pl.when((pl.program_id(0) == 0) & (c == 0))
    def _():
        _cast_gate_weights(wg_ref, wgb_ref)

    @pl.when(c == 0)
    def _():
        carry_ref[...] = jnp.zeros_like(carry_ref)

    rate2, bg_half = _lru_rates(lam_ref[0:1, :], bg_ref)
    ue = _with_halo(prev_ref, u_ref, next_ref, c, n_chunks)
    left = CONV_WIDTH // 2
    xc_half = 0.5 * cb_ref[...]
    for k in range(CONV_WIDTH):
        tap = ue if k == left else _shift_rows(ue, left - k)
        xc_half = xc_half + tap[HALO:HALO + tc] * (0.5 * cw_ref[k:k + 1, :])
    xc_ref[...] = xc_half
    carry_ref[...] = _lru_direction(xc_half, wgb_ref, bg_half, rate2, carry_ref[...], hf_ref, reverse=False)


def _lru_fwd(z, conv_w, conv_b, w_gate, b_gate, lru_lambda, *, batch, seq, tc):
    width = conv_b.shape[1]
    n_chunks = seq // tc
    prev_spec, next_spec = _halo_specs(width, 1, seq=seq, ts=tc, total_rows=batch * seq,
                                       tile_of=lambda b, c: c)
    _, n_head, head, two_head = w_gate.shape
    tile = pl.BlockSpec((tc, width), lambda b, c: (b * n_chunks + c, 0))
    return pl.pallas_call(
        functools.partial(_lru_fwd_kernel, tc=tc, n_chunks=n_chunks),
        grid=(batch, n_chunks),
        in_specs=[prev_spec,
                  pl.BlockSpec((tc, width), lambda b, c: (b * n_chunks + c, 1)),
                  next_spec,
                  pl.BlockSpec((CONV_WIDTH, width), lambda b, c: (0, 0)),
                  pl.BlockSpec((1, width), lambda b, c: (0, 0)),
                  pl.BlockSpec((1, n_head, head, two_head), lambda b, c: (0, 0, 0, 0)),
                  pl.BlockSpec((1, n_head, two_head), lambda b, c: (0, 0, 0)),
                  pl.BlockSpec(lru_lambda.shape, lambda b, c: (0, 0))],
        out_specs=[tile, tile],
        out_shape=[jax.ShapeDtypeStruct((batch * seq, width), F32)] * 2,
        scratch_shapes=[pltpu.VMEM((1, width), F32),
                        pltpu.VMEM((n_head, head, two_head), BF16)],
        compiler_params=_params(("arbitrary", "arbitrary")),
        name="lru_fwd",
    )(z, z, z, conv_w, conv_b, w_gate, b_gate, lru_lambda)


def _lru_bwd_out_kernel(xc_ref, hf_ref, gate_ref, wg_ref, bg_ref, lam_ref, ng_ref,
                        yp_ref, x_ref, mod_ref, w_hbm, b_ref, fg_ref,
                        o_ref, hb_ref, yl_ref, y_ref, carry_ref, wgb_ref, w_ref, stage, sem, *, n_tiles,
                        tiles_per_seq):
    s = pl.program_id(0)
    kp = yp_ref.shape[1]
    n_chunk = o_ref.shape[1] // N_LRU_HEADS

    def project_chunk(h):
        cols = slice(h * n_chunk, (h + 1) * n_chunk)
        y = jnp.dot(yp_ref[...], w_ref[:kp, cols], preferred_element_type=F32)
        y = y + jnp.dot(yl_ref[...], w_ref[kp:, cols], preferred_element_type=F32)
        y_ref[:, cols] = y + b_ref[:, cols]

    def scan(carry_in, after_gates):
        rate2, bg_half = _lru_rates(lam_ref[1:2, :], bg_ref)
        carry_ref[...] = _lru_direction(xc_ref[...], wgb_ref, bg_half, rate2, carry_in, hb_ref, reverse=True,
                                        after_gates=after_gates)

    def finish_scan():
        yl = hf_ref[...] + hb_ref[...]
        rl = lax.rsqrt(jnp.mean(yl * yl, axis=-1, keepdims=True) + EPS)
        yl_ref[...] = ((yl * rl) * ng_ref[...] * gate_ref[...].astype(F32)).astype(yl_ref.dtype)

    def finish_projection():
        gate = mod_ref[2, pl.ds((n_tiles - s) // tiles_per_seq, 1), :]
        xn = x_ref[...] + gate * y_ref[...]
        r = lax.rsqrt(jnp.mean(xn * xn, axis=-1, keepdims=True) + EPS)
        o_ref[...] = (xn * r) * fg_ref[...]

    @pl.when(s == 0)
    def _():
        _cast_gate_weights(wg_ref, wgb_ref)
        begin, advance, n_chunks = _weight_stream(w_hbm, w_ref, stage, sem)
        per_head = n_chunks // N_LRU_HEADS
        begin()
        scan(jnp.zeros_like(carry_ref), lambda h: advance(h * per_head, (h + 1) * per_head))
        advance(N_LRU_HEADS * per_head, n_chunks)
        finish_scan()

    @pl.when((s > 0) & (s < n_tiles))
    def _():
        j = n_tiles - 1 - s
        scan(jnp.where(j % tiles_per_seq == tiles_per_seq - 1, 0.0, carry_ref[...]), project_chunk)
        finish_projection()
        finish_scan()

    @pl.when(s == n_tiles)
    def _():
        for h in range(N_LRU_HEADS):
            project_chunk(h)
        finish_projection()


def _lru_bwd_out_proj(xc, hf, sg, w_gate, b_gate, lru_lambda, out_norm_g, yp, x2, mod3, w_out, b_out,
                      final_g, *, batch, seq, tm):
    d = x2.shape[1]
    width = xc.shape[1]
    tiles_per_seq = seq // tm
    n_tiles = batch * tiles_per_seq
    _, n_head, head, two_head = w_gate.shape

    def scan_tile(s):
        return jnp.maximum(n_tiles - 1 - s, 0)

    def proj_tile(s):
        return jnp.minimum(n_tiles - s, n_tiles - 1)

    return pl.pallas_call(
        functools.partial(_lru_bwd_out_kernel, n_tiles=n_tiles, tiles_per_seq=tiles_per_seq),
        grid=(n_tiles + 1,),
        in_specs=[pl.BlockSpec((tm, width), lambda s: (scan_tile(s), 0)),
                  pl.BlockSpec((tm, width), lambda s: (scan_tile(s), 0)),
                  pl.BlockSpec((tm, width), lambda s: (scan_tile(s), 1)),
                  pl.BlockSpec((1, n_head, head, two_head), lambda s: (1, 0, 0, 0)),
                  pl.BlockSpec((1, n_head, two_head), lambda s: (1, 0, 0)),
                  pl.BlockSpec(lru_lambda.shape, lambda s: (0, 0)),
                  pl.BlockSpec((1, width), lambda s: (0, 0)),
                  pl.BlockSpec((tm, yp.shape[1]), lambda s: (proj_tile(s), 0)),
                  pl.BlockSpec((tm, d), lambda s: (proj_tile(s), 0)),
                  pl.BlockSpec(mod3.shape, lambda s: (0, 0, 0)),
                  pl.BlockSpec(memory_space=pl.ANY),
                  pl.BlockSpec((1, d), lambda s: (0, 0)),
                  pl.BlockSpec((1, d), lambda s: (0, 0))],
        out_specs=pl.BlockSpec((tm, d), lambda s: (proj_tile(s), 0)),
        out_shape=jax.ShapeDtypeStruct((batch * seq, d), F32),
        scratch_shapes=[pltpu.VMEM((tm, width), F32),
                        pltpu.VMEM((tm, width), BF16),
                        pltpu.VMEM((tm, d), F32),
                        pltpu.VMEM((1, width), F32),
                        pltpu.VMEM((n_head, head, two_head), BF16),
                        *_weight_scratch(w_out)],
        compiler_params=_params(("arbitrary",)),
        name="lru_bwd_out_proj",
    )(xc, hf, sg, w_gate, b_gate, lru_lambda, out_norm_g, yp, x2, mod3, w_out, b_out, final_g)


def kernel(x, c, norm_g, w_ada, b_ada, w_in, b_in, w_pool, b_pool, pool_scale, conv_w, conv_b, w_gate,
           b_gate, lru_lambda, out_norm_pool_g, out_norm_lru_g, w_out, b_out, final_norm_g):
    batch, seq, d = x.shape
    assert w_in.shape[0] == 1, "single-layer block only"
    xs = x.reshape(batch * seq, d)
    mod3 = _adaln_mod(c, w_ada, b_ada)
    tile = TIME_TILE_ROWS
    z, sg = _in_proj(xs, mod3, norm_g, w_in[0], b_in, batch=batch, seq=seq, tm=tile)
    yp = _pool_mixer(z, sg, w_pool[0], b_pool[0], pool_scale, out_norm_pool_g, batch=batch, seq=seq,
                     ts=POOL_TILE_ROWS)
    xc, hf = _lru_fwd(z, conv_w[0], conv_b, w_gate[0], b_gate[0], lru_lambda[0], batch=batch, seq=seq, tc=tile)
    out = _lru_bwd_out_proj(xc, hf, sg, w_gate[0], b_gate[0], lru_lambda[0], out_norm_lru_g, yp, xs, mod3,
                            w_out[0], b_out, final_norm_g[None, :], batch=batch, seq=seq, tm=tile)
    return out.reshape(batch, seq, d)
```

```python
import functools
import math

import jax
import jax.numpy as jnp
from jax import lax
from jax.experimental import pallas as pl
from jax.experimental.pallas import tpu as pltpu

EPS = 1e-6
LRU_C = 8.0
POOL_WINDOWS = (2, 4, 8, 16)
N_LRU_HEADS = 4
CONV_WIDTH = 4
HALO = 8
SUBLANES = 8
BF16_ROW_TILE = 16
V7X_VMEM_LIMIT_BYTES = 56 * 1024 * 1024
TIME_TILE_ROWS = 512
FWD_SCAN_TILE_ROWS = 1024
POOL_TILE_ROWS = 2048
N_Z_BLOCKS = 4
W_IN_STAGE_COLS = 512
WEIGHT_STAGE_ROWS = 256
LOG2E = math.log2(math.e)
LN2 = math.log(2.0)

F32 = jnp.float32
BF16 = jnp.bfloat16


def _params(semantics):
    return pltpu.CompilerParams(dimension_semantics=semantics,
                                vmem_limit_bytes=V7X_VMEM_LIMIT_BYTES)


def _sigmoid(v):
    return 0.5 * jnp.tanh(0.5 * v) + 0.5


def _silu(v):
    return v * _sigmoid(v)


def _shift_rows(v, k):
    rows, width = v.shape
    v3 = v.reshape(rows // SUBLANES, SUBLANES, width)
    if k % SUBLANES == 0:
        g = (k // SUBLANES) % v3.shape[0]
        return jnp.concatenate([v3[-g:], v3[:-g]], axis=0).reshape(rows, width) if g else v
    sub = lax.broadcasted_iota(jnp.int32, v3.shape, 1)
    if k > 0:
        rot = pltpu.roll(v3, k, 1)
        other = jnp.concatenate([rot[-1:], rot[:-1]], axis=0)
        out = jnp.where(sub >= k, rot, other)
    else:
        rot = pltpu.roll(v3, SUBLANES + k, 1)
        other = jnp.concatenate([rot[1:], rot[:1]], axis=0)
        out = jnp.where(sub < SUBLANES + k, rot, other)
    return out.reshape(rows, width)


def _weight_stream(w_hbm, w_vmem, stage, sem):
    slots, rows, _ = stage.shape
    n_chunks = w_hbm.shape[0] // rows

    def copy(c):
        return pltpu.make_async_copy(w_hbm.at[pl.ds(c * rows, rows), :], stage.at[c % slots], sem.at[c % slots])

    def begin():
        for c in range(min(slots, n_chunks)):
            copy(c).start()

    def advance(lo, hi):
        for c in range(lo, hi):
            copy(c).wait()
            w_vmem[c * rows:(c + 1) * rows, :] = stage[c % slots].astype(BF16)
            if c + slots < n_chunks:
                copy(c + slots).start()

    return begin, advance, n_chunks


def _weight_scratch(w):
    k, n = w.shape
    assert k % WEIGHT_STAGE_ROWS == 0
    return [pltpu.VMEM((k, n), BF16), pltpu.VMEM((2, WEIGHT_STAGE_ROWS, n), F32), pltpu.SemaphoreType.DMA((2,))]


ADALN_STAGE_SLOTS = 4


def _adaln_kernel(c_ref, w_hbm, b_ref, o_ref, stage, sem):
    batch, d = c_ref.shape
    slots, rows, n = stage.shape
    n_chunks = d // rows
    ca = _silu(c_ref[...]).astype(BF16)
    ca = jnp.concatenate([ca, jnp.zeros((-batch % BF16_ROW_TILE, d), BF16)], axis=0)

    def copy(k):
        return pltpu.make_async_copy(w_hbm.at[0, pl.ds(k * rows, rows), :], stage.at[k % slots], sem.at[k % slots])

    for k in range(min(slots - 1, n_chunks)):
        copy(k).start()
    acc = jnp.zeros((ca.shape[0], n), F32)
    for k in range(n_chunks):
        if k + slots - 1 < n_chunks:
            copy(k + slots - 1).start()
        copy(k).wait()
        acc = acc + jnp.dot(ca[:, k * rows:(k + 1) * rows], stage[k % slots].astype(BF16),
                            preferred_element_type=F32)
    res = acc[:batch] + b_ref[...]
    for part in range(o_ref.shape[0]):
        o_ref[part] = res[:, part * d:(part + 1) * d]


def _adaln_mod(c, w_ada, b_ada):
    batch, d = c.shape
    n = w_ada.shape[2]
    assert d % WEIGHT_STAGE_ROWS == 0 and n == 3 * d
    return pl.pallas_call(
        _adaln_kernel,
        in_specs=[pl.BlockSpec(memory_space=pltpu.VMEM),
                  pl.BlockSpec(memory_space=pl.ANY),
                  pl.BlockSpec(memory_space=pltpu.VMEM)],
        out_specs=pl.BlockSpec(memory_space=pltpu.VMEM),
        out_shape=jax.ShapeDtypeStruct((3, batch, d), F32),
        scratch_shapes=[pltpu.VMEM((ADALN_STAGE_SLOTS, WEIGHT_STAGE_ROWS, n), F32),
                        pltpu.SemaphoreType.DMA((ADALN_STAGE_SLOTS,))],
        compiler_params=pltpu.CompilerParams(vmem_limit_bytes=V7X_VMEM_LIMIT_BYTES),
        name="adaln_mod",
    )(c, w_ada, b_ada)


def _in_proj_kernel(x_ref, mod_ref, g_ref, w_hbm, b_ref, u_ref, sg_ref, w_ref, stage, sem, *, n_chunk):
    first = (pl.program_id(0) == 0) & (pl.program_id(1) == 0)
    n = w_ref.shape[1]
    n_mix = u_ref.shape[1]

    def normalised_input():
        x = x_ref[...]
        r = lax.rsqrt(jnp.mean(x * x, axis=-1, keepdims=True) + EPS)
        b = pl.program_id(0)
        shift = mod_ref[0, pl.ds(b, 1), :]
        scale = mod_ref[1, pl.ds(b, 1), :]
        return ((x * r) * (g_ref[...] * (1.0 + scale)) + shift).astype(BF16)

    def project(hb, sl):
        zj = jnp.dot(hb, w_ref[:, sl], preferred_element_type=F32) + b_ref[:, sl]
        if sl.start < n_mix:
            u_ref[:, sl] = zj
        else:
            sg_ref[:, sl.start - n_mix:sl.stop - n_mix] = _silu(zj).astype(sg_ref.dtype)

    @pl.when(first)
    def _():
        cols = stage.shape[2]

        def copy(c):
            return pltpu.make_async_copy(w_hbm.at[:, pl.ds(c * cols, cols)], stage.at[c % 2], sem.at[c % 2])

        copy(0).start()
        hb = normalised_input()
        for c in range(n // cols):
            if c + 1 < n // cols:
                copy(c + 1).start()
            copy(c).wait()
            sl = slice(c * cols, (c + 1) * cols)
            w_ref[:, sl] = stage[c % 2].astype(BF16)
            project(hb, sl)

    @pl.when(jnp.logical_not(first))
    def _():
        hb = normalised_input()
        for j in range(n // n_chunk):
            project(hb, slice(j * n_chunk, (j + 1) * n_chunk))


def _in_proj(x2, mod3, norm_g, w_in, b_in, *, batch, seq, tm):
    d = x2.shape[1]
    n = w_in.shape[1]
    nt = seq // tm
    rows = batch * seq
    return pl.pallas_call(
        functools.partial(_in_proj_kernel, n_chunk=n // N_Z_BLOCKS),
        grid=(batch, nt),
        in_specs=[pl.BlockSpec((tm, d), lambda b, i: (b * nt + i, 0)),
                  pl.BlockSpec(mod3.shape, lambda b, i: (0, 0, 0)),
                  pl.BlockSpec((1, d), lambda b, i: (0, 0)),
                  pl.BlockSpec(memory_space=pl.ANY),
                  pl.BlockSpec((1, n), lambda b, i: (0, 0))],
        out_specs=[pl.BlockSpec((tm, n // 2), lambda b, i: (b * nt + i, 0)),
                   pl.BlockSpec((tm, n // 2), lambda b, i: (b * nt + i, 0))],
        out_shape=[jax.ShapeDtypeStruct((rows, n // 2), F32), jax.ShapeDtypeStruct((rows, n // 2), BF16)],
        scratch_shapes=[pltpu.VMEM(w_in.shape, BF16),
                        pltpu.VMEM((2, d, W_IN_STAGE_COLS), F32),
                        pltpu.SemaphoreType.DMA((2,))],
        compiler_params=_params(("arbitrary", "arbitrary")),
        name="in_proj",
    )(x2, mod3, norm_g, w_in, b_in)


def _halo_specs(width, col_block, *, seq, ts, total_rows, tile_of):
    per_seq = seq // HALO
    per_tile = ts // HALO
    last = total_rows // HALO - 1

    def prev_map(*idx):
        b, i = idx[0], tile_of(*idx)
        return (jnp.maximum(b * per_seq + i * per_tile - 1, 0), col_block)

    def next_map(*idx):
        b, i = idx[0], tile_of(*idx)
        return (jnp.minimum(b * per_seq + (i + 1) * per_tile, last), col_block)

    return (pl.BlockSpec((HALO, width), prev_map), pl.BlockSpec((HALO, width), next_map))


def _with_halo(prev_ref, cur_ref, next_ref, tile, n_tiles):
    prev = jnp.where(tile > 0, prev_ref[...], 0.0)
    nxt = jnp.where(tile < n_tiles - 1, next_ref[...], 0.0)
    return jnp.concatenate([prev, cur_ref[...], nxt], axis=0)


def _pool_kernel(prev_ref, u_ref, next_ref, gate_ref, w_ref, b_ref, scale_ref, ng_ref, o_ref,
                 *, seq, ts, n_tiles):
    i = pl.program_id(1)
    ue = _with_halo(prev_ref, u_ref, next_ref, i, n_tiles)
    width = ue.shape[1]
    grp = width // len(POOL_WINDOWS)
    t = i * ts + lax.broadcasted_iota(jnp.int32, (ts, 1), 0)
    mixed = []
    for g, w in enumerate(POOL_WINDOWS):
        cols = slice(g * grp, (g + 1) * grp)
        e = ue[:, cols]
        lo = w // 2
        hi = w - lo - 1
        s, span = e, 1
        while span < w:
            s = s + _shift_rows(s, span)
            span *= 2
        total = (_shift_rows(s, -hi) if hi else s)[HALO:HALO + ts]
        cnt = (jnp.minimum(t + hi, seq - 1) - jnp.maximum(t - lo, 0) + 1).astype(F32)
        pooled = total / cnt - e[HALO:HALO + ts]
        m = jnp.dot(pooled.astype(BF16), w_ref[g].astype(BF16), preferred_element_type=F32)
        mixed.append((m + b_ref[g:g + 1, :]) * scale_ref[:, cols])
    y = jnp.concatenate(mixed, axis=-1)
    r = lax.rsqrt(jnp.mean(y * y, axis=-1, keepdims=True) + EPS)
    o_ref[...] = ((y * r) * ng_ref[...] * gate_ref[...].astype(F32)).astype(o_ref.dtype)


def _pool_mixer(z, sg, w_pool, b_pool, pool_scale, out_norm_g, *, batch, seq, ts):
    width = pool_scale.shape[1]
    n_tiles = seq // ts
    prev_spec, next_spec = _halo_specs(width, 0, seq=seq, ts=ts, total_rows=batch * seq,
                                       tile_of=lambda b, i: i)
    n_grp, grp, _ = w_pool.shape
    return pl.pallas_call(
        functools.partial(_pool_kernel, seq=seq, ts=ts, n_tiles=n_tiles),
        grid=(batch, n_tiles),
        in_specs=[prev_spec,
                  pl.BlockSpec((ts, width), lambda b, i: (b * n_tiles + i, 0)),
                  next_spec,
                  pl.BlockSpec((ts, width), lambda b, i: (b * n_tiles + i, 0)),
                  pl.BlockSpec((n_grp, grp, grp), lambda b, i: (0, 0, 0)),
                  pl.BlockSpec((n_grp, grp), lambda b, i: (0, 0)),
                  pl.BlockSpec((1, width), lambda b, i: (0, 0)),
                  pl.BlockSpec((1, width), lambda b, i: (0, 0))],
        out_specs=pl.BlockSpec((ts, width), lambda b, i: (b * n_tiles + i, 0)),
        out_shape=jax.ShapeDtypeStruct((batch * seq, width), BF16),
        compiler_params=_params(("arbitrary", "arbitrary")),
        name="pool_mixer",
    )(z, z, z, sg, w_pool, b_pool, pool_scale, out_norm_g)


def _softplus(v):
    return jnp.maximum(v, 0.0) + jnp.log1p(jnp.exp(-jnp.abs(v)))


def _lru_direction(xc_half, wg_ref, bg_half, rate2, carry, h_ref, *, reverse, after_gates=None):
    tc, width = xc_half.shape
    head = width // N_LRU_HEADS
    groups = tc // SUBLANES
    entry = SUBLANES - 1 if reverse else 0
    sub = lax.broadcasted_iota(jnp.int32, (groups, SUBLANES, head), 1)
    is_entry = sub == entry

    def back(v, d, axis):
        return pltpu.roll(v, SUBLANES - d if reverse else d, axis)

    carries = []
    for h in range(N_LRU_HEADS):
        cols = slice(h * head, (h + 1) * head)
        xh = xc_half[:, cols]
        g = jnp.dot(xh.astype(BF16), wg_ref[h], preferred_element_type=F32) + bg_half[h:h + 1, :]
        if after_gates is not None:
            after_gates(h)
        tr = jnp.tanh(g[:, :head])
        ti = jnp.tanh(g[:, head:])
        log2a = tr * rate2[:, cols] + rate2[:, cols]
        a = jnp.exp2(log2a)
        one_minus_a2 = (a * a + 1.0) * jnp.tanh(log2a * (-LN2))
        mult = jnp.where(one_minus_a2 > 0.0, one_minus_a2 * lax.rsqrt(one_minus_a2), 0.0)
        b = mult * (ti + 1.0) * xh

        a3 = a.reshape(groups, SUBLANES, head)
        b3 = b.reshape(groups, SUBLANES, head)
        a_in = jnp.where(is_entry, a3, 0.0)
        a1 = jnp.where(is_entry, 0.0, a3)
        a2 = a1 * back(a1, 1, 1)
        a4 = a2 * back(a2, 2, 1)
        state = jnp.broadcast_to(carry[:, cols], (SUBLANES, head))
        order = range(groups - 1, -1, -1) if reverse else range(groups)
        for gi in order:
            s = b3[gi] + a_in[gi] * back(state, 1, 0)
            s = s + a1[gi] * back(s, 1, 0)
            s = s + a2[gi] * back(s, 2, 0)
            state = s + a4[gi] * back(s, 4, 0)
            h_ref[gi * SUBLANES:(gi + 1) * SUBLANES, cols] = state
        exit_row = SUBLANES - 1 - entry
        carries.append(state[exit_row:exit_row + 1, :])
    return jnp.concatenate(carries, axis=1)


def _lru_rates(lam_row, bg_ref):
    rate2 = (-0.5 * LRU_C * LOG2E) * _softplus(-lam_row)
    return rate2, 0.5 * bg_ref[0]


def _cast_gate_weights(wg_ref, wgb_ref):
    for h in range(N_LRU_HEADS):
        wgb_ref[h] = wg_ref[0, h].astype(BF16)


def _lru_fwd_kernel(prev_ref, u_ref, next_ref, cw_ref, cb_ref, wg_ref, bg_ref, lam_ref,
                    xc_ref, hf_ref, carry_ref, wgb_ref, *, tc, n_chunks):
    c = pl.program_id(1)

    @pl.when((pl.program_id(0) == 0) & (c == 0))
    def _():
        _cast_gate_weights(wg_ref, wgb_ref)

    @pl.when(c == 0)
    def _():
        carry_ref[...] = jnp.zeros_like(carry_ref)

    rate2, bg_half = _lru_rates(lam_ref[0:1, :], bg_ref)
    ue = _with_halo(prev_ref, u_ref, next_ref, c, n_chunks)
    left = CONV_WIDTH // 2
    xc_half = 0.5 * cb_ref[...]
    for k in range(CONV_WIDTH):
        tap = ue if k == left else _shift_rows(ue, left - k)
        xc_half = xc_half + tap[HALO:HALO + tc] * (0.5 * cw_ref[k:k + 1, :])
    xc_ref[...] = xc_half
    carry_ref[...] = _lru_direction(xc_half, wgb_ref, bg_half, rate2, carry_ref[...], hf_ref, reverse=False)


def _lru_fwd(z, conv_w, conv_b, w_gate, b_gate, lru_lambda, *, batch, seq, tc):
    width = conv_b.shape[1]
    n_chunks = seq // tc
    prev_spec, next_spec = _halo_specs(width, 1, seq=seq, ts=tc, total_rows=batch * seq,
                                       tile_of=lambda b, c: c)
    _, n_head, head, two_head = w_gate.shape
    tile = pl.BlockSpec((tc, width), lambda b, c: (b * n_chunks + c, 0))
    return pl.pallas_call(
        functools.partial(_lru_fwd_kernel, tc=tc, n_chunks=n_chunks),
        grid=(batch, n_chunks),
        in_specs=[prev_spec,
                  pl.BlockSpec((tc, width), lambda b, c: (b * n_chunks + c, 1)),
                  next_spec,
                  pl.BlockSpec((CONV_WIDTH, width), lambda b, c: (0, 0)),
                  pl.BlockSpec((1, width), lambda b, c: (0, 0)),
                  pl.BlockSpec((1, n_head, head, two_head), lambda b, c: (0, 0, 0, 0)),
                  pl.BlockSpec((1, n_head, two_head), lambda b, c: (0, 0, 0)),
                  pl.BlockSpec(lru_lambda.shape, lambda b, c: (0, 0))],
        out_specs=[tile, tile],
        out_shape=[jax.ShapeDtypeStruct((batch * seq, width), F32)] * 2,
        scratch_shapes=[pltpu.VMEM((1, width), F32),
                        pltpu.VMEM((n_head, head, two_head), BF16)],
        compiler_params=_params(("arbitrary", "arbitrary")),
        name="lru_fwd",
    )(z, z, z, conv_w, conv_b, w_gate, b_gate, lru_lambda)


def _lru_bwd_out_kernel(xc_ref, hf_ref, gate_ref, wg_ref, bg_ref, lam_ref, ng_ref,
                        yp_ref, x_ref, mod_ref, w_hbm, b_ref, fg_ref,
                        o_ref, hb_ref, yl_ref, y_ref, carry_ref, wgb_ref, w_ref, stage, sem, *, n_tiles,
                        tiles_per_seq):
    s = pl.program_id(0)
    kp = yp_ref.shape[1]
    n_chunk = o_ref.shape[1] // N_LRU_HEADS

    def project_chunk(h):
        cols = slice(h * n_chunk, (h + 1) * n_chunk)
        y = jnp.dot(yp_ref[...], w_ref[:kp, cols], preferred_element_type=F32)
        y = y + jnp.dot(yl_ref[...], w_ref[kp:, cols], preferred_element_type=F32)
        y_ref[:, cols] = y + b_ref[:, cols]

    def scan(carry_in, after_gates):
        rate2, bg_half = _lru_rates(lam_ref[1:2, :], bg_ref)
        carry_ref[...] = _lru_direction(xc_ref[...], wgb_ref, bg_half, rate2, carry_in, hb_ref, reverse=True,
                                        after_gates=after_gates)

    def finish_scan():
        yl = hf_ref[...] + hb_ref[...]
        rl = lax.rsqrt(jnp.mean(yl * yl, axis=-1, keepdims=True) + EPS)
        yl_ref[...] = ((yl * rl) * ng_ref[...] * gate_ref[...].astype(F32)).astype(yl_ref.dtype)

    def finish_projection():
        gate = mod_ref[2, pl.ds((n_tiles - s) // tiles_per_seq, 1), :]
        xn = x_ref[...] + gate * y_ref[...]
        r = lax.rsqrt(jnp.mean(xn * xn, axis=-1, keepdims=True) + EPS)
        o_ref[...] = (xn * r) * fg_ref[...]

    @pl.when(s == 0)
    def _():
        _cast_gate_weights(wg_ref, wgb_ref)
        begin, advance, n_chunks = _weight_stream(w_hbm, w_ref, stage, sem)
        per_head = n_chunks // N_LRU_HEADS
        begin()
        scan(jnp.zeros_like(carry_ref), lambda h: advance(h * per_head, (h + 1) * per_head))
        advance(N_LRU_HEADS * per_head, n_chunks)
        finish_scan()

    @pl.when((s > 0) & (s < n_tiles))
    def _():
        j = n_tiles - 1 - s
        scan(jnp.where(j % tiles_per_seq == tiles_per_seq - 1, 0.0, carry_ref[...]), project_chunk)
        finish_projection()
        finish_scan()

    @pl.when(s == n_tiles)
    def _():
        for h in range(N_LRU_HEADS):
            project_chunk(h)
        finish_projection()


def _lru_bwd_out_proj(xc, hf, sg, w_gate, b_gate, lru_lambda, out_norm_g, yp, x2, mod3, w_out, b_out,
                      final_g, *, batch, seq, tm):
    d = x2.shape[1]
    width = xc.shape[1]
    tiles_per_seq = seq // tm
    n_tiles = batch * tiles_per_seq
    _, n_head, head, two_head = w_gate.shape

    def scan_tile(s):
        return jnp.maximum(n_tiles - 1 - s, 0)

    def proj_tile(s):
        return jnp.minimum(n_tiles - s, n_tiles - 1)

    return pl.pallas_call(
        functools.partial(_lru_bwd_out_kernel, n_tiles=n_tiles, tiles_per_seq=tiles_per_seq),
        grid=(n_tiles + 1,),
        in_specs=[pl.BlockSpec((tm, width), lambda s: (scan_tile(s), 0)),
                  pl.BlockSpec((tm, width), lambda s: (scan_tile(s), 0)),
                  pl.BlockSpec((tm, width), lambda s: (scan_tile(s), 1)),
                  pl.BlockSpec((1, n_head, head, two_head), lambda s: (1, 0, 0, 0)),
                  pl.BlockSpec((1, n_head, two_head), lambda s: (1, 0, 0)),
                  pl.BlockSpec(lru_lambda.shape, lambda s: (0, 0)),
                  pl.BlockSpec((1, width), lambda s: (0, 0)),
                  pl.BlockSpec((tm, yp.shape[1]), lambda s: (proj_tile(s), 0)),
                  pl.BlockSpec((tm, d), lambda s: (proj_tile(s), 0)),
                  pl.BlockSpec(mod3.shape, lambda s: (0, 0, 0)),
                  pl.BlockSpec(memory_space=pl.ANY),
                  pl.BlockSpec((1, d), lambda s: (0, 0)),
                  pl.BlockSpec((1, d), lambda s: (0, 0))],
        out_specs=pl.BlockSpec((tm, d), lambda s: (proj_tile(s), 0)),
        out_shape=jax.ShapeDtypeStruct((batch * seq, d), F32),
        scratch_shapes=[pltpu.VMEM((tm, width), F32),
                        pltpu.VMEM((tm, width), BF16),
                        pltpu.VMEM((tm, d), F32),
                        pltpu.VMEM((1, width), F32),
                        pltpu.VMEM((n_head, head, two_head), BF16),
                        *_weight_scratch(w_out)],
        compiler_params=_params(("arbitrary",)),
        name="lru_bwd_out_proj",
    )(xc, hf, sg, w_gate, b_gate, lru_lambda, out_norm_g, yp, x2, mod3, w_out, b_out, final_g)


def kernel(x, c, norm_g, w_ada, b_ada, w_in, b_in, w_pool, b_pool, pool_scale, conv_w, conv_b, w_gate,
           b_gate, lru_lambda, out_norm_pool_g, out_norm_lru_g, w_out, b_out, final_norm_g):
    batch, seq, d = x.shape
    assert w_in.shape[0] == 1, "single-layer block only"
    xs = x.reshape(batch * seq, d)
    mod3 = _adaln_mod(c, w_ada, b_ada)
    tile = TIME_TILE_ROWS
    z, sg = _in_proj(xs, mod3, norm_g, w_in[0], b_in, batch=batch, seq=seq, tm=tile)
    yp = _pool_mixer(z, sg, w_pool[0], b_pool[0], pool_scale, out_norm_pool_g, batch=batch, seq=seq,
                     ts=POOL_TILE_ROWS)
    xc, hf = _lru_fwd(z, conv_w[0], conv_b, w_gate[0], b_gate[0], lru_lambda[0], batch=batch, seq=seq,
                      tc=FWD_SCAN_TILE_ROWS)
    out = _lru_bwd_out_proj(xc, hf, sg, w_gate[0], b_gate[0], lru_lambda[0], out_norm_lru_g, yp, xs, mod3,
                            w_out[0], b_out, final_norm_g[None, :], batch=batch, seq=seq, tm=tile)
    return out.reshape(batch, seq, d)
```

```python
import functools
import math

import jax
import jax.numpy as jnp
from jax import lax
from jax.experimental import pallas as pl
from jax.experimental.pallas import tpu as pltpu

EPS = 1e-6
LRU_C = 8.0
POOL_WINDOWS = (2, 4, 8, 16)
N_LRU_HEADS = 4
CONV_WIDTH = 4
LANES = 128
SUBLANES = 8
BF16_ROW_TILE = 16
V7X_VMEM_LIMIT_BYTES = 56 * 1024 * 1024
TIME_TILE_ROWS = 512
POOL_TILE_ROWS = 1024
N_Z_BLOCKS = 4
W_IN_STAGE_COLS = 512
WEIGHT_STAGE_ROWS = 256
LOG2E = math.log2(math.e)
LN2 = math.log(2.0)

F32 = jnp.float32
BF16 = jnp.bfloat16


def _params(semantics):
    return pltpu.CompilerParams(dimension_semantics=semantics,
                                vmem_limit_bytes=V7X_VMEM_LIMIT_BYTES)


def _sigmoid(v):
    return 0.5 * jnp.tanh(0.5 * v) + 0.5


def _silu(v):
    return v * _sigmoid(v)


def _halo_rows(steps, batch):
    return -(-steps * batch // SUBLANES) * SUBLANES


def _shift_rows(v, k):
    rows, width = v.shape
    v3 = v.reshape(rows // SUBLANES, SUBLANES, width)
    q, r = divmod(k, SUBLANES)
    if q % v3.shape[0]:
        g = q % v3.shape[0]
        v3 = jnp.concatenate([v3[-g:], v3[:-g]], axis=0)
    if r:
        sub = lax.broadcasted_iota(jnp.int32, v3.shape, 1)
        rot = pltpu.roll(v3, r, 1)
        other = jnp.concatenate([rot[-1:], rot[:-1]], axis=0)
        v3 = jnp.where(sub >= r, rot, other)
    return v3.reshape(rows, width)


def _weight_stream(w_hbm, w_vmem, stage, sem):
    slots, rows, _ = stage.shape
    n_chunks = w_hbm.shape[0] // rows

    def copy(c):
        return pltpu.make_async_copy(w_hbm.at[pl.ds(c * rows, rows), :], stage.at[c % slots], sem.at[c % slots])

    def begin():
        for c in range(min(slots, n_chunks)):
            copy(c).start()

    def advance(lo, hi):
        for c in range(lo, hi):
            copy(c).wait()
            w_vmem[c * rows:(c + 1) * rows, :] = stage[c % slots].astype(BF16)
            if c + slots < n_chunks:
                copy(c + slots).start()

    return begin, advance, n_chunks


def _weight_scratch(w):
    k, n = w.shape
    assert k % WEIGHT_STAGE_ROWS == 0
    return [pltpu.VMEM((k, n), BF16), pltpu.VMEM((2, WEIGHT_STAGE_ROWS, n), F32), pltpu.SemaphoreType.DMA((2,))]


ADALN_STAGE_SLOTS = 4


def _adaln_kernel(c_ref, w_hbm, b_ref, o_ref, stage, sem):
    batch, d = c_ref.shape
    slots, rows, n = stage.shape
    n_chunks = d // rows
    ca = _silu(c_ref[...]).astype(BF16)
    ca = jnp.concatenate([ca, jnp.zeros((-batch % BF16_ROW_TILE, d), BF16)], axis=0)

    def copy(k):
        return pltpu.make_async_copy(w_hbm.at[0, pl.ds(k * rows, rows), :], stage.at[k % slots], sem.at[k % slots])

    for k in range(min(slots - 1, n_chunks)):
        copy(k).start()
    acc = jnp.zeros((ca.shape[0], n), F32)
    for k in range(n_chunks):
        if k + slots - 1 < n_chunks:
            copy(k + slots - 1).start()
        copy(k).wait()
        acc = acc + jnp.dot(ca[:, k * rows:(k + 1) * rows], stage[k % slots].astype(BF16),
                            preferred_element_type=F32)
    res = acc[:batch] + b_ref[...]
    for part in range(o_ref.shape[0]):
        o_ref[part] = res[:, part * d:(part + 1) * d]


def _adaln_mod(c, w_ada, b_ada):
    batch, d = c.shape
    n = w_ada.shape[2]
    assert d % WEIGHT_STAGE_ROWS == 0 and n == 3 * d
    return pl.pallas_call(
        _adaln_kernel,
        in_specs=[pl.BlockSpec(memory_space=pltpu.VMEM),
                  pl.BlockSpec(memory_space=pl.ANY),
                  pl.BlockSpec(memory_space=pltpu.VMEM)],
        out_specs=pl.BlockSpec(memory_space=pltpu.VMEM),
        out_shape=jax.ShapeDtypeStruct((3, batch, d), F32),
        scratch_shapes=[pltpu.VMEM((ADALN_STAGE_SLOTS, WEIGHT_STAGE_ROWS, n), F32),
                        pltpu.SemaphoreType.DMA((ADALN_STAGE_SLOTS,))],
        compiler_params=pltpu.CompilerParams(vmem_limit_bytes=V7X_VMEM_LIMIT_BYTES),
        name="adaln_mod",
    )(c, w_ada, b_ada)


def _in_proj_kernel(x_ref, mod_ref, g_ref, w_hbm, b_ref, u_ref, sg_ref, w_ref, stage, sem, hs_ref, *, n_chunk):
    first = pl.program_id(0) == 0
    n = w_ref.shape[1]
    n_mix = u_ref.shape[1]
    batch, tt, d = x_ref.shape

    def normalised_input():
        for b in range(batch):
            x = x_ref[b]
            r = lax.rsqrt(jnp.mean(x * x, axis=-1, keepdims=True) + EPS)
            h = (x * r) * (g_ref[...] * (1.0 + mod_ref[1, b:b + 1, :])) + mod_ref[0, b:b + 1, :]
            for c in range(d // LANES):
                hs_ref[c, pl.ds(b, tt, stride=batch), :] = h[:, c * LANES:(c + 1) * LANES]
        return jnp.concatenate([hs_ref[c] for c in range(d // LANES)], axis=1).astype(BF16)

    def project(hb, sl):
        zj = jnp.dot(hb, w_ref[:, sl], preferred_element_type=F32) + b_ref[:, sl]
        if sl.start < n_mix:
            u_ref[:, sl] = zj
        else:
            sg_ref[:, sl.start - n_mix:sl.stop - n_mix] = _silu(zj).astype(sg_ref.dtype)

    @pl.when(first)
    def _():
        cols = stage.shape[2]

        def copy(c):
            return pltpu.make_async_copy(w_hbm.at[:, pl.ds(c * cols, cols)], stage.at[c % 2], sem.at[c % 2])

        copy(0).start()
        hb = normalised_input()
        for c in range(n // cols):
            if c + 1 < n // cols:
                copy(c + 1).start()
            copy(c).wait()
            sl = slice(c * cols, (c + 1) * cols)
            w_ref[:, sl] = stage[c % 2].astype(BF16)
            project(hb, sl)

    @pl.when(jnp.logical_not(first))
    def _():
        hb = normalised_input()
        for j in range(n // n_chunk):
            project(hb, slice(j * n_chunk, (j + 1) * n_chunk))


def _in_proj(x, mod3, norm_g, w_in, b_in, *, tm):
    batch, seq, d = x.shape
    n = w_in.shape[1]
    tt = tm // batch
    rows = batch * seq
    return pl.pallas_call(
        functools.partial(_in_proj_kernel, n_chunk=n // N_Z_BLOCKS),
        grid=(seq // tt,),
        in_specs=[pl.BlockSpec((batch, tt, d), lambda i: (0, i, 0)),
                  pl.BlockSpec(mod3.shape, lambda i: (0, 0, 0)),
                  pl.BlockSpec((1, d), lambda i: (0, 0)),
                  pl.BlockSpec(memory_space=pl.ANY),
                  pl.BlockSpec((1, n), lambda i: (0, 0))],
        out_specs=[pl.BlockSpec((tm, n // 2), lambda i: (i, 0)),
                   pl.BlockSpec((tm, n // 2), lambda i: (i, 0))],
        out_shape=[jax.ShapeDtypeStruct((rows, n // 2), F32), jax.ShapeDtypeStruct((rows, n // 2), BF16)],
        scratch_shapes=[pltpu.VMEM(w_in.shape, BF16),
                        pltpu.VMEM((2, d, W_IN_STAGE_COLS), F32),
                        pltpu.SemaphoreType.DMA((2,)),
                        pltpu.VMEM((d // LANES, tm, LANES), F32)],
        compiler_params=_params(("arbitrary",)),
        name="in_proj",
    )(x, mod3, norm_g, w_in, b_in)


def _halo_specs(width, col_block, *, halo, ts, total_rows):
    per_tile = ts // halo
    last = total_rows // halo - 1
    return (pl.BlockSpec((halo, width), lambda i: (jnp.maximum(i * per_tile - 1, 0), col_block)),
            pl.BlockSpec((halo, width), lambda i: (jnp.minimum((i + 1) * per_tile, last), col_block)))


def _with_halo(prev_ref, cur_ref, next_ref, tile, n_tiles):
    prev = jnp.where(tile > 0, prev_ref[...], 0.0)
    nxt = jnp.where(tile < n_tiles - 1, next_ref[...], 0.0)
    return jnp.concatenate([prev, cur_ref[...], nxt], axis=0)


def _pool_kernel(prev_ref, u_ref, next_ref, gate_ref, w_ref, b_ref, scale_ref, ng_ref, o_ref,
                 *, batch, seq, ts, n_tiles):
    i = pl.program_id(0)
    halo = prev_ref.shape[0]
    ue = _with_halo(prev_ref, u_ref, next_ref, i, n_tiles)
    width = ue.shape[1]
    grp = width // len(POOL_WINDOWS)
    row = lax.broadcasted_iota(jnp.int32, (ts, 1), 0)
    t = i * (ts // batch) + row // batch
    mixed = []
    for g, w in enumerate(POOL_WINDOWS):
        cols = slice(g * grp, (g + 1) * grp)
        e = ue[:, cols]
        lo = w // 2
        hi = w - lo - 1
        s, span = e, 1
        while span < w:
            s = s + _shift_rows(s, span * batch)
            span *= 2
        total = (_shift_rows(s, -hi * batch) if hi else s)[halo:halo + ts]
        cnt = (jnp.minimum(t + hi, seq - 1) - jnp.maximum(t - lo, 0) + 1).astype(F32)
        pooled = total / cnt - e[halo:halo + ts]
        m = jnp.dot(pooled.astype(BF16), w_ref[g].astype(BF16), preferred_element_type=F32)
        mixed.append((m + b_ref[g:g + 1, :]) * scale_ref[:, cols])
    y = jnp.concatenate(mixed, axis=-1)
    r = lax.rsqrt(jnp.mean(y * y, axis=-1, keepdims=True) + EPS)
    o_ref[...] = ((y * r) * ng_ref[...] * gate_ref[...].astype(F32)).astype(o_ref.dtype)


def _pool_mixer(z, sg, w_pool, b_pool, pool_scale, out_norm_g, *, batch, seq, ts):
    width = pool_scale.shape[1]
    n_tiles = batch * seq // ts
    prev_spec, next_spec = _halo_specs(width, 0, halo=_halo_rows(max(POOL_WINDOWS) // 2, batch), ts=ts,
                                       total_rows=batch * seq)
    n_grp, grp, _ = w_pool.shape
    return pl.pallas_call(
        functools.partial(_pool_kernel, batch=batch, seq=seq, ts=ts, n_tiles=n_tiles),
        grid=(n_tiles,),
        in_specs=[prev_spec,
                  pl.BlockSpec((ts, width), lambda i: (i, 0)),
                  next_spec,
                  pl.BlockSpec((ts, width), lambda i: (i, 0)),
                  pl.BlockSpec((n_grp, grp, grp), lambda i: (0, 0, 0)),
                  pl.BlockSpec((n_grp, grp), lambda i: (0, 0)),
                  pl.BlockSpec((1, width), lambda i: (0, 0)),
                  pl.BlockSpec((1, width), lambda i: (0, 0))],
        out_specs=pl.BlockSpec((ts, width), lambda i: (i, 0)),
        out_shape=jax.ShapeDtypeStruct((batch * seq, width), BF16),
        compiler_params=_params(("arbitrary",)),
        name="pool_mixer",
    )(z, z, z, sg, w_pool, b_pool, pool_scale, out_norm_g)


def _softplus(v):
    return jnp.maximum(v, 0.0) + jnp.log1p(jnp.exp(-jnp.abs(v)))


def _lru_direction(xc_half, wg_ref, bg_half, rate2, carry_ref, h_ref, *, batch, reverse, after_gates=None):
    tc, width = xc_half.shape
    head = width // N_LRU_HEADS
    groups = tc // SUBLANES
    steps = SUBLANES // batch
    sub = lax.broadcasted_iota(jnp.int32, (groups, SUBLANES, head), 1)
    is_entry = sub >= SUBLANES - batch if reverse else sub < batch

    def back(v, d, axis):
        return pltpu.roll(v, SUBLANES - d * batch if reverse else d * batch, axis)

    for h in range(N_LRU_HEADS):
        cols = slice(h * head, (h + 1) * head)
        xh = xc_half[:, cols]
        g = jnp.dot(xh.astype(BF16), wg_ref[h], preferred_element_type=F32) + bg_half[h:h + 1, :]
        if after_gates is not None:
            after_gates(h)
        tr = jnp.tanh(g[:, :head])
        ti = jnp.tanh(g[:, head:])
        log2a = tr * rate2[:, cols] + rate2[:, cols]
        a = jnp.exp2(log2a)
        one_minus_a2 = (a * a + 1.0) * jnp.tanh(log2a * (-LN2))
        mult = jnp.where(one_minus_a2 > 0.0, one_minus_a2 * lax.rsqrt(one_minus_a2), 0.0)
        b = mult * (ti + 1.0) * xh

        a3 = a.reshape(groups, SUBLANES, head)
        b3 = b.reshape(groups, SUBLANES, head)
        a_in = jnp.where(is_entry, a3, 0.0)
        decays = [jnp.where(is_entry, 0.0, a3)]
        d = 1
        while 2 * d < steps:
            decays.append(decays[-1] * back(decays[-1], d, 1))
            d *= 2
        state = carry_ref[:, cols]
        order = range(groups - 1, -1, -1) if reverse else range(groups)
        for gi in order:
            state = b3[gi] + a_in[gi] * back(state, 1, 0)
            for level, decay in enumerate(decays):
                state = state + decay[gi] * back(state, 2 ** level, 0)
            h_ref[gi * SUBLANES:(gi + 1) * SUBLANES, cols] = state
        carry_ref[:, cols] = state


def _lru_rates(lam_row, bg_ref):
    rate2 = (-0.5 * LRU_C * LOG2E) * _softplus(-lam_row)
    return rate2, 0.5 * bg_ref[0]


def _cast_gate_weights(wg_ref, wgb_ref):
    for h in range(N_LRU_HEADS):
        wgb_ref[h] = wg_ref[0, h].astype(BF16)


def _lru_fwd_kernel(prev_ref, u_ref, next_ref, cw_ref, cb_ref, wg_ref, bg_ref, lam_ref,
                    xc_ref, hf_ref, carry_ref, wgb_ref, *, batch, tc, n_tiles):
    i = pl.program_id(0)

    @pl.when(i == 0)
    def _():
        _cast_gate_weights(wg_ref, wgb_ref)
        carry_ref[...] = jnp.zeros_like(carry_ref)

    rate2, bg_half = _lru_rates(lam_ref[0:1, :], bg_ref)
    ue = _with_halo(prev_ref, u_ref, next_ref, i, n_tiles)
    halo = prev_ref.shape[0]
    left = CONV_WIDTH // 2
    xc_half = 0.5 * cb_ref[...]
    for k in range(CONV_WIDTH):
        tap = ue if k == left else _shift_rows(ue, (left - k) * batch)
        xc_half = xc_half + tap[halo:halo + tc] * (0.5 * cw_ref[k:k + 1, :])
    xc_ref[...] = xc_half
    _lru_direction(xc_half, wgb_ref, bg_half, rate2, carry_ref, hf_ref, batch=batch, reverse=False)


def _lru_fwd(z, conv_w, conv_b, w_gate, b_gate, lru_lambda, *, batch, seq, tc):
    width = conv_b.shape[1]
    n_tiles = batch * seq // tc
    assert SUBLANES % batch == 0, "a group of 8 rows must hold whole time steps"
    prev_spec, next_spec = _halo_specs(width, 1, halo=_halo_rows(CONV_WIDTH // 2, batch), ts=tc,
                                       total_rows=batch * seq)
    _, n_head, head, two_head = w_gate.shape
    tile = pl.BlockSpec((tc, width), lambda i: (i, 0))
    return pl.pallas_call(
        functools.partial(_lru_fwd_kernel, batch=batch, tc=tc, n_tiles=n_tiles),
        grid=(n_tiles,),
        in_specs=[prev_spec,
                  pl.BlockSpec((tc, width), lambda i: (i, 1)),
                  next_spec,
                  pl.BlockSpec((CONV_WIDTH, width), lambda i: (0, 0)),
                  pl.BlockSpec((1, width), lambda i: (0, 0)),
                  pl.BlockSpec((1, n_head, head, two_head), lambda i: (0, 0, 0, 0)),
                  pl.BlockSpec((1, n_head, two_head), lambda i: (0, 0, 0)),
                  pl.BlockSpec(lru_lambda.shape, lambda i: (0, 0))],
        out_specs=[tile, tile],
        out_shape=[jax.ShapeDtypeStruct((batch * seq, width), F32)] * 2,
        scratch_shapes=[pltpu.VMEM((SUBLANES, width), F32),
                        pltpu.VMEM((n_head, head, two_head), BF16)],
        compiler_params=_params(("arbitrary",)),
        name="lru_fwd",
    )(z, z, z, conv_w, conv_b, w_gate, b_gate, lru_lambda)


def _lru_bwd_out_kernel(xc_ref, hf_ref, gate_ref, wg_ref, bg_ref, lam_ref, ng_ref,
                        yp_ref, x_ref, mod_ref, w_hbm, b_ref, fg_ref,
                        o_ref, hb_ref, yl_ref, y_ref, carry_ref, wgb_ref, w_ref, stage, sem, *, n_tiles):
    s = pl.program_id(0)
    kp = yp_ref.shape[1]
    batch, tt, d = x_ref.shape
    n_chunk = d // N_LRU_HEADS

    def project_chunk(h):
        cols = slice(h * n_chunk, (h + 1) * n_chunk)
        y = jnp.dot(yp_ref[...], w_ref[:kp, cols], preferred_element_type=F32)
        y = y + jnp.dot(yl_ref[...], w_ref[kp:, cols], preferred_element_type=F32)
        y = y + b_ref[:, cols]
        for c in range(n_chunk // LANES):
            y_ref[h * (n_chunk // LANES) + c] = y[:, c * LANES:(c + 1) * LANES]

    def scan(after_gates):
        rate2, bg_half = _lru_rates(lam_ref[1:2, :], bg_ref)
        _lru_direction(xc_ref[...], wgb_ref, bg_half, rate2, carry_ref, hb_ref, batch=batch, reverse=True,
                       after_gates=after_gates)

    def finish_scan():
        yl = hf_ref[...] + hb_ref[...]
        rl = lax.rsqrt(jnp.mean(yl * yl, axis=-1, keepdims=True) + EPS)
        yl_ref[...] = ((yl * rl) * ng_ref[...] * gate_ref[...].astype(F32)).astype(yl_ref.dtype)

    def finish_projection():
        for b in range(batch):
            y = jnp.concatenate([y_ref[c, pl.ds(b, tt, stride=batch), :] for c in range(d // LANES)], axis=1)
            xn = x_ref[b] + mod_ref[2, b:b + 1, :] * y
            r = lax.rsqrt(jnp.mean(xn * xn, axis=-1, keepdims=True) + EPS)
            o_ref[b] = (xn * r) * fg_ref[...]

    @pl.when(s == 0)
    def _():
        _cast_gate_weights(wg_ref, wgb_ref)
        carry_ref[...] = jnp.zeros_like(carry_ref)
        begin, advance, n_chunks = _weight_stream(w_hbm, w_ref, stage, sem)
        per_head = n_chunks // N_LRU_HEADS
        begin()
        scan(lambda h: advance(h * per_head, (h + 1) * per_head))
        advance(N_LRU_HEADS * per_head, n_chunks)
        finish_scan()

    @pl.when((s > 0) & (s < n_tiles))
    def _():
        scan(project_chunk)
        finish_projection()
        finish_scan()

    @pl.when(s == n_tiles)
    def _():
        for h in range(N_LRU_HEADS):
            project_chunk(h)
        finish_projection()


def _lru_bwd_out_proj(xc, hf, sg, w_gate, b_gate, lru_lambda, out_norm_g, yp, x, mod3, w_out, b_out,
                      final_g, *, tm):
    batch, seq, d = x.shape
    width = xc.shape[1]
    tt = tm // batch
    n_tiles = seq // tt
    _, n_head, head, two_head = w_gate.shape

    def scan_tile(s):
        return jnp.maximum(n_tiles - 1 - s, 0)

    def proj_tile(s):
        return jnp.minimum(n_tiles - s, n_tiles - 1)

    return pl.pallas_call(
        functools.partial(_lru_bwd_out_kernel, n_tiles=n_tiles),
        grid=(n_tiles + 1,),
        in_specs=[pl.BlockSpec((tm, width), lambda s: (scan_tile(s), 0)),
                  pl.BlockSpec((tm, width), lambda s: (scan_tile(s), 0)),
                  pl.BlockSpec((tm, width), lambda s: (scan_tile(s), 1)),
                  pl.BlockSpec((1, n_head, head, two_head), lambda s: (1, 0, 0, 0)),
                  pl.BlockSpec((1, n_head, two_head), lambda s: (1, 0, 0)),
                  pl.BlockSpec(lru_lambda.shape, lambda s: (0, 0)),
                  pl.BlockSpec((1, width), lambda s: (0, 0)),
                  pl.BlockSpec((tm, yp.shape[1]), lambda s: (proj_tile(s), 0)),
                  pl.BlockSpec((batch, tt, d), lambda s: (0, proj_tile(s), 0)),
                  pl.BlockSpec(mod3.shape, lambda s: (0, 0, 0)),
                  pl.BlockSpec(memory_space=pl.ANY),
                  pl.BlockSpec((1, d), lambda s: (0, 0)),
                  pl.BlockSpec((1, d), lambda s: (0, 0))],
        out_specs=pl.BlockSpec((batch, tt, d), lambda s: (0, proj_tile(s), 0)),
        out_shape=jax.ShapeDtypeStruct((batch, seq, d), F32),
        scratch_shapes=[pltpu.VMEM((tm, width), F32),
                        pltpu.VMEM((tm, width), BF16),
                        pltpu.VMEM((d // LANES, tm, LANES), F32),
                        pltpu.VMEM((SUBLANES, width), F32),
                        pltpu.VMEM((n_head, head, two_head), BF16),
                        *_weight_scratch(w_out)],
        compiler_params=_params(("arbitrary",)),
        name="lru_bwd_out_proj",
    )(xc, hf, sg, w_gate, b_gate, lru_lambda, out_norm_g, yp, x, mod3, w_out, b_out, final_g)


def kernel(x, c, norm_g, w_ada, b_ada, w_in, b_in, w_pool, b_pool, pool_scale, conv_w, conv_b, w_gate,
           b_gate, lru_lambda, out_norm_pool_g, out_norm_lru_g, w_out, b_out, final_norm_g):
    batch, seq, d = x.shape
    assert w_in.shape[0] == 1, "single-layer block only"
    mod3 = _adaln_mod(c, w_ada, b_ada)
    tile = TIME_TILE_ROWS
    z, sg = _in_proj(x, mod3, norm_g, w_in[0], b_in, tm=tile)
    yp = _pool_mixer(z, sg, w_pool[0], b_pool[0], pool_scale, out_norm_pool_g, batch=batch, seq=seq,
                     ts=POOL_TILE_ROWS)
    xc, hf = _lru_fwd(z, conv_w[0], conv_b, w_gate[0], b_gate[0], lru_lambda[0], batch=batch, seq=seq, tc=tile)
    return _lru_bwd_out_proj(xc, hf, sg, w_gate[0], b_gate[0], lru_lambda[0], out_norm_lru_g, yp, x, mod3,
                             w_out[0], b_out, final_norm_g[None, :], tm=tile)
```

```python
import functools
import math

import jax
import jax.numpy as jnp
from jax import lax
from jax.experimental import pallas as pl
from jax.experimental.pallas import tpu as pltpu

EPS = 1e-6
LRU_C = 8.0
POOL_WINDOWS = (2, 4, 8, 16)
N_LRU_HEADS = 4
CONV_WIDTH = 4
LANES = 128
SUBLANES = 8
BF16_ROW_TILE = 16
V7X_VMEM_LIMIT_BYTES = 56 * 1024 * 1024
TIME_TILE_ROWS = 512
POOL_TILE_ROWS = 1024
N_Z_BLOCKS = 4
W_IN_STAGE_COLS = 512
WEIGHT_STAGE_ROWS = 256
LOG2E = math.log2(math.e)
LN2 = math.log(2.0)
F32_MIN_NORMAL = float(jnp.finfo(jnp.float32).tiny)

F32 = jnp.float32
BF16 = jnp.bfloat16


def _params(semantics):
    return pltpu.CompilerParams(dimension_semantics=semantics,
                                vmem_limit_bytes=V7X_VMEM_LIMIT_BYTES)


def _sigmoid(v):
    return 0.5 * jnp.tanh(0.5 * v) + 0.5


def _silu(v):
    return v * _sigmoid(v)


def _halo_rows(steps, batch):
    return -(-steps * batch // SUBLANES) * SUBLANES


def _shift_rows(v, k):
    rows, width = v.shape
    v3 = v.reshape(rows // SUBLANES, SUBLANES, width)
    q, r = divmod(k, SUBLANES)
    if q % v3.shape[0]:
        g = q % v3.shape[0]
        v3 = jnp.concatenate([v3[-g:], v3[:-g]], axis=0)
    if r:
        sub = lax.broadcasted_iota(jnp.int32, v3.shape, 1)
        rot = pltpu.roll(v3, r, 1)
        other = jnp.concatenate([rot[-1:], rot[:-1]], axis=0)
        v3 = jnp.where(sub >= r, rot, other)
    return v3.reshape(rows, width)


def _weight_stream(w_hbm, w_vmem, stage, sem):
    slots, rows, _ = stage.shape
    n_chunks = w_hbm.shape[0] // rows

    def copy(c):
        return pltpu.make_async_copy(w_hbm.at[pl.ds(c * rows, rows), :], stage.at[c % slots], sem.at[c % slots])

    def begin():
        for c in range(min(slots, n_chunks)):
            copy(c).start()

    def advance(lo, hi):
        for c in range(lo, hi):
            copy(c).wait()
            w_vmem[c * rows:(c + 1) * rows, :] = stage[c % slots].astype(BF16)
            if c + slots < n_chunks:
                copy(c + slots).start()

    return begin, advance, n_chunks


def _weight_scratch(w):
    k, n = w.shape
    assert k % WEIGHT_STAGE_ROWS == 0
    return [pltpu.VMEM((k, n), BF16), pltpu.VMEM((2, WEIGHT_STAGE_ROWS, n), F32), pltpu.SemaphoreType.DMA((2,))]


ADALN_STAGE_SLOTS = 4


def _adaln_kernel(c_ref, w_hbm, b_ref, o_ref, stage, sem):
    batch, d = c_ref.shape
    slots, rows, n = stage.shape
    n_chunks = d // rows
    ca = _silu(c_ref[...]).astype(BF16)
    ca = jnp.concatenate([ca, jnp.zeros((-batch % BF16_ROW_TILE, d), BF16)], axis=0)

    def copy(k):
        return pltpu.make_async_copy(w_hbm.at[0, pl.ds(k * rows, rows), :], stage.at[k % slots], sem.at[k % slots])

    for k in range(min(slots - 1, n_chunks)):
        copy(k).start()
    acc = jnp.zeros((ca.shape[0], n), F32)
    for k in range(n_chunks):
        if k + slots - 1 < n_chunks:
            copy(k + slots - 1).start()
        copy(k).wait()
        acc = acc + jnp.dot(ca[:, k * rows:(k + 1) * rows], stage[k % slots].astype(BF16),
                            preferred_element_type=F32)
    res = acc[:batch] + b_ref[...]
    for part in range(o_ref.shape[0]):
        o_ref[part] = res[:, part * d:(part + 1) * d]


def _adaln_mod(c, w_ada, b_ada):
    batch, d = c.shape
    n = w_ada.shape[2]
    assert d % WEIGHT_STAGE_ROWS == 0 and n == 3 * d
    return pl.pallas_call(
        _adaln_kernel,
        in_specs=[pl.BlockSpec(memory_space=pltpu.VMEM),
                  pl.BlockSpec(memory_space=pl.ANY),
                  pl.BlockSpec(memory_space=pltpu.VMEM)],
        out_specs=pl.BlockSpec(memory_space=pltpu.VMEM),
        out_shape=jax.ShapeDtypeStruct((3, batch, d), F32),
        scratch_shapes=[pltpu.VMEM((ADALN_STAGE_SLOTS, WEIGHT_STAGE_ROWS, n), F32),
                        pltpu.SemaphoreType.DMA((ADALN_STAGE_SLOTS,))],
        compiler_params=pltpu.CompilerParams(vmem_limit_bytes=V7X_VMEM_LIMIT_BYTES),
        name="adaln_mod",
    )(c, w_ada, b_ada)


def _in_proj_kernel(x_ref, mod_ref, g_ref, w_hbm, b_ref, u_ref, sg_ref, w_ref, stage, sem, hs_ref, *, n_chunk):
    first = pl.program_id(0) == 0
    n = w_ref.shape[1]
    n_mix = u_ref.shape[1]
    batch, tt, d = x_ref.shape

    def normalised_input():
        for b in range(batch):
            x = x_ref[b]
            r = lax.rsqrt(jnp.mean(x * x, axis=-1, keepdims=True) + EPS)
            h = (x * r) * (g_ref[...] * (1.0 + mod_ref[1, b:b + 1, :])) + mod_ref[0, b:b + 1, :]
            for c in range(d // LANES):
                hs_ref[c, pl.ds(b, tt, stride=batch), :] = h[:, c * LANES:(c + 1) * LANES]
        return jnp.concatenate([hs_ref[c] for c in range(d // LANES)], axis=1).astype(BF16)

    def project(hb, sl):
        zj = jnp.dot(hb, w_ref[:, sl], preferred_element_type=F32)
        if sl.start < n_mix:
            u_ref[:, sl] = zj + b_ref[:, sl]
        else:
            half = zj + 0.5 * b_ref[:, sl]
            sg_ref[:, sl.start - n_mix:sl.stop - n_mix] = (half * jnp.tanh(half) + half).astype(sg_ref.dtype)

    @pl.when(first)
    def _():
        cols = stage.shape[2]

        def copy(c):
            return pltpu.make_async_copy(w_hbm.at[:, pl.ds(c * cols, cols)], stage.at[c % 2], sem.at[c % 2])

        copy(0).start()
        hb = normalised_input()
        for c in range(n // cols):
            if c + 1 < n // cols:
                copy(c + 1).start()
            copy(c).wait()
            sl = slice(c * cols, (c + 1) * cols)
            halve = 0.5 if sl.start >= n_mix else 1.0
            w_ref[:, sl] = (halve * stage[c % 2]).astype(BF16)
            project(hb, sl)

    @pl.when(jnp.logical_not(first))
    def _():
        hb = normalised_input()
        for j in range(n // n_chunk):
            project(hb, slice(j * n_chunk, (j + 1) * n_chunk))


def _in_proj(x, mod3, norm_g, w_in, b_in, *, tm):
    batch, seq, d = x.shape
    n = w_in.shape[1]
    tt = tm // batch
    rows = batch * seq
    assert (n // 2) % W_IN_STAGE_COLS == 0 and (n // 2) % (n // N_Z_BLOCKS) == 0
    return pl.pallas_call(
        functools.partial(_in_proj_kernel, n_chunk=n // N_Z_BLOCKS),
        grid=(seq // tt,),
        in_specs=[pl.BlockSpec((batch, tt, d), lambda i: (0, i, 0)),
                  pl.BlockSpec(mod3.shape, lambda i: (0, 0, 0)),
                  pl.BlockSpec((1, d), lambda i: (0, 0)),
                  pl.BlockSpec(memory_space=pl.ANY),
                  pl.BlockSpec((1, n), lambda i: (0, 0))],
        out_specs=[pl.BlockSpec((tm, n // 2), lambda i: (i, 0)),
                   pl.BlockSpec((tm, n // 2), lambda i: (i, 0))],
        out_shape=[jax.ShapeDtypeStruct((rows, n // 2), F32), jax.ShapeDtypeStruct((rows, n // 2), BF16)],
        scratch_shapes=[pltpu.VMEM(w_in.shape, BF16),
                        pltpu.VMEM((2, d, W_IN_STAGE_COLS), F32),
                        pltpu.SemaphoreType.DMA((2,)),
                        pltpu.VMEM((d // LANES, tm, LANES), F32)],
        compiler_params=_params(("arbitrary",)),
        name="in_proj",
    )(x, mod3, norm_g, w_in, b_in)


def _halo_specs(width, col_block, *, halo, ts, total_rows):
    per_tile = ts // halo
    last = total_rows // halo - 1
    return (pl.BlockSpec((halo, width), lambda i: (jnp.maximum(i * per_tile - 1, 0), col_block)),
            pl.BlockSpec((halo, width), lambda i: (jnp.minimum((i + 1) * per_tile, last), col_block)))


def _with_halo(prev_ref, cur_ref, next_ref, tile, n_tiles):
    prev = jnp.where(tile > 0, prev_ref[...], 0.0)
    nxt = jnp.where(tile < n_tiles - 1, next_ref[...], 0.0)
    return jnp.concatenate([prev, cur_ref[...], nxt], axis=0)


def _pool_kernel(prev_ref, u_ref, next_ref, gate_ref, w_ref, b_ref, scale_ref, ng_ref, o_ref,
                 *, batch, seq, ts, n_tiles):
    i = pl.program_id(0)
    halo = prev_ref.shape[0]
    grp = u_ref.shape[1] // len(POOL_WINDOWS)

    def mix(at_sequence_end):
        ue = _with_halo(prev_ref, u_ref, next_ref, i, n_tiles)
        row = lax.broadcasted_iota(jnp.int32, (ts, 1), 0)
        t = i * (ts // batch) + row // batch
        mixed = []
        for g, w in enumerate(POOL_WINDOWS):
            cols = slice(g * grp, (g + 1) * grp)
            e = ue[:, cols]
            lo = w // 2
            hi = w - lo - 1
            s, span = e, 1
            while span < w:
                s = s + _shift_rows(s, span * batch)
                span *= 2
            total = (_shift_rows(s, -hi * batch) if hi else s)[halo:halo + ts]
            if at_sequence_end:
                cnt = (jnp.minimum(t + hi, seq - 1) - jnp.maximum(t - lo, 0) + 1).astype(F32)
            else:
                cnt = float(w)
            pooled = total / cnt - e[halo:halo + ts]
            m = jnp.dot(pooled.astype(BF16), w_ref[g].astype(BF16), preferred_element_type=F32)
            mixed.append((m + b_ref[g:g + 1, :]) * scale_ref[:, cols])
        y = jnp.concatenate(mixed, axis=-1)
        r = lax.rsqrt(jnp.mean(y * y, axis=-1, keepdims=True) + EPS)
        o_ref[...] = ((y * r) * ng_ref[...] * gate_ref[...].astype(F32)).astype(o_ref.dtype)

    interior = (i > 0) & (i < n_tiles - 1)
    pl.when(jnp.logical_not(interior))(functools.partial(mix, True))
    pl.when(interior)(functools.partial(mix, False))


def _pool_mixer(z, sg, w_pool, b_pool, pool_scale, out_norm_g, *, batch, seq, ts):
    width = pool_scale.shape[1]
    n_tiles = batch * seq // ts
    prev_spec, next_spec = _halo_specs(width, 0, halo=_halo_rows(max(POOL_WINDOWS) // 2, batch), ts=ts,
                                       total_rows=batch * seq)
    n_grp, grp, _ = w_pool.shape
    assert ts // batch >= max(POOL_WINDOWS)
    return pl.pallas_call(
        functools.partial(_pool_kernel, batch=batch, seq=seq, ts=ts, n_tiles=n_tiles),
        grid=(n_tiles,),
        in_specs=[prev_spec,
                  pl.BlockSpec((ts, width), lambda i: (i, 0)),
                  next_spec,
                  pl.BlockSpec((ts, width), lambda i: (i, 0)),
                  pl.BlockSpec((n_grp, grp, grp), lambda i: (0, 0, 0)),
                  pl.BlockSpec((n_grp, grp), lambda i: (0, 0)),
                  pl.BlockSpec((1, width), lambda i: (0, 0)),
                  pl.BlockSpec((1, width), lambda i: (0, 0))],
        out_specs=pl.BlockSpec((ts, width), lambda i: (i, 0)),
        out_shape=jax.ShapeDtypeStruct((batch * seq, width), BF16),
        compiler_params=_params(("arbitrary",)),
        name="pool_mixer",
    )(z, z, z, sg, w_pool, b_pool, pool_scale, out_norm_g)


def _softplus(v):
    return jnp.maximum(v, 0.0) + jnp.log1p(jnp.exp(-jnp.abs(v)))


def _lru_direction(xc_half, wg_ref, bg_half, rate2, carry_ref, h_ref, *, batch, reverse, after_gates=None):
    tc, width = xc_half.shape
    head = width // N_LRU_HEADS
    groups = tc // SUBLANES
    steps = SUBLANES // batch
    sub = lax.broadcasted_iota(jnp.int32, (SUBLANES, head), 0)
    step_of_row = (SUBLANES - 1 - sub if reverse else sub) // batch
    shift = SUBLANES - batch if reverse else batch

    for h in range(N_LRU_HEADS):
        cols = slice(h * head, (h + 1) * head)
        xh = xc_half[:, cols]
        g = jnp.dot(xh.astype(BF16), wg_ref[h], preferred_element_type=F32) + bg_half[h:h + 1, :]
        if after_gates is not None:
            after_gates(h)
        tr = jnp.tanh(g[:, :head])
        ti = jnp.tanh(g[:, head:])
        log2a = tr * rate2[:, cols] + rate2[:, cols]
        a = jnp.exp2(log2a)
        one_minus_a2 = (a * a + 1.0) * jnp.tanh(log2a * (-LN2))
        mult = one_minus_a2 * lax.rsqrt(jnp.maximum(one_minus_a2, F32_MIN_NORMAL))
        b = mult * (ti + 1.0) * xh

        a3 = a.reshape(groups, SUBLANES, head)
        b3 = b.reshape(groups, SUBLANES, head)
        state = carry_ref[:, cols]
        order = range(groups - 1, -1, -1) if reverse else range(groups)
        for gi in order:
            passes = []
            for _ in range(steps):
                state = b3[gi] + a3[gi] * pltpu.roll(state, shift, 0)
                passes.append(state)
            out = passes[0]
            for k in range(1, steps):
                out = jnp.where(step_of_row == k, passes[k], out)
            h_ref[gi * SUBLANES:(gi + 1) * SUBLANES, cols] = out
        carry_ref[:, cols] = state


def _lru_rates(lam_row, bg_ref):
    rate2 = (-0.5 * LRU_C * LOG2E) * _softplus(-lam_row)
    return rate2, 0.5 * bg_ref[0]


def _cast_gate_weights(wg_ref, wgb_ref):
    for h in range(N_LRU_HEADS):
        wgb_ref[h] = wg_ref[0, h].astype(BF16)


def _lru_fwd_kernel(prev_ref, u_ref, next_ref, cw_ref, cb_ref, wg_ref, bg_ref, lam_ref,
                    xc_ref, hf_ref, carry_ref, wgb_ref, *, batch, tc, n_tiles):
    i = pl.program_id(0)

    @pl.when(i == 0)
    def _():
        _cast_gate_weights(wg_ref, wgb_ref)
        carry_ref[...] = jnp.zeros_like(carry_ref)

    rate2, bg_half = _lru_rates(lam_ref[0:1, :], bg_ref)
    ue = _with_halo(prev_ref, u_ref, next_ref, i, n_tiles)
    halo = prev_ref.shape[0]
    left = CONV_WIDTH // 2
    xc_half = 0.5 * cb_ref[...]
    for k in range(CONV_WIDTH):
        tap = ue if k == left else _shift_rows(ue, (left - k) * batch)
        xc_half = xc_half + tap[halo:halo + tc] * (0.5 * cw_ref[k:k + 1, :])
    xc_ref[...] = xc_half
    _lru_direction(xc_half, wgb_ref, bg_half, rate2, carry_ref, hf_ref, batch=batch, reverse=False)


def _lru_fwd(z, conv_w, conv_b, w_gate, b_gate, lru_lambda, *, batch, seq, tc):
    width = conv_b.shape[1]
    n_tiles = batch * seq // tc
    assert SUBLANES % batch == 0, "a group of 8 rows must hold whole time steps"
    prev_spec, next_spec = _halo_specs(width, 1, halo=_halo_rows(CONV_WIDTH // 2, batch), ts=tc,
                                       total_rows=batch * seq)
    _, n_head, head, two_head = w_gate.shape
    tile = pl.BlockSpec((tc, width), lambda i: (i, 0))
    return pl.pallas_call(
        functools.partial(_lru_fwd_kernel, batch=batch, tc=tc, n_tiles=n_tiles),
        grid=(n_tiles,),
        in_specs=[prev_spec,
                  pl.BlockSpec((tc, width), lambda i: (i, 1)),
                  next_spec,
                  pl.BlockSpec((CONV_WIDTH, width), lambda i: (0, 0)),
                  pl.BlockSpec((1, width), lambda i: (0, 0)),
                  pl.BlockSpec((1, n_head, head, two_head), lambda i: (0, 0, 0, 0)),
                  pl.BlockSpec((1, n_head, two_head), lambda i: (0, 0, 0)),
                  pl.BlockSpec(lru_lambda.shape, lambda i: (0, 0))],
        out_specs=[tile, tile],
        out_shape=[jax.ShapeDtypeStruct((batch * seq, width), F32)] * 2,
        scratch_shapes=[pltpu.VMEM((SUBLANES, width), F32),
                        pltpu.VMEM((n_head, head, two_head), BF16)],
        compiler_params=_params(("arbitrary",)),
        name="lru_fwd",
    )(z, z, z, conv_w, conv_b, w_gate, b_gate, lru_lambda)


def _lru_bwd_out_kernel(xc_ref, hf_ref, gate_ref, wg_ref, bg_ref, lam_ref, ng_ref,
                        yp_ref, x_ref, mod_ref, w_hbm, b_ref, fg_ref,
                        o_ref, hb_ref, yl_ref, y_ref, carry_ref, wgb_ref, w_ref, stage, sem, *, n_tiles):
    s = pl.program_id(0)
    kp = yp_ref.shape[1]
    batch, tt, d = x_ref.shape
    n_chunk = d // N_LRU_HEADS

    def project_chunk(h):
        cols = slice(h * n_chunk, (h + 1) * n_chunk)
        y = jnp.dot(yp_ref[...], w_ref[:kp, cols], preferred_element_type=F32)
        y = y + jnp.dot(yl_ref[...], w_ref[kp:, cols], preferred_element_type=F32)
        y = y + b_ref[:, cols]
        for c in range(n_chunk // LANES):
            y_ref[h * (n_chunk // LANES) + c] = y[:, c * LANES:(c + 1) * LANES]

    def scan(after_gates):
        rate2, bg_half = _lru_rates(lam_ref[1:2, :], bg_ref)
        _lru_direction(xc_ref[...], wgb_ref, bg_half, rate2, carry_ref, hb_ref, batch=batch, reverse=True,
                       after_gates=after_gates)

    def finish_scan():
        yl = hf_ref[...] + hb_ref[...]
        rl = lax.rsqrt(jnp.mean(yl * yl, axis=-1, keepdims=True) + EPS)
        yl_ref[...] = ((yl * rl) * ng_ref[...] * gate_ref[...].astype(F32)).astype(yl_ref.dtype)

    def finish_projection():
        for b in range(batch):
            y = jnp.concatenate([y_ref[c, pl.ds(b, tt, stride=batch), :] for c in range(d // LANES)], axis=1)
            xn = x_ref[b] + mod_ref[2, b:b + 1, :] * y
            r = lax.rsqrt(jnp.mean(xn * xn, axis=-1, keepdims=True) + EPS)
            o_ref[b] = (xn * r) * fg_ref[...]

    @pl.when(s == 0)
    def _():
        _cast_gate_weights(wg_ref, wgb_ref)
        carry_ref[...] = jnp.zeros_like(carry_ref)
        begin, advance, n_chunks = _weight_stream(w_hbm, w_ref, stage, sem)
        per_head = n_chunks // N_LRU_HEADS
        begin()
        scan(lambda h: advance(h * per_head, (h + 1) * per_head))
        advance(N_LRU_HEADS * per_head, n_chunks)
        finish_scan()

    @pl.when((s > 0) & (s < n_tiles))
    def _():
        scan(project_chunk)
        finish_projection()
        finish_scan()

    @pl.when(s == n_tiles)
    def _():
        for h in range(N_LRU_HEADS):
            project_chunk(h)
        finish_projection()


def _lru_bwd_out_proj(xc, hf, sg, w_gate, b_gate, lru_lambda, out_norm_g, yp, x, mod3, w_out, b_out,
                      final_g, *, tm):
    batch, seq, d = x.shape
    width = xc.shape[1]
    tt = tm // batch
    n_tiles = seq // tt
    _, n_head, head, two_head = w_gate.shape

    def scan_tile(s):
        return jnp.maximum(n_tiles - 1 - s, 0)

    def proj_tile(s):
        return jnp.minimum(n_tiles - s, n_tiles - 1)

    return pl.pallas_call(
        functools.partial(_lru_bwd_out_kernel, n_tiles=n_tiles),
        grid=(n_tiles + 1,),
        in_specs=[pl.BlockSpec((tm, width), lambda s: (scan_tile(s), 0)),
                  pl.BlockSpec((tm, width), lambda s: (scan_tile(s), 0)),
                  pl.BlockSpec((tm, width), lambda s: (scan_tile(s), 1)),
                  pl.BlockSpec((1, n_head, head, two_head), lambda s: (1, 0, 0, 0)),
                  pl.BlockSpec((1, n_head, two_head), lambda s: (1, 0, 0)),
                  pl.BlockSpec(lru_lambda.shape, lambda s: (0, 0)),
                  pl.BlockSpec((1, width), lambda s: (0, 0)),
                  pl.BlockSpec((tm, yp.shape[1]), lambda s: (proj_tile(s), 0)),
                  pl.BlockSpec((batch, tt, d), lambda s: (0, proj_tile(s), 0)),
                  pl.BlockSpec(mod3.shape, lambda s: (0, 0, 0)),
                  pl.BlockSpec(memory_space=pl.ANY),
                  pl.BlockSpec((1, d), lambda s: (0, 0)),
                  pl.BlockSpec((1, d), lambda s: (0, 0))],
        out_specs=pl.BlockSpec((batch, tt, d), lambda s: (0, proj_tile(s), 0)),
        out_shape=jax.ShapeDtypeStruct((batch, seq, d), F32),
        scratch_shapes=[pltpu.VMEM((tm, width), F32),
                        pltpu.VMEM((tm, width), BF16),
                        pltpu.VMEM((d // LANES, tm, LANES), F32),
                        pltpu.VMEM((SUBLANES, width), F32),
                        pltpu.VMEM((n_head, head, two_head), BF16),
                        *_weight_scratch(w_out)],
        compiler_params=_params(("arbitrary",)),
        name="lru_bwd_out_proj",
    )(xc, hf, sg, w_gate, b_gate, lru_lambda, out_norm_g, yp, x, mod3, w_out, b_out, final_g)


def kernel(x, c, norm_g, w_ada, b_ada, w_in, b_in, w_pool, b_pool, pool_scale, conv_w, conv_b, w_gate,
           b_gate, lru_lambda, out_norm_pool_g, out_norm_lru_g, w_out, b_out, final_norm_g):
    batch, seq, d = x.shape
    assert w_in.shape[0] == 1, "single-layer block only"
    mod3 = _adaln_mod(c, w_ada, b_ada)
    tile = TIME_TILE_ROWS
    z, sg = _in_proj(x, mod3, norm_g, w_in[0], b_in, tm=tile)
    yp = _pool_mixer(z, sg, w_pool[0], b_pool[0], pool_scale, out_norm_pool_g, batch=batch, seq=seq,
                     ts=POOL_TILE_ROWS)
    xc, hf = _lru_fwd(z, conv_w[0], conv_b, w_gate[0], b_gate[0], lru_lambda[0], batch=batch, seq=seq, tc=tile)
    return _lru_bwd_out_proj(xc, hf, sg, w_gate[0], b_gate[0], lru_lambda[0], out_norm_lru_g, yp, x, mod3,
                             w_out[0], b_out, final_norm_g[None, :], tm=tile)
```

```python
import functools
import math

import jax
import jax.numpy as jnp
from jax import lax
from jax.experimental import pallas as pl
from jax.experimental.pallas import tpu as pltpu

EPS = 1e-6
LRU_C = 8.0
POOL_WINDOWS = (2, 4, 8, 16)
N_LRU_HEADS = 4
CONV_WIDTH = 4
LANES = 128
SUBLANES = 8
BF16_ROW_TILE = 16
V7X_VMEM_LIMIT_BYTES = 56 * 1024 * 1024
TIME_TILE_ROWS = 512
POOL_TILE_ROWS = 1024
N_Z_BLOCKS = 4
W_IN_STAGE_COLS = 512
WEIGHT_STAGE_ROWS = 256
LOG2E = math.log2(math.e)
LN2 = math.log(2.0)
F32_MIN_NORMAL = float(jnp.finfo(jnp.float32).tiny)

F32 = jnp.float32
BF16 = jnp.bfloat16


def _params(semantics):
    return pltpu.CompilerParams(dimension_semantics=semantics,
                                vmem_limit_bytes=V7X_VMEM_LIMIT_BYTES)


def _sigmoid(v):
    return 0.5 * jnp.tanh(0.5 * v) + 0.5


def _silu(v):
    return v * _sigmoid(v)


def _halo_rows(steps, batch):
    return -(-steps * batch // SUBLANES) * SUBLANES


def _shift_rows(v, k):
    rows, width = v.shape
    v3 = v.reshape(rows // SUBLANES, SUBLANES, width)
    q, r = divmod(k, SUBLANES)
    if q % v3.shape[0]:
        g = q % v3.shape[0]
        v3 = jnp.concatenate([v3[-g:], v3[:-g]], axis=0)
    if r:
        sub = lax.broadcasted_iota(jnp.int32, v3.shape, 1)
        rot = pltpu.roll(v3, r, 1)
        other = jnp.concatenate([rot[-1:], rot[:-1]], axis=0)
        v3 = jnp.where(sub >= r, rot, other)
    return v3.reshape(rows, width)


def _weight_stream(w_hbm, w_vmem, stage, sem):
    slots, rows, _ = stage.shape
    n_chunks = w_hbm.shape[0] // rows

    def copy(c):
        return pltpu.make_async_copy(w_hbm.at[pl.ds(c * rows, rows), :], stage.at[c % slots], sem.at[c % slots])

    def begin():
        for c in range(min(slots, n_chunks)):
            copy(c).start()

    def advance(lo, hi):
        for c in range(lo, hi):
            copy(c).wait()
            w_vmem[c * rows:(c + 1) * rows, :] = stage[c % slots].astype(BF16)
            if c + slots < n_chunks:
                copy(c + slots).start()

    return begin, advance, n_chunks


def _weight_scratch(w):
    k, n = w.shape
    assert k % WEIGHT_STAGE_ROWS == 0
    return [pltpu.VMEM((k, n), BF16), pltpu.VMEM((2, WEIGHT_STAGE_ROWS, n), F32), pltpu.SemaphoreType.DMA((2,))]


ADALN_STAGE_SLOTS = 4


def _adaln_kernel(c_ref, w_hbm, b_ref, o_ref, stage, sem):
    batch, d = c_ref.shape
    slots, rows, n = stage.shape
    n_chunks = d // rows
    ca = _silu(c_ref[...]).astype(BF16)
    ca = jnp.concatenate([ca, jnp.zeros((-batch % BF16_ROW_TILE, d), BF16)], axis=0)

    def copy(k):
        return pltpu.make_async_copy(w_hbm.at[0, pl.ds(k * rows, rows), :], stage.at[k % slots], sem.at[k % slots])

    for k in range(min(slots - 1, n_chunks)):
        copy(k).start()
    acc = jnp.zeros((ca.shape[0], n), F32)
    for k in range(n_chunks):
        if k + slots - 1 < n_chunks:
            copy(k + slots - 1).start()
        copy(k).wait()
        acc = acc + jnp.dot(ca[:, k * rows:(k + 1) * rows], stage[k % slots].astype(BF16),
                            preferred_element_type=F32)
    res = acc[:batch] + b_ref[...]
    for part in range(o_ref.shape[0]):
        o_ref[part] = res[:, part * d:(part + 1) * d]


def _adaln_mod(c, w_ada, b_ada):
    batch, d = c.shape
    n = w_ada.shape[2]
    assert d % WEIGHT_STAGE_ROWS == 0 and n == 3 * d
    return pl.pallas_call(
        _adaln_kernel,
        in_specs=[pl.BlockSpec(memory_space=pltpu.VMEM),
                  pl.BlockSpec(memory_space=pl.ANY),
                  pl.BlockSpec(memory_space=pltpu.VMEM)],
        out_specs=pl.BlockSpec(memory_space=pltpu.VMEM),
        out_shape=jax.ShapeDtypeStruct((3, batch, d), F32),
        scratch_shapes=[pltpu.VMEM((ADALN_STAGE_SLOTS, WEIGHT_STAGE_ROWS, n), F32),
                        pltpu.SemaphoreType.DMA((ADALN_STAGE_SLOTS,))],
        compiler_params=pltpu.CompilerParams(vmem_limit_bytes=V7X_VMEM_LIMIT_BYTES),
        name="adaln_mod",
    )(c, w_ada, b_ada)


def _in_proj_kernel(x_ref, mod_ref, g_ref, w_hbm, b_ref, u_ref, sg_ref, w_ref, stage, sem, hs_ref, *, n_chunk):
    first = pl.program_id(0) == 0
    n = w_ref.shape[1]
    n_mix = u_ref.shape[1]
    batch, tt, d = x_ref.shape

    def normalised_input():
        for b in range(batch):
            x = x_ref[b]
            r = lax.rsqrt(jnp.mean(x * x, axis=-1, keepdims=True) + EPS)
            h = (x * r) * (g_ref[...] * (1.0 + mod_ref[1, b:b + 1, :])) + mod_ref[0, b:b + 1, :]
            for c in range(d // LANES):
                hs_ref[c, pl.ds(b, tt, stride=batch), :] = h[:, c * LANES:(c + 1) * LANES]
        return jnp.concatenate([hs_ref[c] for c in range(d // LANES)], axis=1).astype(BF16)

    def project(hb, sl):
        zj = jnp.dot(hb, w_ref[:, sl], preferred_element_type=F32)
        if sl.start < n_mix:
            u_ref[:, sl] = zj
        else:
            half = (zj + 0.5 * b_ref[:, sl]).astype(sg_ref.dtype)
            sg_ref[:, sl.start - n_mix:sl.stop - n_mix] = half * jnp.tanh(half) + half

    @pl.when(first)
    def _():
        cols = stage.shape[2]

        def copy(c):
            return pltpu.make_async_copy(w_hbm.at[:, pl.ds(c * cols, cols)], stage.at[c % 2], sem.at[c % 2])

        copy(0).start()
        hb = normalised_input()
        for c in range(n // cols):
            if c + 1 < n // cols:
                copy(c + 1).start()
            copy(c).wait()
            sl = slice(c * cols, (c + 1) * cols)
            halve = 0.5 if sl.start >= n_mix else 1.0
            w_ref[:, sl] = (halve * stage[c % 2]).astype(BF16)
            project(hb, sl)

    @pl.when(jnp.logical_not(first))
    def _():
        hb = normalised_input()
        for j in range(n // n_chunk):
            project(hb, slice(j * n_chunk, (j + 1) * n_chunk))


def _in_proj(x, mod3, norm_g, w_in, b_in, *, tm):
    batch, seq, d = x.shape
    n = w_in.shape[1]
    tt = tm // batch
    rows = batch * seq
    assert (n // 2) % W_IN_STAGE_COLS == 0 and (n // 2) % (n // N_Z_BLOCKS) == 0
    return pl.pallas_call(
        functools.partial(_in_proj_kernel, n_chunk=n // N_Z_BLOCKS),
        grid=(seq // tt,),
        in_specs=[pl.BlockSpec((batch, tt, d), lambda i: (0, i, 0)),
                  pl.BlockSpec(mod3.shape, lambda i: (0, 0, 0)),
                  pl.BlockSpec((1, d), lambda i: (0, 0)),
                  pl.BlockSpec(memory_space=pl.ANY),
                  pl.BlockSpec((1, n), lambda i: (0, 0))],
        out_specs=[pl.BlockSpec((tm, n // 2), lambda i: (i, 0)),
                   pl.BlockSpec((tm, n // 2), lambda i: (i, 0))],
        out_shape=[jax.ShapeDtypeStruct((rows, n // 2), F32), jax.ShapeDtypeStruct((rows, n // 2), BF16)],
        scratch_shapes=[pltpu.VMEM(w_in.shape, BF16),
                        pltpu.VMEM((2, d, W_IN_STAGE_COLS), F32),
                        pltpu.SemaphoreType.DMA((2,)),
                        pltpu.VMEM((d // LANES, tm, LANES), F32)],
        compiler_params=_params(("arbitrary",)),
        name="in_proj",
    )(x, mod3, norm_g, w_in, b_in)


def _halo_specs(width, col_block, *, halo, ts, total_rows):
    per_tile = ts // halo
    last = total_rows // halo - 1
    return (pl.BlockSpec((halo, width), lambda i: (jnp.maximum(i * per_tile - 1, 0), col_block)),
            pl.BlockSpec((halo, width), lambda i: (jnp.minimum((i + 1) * per_tile, last), col_block)))


def _with_halo(prev_ref, cur_ref, next_ref, tile, n_tiles, outside=0.0):
    prev = jnp.where(tile > 0, prev_ref[...], outside)
    nxt = jnp.where(tile < n_tiles - 1, next_ref[...], outside)
    return jnp.concatenate([prev, cur_ref[...], nxt], axis=0)


def _pool_kernel(prev_ref, u_ref, next_ref, gate_ref, w_ref, b_ref, scale_ref, ng_ref, o_ref,
                 *, batch, seq, ts, n_tiles):
    i = pl.program_id(0)
    halo = prev_ref.shape[0]
    grp = u_ref.shape[1] // len(POOL_WINDOWS)

    def mix(at_sequence_end):
        ue = _with_halo(prev_ref, u_ref, next_ref, i, n_tiles)
        row = lax.broadcasted_iota(jnp.int32, (ts, 1), 0)
        t = i * (ts // batch) + row // batch
        mixed = []
        for g, w in enumerate(POOL_WINDOWS):
            cols = slice(g * grp, (g + 1) * grp)
            e = ue[:, cols]
            lo = w // 2
            hi = w - lo - 1
            s, span = e, 1
            while span < w:
                s = s + _shift_rows(s, span * batch)
                span *= 2
            total = (_shift_rows(s, -hi * batch) if hi else s)[halo:halo + ts]
            if at_sequence_end:
                cnt = (jnp.minimum(t + hi, seq - 1) - jnp.maximum(t - lo, 0) + 1).astype(F32)
            else:
                cnt = float(w)
            pooled = total / cnt - e[halo:halo + ts]
            m = jnp.dot(pooled.astype(BF16), w_ref[g].astype(BF16), preferred_element_type=F32)
            mixed.append((m + b_ref[g:g + 1, :]) * scale_ref[:, cols])
        y = jnp.concatenate(mixed, axis=-1)
        r = lax.rsqrt(jnp.mean(y * y, axis=-1, keepdims=True) + EPS)
        o_ref[...] = ((y * r) * ng_ref[...] * gate_ref[...].astype(F32)).astype(o_ref.dtype)

    interior = (i > 0) & (i < n_tiles - 1)
    pl.when(jnp.logical_not(interior))(functools.partial(mix, True))
    pl.when(interior)(functools.partial(mix, False))


def _pool_mixer(z, sg, w_pool, b_pool, pool_scale, out_norm_g, *, batch, seq, ts):
    width = pool_scale.shape[1]
    n_tiles = batch * seq // ts
    prev_spec, next_spec = _halo_specs(width, 0, halo=_halo_rows(max(POOL_WINDOWS) // 2, batch), ts=ts,
                                       total_rows=batch * seq)
    n_grp, grp, _ = w_pool.shape
    assert ts // batch >= max(POOL_WINDOWS)
    return pl.pallas_call(
        functools.partial(_pool_kernel, batch=batch, seq=seq, ts=ts, n_tiles=n_tiles),
        grid=(n_tiles,),
        in_specs=[prev_spec,
                  pl.BlockSpec((ts, width), lambda i: (i, 0)),
                  next_spec,
                  pl.BlockSpec((ts, width), lambda i: (i, 0)),
                  pl.BlockSpec((n_grp, grp, grp), lambda i: (0, 0, 0)),
                  pl.BlockSpec((n_grp, grp), lambda i: (0, 0)),
                  pl.BlockSpec((1, width), lambda i: (0, 0)),
                  pl.BlockSpec((1, width), lambda i: (0, 0))],
        out_specs=pl.BlockSpec((ts, width), lambda i: (i, 0)),
        out_shape=jax.ShapeDtypeStruct((batch * seq, width), BF16),
        compiler_params=_params(("arbitrary",)),
        name="pool_mixer",
    )(z, z, z, sg, w_pool, b_pool, pool_scale, out_norm_g)


def _softplus(v):
    return jnp.maximum(v, 0.0) + jnp.log1p(jnp.exp(-jnp.abs(v)))


def _lru_direction(xc_half, wg_ref, bg_half, rate2, carry_ref, h_ref, *, batch, reverse, after_gates=None):
    tc, width = xc_half.shape
    head = width // N_LRU_HEADS
    groups = tc // SUBLANES
    steps = SUBLANES // batch
    sub = lax.broadcasted_iota(jnp.int32, (SUBLANES, head), 0)
    step_of_row = (SUBLANES - 1 - sub if reverse else sub) // batch
    shift = SUBLANES - batch if reverse else batch

    for h in range(N_LRU_HEADS):
        cols = slice(h * head, (h + 1) * head)
        xh = xc_half[:, cols]
        g = jnp.dot(xh.astype(BF16), wg_ref[h], preferred_element_type=F32) + bg_half[h:h + 1, :]
        if after_gates is not None:
            after_gates(h)
        tr = jnp.tanh(g[:, :head])
        ti = jnp.tanh(g[:, head:])
        log2a = tr * rate2[:, cols] + rate2[:, cols]
        a = jnp.exp2(log2a)
        one_minus_a2 = (a * a + 1.0) * jnp.tanh(log2a * (-LN2))
        mult = one_minus_a2 * lax.rsqrt(jnp.maximum(one_minus_a2, F32_MIN_NORMAL))
        b = mult * (ti + 1.0) * xh

        a3 = a.reshape(groups, SUBLANES, head)
        b3 = b.reshape(groups, SUBLANES, head)
        state = carry_ref[:, cols]
        order = range(groups - 1, -1, -1) if reverse else range(groups)
        for gi in order:
            passes = []
            for _ in range(steps):
                state = b3[gi] + a3[gi] * pltpu.roll(state, shift, 0)
                passes.append(state)
            out = passes[0]
            for k in range(1, steps):
                out = jnp.where(step_of_row == k, passes[k], out)
            h_ref[gi * SUBLANES:(gi + 1) * SUBLANES, cols] = out
        carry_ref[:, cols] = state


def _lru_rates(lam_row, bg_ref):
    rate2 = (-0.5 * LRU_C * LOG2E) * _softplus(-lam_row)
    return rate2, 0.5 * bg_ref[0]


def _cast_gate_weights(wg_ref, wgb_ref):
    for h in range(N_LRU_HEADS):
        wgb_ref[h] = wg_ref[0, h].astype(BF16)


def _lru_fwd_kernel(prev_ref, u_ref, next_ref, bu_ref, cw_ref, cb_ref, wg_ref, bg_ref, lam_ref,
                    xc_ref, hf_ref, carry_ref, wgb_ref, *, batch, tc, n_tiles):
    i = pl.program_id(0)

    @pl.when(i == 0)
    def _():
        _cast_gate_weights(wg_ref, wgb_ref)
        carry_ref[...] = jnp.zeros_like(carry_ref)

    rate2, bg_half = _lru_rates(lam_ref[0:1, :], bg_ref)
    ue = _with_halo(prev_ref, u_ref, next_ref, i, n_tiles, outside=-bu_ref[...])
    halo = prev_ref.shape[0]
    left = CONV_WIDTH // 2
    xc_half = 0.5 * (cb_ref[...] + bu_ref[...] * jnp.sum(cw_ref[...], axis=0, keepdims=True))
    for k in range(CONV_WIDTH):
        tap = ue if k == left else _shift_rows(ue, (left - k) * batch)
        xc_half = xc_half + tap[halo:halo + tc] * (0.5 * cw_ref[k:k + 1, :])
    xc_ref[...] = xc_half
    _lru_direction(xc_half, wgb_ref, bg_half, rate2, carry_ref, hf_ref, batch=batch, reverse=False)


def _lru_fwd(z, b_in, conv_w, conv_b, w_gate, b_gate, lru_lambda, *, batch, seq, tc):
    width = conv_b.shape[1]
    n_tiles = batch * seq // tc
    assert SUBLANES % batch == 0, "a group of 8 rows must hold whole time steps"
    prev_spec, next_spec = _halo_specs(width, 1, halo=_halo_rows(CONV_WIDTH // 2, batch), ts=tc,
                                       total_rows=batch * seq)
    _, n_head, head, two_head = w_gate.shape
    tile = pl.BlockSpec((tc, width), lambda i: (i, 0))
    return pl.pallas_call(
        functools.partial(_lru_fwd_kernel, batch=batch, tc=tc, n_tiles=n_tiles),
        grid=(n_tiles,),
        in_specs=[prev_spec,
                  pl.BlockSpec((tc, width), lambda i: (i, 1)),
                  next_spec,
                  pl.BlockSpec((1, width), lambda i: (0, 1)),
                  pl.BlockSpec((CONV_WIDTH, width), lambda i: (0, 0)),
                  pl.BlockSpec((1, width), lambda i: (0, 0)),
                  pl.BlockSpec((1, n_head, head, two_head), lambda i: (0, 0, 0, 0)),
                  pl.BlockSpec((1, n_head, two_head), lambda i: (0, 0, 0)),
                  pl.BlockSpec(lru_lambda.shape, lambda i: (0, 0))],
        out_specs=[tile, tile],
        out_shape=[jax.ShapeDtypeStruct((batch * seq, width), F32)] * 2,
        scratch_shapes=[pltpu.VMEM((SUBLANES, width), F32),
                        pltpu.VMEM((n_head, head, two_head), BF16)],
        compiler_params=_params(("arbitrary",)),
        name="lru_fwd",
    )(z, z, z, b_in, conv_w, conv_b, w_gate, b_gate, lru_lambda)


def _lru_bwd_out_kernel(xc_ref, hf_ref, gate_ref, wg_ref, bg_ref, lam_ref, ng_ref,
                        yp_ref, x_ref, mod_ref, w_hbm, b_ref, fg_ref,
                        o_ref, hb_ref, yl_ref, y_ref, carry_ref, wgb_ref, w_ref, stage, sem, *, n_tiles):
    s = pl.program_id(0)
    kp = yp_ref.shape[1]
    batch, tt, d = x_ref.shape
    n_chunk = d // N_LRU_HEADS

    def project_chunk(h):
        cols = slice(h * n_chunk, (h + 1) * n_chunk)
        y = jnp.dot(yp_ref[...], w_ref[:kp, cols], preferred_element_type=F32)
        y = y + jnp.dot(yl_ref[...], w_ref[kp:, cols], preferred_element_type=F32)
        y = y + b_ref[:, cols]
        for c in range(n_chunk // LANES):
            y_ref[h * (n_chunk // LANES) + c] = y[:, c * LANES:(c + 1) * LANES]

    def scan(after_gates):
        rate2, bg_half = _lru_rates(lam_ref[1:2, :], bg_ref)
        _lru_direction(xc_ref[...], wgb_ref, bg_half, rate2, carry_ref, hb_ref, batch=batch, reverse=True,
                       after_gates=after_gates)

    def finish_scan():
        yl = hf_ref[...] + hb_ref[...]
        rl = lax.rsqrt(jnp.mean(yl * yl, axis=-1, keepdims=True) + EPS)
        yl_ref[...] = ((yl * rl) * ng_ref[...] * gate_ref[...].astype(F32)).astype(yl_ref.dtype)

    def finish_projection():
        for b in range(batch):
            y = jnp.concatenate([y_ref[c, pl.ds(b, tt, stride=batch), :] for c in range(d // LANES)], axis=1)
            xn = x_ref[b] + mod_ref[2, b:b + 1, :] * y
            r = lax.rsqrt(jnp.mean(xn * xn, axis=-1, keepdims=True) + EPS)
            o_ref[b] = (xn * r) * fg_ref[...]

    @pl.when(s == 0)
    def _():
        _cast_gate_weights(wg_ref, wgb_ref)
        carry_ref[...] = jnp.zeros_like(carry_ref)
        begin, advance, n_chunks = _weight_stream(w_hbm, w_ref, stage, sem)
        per_head = n_chunks // N_LRU_HEADS
        begin()
        scan(lambda h: advance(h * per_head, (h + 1) * per_head))
        advance(N_LRU_HEADS * per_head, n_chunks)
        finish_scan()

    @pl.when((s > 0) & (s < n_tiles))
    def _():
        scan(project_chunk)
        finish_projection()
        finish_scan()

    @pl.when(s == n_tiles)
    def _():
        for h in range(N_LRU_HEADS):
            project_chunk(h)
        finish_projection()


def _lru_bwd_out_proj(xc, hf, sg, w_gate, b_gate, lru_lambda, out_norm_g, yp, x, mod3, w_out, b_out,
                      final_g, *, tm):
    batch, seq, d = x.shape
    width = xc.shape[1]
    tt = tm // batch
    n_tiles = seq // tt
    _, n_head, head, two_head = w_gate.shape

    def scan_tile(s):
        return jnp.maximum(n_tiles - 1 - s, 0)

    def proj_tile(s):
        return jnp.minimum(n_tiles - s, n_tiles - 1)

    return pl.pallas_call(
        functools.partial(_lru_bwd_out_kernel, n_tiles=n_tiles),
        grid=(n_tiles + 1,),
        in_specs=[pl.BlockSpec((tm, width), lambda s: (scan_tile(s), 0)),
                  pl.BlockSpec((tm, width), lambda s: (scan_tile(s), 0)),
                  pl.BlockSpec((tm, width), lambda s: (scan_tile(s), 1)),
                  pl.BlockSpec((1, n_head, head, two_head), lambda s: (1, 0, 0, 0)),
                  pl.BlockSpec((1, n_head, two_head), lambda s: (1, 0, 0)),
                  pl.BlockSpec(lru_lambda.shape, lambda s: (0, 0)),
                  pl.BlockSpec((1, width), lambda s: (0, 0)),
                  pl.BlockSpec((tm, yp.shape[1]), lambda s: (proj_tile(s), 0)),
                  pl.BlockSpec((batch, tt, d), lambda s: (0, proj_tile(s), 0)),
                  pl.BlockSpec(mod3.shape, lambda s: (0, 0, 0)),
                  pl.BlockSpec(memory_space=pl.ANY),
                  pl.BlockSpec((1, d), lambda s: (0, 0)),
                  pl.BlockSpec((1, d), lambda s: (0, 0))],
        out_specs=pl.BlockSpec((batch, tt, d), lambda s: (0, proj_tile(s), 0)),
        out_shape=jax.ShapeDtypeStruct((batch, seq, d), F32),
        scratch_shapes=[pltpu.VMEM((tm, width), F32),
                        pltpu.VMEM((tm, width), BF16),
                        pltpu.VMEM((d // LANES, tm, LANES), F32),
                        pltpu.VMEM((SUBLANES, width), F32),
                        pltpu.VMEM((n_head, head, two_head), BF16),
                        *_weight_scratch(w_out)],
        compiler_params=_params(("arbitrary",)),
        name="lru_bwd_out_proj",
    )(xc, hf, sg, w_gate, b_gate, lru_lambda, out_norm_g, yp, x, mod3, w_out, b_out, final_g)


def kernel(x, c, norm_g, w_ada, b_ada, w_in, b_in, w_pool, b_pool, pool_scale, conv_w, conv_b, w_gate,
           b_gate, lru_lambda, out_norm_pool_g, out_norm_lru_g, w_out, b_out, final_norm_g):
    batch, seq, d = x.shape
    assert w_in.shape[0] == 1, "single-layer block only"
    mod3 = _adaln_mod(c, w_ada, b_ada)
    tile = TIME_TILE_ROWS
    z, sg = _in_proj(x, mod3, norm_g, w_in[0], b_in, tm=tile)
    yp = _pool_mixer(z, sg, w_pool[0], b_pool[0], pool_scale, out_norm_pool_g, batch=batch, seq=seq,
                     ts=POOL_TILE_ROWS)
    xc, hf = _lru_fwd(z, b_in, conv_w[0], conv_b, w_gate[0], b_gate[0], lru_lambda[0], batch=batch, seq=seq, tc=tile)
    return _lru_bwd_out_proj(xc, hf, sg, w_gate[0], b_gate[0], lru_lambda[0], out_norm_lru_g, yp, x, mod3,
                             w_out[0], b_out, final_norm_g[None, :], tm=tile)
```

```python
import functools
import math

import jax
import jax.numpy as jnp
from jax import lax
from jax.experimental import pallas as pl
from jax.experimental.pallas import tpu as pltpu

EPS = 1e-6
LRU_C = 8.0
POOL_WINDOWS = (2, 4, 8, 16)
N_LRU_HEADS = 4
CONV_WIDTH = 4
LANES = 128
SUBLANES = 8
BF16_ROW_TILE = 16
V7X_VMEM_LIMIT_BYTES = 56 * 1024 * 1024
TIME_TILE_ROWS = 512
POOL_TILE_ROWS = 1024
N_Z_BLOCKS = 4
W_IN_STAGE_COLS = 512
WEIGHT_STAGE_ROWS = 256
LOG2E = math.log2(math.e)
LN2 = math.log(2.0)
F32_MIN_NORMAL = float(jnp.finfo(jnp.float32).tiny)

F32 = jnp.float32
BF16 = jnp.bfloat16


def _params(semantics):
    return pltpu.CompilerParams(dimension_semantics=semantics,
                                vmem_limit_bytes=V7X_VMEM_LIMIT_BYTES)


def _sigmoid(v):
    return 0.5 * jnp.tanh(0.5 * v) + 0.5


def _silu(v):
    return v * _sigmoid(v)


def _halo_rows(steps, batch):
    return -(-steps * batch // SUBLANES) * SUBLANES


def _shift_rows(v, k):
    rows, width = v.shape
    v3 = v.reshape(rows // SUBLANES, SUBLANES, width)
    q, r = divmod(k, SUBLANES)
    if q % v3.shape[0]:
        g = q % v3.shape[0]
        v3 = jnp.concatenate([v3[-g:], v3[:-g]], axis=0)
    if r:
        sub = lax.broadcasted_iota(jnp.int32, v3.shape, 1)
        rot = pltpu.roll(v3, r, 1)
        other = jnp.concatenate([rot[-1:], rot[:-1]], axis=0)
        v3 = jnp.where(sub >= r, rot, other)
    return v3.reshape(rows, width)


def _weight_stream(w_hbm, w_vmem, stage, sem, row_gain=None):
    slots, rows, _ = stage.shape
    n_chunks = w_hbm.shape[0] // rows

    def copy(c):
        return pltpu.make_async_copy(w_hbm.at[pl.ds(c * rows, rows), :], stage.at[c % slots], sem.at[c % slots])

    def begin():
        for c in range(min(slots, n_chunks)):
            copy(c).start()

    def advance(lo, hi):
        for c in range(lo, hi):
            copy(c).wait()
            chunk = stage[c % slots]
            if row_gain is not None:
                chunk = chunk * row_gain(c)
            w_vmem[c * rows:(c + 1) * rows, :] = chunk.astype(BF16)
            if c + slots < n_chunks:
                copy(c + slots).start()

    return begin, advance, n_chunks


def _weight_scratch(w):
    k, n = w.shape
    assert k % WEIGHT_STAGE_ROWS == 0
    return [pltpu.VMEM((k, n), BF16), pltpu.VMEM((2, WEIGHT_STAGE_ROWS, n), F32), pltpu.SemaphoreType.DMA((2,))]


ADALN_STAGE_SLOTS = 4


def _adaln_kernel(c_ref, w_hbm, b_ref, o_ref, stage, sem):
    batch, d = c_ref.shape
    slots, rows, n = stage.shape
    n_chunks = d // rows
    ca = _silu(c_ref[...]).astype(BF16)
    ca = jnp.concatenate([ca, jnp.zeros((-batch % BF16_ROW_TILE, d), BF16)], axis=0)

    def copy(k):
        return pltpu.make_async_copy(w_hbm.at[0, pl.ds(k * rows, rows), :], stage.at[k % slots], sem.at[k % slots])

    for k in range(min(slots - 1, n_chunks)):
        copy(k).start()
    acc = jnp.zeros((ca.shape[0], n), F32)
    for k in range(n_chunks):
        if k + slots - 1 < n_chunks:
            copy(k + slots - 1).start()
        copy(k).wait()
        acc = acc + jnp.dot(ca[:, k * rows:(k + 1) * rows], stage[k % slots].astype(BF16),
                            preferred_element_type=F32)
    res = acc[:batch] + b_ref[...]
    for part in range(o_ref.shape[0]):
        o_ref[part] = res[:, part * d:(part + 1) * d]


def _adaln_mod(c, w_ada, b_ada):
    batch, d = c.shape
    n = w_ada.shape[2]
    assert d % WEIGHT_STAGE_ROWS == 0 and n == 3 * d
    return pl.pallas_call(
        _adaln_kernel,
        in_specs=[pl.BlockSpec(memory_space=pltpu.VMEM),
                  pl.BlockSpec(memory_space=pl.ANY),
                  pl.BlockSpec(memory_space=pltpu.VMEM)],
        out_specs=pl.BlockSpec(memory_space=pltpu.VMEM),
        out_shape=jax.ShapeDtypeStruct((3, batch, d), F32),
        scratch_shapes=[pltpu.VMEM((ADALN_STAGE_SLOTS, WEIGHT_STAGE_ROWS, n), F32),
                        pltpu.SemaphoreType.DMA((ADALN_STAGE_SLOTS,))],
        compiler_params=pltpu.CompilerParams(vmem_limit_bytes=V7X_VMEM_LIMIT_BYTES),
        name="adaln_mod",
    )(c, w_ada, b_ada)


def _in_proj_kernel(x_ref, mod_ref, g_ref, w_hbm, b_ref, u_ref, sg_ref, w_ref, stage, sem, hs_ref, *, n_chunk):
    first = pl.program_id(0) == 0
    n = w_ref.shape[1]
    n_mix = u_ref.shape[1]
    batch, tt, d = x_ref.shape

    def normalised_input():
        for b in range(batch):
            x = x_ref[b]
            r = lax.rsqrt(jnp.mean(x * x, axis=-1, keepdims=True) + EPS)
            h = (x * r) * (g_ref[...] * (1.0 + mod_ref[1, b:b + 1, :])) + mod_ref[0, b:b + 1, :]
            for c in range(d // LANES):
                hs_ref[c, pl.ds(b, tt, stride=batch), :] = h[:, c * LANES:(c + 1) * LANES]
        return jnp.concatenate([hs_ref[c] for c in range(d // LANES)], axis=1).astype(BF16)

    def project(hb, sl):
        zj = jnp.dot(hb, w_ref[:, sl], preferred_element_type=F32)
        if sl.start < n_mix:
            u_ref[:, sl] = zj + b_ref[:, sl]
        else:
            half = zj + 0.5 * b_ref[:, sl]
            sg_ref[:, sl.start - n_mix:sl.stop - n_mix] = (half * jnp.tanh(half) + half).astype(sg_ref.dtype)

    @pl.when(first)
    def _():
        cols = stage.shape[2]

        def copy(c):
            return pltpu.make_async_copy(w_hbm.at[:, pl.ds(c * cols, cols)], stage.at[c % 2], sem.at[c % 2])

        copy(0).start()
        hb = normalised_input()
        for c in range(n // cols):
            if c + 1 < n // cols:
                copy(c + 1).start()
            copy(c).wait()
            sl = slice(c * cols, (c + 1) * cols)
            halve = 0.5 if sl.start >= n_mix else 1.0
            w_ref[:, sl] = (halve * stage[c % 2]).astype(BF16)
            project(hb, sl)

    @pl.when(jnp.logical_not(first))
    def _():
        hb = normalised_input()
        for j in range(n // n_chunk):
            project(hb, slice(j * n_chunk, (j + 1) * n_chunk))


def _in_proj(x, mod3, norm_g, w_in, b_in, *, tm):
    batch, seq, d = x.shape
    n = w_in.shape[1]
    tt = tm // batch
    rows = batch * seq
    assert (n // 2) % W_IN_STAGE_COLS == 0 and (n // 2) % (n // N_Z_BLOCKS) == 0
    return pl.pallas_call(
        functools.partial(_in_proj_kernel, n_chunk=n // N_Z_BLOCKS),
        grid=(seq // tt,),
        in_specs=[pl.BlockSpec((batch, tt, d), lambda i: (0, i, 0)),
                  pl.BlockSpec(mod3.shape, lambda i: (0, 0, 0)),
                  pl.BlockSpec((1, d), lambda i: (0, 0)),
                  pl.BlockSpec(memory_space=pl.ANY),
                  pl.BlockSpec((1, n), lambda i: (0, 0))],
        out_specs=[pl.BlockSpec((tm, n // 2), lambda i: (i, 0)),
                   pl.BlockSpec((tm, n // 2), lambda i: (i, 0))],
        out_shape=[jax.ShapeDtypeStruct((rows, n // 2), F32), jax.ShapeDtypeStruct((rows, n // 2), BF16)],
        scratch_shapes=[pltpu.VMEM(w_in.shape, BF16),
                        pltpu.VMEM((2, d, W_IN_STAGE_COLS), F32),
                        pltpu.SemaphoreType.DMA((2,)),
                        pltpu.VMEM((d // LANES, tm, LANES), F32)],
        compiler_params=_params(("arbitrary",)),
        name="in_proj",
    )(x, mod3, norm_g, w_in, b_in)


def _halo_specs(width, col_block, *, halo, ts, total_rows):
    per_tile = ts // halo
    last = total_rows // halo - 1
    return (pl.BlockSpec((halo, width), lambda i: (jnp.maximum(i * per_tile - 1, 0), col_block)),
            pl.BlockSpec((halo, width), lambda i: (jnp.minimum((i + 1) * per_tile, last), col_block)))


def _with_halo(prev_ref, cur_ref, next_ref, tile, n_tiles):
    prev = jnp.where(tile > 0, prev_ref[...], 0.0)
    nxt = jnp.where(tile < n_tiles - 1, next_ref[...], 0.0)
    return jnp.concatenate([prev, cur_ref[...], nxt], axis=0)


def _pool_kernel(prev_ref, u_ref, next_ref, gate_ref, w_ref, b_ref, scale_ref, o_ref,
                 *, batch, seq, ts, n_tiles):
    i = pl.program_id(0)
    halo = prev_ref.shape[0]
    grp = u_ref.shape[1] // len(POOL_WINDOWS)

    def mix(at_sequence_end):
        ue = _with_halo(prev_ref, u_ref, next_ref, i, n_tiles)
        row = lax.broadcasted_iota(jnp.int32, (ts, 1), 0)
        t = i * (ts // batch) + row // batch
        mixed = []
        for g, w in enumerate(POOL_WINDOWS):
            cols = slice(g * grp, (g + 1) * grp)
            e = ue[:, cols]
            lo = w // 2
            hi = w - lo - 1
            s, span = e, 1
            while span < w:
                s = s + _shift_rows(s, span * batch)
                span *= 2
            total = (_shift_rows(s, -hi * batch) if hi else s)[halo:halo + ts]
            if at_sequence_end:
                cnt = (jnp.minimum(t + hi, seq - 1) - jnp.maximum(t - lo, 0) + 1).astype(F32)
            else:
                cnt = float(w)
            pooled = total / cnt - e[halo:halo + ts]
            scale = scale_ref[:, cols]
            m = jnp.dot(pooled.astype(BF16), (w_ref[g] * scale).astype(BF16), preferred_element_type=F32)
            mixed.append(m + b_ref[g:g + 1, :] * scale)
        y = jnp.concatenate(mixed, axis=-1)
        r = lax.rsqrt(jnp.mean(y * y, axis=-1, keepdims=True) + EPS)
        o_ref[...] = (y * r).astype(o_ref.dtype) * gate_ref[...]

    interior = (i > 0) & (i < n_tiles - 1)
    pl.when(jnp.logical_not(interior))(functools.partial(mix, True))
    pl.when(interior)(functools.partial(mix, False))


def _pool_mixer(z, sg, w_pool, b_pool, pool_scale, *, batch, seq, ts):
    width = pool_scale.shape[1]
    n_tiles = batch * seq // ts
    prev_spec, next_spec = _halo_specs(width, 0, halo=_halo_rows(max(POOL_WINDOWS) // 2, batch), ts=ts,
                                       total_rows=batch * seq)
    n_grp, grp, _ = w_pool.shape
    assert ts // batch >= max(POOL_WINDOWS)
    return pl.pallas_call(
        functools.partial(_pool_kernel, batch=batch, seq=seq, ts=ts, n_tiles=n_tiles),
        grid=(n_tiles,),
        in_specs=[prev_spec,
                  pl.BlockSpec((ts, width), lambda i: (i, 0)),
                  next_spec,
                  pl.BlockSpec((ts, width), lambda i: (i, 0)),
                  pl.BlockSpec((n_grp, grp, grp), lambda i: (0, 0, 0)),
                  pl.BlockSpec((n_grp, grp), lambda i: (0, 0)),
                  pl.BlockSpec((1, width), lambda i: (0, 0))],
        out_specs=pl.BlockSpec((ts, width), lambda i: (i, 0)),
        out_shape=jax.ShapeDtypeStruct((batch * seq, width), BF16),
        compiler_params=_params(("arbitrary",)),
        name="pool_mixer",
    )(z, z, z, sg, w_pool, b_pool, pool_scale)


def _softplus(v):
    return jnp.maximum(v, 0.0) + jnp.log1p(jnp.exp(-jnp.abs(v)))


def _lru_direction(xc_half, wg_ref, bg_half, rate2, carry_ref, h_ref, *, batch, reverse, after_gates=None):
    tc, width = xc_half.shape
    head = width // N_LRU_HEADS
    groups = tc // SUBLANES
    steps = SUBLANES // batch
    sub = lax.broadcasted_iota(jnp.int32, (SUBLANES, head), 0)
    step_of_row = (SUBLANES - 1 - sub if reverse else sub) // batch
    shift = SUBLANES - batch if reverse else batch

    for h in range(N_LRU_HEADS):
        cols = slice(h * head, (h + 1) * head)
        xh = xc_half[:, cols]
        g = jnp.dot(xh.astype(BF16), wg_ref[h], preferred_element_type=F32) + bg_half[h:h + 1, :]
        if after_gates is not None:
            after_gates(h)
        tr = jnp.tanh(g[:, :head])
        ti = jnp.tanh(g[:, head:])
        log2a = tr * rate2[:, cols] + rate2[:, cols]
        a = jnp.exp2(log2a)
        one_minus_a2 = (a * a + 1.0) * jnp.tanh(log2a * (-LN2))
        mult = one_minus_a2 * lax.rsqrt(jnp.maximum(one_minus_a2, F32_MIN_NORMAL))
        b = mult * (ti + 1.0) * xh

        a3 = a.reshape(groups, SUBLANES, head)
        b3 = b.reshape(groups, SUBLANES, head)
        state = carry_ref[:, cols]
        order = range(groups - 1, -1, -1) if reverse else range(groups)
        for gi in order:
            passes = []
            for _ in range(steps):
                state = b3[gi] + a3[gi] * pltpu.roll(state, shift, 0)
                passes.append(state)
            out = passes[0]
            for k in range(1, steps):
                out = jnp.where(step_of_row == k, passes[k], out)
            h_ref[gi * SUBLANES:(gi + 1) * SUBLANES, cols] = out
        carry_ref[:, cols] = state


def _lru_rates(lam_row, bg_ref):
    rate2 = (-0.5 * LRU_C * LOG2E) * _softplus(-lam_row)
    return rate2, 0.5 * bg_ref[0]


def _cast_gate_weights(wg_ref, wgb_ref):
    for h in range(N_LRU_HEADS):
        wgb_ref[h] = wg_ref[0, h].astype(BF16)


def _lru_fwd_kernel(prev_ref, u_ref, next_ref, cw_ref, cb_ref, wg_ref, bg_ref, lam_ref,
                    xc_ref, hf_ref, carry_ref, wgb_ref, *, batch, tc, n_tiles):
    i = pl.program_id(0)

    @pl.when(i == 0)
    def _():
        _cast_gate_weights(wg_ref, wgb_ref)
        carry_ref[...] = jnp.zeros_like(carry_ref)

    rate2, bg_half = _lru_rates(lam_ref[0:1, :], bg_ref)
    ue = _with_halo(prev_ref, u_ref, next_ref, i, n_tiles)
    halo = prev_ref.shape[0]
    left = CONV_WIDTH // 2
    xc_half = 0.5 * cb_ref[...]
    for k in range(CONV_WIDTH):
        tap = ue if k == left else _shift_rows(ue, (left - k) * batch)
        xc_half = xc_half + tap[halo:halo + tc] * (0.5 * cw_ref[k:k + 1, :])
    xc_ref[...] = xc_half
    _lru_direction(xc_half, wgb_ref, bg_half, rate2, carry_ref, hf_ref, batch=batch, reverse=False)


def _lru_fwd(z, conv_w, conv_b, w_gate, b_gate, lru_lambda, *, batch, seq, tc):
    width = conv_b.shape[1]
    n_tiles = batch * seq // tc
    assert SUBLANES % batch == 0, "a group of 8 rows must hold whole time steps"
    prev_spec, next_spec = _halo_specs(width, 1, halo=_halo_rows(CONV_WIDTH // 2, batch), ts=tc,
                                       total_rows=batch * seq)
    _, n_head, head, two_head = w_gate.shape
    tile = pl.BlockSpec((tc, width), lambda i: (i, 0))
    return pl.pallas_call(
        functools.partial(_lru_fwd_kernel, batch=batch, tc=tc, n_tiles=n_tiles),
        grid=(n_tiles,),
        in_specs=[prev_spec,
                  pl.BlockSpec((tc, width), lambda i: (i, 1)),
                  next_spec,
                  pl.BlockSpec((CONV_WIDTH, width), lambda i: (0, 0)),
                  pl.BlockSpec((1, width), lambda i: (0, 0)),
                  pl.BlockSpec((1, n_head, head, two_head), lambda i: (0, 0, 0, 0)),
                  pl.BlockSpec((1, n_head, two_head), lambda i: (0, 0, 0)),
                  pl.BlockSpec(lru_lambda.shape, lambda i: (0, 0))],
        out_specs=[tile, tile],
        out_shape=[jax.ShapeDtypeStruct((batch * seq, width), F32)] * 2,
        scratch_shapes=[pltpu.VMEM((SUBLANES, width), F32),
                        pltpu.VMEM((n_head, head, two_head), BF16)],
        compiler_params=_params(("arbitrary",)),
        name="lru_fwd",
    )(z, z, z, conv_w, conv_b, w_gate, b_gate, lru_lambda)


def _lru_bwd_out_kernel(xc_ref, hf_ref, gate_ref, wg_ref, bg_ref, lam_ref, ngp_ref, ng_ref,
                        yp_ref, x_ref, mod_ref, w_hbm, b_ref, fg_ref,
                        o_ref, hb_ref, yl_ref, y_ref, carry_ref, wgb_ref, w_ref, stage, sem, *, n_tiles):
    s = pl.program_id(0)
    kp = yp_ref.shape[1]
    batch, tt, d = x_ref.shape
    n_chunk = d // N_LRU_HEADS

    def project_chunk(h):
        cols = slice(h * n_chunk, (h + 1) * n_chunk)
        y = jnp.dot(yp_ref[...], w_ref[:kp, cols], preferred_element_type=F32)
        y = y + jnp.dot(yl_ref[...], w_ref[kp:, cols], preferred_element_type=F32)
        y = y + b_ref[:, cols]
        for c in range(n_chunk // LANES):
            y_ref[h * (n_chunk // LANES) + c] = y[:, c * LANES:(c + 1) * LANES]

    def scan(after_gates):
        rate2, bg_half = _lru_rates(lam_ref[1:2, :], bg_ref)
        _lru_direction(xc_ref[...], wgb_ref, bg_half, rate2, carry_ref, hb_ref, batch=batch, reverse=True,
                       after_gates=after_gates)

    def finish_scan():
        yl = hf_ref[...] + hb_ref[...]
        rl = lax.rsqrt(jnp.mean(yl * yl, axis=-1, keepdims=True) + EPS)
        yl_ref[...] = (yl * rl).astype(yl_ref.dtype) * gate_ref[...]

    def finish_projection():
        for b in range(batch):
            y = jnp.concatenate([y_ref[c, pl.ds(b, tt, stride=batch), :] for c in range(d // LANES)], axis=1)
            xn = x_ref[b] + mod_ref[2, b:b + 1, :] * y
            r = lax.rsqrt(jnp.mean(xn * xn, axis=-1, keepdims=True) + EPS)
            o_ref[b] = (xn * r) * fg_ref[...]

    @pl.when(s == 0)
    def _():
        _cast_gate_weights(wg_ref, wgb_ref)
        carry_ref[...] = jnp.zeros_like(carry_ref)
        rows = stage.shape[1]

        def row_gain(c):
            g, lo = (ngp_ref, c * rows) if c * rows < kp else (ng_ref, c * rows - kp)
            return jnp.transpose(jnp.broadcast_to(g[:, lo:lo + rows], (LANES, rows)))[:, :1]

        begin, advance, n_chunks = _weight_stream(w_hbm, w_ref, stage, sem, row_gain)
        per_head = n_chunks // N_LRU_HEADS
        begin()
        scan(lambda h: advance(h * per_head, (h + 1) * per_head))
        advance(N_LRU_HEADS * per_head, n_chunks)
        finish_scan()

    @pl.when((s > 0) & (s < n_tiles))
    def _():
        scan(project_chunk)
        finish_projection()
        finish_scan()

    @pl.when(s == n_tiles)
    def _():
        for h in range(N_LRU_HEADS):
            project_chunk(h)
        finish_projection()


def _lru_bwd_out_proj(xc, hf, sg, w_gate, b_gate, lru_lambda, out_norm_pool_g, out_norm_lru_g, yp, x, mod3, w_out,
                      b_out, final_g, *, tm):
    batch, seq, d = x.shape
    width = xc.shape[1]
    tt = tm // batch
    n_tiles = seq // tt
    _, n_head, head, two_head = w_gate.shape

    def scan_tile(s):
        return jnp.maximum(n_tiles - 1 - s, 0)

    def proj_tile(s):
        return jnp.minimum(n_tiles - s, n_tiles - 1)

    return pl.pallas_call(
        functools.partial(_lru_bwd_out_kernel, n_tiles=n_tiles),
        grid=(n_tiles + 1,),
        in_specs=[pl.BlockSpec((tm, width), lambda s: (scan_tile(s), 0)),
                  pl.BlockSpec((tm, width), lambda s: (scan_tile(s), 0)),
                  pl.BlockSpec((tm, width), lambda s: (scan_tile(s), 1)),
                  pl.BlockSpec((1, n_head, head, two_head), lambda s: (1, 0, 0, 0)),
                  pl.BlockSpec((1, n_head, two_head), lambda s: (1, 0, 0)),
                  pl.BlockSpec(lru_lambda.shape, lambda s: (0, 0)),
                  pl.BlockSpec((1, yp.shape[1]), lambda s: (0, 0)),
                  pl.BlockSpec((1, width), lambda s: (0, 0)),
                  pl.BlockSpec((tm, yp.shape[1]), lambda s: (proj_tile(s), 0)),
                  pl.BlockSpec((batch, tt, d), lambda s: (0, proj_tile(s), 0)),
                  pl.BlockSpec(mod3.shape, lambda s: (0, 0, 0)),
                  pl.BlockSpec(memory_space=pl.ANY),
                  pl.BlockSpec((1, d), lambda s: (0, 0)),
                  pl.BlockSpec((1, d), lambda s: (0, 0))],
        out_specs=pl.BlockSpec((batch, tt, d), lambda s: (0, proj_tile(s), 0)),
        out_shape=jax.ShapeDtypeStruct((batch, seq, d), F32),
        scratch_shapes=[pltpu.VMEM((tm, width), F32),
                        pltpu.VMEM((tm, width), BF16),
                        pltpu.VMEM((d // LANES, tm, LANES), F32),
                        pltpu.VMEM((SUBLANES, width), F32),
                        pltpu.VMEM((n_head, head, two_head), BF16),
                        *_weight_scratch(w_out)],
        compiler_params=_params(("arbitrary",)),
        name="lru_bwd_out_proj",
    )(xc, hf, sg, w_gate, b_gate, lru_lambda, out_norm_pool_g, out_norm_lru_g, yp, x, mod3, w_out, b_out, final_g)


def kernel(x, c, norm_g, w_ada, b_ada, w_in, b_in, w_pool, b_pool, pool_scale, conv_w, conv_b, w_gate,
           b_gate, lru_lambda, out_norm_pool_g, out_norm_lru_g, w_out, b_out, final_norm_g):
    batch, seq, d = x.shape
    assert w_in.shape[0] == 1, "single-layer block only"
    mod3 = _adaln_mod(c, w_ada, b_ada)
    tile = TIME_TILE_ROWS
    z, sg = _in_proj(x, mod3, norm_g, w_in[0], b_in, tm=tile)
    yp = _pool_mixer(z, sg, w_pool[0], b_pool[0], pool_scale, batch=batch, seq=seq, ts=POOL_TILE_ROWS)
    xc, hf = _lru_fwd(z, conv_w[0], conv_b, w_gate[0], b_gate[0], lru_lambda[0], batch=batch, seq=seq, tc=tile)
    return _lru_bwd_out_proj(xc, hf, sg, w_gate[0], b_gate[0], lru_lambda[0], out_norm_pool_g, out_norm_lru_g, yp,
                             x, mod3, w_out[0], b_out, final_norm_g[None, :], tm=tile)
```

```python
import functools
import math

import jax
import jax.numpy as jnp
from jax import lax
from jax.experimental import pallas as pl
from jax.experimental.pallas import tpu as pltpu

EPS = 1e-6
LRU_C = 8.0
POOL_WINDOWS = (2, 4, 8, 16)
N_LRU_HEADS = 4
CONV_WIDTH = 4
LANES = 128
SUBLANES = 8
BF16_ROW_TILE = 16
V7X_VMEM_LIMIT_BYTES = 56 * 1024 * 1024
TIME_TILE_ROWS = 512
N_Z_BLOCKS = 4
W_IN_STAGE_COLS = 512
WEIGHT_STAGE_ROWS = 256
LOG2E = math.log2(math.e)
LN2 = math.log(2.0)
F32_MIN_NORMAL = float(jnp.finfo(jnp.float32).tiny)

F32 = jnp.float32
BF16 = jnp.bfloat16


def _params(semantics):
    return pltpu.CompilerParams(dimension_semantics=semantics,
                                vmem_limit_bytes=V7X_VMEM_LIMIT_BYTES)


def _sigmoid(v):
    return 0.5 * jnp.tanh(0.5 * v) + 0.5


def _silu(v):
    return v * _sigmoid(v)


def _halo_rows(steps, batch):
    return -(-steps * batch // SUBLANES) * SUBLANES


def _shift_rows(v, k):
    rows, width = v.shape
    v3 = v.reshape(rows // SUBLANES, SUBLANES, width)
    q, r = divmod(k, SUBLANES)
    if q % v3.shape[0]:
        g = q % v3.shape[0]
        v3 = jnp.concatenate([v3[-g:], v3[:-g]], axis=0)
    if r:
        sub = lax.broadcasted_iota(jnp.int32, v3.shape, 1)
        rot = pltpu.roll(v3, r, 1)
        other = jnp.concatenate([rot[-1:], rot[:-1]], axis=0)
        v3 = jnp.where(sub >= r, rot, other)
    return v3.reshape(rows, width)


def _weight_stream(w_hbm, w_vmem, stage, sem, row_gain=None):
    slots, rows, _ = stage.shape
    n_chunks = w_hbm.shape[0] // rows

    def copy(c):
        return pltpu.make_async_copy(w_hbm.at[pl.ds(c * rows, rows), :], stage.at[c % slots], sem.at[c % slots])

    def begin():
        for c in range(min(slots, n_chunks)):
            copy(c).start()

    def advance(lo, hi):
        for c in range(lo, hi):
            copy(c).wait()
            chunk = stage[c % slots]
            if row_gain is not None:
                chunk = chunk * row_gain(c)
            w_vmem[c * rows:(c + 1) * rows, :] = chunk.astype(BF16)
            if c + slots < n_chunks:
                copy(c + slots).start()

    return begin, advance, n_chunks


def _weight_scratch(w):
    k, n = w.shape
    assert k % WEIGHT_STAGE_ROWS == 0
    return [pltpu.VMEM((k, n), BF16), pltpu.VMEM((2, WEIGHT_STAGE_ROWS, n), F32), pltpu.SemaphoreType.DMA((2,))]


ADALN_STAGE_SLOTS = 4


def _adaln_kernel(c_ref, w_hbm, b_ref, o_ref, stage, sem):
    batch, d = c_ref.shape
    slots, rows, n = stage.shape
    n_chunks = d // rows
    ca = _silu(c_ref[...]).astype(BF16)
    ca = jnp.concatenate([ca, jnp.zeros((-batch % BF16_ROW_TILE, d), BF16)], axis=0)

    def copy(k):
        return pltpu.make_async_copy(w_hbm.at[0, pl.ds(k * rows, rows), :], stage.at[k % slots], sem.at[k % slots])

    for k in range(min(slots - 1, n_chunks)):
        copy(k).start()
    acc = jnp.zeros((ca.shape[0], n), F32)
    for k in range(n_chunks):
        if k + slots - 1 < n_chunks:
            copy(k + slots - 1).start()
        copy(k).wait()
        acc = acc + jnp.dot(ca[:, k * rows:(k + 1) * rows], stage[k % slots].astype(BF16),
                            preferred_element_type=F32)
    res = acc[:batch] + b_ref[...]
    for part in range(o_ref.shape[0]):
        o_ref[part] = res[:, part * d:(part + 1) * d]


def _adaln_mod(c, w_ada, b_ada):
    batch, d = c.shape
    n = w_ada.shape[2]
    assert d % WEIGHT_STAGE_ROWS == 0 and n == 3 * d
    return pl.pallas_call(
        _adaln_kernel,
        in_specs=[pl.BlockSpec(memory_space=pltpu.VMEM),
                  pl.BlockSpec(memory_space=pl.ANY),
                  pl.BlockSpec(memory_space=pltpu.VMEM)],
        out_specs=pl.BlockSpec(memory_space=pltpu.VMEM),
        out_shape=jax.ShapeDtypeStruct((3, batch, d), F32),
        scratch_shapes=[pltpu.VMEM((ADALN_STAGE_SLOTS, WEIGHT_STAGE_ROWS, n), F32),
                        pltpu.SemaphoreType.DMA((ADALN_STAGE_SLOTS,))],
        compiler_params=pltpu.CompilerParams(vmem_limit_bytes=V7X_VMEM_LIMIT_BYTES),
        name="adaln_mod",
    )(c, w_ada, b_ada)


def _in_proj_kernel(x_ref, mod_ref, g_ref, w_hbm, b_ref, u_ref, sg_ref, w_ref, stage, sem, hs_ref, *, n_chunk):
    first = pl.program_id(0) == 0
    n = w_ref.shape[1]
    n_mix = u_ref.shape[1]
    batch, tt, d = x_ref.shape

    def normalised_input():
        for b in range(batch):
            x = x_ref[b]
            r = lax.rsqrt(jnp.mean(x * x, axis=-1, keepdims=True) + EPS)
            h = (x * r) * (g_ref[...] * (1.0 + mod_ref[1, b:b + 1, :])) + mod_ref[0, b:b + 1, :]
            for c in range(d // LANES):
                hs_ref[c, pl.ds(b, tt, stride=batch), :] = h[:, c * LANES:(c + 1) * LANES]
        return jnp.concatenate([hs_ref[c] for c in range(d // LANES)], axis=1).astype(BF16)

    def project(hb, sl):
        zj = jnp.dot(hb, w_ref[:, sl], preferred_element_type=F32)
        if sl.start < n_mix:
            u_ref[:, sl] = zj + b_ref[:, sl]
        else:
            half = zj + 0.5 * b_ref[:, sl]
            sg_ref[:, sl.start - n_mix:sl.stop - n_mix] = (half * jnp.tanh(half) + half).astype(sg_ref.dtype)

    @pl.when(first)
    def _():
        cols = stage.shape[2]

        def copy(c):
            return pltpu.make_async_copy(w_hbm.at[:, pl.ds(c * cols, cols)], stage.at[c % 2], sem.at[c % 2])

        copy(0).start()
        hb = normalised_input()
        for c in range(n // cols):
            if c + 1 < n // cols:
                copy(c + 1).start()
            copy(c).wait()
            sl = slice(c * cols, (c + 1) * cols)
            halve = 0.5 if sl.start >= n_mix else 1.0
            w_ref[:, sl] = (halve * stage[c % 2]).astype(BF16)
            project(hb, sl)

    @pl.when(jnp.logical_not(first))
    def _():
        hb = normalised_input()
        for j in range(n // n_chunk):
            project(hb, slice(j * n_chunk, (j + 1) * n_chunk))


def _in_proj(x, mod3, norm_g, w_in, b_in, *, tm):
    batch, seq, d = x.shape
    n = w_in.shape[1]
    tt = tm // batch
    rows = batch * seq
    assert (n // 2) % W_IN_STAGE_COLS == 0 and (n // 2) % (n // N_Z_BLOCKS) == 0
    return pl.pallas_call(
        functools.partial(_in_proj_kernel, n_chunk=n // N_Z_BLOCKS),
        grid=(seq // tt,),
        in_specs=[pl.BlockSpec((batch, tt, d), lambda i: (0, i, 0)),
                  pl.BlockSpec(mod3.shape, lambda i: (0, 0, 0)),
                  pl.BlockSpec((1, d), lambda i: (0, 0)),
                  pl.BlockSpec(memory_space=pl.ANY),
                  pl.BlockSpec((1, n), lambda i: (0, 0))],
        out_specs=[pl.BlockSpec((tm, n // 2), lambda i: (i, 0)),
                   pl.BlockSpec((tm, n // 2), lambda i: (i, 0))],
        out_shape=[jax.ShapeDtypeStruct((rows, n // 2), F32), jax.ShapeDtypeStruct((rows, n // 2), BF16)],
        scratch_shapes=[pltpu.VMEM(w_in.shape, BF16),
                        pltpu.VMEM((2, d, W_IN_STAGE_COLS), F32),
                        pltpu.SemaphoreType.DMA((2,)),
                        pltpu.VMEM((d // LANES, tm, LANES), F32)],
        compiler_params=_params(("arbitrary",)),
        name="in_proj",
    )(x, mod3, norm_g, w_in, b_in)


def _halo_specs(width, *, halo, ts, total_rows):
    per_tile = ts // halo
    last = total_rows // halo - 1
    return (pl.BlockSpec((halo, width), lambda i: (jnp.maximum(i * per_tile - 1, 0), 0)),
            pl.BlockSpec((halo, width), lambda i: (jnp.minimum((i + 1) * per_tile, last), 0)))


def _with_halo(prev, cur, nxt, tile, n_tiles):
    prev = jnp.where(tile > 0, prev, 0.0)
    nxt = jnp.where(tile < n_tiles - 1, nxt, 0.0)
    return jnp.concatenate([prev, cur, nxt], axis=0)


def _pool_mix(ue, gate_ref, w_ref, b_ref, scale_ref, o_ref, *, tile, batch, seq, halo, at_sequence_end):
    ts = o_ref.shape[0]
    grp = ue.shape[1] // len(POOL_WINDOWS)
    row = lax.broadcasted_iota(jnp.int32, (ts, 1), 0)
    t = tile * (ts // batch) + row // batch
    mixed = []
    for g, w in enumerate(POOL_WINDOWS):
        cols = slice(g * grp, (g + 1) * grp)
        e = ue[:, cols]
        lo = w // 2
        hi = w - lo - 1
        s, span = e, 1
        while span < w:
            s = s + _shift_rows(s, span * batch)
            span *= 2
        total = (_shift_rows(s, -hi * batch) if hi else s)[halo:halo + ts]
        if at_sequence_end:
            cnt = (jnp.minimum(t + hi, seq - 1) - jnp.maximum(t - lo, 0) + 1).astype(F32)
        else:
            cnt = float(w)
        pooled = total / cnt - e[halo:halo + ts]
        scale = scale_ref[:, cols]
        m = jnp.dot(pooled.astype(BF16), (w_ref[g] * scale).astype(BF16), preferred_element_type=F32)
        mixed.append(m + b_ref[g:g + 1, :] * scale)
    y = jnp.concatenate(mixed, axis=-1)
    r = lax.rsqrt(jnp.mean(y * y, axis=-1, keepdims=True) + EPS)
    o_ref[...] = (y * r).astype(o_ref.dtype) * gate_ref[...]


def _softplus(v):
    return jnp.maximum(v, 0.0) + jnp.log1p(jnp.exp(-jnp.abs(v)))


def _lru_direction(xc_half, wg_ref, bg_half, rate2, carry_ref, h_ref, *, batch, reverse, after_gates=None):
    tc, width = xc_half.shape
    head = width // N_LRU_HEADS
    groups = tc // SUBLANES
    steps = SUBLANES // batch
    sub = lax.broadcasted_iota(jnp.int32, (SUBLANES, head), 0)
    step_of_row = (SUBLANES - 1 - sub if reverse else sub) // batch
    shift = SUBLANES - batch if reverse else batch

    for h in range(N_LRU_HEADS):
        cols = slice(h * head, (h + 1) * head)
        xh = xc_half[:, cols]
        g = jnp.dot(xh.astype(BF16), wg_ref[h], preferred_element_type=F32) + bg_half[h:h + 1, :]
        if after_gates is not None:
            after_gates(h)
        tr = jnp.tanh(g[:, :head])
        ti = jnp.tanh(g[:, head:])
        log2a = tr * rate2[:, cols] + rate2[:, cols]
        a = jnp.exp2(log2a)
        one_minus_a2 = (a * a + 1.0) * jnp.tanh(log2a * (-LN2))
        mult = one_minus_a2 * lax.rsqrt(jnp.maximum(one_minus_a2, F32_MIN_NORMAL))
        b = mult * (ti + 1.0) * xh

        a3 = a.reshape(groups, SUBLANES, head)
        b3 = b.reshape(groups, SUBLANES, head)
        state = carry_ref[:, cols]
        order = range(groups - 1, -1, -1) if reverse else range(groups)
        for gi in order:
            passes = []
            for _ in range(steps):
                state = b3[gi] + a3[gi] * pltpu.roll(state, shift, 0)
                passes.append(state)
            out = passes[0]
            for k in range(1, steps):
                out = jnp.where(step_of_row == k, passes[k], out)
            h_ref[gi * SUBLANES:(gi + 1) * SUBLANES, cols] = out
        carry_ref[:, cols] = state


def _lru_rates(lam_row, bg_ref):
    rate2 = (-0.5 * LRU_C * LOG2E) * _softplus(-lam_row)
    return rate2, 0.5 * bg_ref[0]


def _cast_gate_weights(wg_ref, wgb_ref):
    for h in range(N_LRU_HEADS):
        wgb_ref[h] = wg_ref[0, h].astype(BF16)


def _mixers_fwd_kernel(prev_ref, z_ref, next_ref, gate_ref, wp_ref, bp_ref, scale_ref, cw_ref, cb_ref, wg_ref,
                       bg_ref, lam_ref, yp_ref, xc_ref, hf_ref, carry_ref, wgb_ref, *, batch, seq, n_tiles):
    i = pl.program_id(0)
    tc = z_ref.shape[0]
    halo = prev_ref.shape[0]
    width = yp_ref.shape[1]

    @pl.when(i == 0)
    def _():
        _cast_gate_weights(wg_ref, wgb_ref)
        carry_ref[...] = jnp.zeros_like(carry_ref)

    def pool(at_sequence_end):
        ue = _with_halo(prev_ref[:, :width], z_ref[:, :width], next_ref[:, :width], i, n_tiles)
        _pool_mix(ue, gate_ref, wp_ref, bp_ref, scale_ref, yp_ref, tile=i, batch=batch, seq=seq, halo=halo,
                  at_sequence_end=at_sequence_end)

    interior = (i > 0) & (i < n_tiles - 1)
    pl.when(jnp.logical_not(interior))(functools.partial(pool, True))
    pl.when(interior)(functools.partial(pool, False))

    rate2, bg_half = _lru_rates(lam_ref[0:1, :], bg_ref)
    ch = _halo_rows(CONV_WIDTH // 2, batch)
    ue = _with_halo(prev_ref[halo - ch:, width:], z_ref[:, width:], next_ref[:ch, width:], i, n_tiles)
    left = CONV_WIDTH // 2
    xc_half = 0.5 * cb_ref[...]
    for k in range(CONV_WIDTH):
        tap = ue if k == left else _shift_rows(ue, (left - k) * batch)
        xc_half = xc_half + tap[ch:ch + tc] * (0.5 * cw_ref[k:k + 1, :])
    xc_ref[...] = xc_half
    _lru_direction(xc_half, wgb_ref, bg_half, rate2, carry_ref, hf_ref, batch=batch, reverse=False)


def _mixers_fwd(z, sg, w_pool, b_pool, pool_scale, conv_w, conv_b, w_gate, b_gate, lru_lambda, *, batch, seq, tc):
    width = conv_b.shape[1]
    n_tiles = batch * seq // tc
    assert SUBLANES % batch == 0, "a group of 8 rows must hold whole time steps"
    assert tc // batch >= max(POOL_WINDOWS)
    prev_spec, next_spec = _halo_specs(z.shape[1], halo=_halo_rows(max(POOL_WINDOWS) // 2, batch), ts=tc,
                                       total_rows=batch * seq)
    n_grp, grp, _ = w_pool.shape
    _, n_head, head, two_head = w_gate.shape
    tile = pl.BlockSpec((tc, width), lambda i: (i, 0))
    return pl.pallas_call(
        functools.partial(_mixers_fwd_kernel, batch=batch, seq=seq, n_tiles=n_tiles),
        grid=(n_tiles,),
        in_specs=[prev_spec,
                  pl.BlockSpec((tc, z.shape[1]), lambda i: (i, 0)),
                  next_spec,
                  tile,
                  pl.BlockSpec((n_grp, grp, grp), lambda i: (0, 0, 0)),
                  pl.BlockSpec((n_grp, grp), lambda i: (0, 0)),
                  pl.BlockSpec((1, width), lambda i: (0, 0)),
                  pl.BlockSpec((CONV_WIDTH, width), lambda i: (0, 0)),
                  pl.BlockSpec((1, width), lambda i: (0, 0)),
                  pl.BlockSpec((1, n_head, head, two_head), lambda i: (0, 0, 0, 0)),
                  pl.BlockSpec((1, n_head, two_head), lambda i: (0, 0, 0)),
                  pl.BlockSpec(lru_lambda.shape, lambda i: (0, 0))],
        out_specs=[tile, tile, tile],
        out_shape=[jax.ShapeDtypeStruct((batch * seq, width), BF16)] + [jax.ShapeDtypeStruct((batch * seq, width), F32)] * 2,
        scratch_shapes=[pltpu.VMEM((SUBLANES, width), F32),
                        pltpu.VMEM((n_head, head, two_head), BF16)],
        compiler_params=_params(("arbitrary",)),
        name="mixers_fwd",
    )(z, z, z, sg, w_pool, b_pool, pool_scale, conv_w, conv_b, w_gate, b_gate, lru_lambda)


def _lru_bwd_out_kernel(xc_ref, hf_ref, gate_ref, wg_ref, bg_ref, lam_ref, ngp_ref, ng_ref,
                        yp_ref, x_ref, mod_ref, w_hbm, b_ref, fg_ref,
                        o_ref, hb_ref, yl_ref, y_ref, carry_ref, wgb_ref, w_ref, stage, sem, *, n_tiles):
    s = pl.program_id(0)
    kp = yp_ref.shape[1]
    batch, tt, d = x_ref.shape
    n_chunk = d // N_LRU_HEADS

    def project_chunk(h):
        cols = slice(h * n_chunk, (h + 1) * n_chunk)
        y = jnp.dot(yp_ref[...], w_ref[:kp, cols], preferred_element_type=F32)
        y = y + jnp.dot(yl_ref[...], w_ref[kp:, cols], preferred_element_type=F32)
        y = y + b_ref[:, cols]
        for c in range(n_chunk // LANES):
            y_ref[h * (n_chunk // LANES) + c] = y[:, c * LANES:(c + 1) * LANES]

    def scan(after_gates):
        rate2, bg_half = _lru_rates(lam_ref[1:2, :], bg_ref)
        _lru_direction(xc_ref[...], wgb_ref, bg_half, rate2, carry_ref, hb_ref, batch=batch, reverse=True,
                       after_gates=after_gates)

    def finish_scan():
        yl = hf_ref[...] + hb_ref[...]
        rl = lax.rsqrt(jnp.mean(yl * yl, axis=-1, keepdims=True) + EPS)
        yl_ref[...] = (yl * rl).astype(yl_ref.dtype) * gate_ref[...]

    def finish_projection():
        for b in range(batch):
            y = jnp.concatenate([y_ref[c, pl.ds(b, tt, stride=batch), :] for c in range(d // LANES)], axis=1)
            xn = x_ref[b] + mod_ref[2, b:b + 1, :] * y
            r = lax.rsqrt(jnp.mean(xn * xn, axis=-1, keepdims=True) + EPS)
            o_ref[b] = (xn * r) * fg_ref[...]

    @pl.when(s == 0)
    def _():
        _cast_gate_weights(wg_ref, wgb_ref)
        carry_ref[...] = jnp.zeros_like(carry_ref)
        rows = stage.shape[1]

        def row_gain(c):
            g, lo = (ngp_ref, c * rows) if c * rows < kp else (ng_ref, c * rows - kp)
            return jnp.transpose(jnp.broadcast_to(g[:, lo:lo + rows], (LANES, rows)))[:, :1]

        begin, advance, n_chunks = _weight_stream(w_hbm, w_ref, stage, sem, row_gain)
        per_head = n_chunks // N_LRU_HEADS
        begin()
        scan(lambda h: advance(h * per_head, (h + 1) * per_head))
        advance(N_LRU_HEADS * per_head, n_chunks)
        finish_scan()

    @pl.when((s > 0) & (s < n_tiles))
    def _():
        scan(project_chunk)
        finish_projection()
        finish_scan()

    @pl.when(s == n_tiles)
    def _():
        for h in range(N_LRU_HEADS):
            project_chunk(h)
        finish_projection()


def _lru_bwd_out_proj(xc, hf, sg, w_gate, b_gate, lru_lambda, out_norm_pool_g, out_norm_lru_g, yp, x, mod3, w_out,
                      b_out, final_g, *, tm):
    batch, seq, d = x.shape
    width = xc.shape[1]
    tt = tm // batch
    n_tiles = seq // tt
    _, n_head, head, two_head = w_gate.shape

    def scan_tile(s):
        return jnp.maximum(n_tiles - 1 - s, 0)

    def proj_tile(s):
        return jnp.minimum(n_tiles - s, n_tiles - 1)

    return pl.pallas_call(
        functools.partial(_lru_bwd_out_kernel, n_tiles=n_tiles),
        grid=(n_tiles + 1,),
        in_specs=[pl.BlockSpec((tm, width), lambda s: (scan_tile(s), 0)),
                  pl.BlockSpec((tm, width), lambda s: (scan_tile(s), 0)),
                  pl.BlockSpec((tm, width), lambda s: (scan_tile(s), 1)),
                  pl.BlockSpec((1, n_head, head, two_head), lambda s: (1, 0, 0, 0)),
                  pl.BlockSpec((1, n_head, two_head), lambda s: (1, 0, 0)),
                  pl.BlockSpec(lru_lambda.shape, lambda s: (0, 0)),
                  pl.BlockSpec((1, yp.shape[1]), lambda s: (0, 0)),
                  pl.BlockSpec((1, width), lambda s: (0, 0)),
                  pl.BlockSpec((tm, yp.shape[1]), lambda s: (proj_tile(s), 0)),
                  pl.BlockSpec((batch, tt, d), lambda s: (0, proj_tile(s), 0)),
                  pl.BlockSpec(mod3.shape, lambda s: (0, 0, 0)),
                  pl.BlockSpec(memory_space=pl.ANY),
                  pl.BlockSpec((1, d), lambda s: (0, 0)),
                  pl.BlockSpec((1, d), lambda s: (0, 0))],
        out_specs=pl.BlockSpec((batch, tt, d), lambda s: (0, proj_tile(s), 0)),
        out_shape=jax.ShapeDtypeStruct((batch, seq, d), F32),
        scratch_shapes=[pltpu.VMEM((tm, width), F32),
                        pltpu.VMEM((tm, width), BF16),
                        pltpu.VMEM((d // LANES, tm, LANES), F32),
                        pltpu.VMEM((SUBLANES, width), F32),
                        pltpu.VMEM((n_head, head, two_head), BF16),
                        *_weight_scratch(w_out)],
        compiler_params=_params(("arbitrary",)),
        name="lru_bwd_out_proj",
    )(xc, hf, sg, w_gate, b_gate, lru_lambda, out_norm_pool_g, out_norm_lru_g, yp, x, mod3, w_out, b_out, final_g)


def kernel(x, c, norm_g, w_ada, b_ada, w_in, b_in, w_pool, b_pool, pool_scale, conv_w, conv_b, w_gate,
           b_gate, lru_lambda, out_norm_pool_g, out_norm_lru_g, w_out, b_out, final_norm_g):
    batch, seq, d = x.shape
    assert w_in.shape[0] == 1, "single-layer block only"
    mod3 = _adaln_mod(c, w_ada, b_ada)
    tile = TIME_TILE_ROWS
    z, sg = _in_proj(x, mod3, norm_g, w_in[0], b_in, tm=tile)
    yp, xc, hf = _mixers_fwd(z, sg, w_pool[0], b_pool[0], pool_scale, conv_w[0], conv_b, w_gate[0], b_gate[0],
                             lru_lambda[0], batch=batch, seq=seq, tc=tile)
    return _lru_bwd_out_proj(xc, hf, sg, w_gate[0], b_gate[0], lru_lambda[0], out_norm_pool_g, out_norm_lru_g, yp,
                             x, mod3, w_out[0], b_out, final_norm_g[None, :], tm=tile)
```

```python
import functools
import math

import jax
import jax.numpy as jnp
from jax import lax
from jax.experimental import pallas as pl
from jax.experimental.pallas import tpu as pltpu

EPS = 1e-6
LRU_C = 8.0
POOL_WINDOWS = (2, 4, 8, 16)
N_LRU_HEADS = 4
CONV_WIDTH = 4
LANES = 128
SUBLANES = 8
BF16_ROW_TILE = 16
V7X_VMEM_LIMIT_BYTES = 56 * 1024 * 1024
TIME_TILE_ROWS = 512
N_Z_BLOCKS = 4
W_IN_STAGE_COLS = 512
WEIGHT_STAGE_ROWS = 256
LOG2E = math.log2(math.e)
LN2 = math.log(2.0)
F32_MIN_NORMAL = float(jnp.finfo(jnp.float32).tiny)

F32 = jnp.float32
BF16 = jnp.bfloat16


def _params(semantics):
    return pltpu.CompilerParams(dimension_semantics=semantics,
                                vmem_limit_bytes=V7X_VMEM_LIMIT_BYTES)


def _sigmoid(v):
    return 0.5 * jnp.tanh(0.5 * v) + 0.5


def _silu(v):
    return v * _sigmoid(v)


def _halo_rows(steps, batch):
    return -(-steps * batch // SUBLANES) * SUBLANES


def _shift_rows(v, k):
    rows, width = v.shape
    v3 = v.reshape(rows // SUBLANES, SUBLANES, width)
    q, r = divmod(k, SUBLANES)
    if q % v3.shape[0]:
        g = q % v3.shape[0]
        v3 = jnp.concatenate([v3[-g:], v3[:-g]], axis=0)
    if r:
        sub = lax.broadcasted_iota(jnp.int32, v3.shape, 1)
        rot = pltpu.roll(v3, r, 1)
        other = jnp.concatenate([rot[-1:], rot[:-1]], axis=0)
        v3 = jnp.where(sub >= r, rot, other)
    return v3.reshape(rows, width)


def _weight_stream(w_hbm, w_vmem, stage, sem, row_gain=None):
    slots, rows, _ = stage.shape
    n_chunks = w_hbm.shape[0] // rows

    def copy(c):
        return pltpu.make_async_copy(w_hbm.at[pl.ds(c * rows, rows), :], stage.at[c % slots], sem.at[c % slots])

    def begin():
        for c in range(min(slots, n_chunks)):
            copy(c).start()

    def advance(lo, hi):
        for c in range(lo, hi):
            copy(c).wait()
            chunk = stage[c % slots]
            if row_gain is not None:
                chunk = chunk * row_gain(c)
            w_vmem[c * rows:(c + 1) * rows, :] = chunk.astype(BF16)
            if c + slots < n_chunks:
                copy(c + slots).start()

    return begin, advance, n_chunks


def _weight_scratch(w):
    k, n = w.shape
    assert k % WEIGHT_STAGE_ROWS == 0
    return [pltpu.VMEM((k, n), BF16), pltpu.VMEM((2, WEIGHT_STAGE_ROWS, n), F32), pltpu.SemaphoreType.DMA((2,))]


ADALN_STAGE_SLOTS = 4


def _adaln_kernel(c_ref, w_hbm, b_ref, o_ref, stage, sem):
    batch, d = c_ref.shape
    slots, rows, n = stage.shape
    n_chunks = d // rows
    ca = _silu(c_ref[...]).astype(BF16)
    ca = jnp.concatenate([ca, jnp.zeros((-batch % BF16_ROW_TILE, d), BF16)], axis=0)

    def copy(k):
        return pltpu.make_async_copy(w_hbm.at[0, pl.ds(k * rows, rows), :], stage.at[k % slots], sem.at[k % slots])

    for k in range(min(slots - 1, n_chunks)):
        copy(k).start()
    acc = jnp.zeros((ca.shape[0], n), F32)
    for k in range(n_chunks):
        if k + slots - 1 < n_chunks:
            copy(k + slots - 1).start()
        copy(k).wait()
        acc = acc + jnp.dot(ca[:, k * rows:(k + 1) * rows], stage[k % slots].astype(BF16),
                            preferred_element_type=F32)
    res = acc[:batch] + b_ref[...]
    for part in range(o_ref.shape[0]):
        o_ref[part] = res[:, part * d:(part + 1) * d]


def _adaln_mod(c, w_ada, b_ada):
    batch, d = c.shape
    n = w_ada.shape[2]
    assert d % WEIGHT_STAGE_ROWS == 0 and n == 3 * d
    return pl.pallas_call(
        _adaln_kernel,
        in_specs=[pl.BlockSpec(memory_space=pltpu.VMEM),
                  pl.BlockSpec(memory_space=pl.ANY),
                  pl.BlockSpec(memory_space=pltpu.VMEM)],
        out_specs=pl.BlockSpec(memory_space=pltpu.VMEM),
        out_shape=jax.ShapeDtypeStruct((3, batch, d), F32),
        scratch_shapes=[pltpu.VMEM((ADALN_STAGE_SLOTS, WEIGHT_STAGE_ROWS, n), F32),
                        pltpu.SemaphoreType.DMA((ADALN_STAGE_SLOTS,))],
        compiler_params=pltpu.CompilerParams(vmem_limit_bytes=V7X_VMEM_LIMIT_BYTES),
        name="adaln_mod",
    )(c, w_ada, b_ada)


def _in_proj_kernel(x_ref, mod_ref, g_ref, w_hbm, b_ref, u_ref, sg_ref, w_ref, stage, sem, hs_ref, *, n_chunk):
    first = pl.program_id(0) == 0
    n = w_ref.shape[1]
    n_mix = u_ref.shape[1]
    batch, tt, d = x_ref.shape

    def normalised_input():
        for b in range(batch):
            x = x_ref[b]
            r = lax.rsqrt(jnp.mean(x * x, axis=-1, keepdims=True) + EPS)
            h = (x * r) * (g_ref[...] * (1.0 + mod_ref[1, b:b + 1, :])) + mod_ref[0, b:b + 1, :]
            for c in range(d // LANES):
                hs_ref[c, pl.ds(b, tt, stride=batch), :] = h[:, c * LANES:(c + 1) * LANES]
        return jnp.concatenate([hs_ref[c] for c in range(d // LANES)], axis=1).astype(BF16)

    def project(hb, sl):
        zj = jnp.dot(hb, w_ref[:, sl], preferred_element_type=F32)
        if sl.start < n_mix:
            u_ref[:, sl] = zj + b_ref[:, sl]
        else:
            half = zj + 0.5 * b_ref[:, sl]
            sg_ref[:, sl.start - n_mix:sl.stop - n_mix] = (half * jnp.tanh(half) + half).astype(sg_ref.dtype)

    @pl.when(first)
    def _():
        cols = stage.shape[2]

        def copy(c):
            return pltpu.make_async_copy(w_hbm.at[:, pl.ds(c * cols, cols)], stage.at[c % 2], sem.at[c % 2])

        copy(0).start()
        hb = normalised_input()
        for c in range(n // cols):
            if c + 1 < n // cols:
                copy(c + 1).start()
            copy(c).wait()
            sl = slice(c * cols, (c + 1) * cols)
            halve = 0.5 if sl.start >= n_mix else 1.0
            w_ref[:, sl] = (halve * stage[c % 2]).astype(BF16)
            project(hb, sl)

    @pl.when(jnp.logical_not(first))
    def _():
        hb = normalised_input()
        for j in range(n // n_chunk):
            project(hb, slice(j * n_chunk, (j + 1) * n_chunk))


def _in_proj(x, mod3, norm_g, w_in, b_in, *, tm):
    batch, seq, d = x.shape
    n = w_in.shape[1]
    tt = tm // batch
    rows = batch * seq
    assert (n // 2) % W_IN_STAGE_COLS == 0 and (n // 2) % (n // N_Z_BLOCKS) == 0
    return pl.pallas_call(
        functools.partial(_in_proj_kernel, n_chunk=n // N_Z_BLOCKS),
        grid=(seq // tt,),
        in_specs=[pl.BlockSpec((batch, tt, d), lambda i: (0, i, 0)),
                  pl.BlockSpec(mod3.shape, lambda i: (0, 0, 0)),
                  pl.BlockSpec((1, d), lambda i: (0, 0)),
                  pl.BlockSpec(memory_space=pl.ANY),
                  pl.BlockSpec((1, n), lambda i: (0, 0))],
        out_specs=[pl.BlockSpec((tm, n // 2), lambda i: (i, 0)),
                   pl.BlockSpec((tm, n // 2), lambda i: (i, 0))],
        out_shape=[jax.ShapeDtypeStruct((rows, n // 2), F32), jax.ShapeDtypeStruct((rows, n // 2), BF16)],
        scratch_shapes=[pltpu.VMEM(w_in.shape, BF16),
                        pltpu.VMEM((2, d, W_IN_STAGE_COLS), F32),
                        pltpu.SemaphoreType.DMA((2,)),
                        pltpu.VMEM((d // LANES, tm, LANES), F32)],
        compiler_params=_params(("arbitrary",)),
        name="in_proj",
    )(x, mod3, norm_g, w_in, b_in)


def _halo_specs(width, *, halo, ts, total_rows):
    per_tile = ts // halo
    last = total_rows // halo - 1
    return (pl.BlockSpec((halo, width), lambda i: (jnp.maximum(i * per_tile - 1, 0), 0)),
            pl.BlockSpec((halo, width), lambda i: (jnp.minimum((i + 1) * per_tile, last), 0)))


def _with_halo(prev, cur, nxt, tile, n_tiles):
    prev = jnp.where(tile > 0, prev, 0.0)
    nxt = jnp.where(tile < n_tiles - 1, nxt, 0.0)
    return jnp.concatenate([prev, cur, nxt], axis=0)


def _pool_mix(ue, gate_ref, w_ref, b_ref, scale_ref, o_ref, *, tile, batch, seq, halo, at_sequence_end):
    ts = o_ref.shape[0]
    grp = ue.shape[1] // len(POOL_WINDOWS)
    assert all(w % 2 == 0 for w in POOL_WINDOWS), "the window sums pair steps of opposite parity"
    row = lax.broadcasted_iota(jnp.int32, (ts, 1), 0)
    t = tile * (ts // batch) + row // batch
    mixed = []
    for g, w in enumerate(POOL_WINDOWS):
        cols = slice(g * grp, (g + 1) * grp)
        e = ue[:, cols]
        lo = w // 2
        hi = w - lo - 1
        s, span = e, 2
        while span < w:
            s = s + _shift_rows(s, span * batch)
            span *= 2
        if hi % 2:
            s = _shift_rows(s, -(hi - 1) * batch)
            total = s + _shift_rows(s, -batch)
        else:
            s = _shift_rows(s, -hi * batch)
            total = s + _shift_rows(s, batch)
        total = total[halo:halo + ts]
        if at_sequence_end:
            cnt = (jnp.minimum(t + hi, seq - 1) - jnp.maximum(t - lo, 0) + 1).astype(F32)
        else:
            cnt = float(w)
        pooled = total / cnt - e[halo:halo + ts]
        scale = scale_ref[:, cols]
        m = jnp.dot(pooled.astype(BF16), (w_ref[g] * scale).astype(BF16), preferred_element_type=F32)
        mixed.append(m + b_ref[g:g + 1, :] * scale)
    y = jnp.concatenate(mixed, axis=-1)
    r = lax.rsqrt(jnp.mean(y * y, axis=-1, keepdims=True) + EPS)
    o_ref[...] = (y * r).astype(o_ref.dtype) * gate_ref[...]


def _softplus(v):
    return jnp.maximum(v, 0.0) + jnp.log1p(jnp.exp(-jnp.abs(v)))


def _lru_direction(xc_half, wg_ref, bg_half, rate2, carry_ref, h_ref, *, batch, reverse, after_gates=None):
    tc, width = xc_half.shape
    head = width // N_LRU_HEADS
    groups = tc // SUBLANES
    steps = SUBLANES // batch
    sub = lax.broadcasted_iota(jnp.int32, (SUBLANES, head), 0)
    step_of_row = (SUBLANES - 1 - sub if reverse else sub) // batch
    shift = SUBLANES - batch if reverse else batch

    for h in range(N_LRU_HEADS):
        cols = slice(h * head, (h + 1) * head)
        xh = xc_half[:, cols]
        g = jnp.dot(xh.astype(BF16), wg_ref[h], preferred_element_type=F32) + bg_half[h:h + 1, :]
        if after_gates is not None:
            after_gates(h)
        tr = jnp.tanh(g[:, :head])
        ti = jnp.tanh(g[:, head:])
        log2a = tr * rate2[:, cols] + rate2[:, cols]
        a = jnp.exp2(log2a)
        one_minus_a2 = (a * a + 1.0) * jnp.tanh(log2a * (-LN2))
        mult = one_minus_a2 * lax.rsqrt(jnp.maximum(one_minus_a2, F32_MIN_NORMAL))
        b = mult * (ti + 1.0) * xh

        a3 = a.reshape(groups, SUBLANES, head)
        b3 = b.reshape(groups, SUBLANES, head)
        state = carry_ref[:, cols]
        order = range(groups - 1, -1, -1) if reverse else range(groups)
        for gi in order:
            passes = []
            for _ in range(steps):
                state = b3[gi] + a3[gi] * pltpu.roll(state, shift, 0)
                passes.append(state)
            out = passes[0]
            for k in range(1, steps):
                out = jnp.where(step_of_row == k, passes[k], out)
            h_ref[gi * SUBLANES:(gi + 1) * SUBLANES, cols] = out
        carry_ref[:, cols] = state


def _lru_rates(lam_row, bg_ref):
    rate2 = (-0.5 * LRU_C * LOG2E) * _softplus(-lam_row)
    return rate2, 0.5 * bg_ref[0]


def _cast_gate_weights(wg_ref, wgb_ref):
    for h in range(N_LRU_HEADS):
        wgb_ref[h] = wg_ref[0, h].astype(BF16)


def _mixers_fwd_kernel(prev_ref, z_ref, next_ref, gate_ref, wp_ref, bp_ref, scale_ref, cw_ref, cb_ref, wg_ref,
                       bg_ref, lam_ref, yp_ref, xc_ref, hf_ref, carry_ref, wgb_ref, *, batch, seq, n_tiles):
    i = pl.program_id(0)
    tc = z_ref.shape[0]
    halo = prev_ref.shape[0]
    width = yp_ref.shape[1]

    @pl.when(i == 0)
    def _():
        _cast_gate_weights(wg_ref, wgb_ref)
        carry_ref[...] = jnp.zeros_like(carry_ref)

    def pool(at_sequence_end):
        ue = _with_halo(prev_ref[:, :width], z_ref[:, :width], next_ref[:, :width], i, n_tiles)
        _pool_mix(ue, gate_ref, wp_ref, bp_ref, scale_ref, yp_ref, tile=i, batch=batch, seq=seq, halo=halo,
                  at_sequence_end=at_sequence_end)

    interior = (i > 0) & (i < n_tiles - 1)
    pl.when(jnp.logical_not(interior))(functools.partial(pool, True))
    pl.when(interior)(functools.partial(pool, False))

    rate2, bg_half = _lru_rates(lam_ref[0:1, :], bg_ref)
    ch = _halo_rows(CONV_WIDTH // 2, batch)
    ue = _with_halo(prev_ref[halo - ch:, width:], z_ref[:, width:], next_ref[:ch, width:], i, n_tiles)
    left = CONV_WIDTH // 2
    xc_half = 0.5 * cb_ref[...]
    for k in range(CONV_WIDTH):
        tap = ue if k == left else _shift_rows(ue, (left - k) * batch)
        xc_half = xc_half + tap[ch:ch + tc] * (0.5 * cw_ref[k:k + 1, :])
    xc_ref[...] = xc_half
    _lru_direction(xc_half, wgb_ref, bg_half, rate2, carry_ref, hf_ref, batch=batch, reverse=False)


def _mixers_fwd(z, sg, w_pool, b_pool, pool_scale, conv_w, conv_b, w_gate, b_gate, lru_lambda, *, batch, seq, tc):
    width = conv_b.shape[1]
    n_tiles = batch * seq // tc
    assert SUBLANES % batch == 0, "a group of 8 rows must hold whole time steps"
    assert tc // batch >= max(POOL_WINDOWS)
    prev_spec, next_spec = _halo_specs(z.shape[1], halo=_halo_rows(max(POOL_WINDOWS) // 2, batch), ts=tc,
                                       total_rows=batch * seq)
    n_grp, grp, _ = w_pool.shape
    _, n_head, head, two_head = w_gate.shape
    tile = pl.BlockSpec((tc, width), lambda i: (i, 0))
    return pl.pallas_call(
        functools.partial(_mixers_fwd_kernel, batch=batch, seq=seq, n_tiles=n_tiles),
        grid=(n_tiles,),
        in_specs=[prev_spec,
                  pl.BlockSpec((tc, z.shape[1]), lambda i: (i, 0)),
                  next_spec,
                  tile,
                  pl.BlockSpec((n_grp, grp, grp), lambda i: (0, 0, 0)),
                  pl.BlockSpec((n_grp, grp), lambda i: (0, 0)),
                  pl.BlockSpec((1, width), lambda i: (0, 0)),
                  pl.BlockSpec((CONV_WIDTH, width), lambda i: (0, 0)),
                  pl.BlockSpec((1, width), lambda i: (0, 0)),
                  pl.BlockSpec((1, n_head, head, two_head), lambda i: (0, 0, 0, 0)),
                  pl.BlockSpec((1, n_head, two_head), lambda i: (0, 0, 0)),
                  pl.BlockSpec(lru_lambda.shape, lambda i: (0, 0))],
        out_specs=[tile, tile, tile],
        out_shape=[jax.ShapeDtypeStruct((batch * seq, width), BF16)] + [jax.ShapeDtypeStruct((batch * seq, width), F32)] * 2,
        scratch_shapes=[pltpu.VMEM((SUBLANES, width), F32),
                        pltpu.VMEM((n_head, head, two_head), BF16)],
        compiler_params=_params(("arbitrary",)),
        name="mixers_fwd",
    )(z, z, z, sg, w_pool, b_pool, pool_scale, conv_w, conv_b, w_gate, b_gate, lru_lambda)


def _lru_bwd_out_kernel(xc_ref, hf_ref, gate_ref, wg_ref, bg_ref, lam_ref, ngp_ref, ng_ref,
                        yp_ref, x_ref, mod_ref, w_hbm, b_ref, fg_ref,
                        o_ref, hb_ref, yl_ref, y_ref, carry_ref, wgb_ref, w_ref, stage, sem, *, n_tiles):
    s = pl.program_id(0)
    kp = yp_ref.shape[1]
    batch, tt, d = x_ref.shape
    n_chunk = d // N_LRU_HEADS

    def project_chunk(h):
        cols = slice(h * n_chunk, (h + 1) * n_chunk)
        y = jnp.dot(yp_ref[...], w_ref[:kp, cols], preferred_element_type=F32)
        y = y + jnp.dot(yl_ref[...], w_ref[kp:, cols], preferred_element_type=F32)
        y = y + b_ref[:, cols]
        for c in range(n_chunk // LANES):
            y_ref[h * (n_chunk // LANES) + c] = y[:, c * LANES:(c + 1) * LANES]

    def scan(after_gates):
        rate2, bg_half = _lru_rates(lam_ref[1:2, :], bg_ref)
        _lru_direction(xc_ref[...], wgb_ref, bg_half, rate2, carry_ref, hb_ref, batch=batch, reverse=True,
                       after_gates=after_gates)

    def finish_scan():
        yl = hf_ref[...] + hb_ref[...]
        rl = lax.rsqrt(jnp.mean(yl * yl, axis=-1, keepdims=True) + EPS)
        yl_ref[...] = (yl * rl).astype(yl_ref.dtype) * gate_ref[...]

    def finish_projection():
        for b in range(batch):
            y = jnp.concatenate([y_ref[c, pl.ds(b, tt, stride=batch), :] for c in range(d // LANES)], axis=1)
            xn = x_ref[b] + mod_ref[2, b:b + 1, :] * y
            r = lax.rsqrt(jnp.mean(xn * xn, axis=-1, keepdims=True) + EPS)
            o_ref[b] = (xn * r) * fg_ref[...]

    @pl.when(s == 0)
    def _():
        _cast_gate_weights(wg_ref, wgb_ref)
        carry_ref[...] = jnp.zeros_like(carry_ref)
        rows = stage.shape[1]

        def row_gain(c):
            g, lo = (ngp_ref, c * rows) if c * rows < kp else (ng_ref, c * rows - kp)
            return jnp.transpose(jnp.broadcast_to(g[:, lo:lo + rows], (LANES, rows)))[:, :1]

        begin, advance, n_chunks = _weight_stream(w_hbm, w_ref, stage, sem, row_gain)
        per_head = n_chunks // N_LRU_HEADS
        begin()
        scan(lambda h: advance(h * per_head, (h + 1) * per_head))
        advance(N_LRU_HEADS * per_head, n_chunks)
        finish_scan()

    @pl.when((s > 0) & (s < n_tiles))
    def _():
        scan(project_chunk)
        finish_projection()
        finish_scan()

    @pl.when(s == n_tiles)
    def _():
        for h in range(N_LRU_HEADS):
            project_chunk(h)
        finish_projection()


def _lru_bwd_out_proj(xc, hf, sg, w_gate, b_gate, lru_lambda, out_norm_pool_g, out_norm_lru_g, yp, x, mod3, w_out,
                      b_out, final_g, *, tm):
    batch, seq, d = x.shape
    width = xc.shape[1]
    tt = tm // batch
    n_tiles = seq // tt
    _, n_head, head, two_head = w_gate.shape

    def scan_tile(s):
        return jnp.maximum(n_tiles - 1 - s, 0)

    def proj_tile(s):
        return jnp.minimum(n_tiles - s, n_tiles - 1)

    return pl.pallas_call(
        functools.partial(_lru_bwd_out_kernel, n_tiles=n_tiles),
        grid=(n_tiles + 1,),
        in_specs=[pl.BlockSpec((tm, width), lambda s: (scan_tile(s), 0)),
                  pl.BlockSpec((tm, width), lambda s: (scan_tile(s), 0)),
                  pl.BlockSpec((tm, width), lambda s: (scan_tile(s), 1)),
                  pl.BlockSpec((1, n_head, head, two_head), lambda s: (1, 0, 0, 0)),
                  pl.BlockSpec((1, n_head, two_head), lambda s: (1, 0, 0)),
                  pl.BlockSpec(lru_lambda.shape, lambda s: (0, 0)),
                  pl.BlockSpec((1, yp.shape[1]), lambda s: (0, 0)),
                  pl.BlockSpec((1, width), lambda s: (0, 0)),
                  pl.BlockSpec((tm, yp.shape[1]), lambda s: (proj_tile(s), 0)),
                  pl.BlockSpec((batch, tt, d), lambda s: (0, proj_tile(s), 0)),
                  pl.BlockSpec(mod3.shape, lambda s: (0, 0, 0)),
                  pl.BlockSpec(memory_space=pl.ANY),
                  pl.BlockSpec((1, d), lambda s: (0, 0)),
                  pl.BlockSpec((1, d), lambda s: (0, 0))],
        out_specs=pl.BlockSpec((batch, tt, d), lambda s: (0, proj_tile(s), 0)),
        out_shape=jax.ShapeDtypeStruct((batch, seq, d), F32),
        scratch_shapes=[pltpu.VMEM((tm, width), F32),
                        pltpu.VMEM((tm, width), BF16),
                        pltpu.VMEM((d // LANES, tm, LANES), F32),
                        pltpu.VMEM((SUBLANES, width), F32),
                        pltpu.VMEM((n_head, head, two_head), BF16),
                        *_weight_scratch(w_out)],
        compiler_params=_params(("arbitrary",)),
        name="lru_bwd_out_proj",
    )(xc, hf, sg, w_gate, b_gate, lru_lambda, out_norm_pool_g, out_norm_lru_g, yp, x, mod3, w_out, b_out, final_g)


def kernel(x, c, norm_g, w_ada, b_ada, w_in, b_in, w_pool, b_pool, pool_scale, conv_w, conv_b, w_gate,
           b_gate, lru_lambda, out_norm_pool_g, out_norm_lru_g, w_out, b_out, final_norm_g):
    batch, seq, d = x.shape
    assert w_in.shape[0] == 1, "single-layer block only"
    mod3 = _adaln_mod(c, w_ada, b_ada)
    tile = TIME_TILE_ROWS
    z, sg = _in_proj(x, mod3, norm_g, w_in[0], b_in, tm=tile)
    yp, xc, hf = _mixers_fwd(z, sg, w_pool[0], b_pool[0], pool_scale, conv_w[0], conv_b, w_gate[0], b_gate[0],
                             lru_lambda[0], batch=batch, seq=seq, tc=tile)
    return _lru_bwd_out_proj(xc, hf, sg, w_gate[0], b_gate[0], lru_lambda[0], out_norm_pool_g, out_norm_lru_g, yp,
                             x, mod3, w_out[0], b_out, final_norm_g[None, :], tm=tile)
```

```python
import functools
import math

import jax
import jax.numpy as jnp
from jax import lax
from jax.experimental import pallas as pl
from jax.experimental.pallas import tpu as pltpu

EPS = 1e-6
LRU_C = 8.0
POOL_WINDOWS = (2, 4, 8, 16)
N_LRU_HEADS = 4
CONV_WIDTH = 4
LANES = 128
SUBLANES = 8
BF16_ROW_TILE = 16
V7X_VMEM_LIMIT_BYTES = 56 * 1024 * 1024
TIME_TILE_ROWS = 512
N_Z_BLOCKS = 4
W_IN_STAGE_COLS = 512
WEIGHT_STAGE_ROWS = 256
LOG2E = math.log2(math.e)
LN2 = math.log(2.0)
F32_MIN_NORMAL = float(jnp.finfo(jnp.float32).tiny)

F32 = jnp.float32
BF16 = jnp.bfloat16


def _params(semantics):
    return pltpu.CompilerParams(dimension_semantics=semantics,
                                vmem_limit_bytes=V7X_VMEM_LIMIT_BYTES)


def _sigmoid(v):
    return 0.5 * jnp.tanh(0.5 * v) + 0.5


def _silu(v):
    return v * _sigmoid(v)


def _halo_rows(steps, batch):
    return -(-steps * batch // SUBLANES) * SUBLANES


def _shift_rows(v, k):
    rows, width = v.shape
    v3 = v.reshape(rows // SUBLANES, SUBLANES, width)
    q, r = divmod(k, SUBLANES)
    if q % v3.shape[0]:
        g = q % v3.shape[0]
        v3 = jnp.concatenate([v3[-g:], v3[:-g]], axis=0)
    if r:
        sub = lax.broadcasted_iota(jnp.int32, v3.shape, 1)
        rot = pltpu.roll(v3, r, 1)
        other = jnp.concatenate([rot[-1:], rot[:-1]], axis=0)
        v3 = jnp.where(sub >= r, rot, other)
    return v3.reshape(rows, width)


def _weight_stream(w_hbm, w_vmem, stage, sem, row_gain=None):
    slots, rows, _ = stage.shape
    n_chunks = w_hbm.shape[0] // rows

    def copy(c):
        return pltpu.make_async_copy(w_hbm.at[pl.ds(c * rows, rows), :], stage.at[c % slots], sem.at[c % slots])

    def begin():
        for c in range(min(slots, n_chunks)):
            copy(c).start()

    def advance(lo, hi):
        for c in range(lo, hi):
            copy(c).wait()
            chunk = stage[c % slots]
            if row_gain is not None:
                chunk = chunk * row_gain(c)
            w_vmem[c * rows:(c + 1) * rows, :] = chunk.astype(BF16)
            if c + slots < n_chunks:
                copy(c + slots).start()

    return begin, advance, n_chunks


def _weight_scratch(w):
    k, n = w.shape
    assert k % WEIGHT_STAGE_ROWS == 0
    return [pltpu.VMEM((k, n), BF16), pltpu.VMEM((2, WEIGHT_STAGE_ROWS, n), F32), pltpu.SemaphoreType.DMA((2,))]


ADALN_STAGE_SLOTS = 4


def _adaln_kernel(c_ref, w_hbm, b_ref, o_ref, stage, sem):
    batch, d = c_ref.shape
    slots, rows, n = stage.shape
    n_chunks = d // rows
    ca = _silu(c_ref[...]).astype(BF16)
    ca = jnp.concatenate([ca, jnp.zeros((-batch % BF16_ROW_TILE, d), BF16)], axis=0)

    def copy(k):
        return pltpu.make_async_copy(w_hbm.at[0, pl.ds(k * rows, rows), :], stage.at[k % slots], sem.at[k % slots])

    for k in range(min(slots - 1, n_chunks)):
        copy(k).start()
    acc = jnp.zeros((ca.shape[0], n), F32)
    for k in range(n_chunks):
        if k + slots - 1 < n_chunks:
            copy(k + slots - 1).start()
        copy(k).wait()
        acc = acc + jnp.dot(ca[:, k * rows:(k + 1) * rows], stage[k % slots].astype(BF16),
                            preferred_element_type=F32)
    res = acc[:batch] + b_ref[...]
    for part in range(o_ref.shape[0]):
        o_ref[part] = res[:, part * d:(part + 1) * d]


def _adaln_mod(c, w_ada, b_ada):
    batch, d = c.shape
    n = w_ada.shape[2]
    assert d % WEIGHT_STAGE_ROWS == 0 and n == 3 * d
    return pl.pallas_call(
        _adaln_kernel,
        in_specs=[pl.BlockSpec(memory_space=pltpu.VMEM),
                  pl.BlockSpec(memory_space=pl.ANY),
                  pl.BlockSpec(memory_space=pltpu.VMEM)],
        out_specs=pl.BlockSpec(memory_space=pltpu.VMEM),
        out_shape=jax.ShapeDtypeStruct((3, batch, d), F32),
        scratch_shapes=[pltpu.VMEM((ADALN_STAGE_SLOTS, WEIGHT_STAGE_ROWS, n), F32),
                        pltpu.SemaphoreType.DMA((ADALN_STAGE_SLOTS,))],
        compiler_params=pltpu.CompilerParams(vmem_limit_bytes=V7X_VMEM_LIMIT_BYTES),
        name="adaln_mod",
    )(c, w_ada, b_ada)


def _in_proj_kernel(x_ref, mod_ref, g_ref, w_hbm, b_ref, u_ref, sg_ref, w_ref, stage, sem, hs_ref, *, n_chunk):
    first = pl.program_id(0) == 0
    n = w_ref.shape[1]
    n_mix = u_ref.shape[1]
    batch, tt, d = x_ref.shape

    def normalised_input():
        for b in range(batch):
            x = x_ref[b]
            r = lax.rsqrt(jnp.mean(x * x, axis=-1, keepdims=True) + EPS)
            h = (x * r) * (g_ref[...] * (1.0 + mod_ref[1, b:b + 1, :])) + mod_ref[0, b:b + 1, :]
            for c in range(d // LANES):
                hs_ref[c, pl.ds(b, tt, stride=batch), :] = h[:, c * LANES:(c + 1) * LANES]
        return jnp.concatenate([hs_ref[c] for c in range(d // LANES)], axis=1).astype(BF16)

    def project(hb, sl):
        zj = jnp.dot(hb, w_ref[:, sl], preferred_element_type=F32)
        if sl.start < n_mix:
            u_ref[:, sl] = zj + b_ref[:, sl]
        else:
            half = zj + 0.5 * b_ref[:, sl]
            sg_ref[:, sl.start - n_mix:sl.stop - n_mix] = (half * jnp.tanh(half) + half).astype(sg_ref.dtype)

    @pl.when(first)
    def _():
        cols = stage.shape[2]

        def copy(c):
            return pltpu.make_async_copy(w_hbm.at[:, pl.ds(c * cols, cols)], stage.at[c % 2], sem.at[c % 2])

        copy(0).start()
        hb = normalised_input()
        for c in range(n // cols):
            if c + 1 < n // cols:
                copy(c + 1).start()
            copy(c).wait()
            sl = slice(c * cols, (c + 1) * cols)
            halve = 0.5 if sl.start >= n_mix else 1.0
            w_ref[:, sl] = (halve * stage[c % 2]).astype(BF16)
            project(hb, sl)

    @pl.when(jnp.logical_not(first))
    def _():
        hb = normalised_input()
        for j in range(n // n_chunk):
            project(hb, slice(j * n_chunk, (j + 1) * n_chunk))


def _in_proj(x, mod3, norm_g, w_in, b_in, *, tm):
    batch, seq, d = x.shape
    n = w_in.shape[1]
    tt = tm // batch
    rows = batch * seq
    assert (n // 2) % W_IN_STAGE_COLS == 0 and (n // 2) % (n // N_Z_BLOCKS) == 0
    return pl.pallas_call(
        functools.partial(_in_proj_kernel, n_chunk=n // N_Z_BLOCKS),
        grid=(seq // tt,),
        in_specs=[pl.BlockSpec((batch, tt, d), lambda i: (0, i, 0)),
                  pl.BlockSpec(mod3.shape, lambda i: (0, 0, 0)),
                  pl.BlockSpec((1, d), lambda i: (0, 0)),
                  pl.BlockSpec(memory_space=pl.ANY),
                  pl.BlockSpec((1, n), lambda i: (0, 0))],
        out_specs=[pl.BlockSpec((tm, n // 2), lambda i: (i, 0)),
                   pl.BlockSpec((tm, n // 2), lambda i: (i, 0))],
        out_shape=[jax.ShapeDtypeStruct((rows, n // 2), F32), jax.ShapeDtypeStruct((rows, n // 2), BF16)],
        scratch_shapes=[pltpu.VMEM(w_in.shape, BF16),
                        pltpu.VMEM((2, d, W_IN_STAGE_COLS), F32),
                        pltpu.SemaphoreType.DMA((2,)),
                        pltpu.VMEM((d // LANES, tm, LANES), F32)],
        compiler_params=_params(("arbitrary",)),
        name="in_proj",
    )(x, mod3, norm_g, w_in, b_in)


def _halo_specs(width, *, halo, ts, total_rows):
    per_tile = ts // halo
    last = total_rows // halo - 1
    return (pl.BlockSpec((halo, width), lambda i: (jnp.maximum(i * per_tile - 1, 0), 0)),
            pl.BlockSpec((halo, width), lambda i: (jnp.minimum((i + 1) * per_tile, last), 0)))


def _with_halo(prev, cur, nxt, tile, n_tiles):
    prev = jnp.where(tile > 0, prev, 0.0)
    nxt = jnp.where(tile < n_tiles - 1, nxt, 0.0)
    return jnp.concatenate([prev, cur, nxt], axis=0)


def _pool_mix(ue, gate_ref, w_ref, b_ref, scale_ref, o_ref, *, tile, batch, seq, halo, at_sequence_end):
    ts = o_ref.shape[0]
    grp = ue.shape[1] // len(POOL_WINDOWS)
    assert all(w % 2 == 0 for w in POOL_WINDOWS), "the window sums pair steps of opposite parity"
    row = lax.broadcasted_iota(jnp.int32, (ts, 1), 0)
    t = tile * (ts // batch) + row // batch
    mixed = []
    for g, w in enumerate(POOL_WINDOWS):
        cols = slice(g * grp, (g + 1) * grp)
        e = ue[:, cols]
        lo = w // 2
        hi = w - lo - 1
        s, span = e, 2
        while span < w:
            s = s + _shift_rows(s, span * batch)
            span *= 2
        if hi % 2:
            s = _shift_rows(s, -(hi - 1) * batch)
            total = s + _shift_rows(s, -batch)
        else:
            s = _shift_rows(s, -hi * batch)
            total = s + _shift_rows(s, batch)
        total = total[halo:halo + ts]
        if at_sequence_end:
            cnt = (jnp.minimum(t + hi, seq - 1) - jnp.maximum(t - lo, 0) + 1).astype(F32)
        else:
            cnt = float(w)
        pooled = total / cnt - e[halo:halo + ts]
        scale = scale_ref[:, cols]
        m = jnp.dot(pooled.astype(BF16), (w_ref[g] * scale).astype(BF16), preferred_element_type=F32)
        mixed.append(m + b_ref[g:g + 1, :] * scale)
    y = jnp.concatenate(mixed, axis=-1)
    r = lax.rsqrt(jnp.mean(y * y, axis=-1, keepdims=True) + EPS)
    o_ref[...] = (y * r).astype(o_ref.dtype) * gate_ref[...]


def _softplus(v):
    return jnp.maximum(v, 0.0) + jnp.log1p(jnp.exp(-jnp.abs(v)))


def _lru_direction(xc_half, wg_ref, bg_half, rate2, carry_ref, h_ref, *, batch, reverse, after_gates=None):
    tc, width = xc_half.shape
    head = width // N_LRU_HEADS
    groups = tc // SUBLANES
    steps = SUBLANES // batch
    shift = SUBLANES - batch if reverse else batch

    for h in range(N_LRU_HEADS):
        cols = slice(h * head, (h + 1) * head)
        xh = xc_half[:, cols]
        g = jnp.dot(xh.astype(BF16), wg_ref[h], preferred_element_type=F32) + bg_half[h:h + 1, :]
        if after_gates is not None:
            after_gates(h)
        tr = jnp.tanh(g[:, :head])
        ti = jnp.tanh(g[:, head:])
        log2a = tr * rate2[:, cols] + rate2[:, cols]
        a = jnp.exp2(log2a)
        one_minus_a2 = (a * a + 1.0) * jnp.tanh(log2a * (-LN2))
        mult = one_minus_a2 * lax.rsqrt(jnp.maximum(one_minus_a2, F32_MIN_NORMAL))
        b = mult * (ti + 1.0) * xh

        a3 = a.reshape(groups, SUBLANES, head)
        b3 = b.reshape(groups, SUBLANES, head)
        state = carry_ref[:, cols]
        order = range(groups - 1, -1, -1) if reverse else range(groups)
        for gi in order:
            for k in range(steps):
                state = b3[gi] + a3[gi] * pltpu.roll(state, shift, 0)
                lo = (steps - 1 - k if reverse else k) * batch
                h_ref[gi * SUBLANES + lo:gi * SUBLANES + lo + batch, cols] = state[lo:lo + batch]
        carry_ref[:, cols] = state


def _lru_rates(lam_row, bg_ref):
    rate2 = (-0.5 * LRU_C * LOG2E) * _softplus(-lam_row)
    return rate2, 0.5 * bg_ref[0]


def _cast_gate_weights(wg_ref, wgb_ref):
    for h in range(N_LRU_HEADS):
        wgb_ref[h] = wg_ref[0, h].astype(BF16)


def _mixers_fwd_kernel(prev_ref, z_ref, next_ref, gate_ref, wp_ref, bp_ref, scale_ref, cw_ref, cb_ref, wg_ref,
                       bg_ref, lam_ref, yp_ref, xc_ref, hf_ref, carry_ref, wgb_ref, *, batch, seq, n_tiles):
    i = pl.program_id(0)
    tc = z_ref.shape[0]
    halo = prev_ref.shape[0]
    width = yp_ref.shape[1]

    @pl.when(i == 0)
    def _():
        _cast_gate_weights(wg_ref, wgb_ref)
        carry_ref[...] = jnp.zeros_like(carry_ref)

    def pool(at_sequence_end):
        ue = _with_halo(prev_ref[:, :width], z_ref[:, :width], next_ref[:, :width], i, n_tiles)
        _pool_mix(ue, gate_ref, wp_ref, bp_ref, scale_ref, yp_ref, tile=i, batch=batch, seq=seq, halo=halo,
                  at_sequence_end=at_sequence_end)

    interior = (i > 0) & (i < n_tiles - 1)
    pl.when(jnp.logical_not(interior))(functools.partial(pool, True))
    pl.when(interior)(functools.partial(pool, False))

    rate2, bg_half = _lru_rates(lam_ref[0:1, :], bg_ref)
    ch = _halo_rows(CONV_WIDTH // 2, batch)
    ue = _with_halo(prev_ref[halo - ch:, width:], z_ref[:, width:], next_ref[:ch, width:], i, n_tiles)
    left = CONV_WIDTH // 2
    xc_half = 0.5 * cb_ref[...]
    for k in range(CONV_WIDTH):
        tap = ue if k == left else _shift_rows(ue, (left - k) * batch)
        xc_half = xc_half + tap[ch:ch + tc] * (0.5 * cw_ref[k:k + 1, :])
    xc_ref[...] = xc_half
    _lru_direction(xc_half, wgb_ref, bg_half, rate2, carry_ref, hf_ref, batch=batch, reverse=False)


def _mixers_fwd(z, sg, w_pool, b_pool, pool_scale, conv_w, conv_b, w_gate, b_gate, lru_lambda, *, batch, seq, tc):
    width = conv_b.shape[1]
    n_tiles = batch * seq // tc
    assert SUBLANES % batch == 0, "a group of 8 rows must hold whole time steps"
    assert tc // batch >= max(POOL_WINDOWS)
    prev_spec, next_spec = _halo_specs(z.shape[1], halo=_halo_rows(max(POOL_WINDOWS) // 2, batch), ts=tc,
                                       total_rows=batch * seq)
    n_grp, grp, _ = w_pool.shape
    _, n_head, head, two_head = w_gate.shape
    tile = pl.BlockSpec((tc, width), lambda i: (i, 0))
    return pl.pallas_call(
        functools.partial(_mixers_fwd_kernel, batch=batch, seq=seq, n_tiles=n_tiles),
        grid=(n_tiles,),
        in_specs=[prev_spec,
                  pl.BlockSpec((tc, z.shape[1]), lambda i: (i, 0)),
                  next_spec,
                  tile,
                  pl.BlockSpec((n_grp, grp, grp), lambda i: (0, 0, 0)),
                  pl.BlockSpec((n_grp, grp), lambda i: (0, 0)),
                  pl.BlockSpec((1, width), lambda i: (0, 0)),
                  pl.BlockSpec((CONV_WIDTH, width), lambda i: (0, 0)),
                  pl.BlockSpec((1, width), lambda i: (0, 0)),
                  pl.BlockSpec((1, n_head, head, two_head), lambda i: (0, 0, 0, 0)),
                  pl.BlockSpec((1, n_head, two_head), lambda i: (0, 0, 0)),
                  pl.BlockSpec(lru_lambda.shape, lambda i: (0, 0))],
        out_specs=[tile, tile, tile],
        out_shape=[jax.ShapeDtypeStruct((batch * seq, width), BF16)] + [jax.ShapeDtypeStruct((batch * seq, width), F32)] * 2,
        scratch_shapes=[pltpu.VMEM((SUBLANES, width), F32),
                        pltpu.VMEM((n_head, head, two_head), BF16)],
        compiler_params=_params(("arbitrary",)),
        name="mixers_fwd",
    )(z, z, z, sg, w_pool, b_pool, pool_scale, conv_w, conv_b, w_gate, b_gate, lru_lambda)


def _lru_bwd_out_kernel(xc_ref, hf_ref, gate_ref, wg_ref, bg_ref, lam_ref, ngp_ref, ng_ref,
                        yp_ref, x_ref, mod_ref, w_hbm, b_ref, fg_ref,
                        o_ref, hb_ref, yl_ref, y_ref, carry_ref, wgb_ref, w_ref, stage, sem, *, n_tiles):
    s = pl.program_id(0)
    kp = yp_ref.shape[1]
    batch, tt, d = x_ref.shape
    n_chunk = d // N_LRU_HEADS

    def project_chunk(h):
        cols = slice(h * n_chunk, (h + 1) * n_chunk)
        y = jnp.dot(yp_ref[...], w_ref[:kp, cols], preferred_element_type=F32)
        y = y + jnp.dot(yl_ref[...], w_ref[kp:, cols], preferred_element_type=F32)
        y = y + b_ref[:, cols]
        for c in range(n_chunk // LANES):
            y_ref[h * (n_chunk // LANES) + c] = y[:, c * LANES:(c + 1) * LANES]

    def scan(after_gates):
        rate2, bg_half = _lru_rates(lam_ref[1:2, :], bg_ref)
        _lru_direction(xc_ref[...], wgb_ref, bg_half, rate2, carry_ref, hb_ref, batch=batch, reverse=True,
                       after_gates=after_gates)

    def finish_scan():
        yl = hf_ref[...] + hb_ref[...]
        rl = lax.rsqrt(jnp.mean(yl * yl, axis=-1, keepdims=True) + EPS)
        yl_ref[...] = (yl * rl).astype(yl_ref.dtype) * gate_ref[...]

    def finish_projection():
        for b in range(batch):
            y = jnp.concatenate([y_ref[c, pl.ds(b, tt, stride=batch), :] for c in range(d // LANES)], axis=1)
            xn = x_ref[b] + mod_ref[2, b:b + 1, :] * y
            r = lax.rsqrt(jnp.mean(xn * xn, axis=-1, keepdims=True) + EPS)
            o_ref[b] = (xn * r) * fg_ref[...]

    @pl.when(s == 0)
    def _():
        _cast_gate_weights(wg_ref, wgb_ref)
        carry_ref[...] = jnp.zeros_like(carry_ref)
        rows = stage.shape[1]

        def row_gain(c):
            g, lo = (ngp_ref, c * rows) if c * rows < kp else (ng_ref, c * rows - kp)
            return jnp.transpose(jnp.broadcast_to(g[:, lo:lo + rows], (LANES, rows)))[:, :1]

        begin, advance, n_chunks = _weight_stream(w_hbm, w_ref, stage, sem, row_gain)
        per_head = n_chunks // N_LRU_HEADS
        begin()
        scan(lambda h: advance(h * per_head, (h + 1) * per_head))
        advance(N_LRU_HEADS * per_head, n_chunks)
        finish_scan()

    @pl.when((s > 0) & (s < n_tiles))
    def _():
        scan(project_chunk)
        finish_projection()
        finish_scan()

    @pl.when(s == n_tiles)
    def _():
        for h in range(N_LRU_HEADS):
            project_chunk(h)
        finish_projection()


def _lru_bwd_out_proj(xc, hf, sg, w_gate, b_gate, lru_lambda, out_norm_pool_g, out_norm_lru_g, yp, x, mod3, w_out,
                      b_out, final_g, *, tm):
    batch, seq, d = x.shape
    width = xc.shape[1]
    tt = tm // batch
    n_tiles = seq // tt
    _, n_head, head, two_head = w_gate.shape

    def scan_tile(s):
        return jnp.maximum(n_tiles - 1 - s, 0)

    def proj_tile(s):
        return jnp.minimum(n_tiles - s, n_tiles - 1)

    return pl.pallas_call(
        functools.partial(_lru_bwd_out_kernel, n_tiles=n_tiles),
        grid=(n_tiles + 1,),
        in_specs=[pl.BlockSpec((tm, width), lambda s: (scan_tile(s), 0)),
                  pl.BlockSpec((tm, width), lambda s: (scan_tile(s), 0)),
                  pl.BlockSpec((tm, width), lambda s: (scan_tile(s), 1)),
                  pl.BlockSpec((1, n_head, head, two_head), lambda s: (1, 0, 0, 0)),
                  pl.BlockSpec((1, n_head, two_head), lambda s: (1, 0, 0)),
                  pl.BlockSpec(lru_lambda.shape, lambda s: (0, 0)),
                  pl.BlockSpec((1, yp.shape[1]), lambda s: (0, 0)),
                  pl.BlockSpec((1, width), lambda s: (0, 0)),
                  pl.BlockSpec((tm, yp.shape[1]), lambda s: (proj_tile(s), 0)),
                  pl.BlockSpec((batch, tt, d), lambda s: (0, proj_tile(s), 0)),
                  pl.BlockSpec(mod3.shape, lambda s: (0, 0, 0)),
                  pl.BlockSpec(memory_space=pl.ANY),
                  pl.BlockSpec((1, d), lambda s: (0, 0)),
                  pl.BlockSpec((1, d), lambda s: (0, 0))],
        out_specs=pl.BlockSpec((batch, tt, d), lambda s: (0, proj_tile(s), 0)),
        out_shape=jax.ShapeDtypeStruct((batch, seq, d), F32),
        scratch_shapes=[pltpu.VMEM((tm, width), F32),
                        pltpu.VMEM((tm, width), BF16),
                        pltpu.VMEM((d // LANES, tm, LANES), F32),
                        pltpu.VMEM((SUBLANES, width), F32),
                        pltpu.VMEM((n_head, head, two_head), BF16),
                        *_weight_scratch(w_out)],
        compiler_params=_params(("arbitrary",)),
        name="lru_bwd_out_proj",
    )(xc, hf, sg, w_gate, b_gate, lru_lambda, out_norm_pool_g, out_norm_lru_g, yp, x, mod3, w_out, b_out, final_g)


def kernel(x, c, norm_g, w_ada, b_ada, w_in, b_in, w_pool, b_pool, pool_scale, conv_w, conv_b, w_gate,
           b_gate, lru_lambda, out_norm_pool_g, out_norm_lru_g, w_out, b_out, final_norm_g):
    batch, seq, d = x.shape
    assert w_in.shape[0] == 1, "single-layer block only"
    mod3 = _adaln_mod(c, w_ada, b_ada)
    tile = TIME_TILE_ROWS
    z, sg = _in_proj(x, mod3, norm_g, w_in[0], b_in, tm=tile)
    yp, xc, hf = _mixers_fwd(z, sg, w_pool[0], b_pool[0], pool_scale, conv_w[0], conv_b, w_gate[0], b_gate[0],
                             lru_lambda[0], batch=batch, seq=seq, tc=tile)
    return _lru_bwd_out_proj(xc, hf, sg, w_gate[0], b_gate[0], lru_lambda[0], out_norm_pool_g, out_norm_lru_g, yp,
                             x, mod3, w_out[0], b_out, final_norm_g[None, :], tm=tile)
```

```python
import functools
import math

import jax
import jax.numpy as jnp
from jax import lax
from jax.experimental import pallas as pl
from jax.experimental.pallas import tpu as pltpu

EPS = 1e-6
LRU_C = 8.0
POOL_WINDOWS = (2, 4, 8, 16)
N_LRU_HEADS = 4
CONV_WIDTH = 4
LANES = 128
SUBLANES = 8
BF16_ROW_TILE = 16
V7X_VMEM_LIMIT_BYTES = 56 * 1024 * 1024
TIME_TILE_ROWS = 512
N_Z_BLOCKS = 4
W_IN_STAGE_COLS = 512
WEIGHT_STAGE_ROWS = 256
LOG2E = math.log2(math.e)
LN2 = math.log(2.0)
F32_MIN_NORMAL = float(jnp.finfo(jnp.float32).tiny)

F32 = jnp.float32
BF16 = jnp.bfloat16


def _params(semantics):
    return pltpu.CompilerParams(dimension_semantics=semantics,
                                vmem_limit_bytes=V7X_VMEM_LIMIT_BYTES)


def _sigmoid(v):
    return 0.5 * jnp.tanh(0.5 * v) + 0.5


def _silu(v):
    return v * _sigmoid(v)


def _halo_rows(steps, batch):
    return -(-steps * batch // SUBLANES) * SUBLANES


def _shift_rows(v, k):
    rows, width = v.shape
    v3 = v.reshape(rows // SUBLANES, SUBLANES, width)
    q, r = divmod(k, SUBLANES)
    if q % v3.shape[0]:
        g = q % v3.shape[0]
        v3 = jnp.concatenate([v3[-g:], v3[:-g]], axis=0)
    if r:
        sub = lax.broadcasted_iota(jnp.int32, v3.shape, 1)
        rot = pltpu.roll(v3, r, 1)
        other = jnp.concatenate([rot[-1:], rot[:-1]], axis=0)
        v3 = jnp.where(sub >= r, rot, other)
    return v3.reshape(rows, width)


def _weight_stream(w_hbm, w_vmem, stage, sem, row_gain=None):
    slots, rows, _ = stage.shape
    n_chunks = w_hbm.shape[0] // rows

    def copy(c):
        return pltpu.make_async_copy(w_hbm.at[pl.ds(c * rows, rows), :], stage.at[c % slots], sem.at[c % slots])

    def begin():
        for c in range(min(slots, n_chunks)):
            copy(c).start()

    def advance(lo, hi):
        for c in range(lo, hi):
            copy(c).wait()
            chunk = stage[c % slots]
            if row_gain is not None:
                chunk = chunk * row_gain(c)
            w_vmem[c * rows:(c + 1) * rows, :] = chunk.astype(BF16)
            if c + slots < n_chunks:
                copy(c + slots).start()

    return begin, advance, n_chunks


def _weight_scratch(w):
    k, n = w.shape
    assert k % WEIGHT_STAGE_ROWS == 0
    return [pltpu.VMEM((k, n), BF16), pltpu.VMEM((2, WEIGHT_STAGE_ROWS, n), F32), pltpu.SemaphoreType.DMA((2,))]


ADALN_STAGE_SLOTS = 4


def _adaln_kernel(c_ref, w_hbm, b_ref, o_ref, stage, sem):
    batch, d = c_ref.shape
    slots, rows, n = stage.shape
    n_chunks = d // rows
    ca = _silu(c_ref[...]).astype(BF16)
    ca = jnp.concatenate([ca, jnp.zeros((-batch % BF16_ROW_TILE, d), BF16)], axis=0)

    def copy(k):
        return pltpu.make_async_copy(w_hbm.at[0, pl.ds(k * rows, rows), :], stage.at[k % slots], sem.at[k % slots])

    for k in range(min(slots - 1, n_chunks)):
        copy(k).start()
    acc = jnp.zeros((ca.shape[0], n), F32)
    for k in range(n_chunks):
        if k + slots - 1 < n_chunks:
            copy(k + slots - 1).start()
        copy(k).wait()
        acc = acc + jnp.dot(ca[:, k * rows:(k + 1) * rows], stage[k % slots].astype(BF16),
                            preferred_element_type=F32)
    res = acc[:batch] + b_ref[...]
    for part in range(o_ref.shape[0]):
        o_ref[part] = res[:, part * d:(part + 1) * d]


def _adaln_mod(c, w_ada, b_ada):
    batch, d = c.shape
    n = w_ada.shape[2]
    assert d % WEIGHT_STAGE_ROWS == 0 and n == 3 * d
    return pl.pallas_call(
        _adaln_kernel,
        in_specs=[pl.BlockSpec(memory_space=pltpu.VMEM),
                  pl.BlockSpec(memory_space=pl.ANY),
                  pl.BlockSpec(memory_space=pltpu.VMEM)],
        out_specs=pl.BlockSpec(memory_space=pltpu.VMEM),
        out_shape=jax.ShapeDtypeStruct((3, batch, d), F32),
        scratch_shapes=[pltpu.VMEM((ADALN_STAGE_SLOTS, WEIGHT_STAGE_ROWS, n), F32),
                        pltpu.SemaphoreType.DMA((ADALN_STAGE_SLOTS,))],
        compiler_params=pltpu.CompilerParams(vmem_limit_bytes=V7X_VMEM_LIMIT_BYTES),
        name="adaln_mod",
    )(c, w_ada, b_ada)


def _in_proj_kernel(x_ref, mod_ref, g_ref, w_hbm, b_ref, u_ref, sg_ref, w_ref, stage, sem, hs_ref, *, n_chunk):
    first = pl.program_id(0) == 0
    n = w_ref.shape[1]
    n_mix = u_ref.shape[1]
    batch, tt, d = x_ref.shape

    def normalised_input():
        for b in range(batch):
            x = x_ref[b]
            xn = x * lax.rsqrt(jnp.mean(x * x, axis=-1, keepdims=True) + EPS)
            for c in range(d // LANES):
                hs_ref[c, pl.ds(b, tt, stride=batch), :] = xn[:, c * LANES:(c + 1) * LANES]
        xb = jnp.concatenate([hs_ref[c] for c in range(d // LANES)], axis=1).astype(BF16)
        reps = BF16_ROW_TILE // batch
        gain = jnp.concatenate([g_ref[...] * (1.0 + mod_ref[1])] * reps, axis=0).astype(BF16)
        shift = jnp.concatenate([mod_ref[0]] * reps, axis=0).astype(BF16)
        h = xb.reshape(tt * batch // BF16_ROW_TILE, BF16_ROW_TILE, d) * gain[None] + shift[None]
        return h.reshape(tt * batch, d)

    def project(hb, sl):
        zj = jnp.dot(hb, w_ref[:, sl], preferred_element_type=F32)
        if sl.start < n_mix:
            u_ref[:, sl] = zj + b_ref[:, sl]
        else:
            half = zj + 0.5 * b_ref[:, sl]
            sg_ref[:, sl.start - n_mix:sl.stop - n_mix] = (half * jnp.tanh(half) + half).astype(sg_ref.dtype)

    @pl.when(first)
    def _():
        cols = stage.shape[2]

        def copy(c):
            return pltpu.make_async_copy(w_hbm.at[:, pl.ds(c * cols, cols)], stage.at[c % 2], sem.at[c % 2])

        copy(0).start()
        hb = normalised_input()
        for c in range(n // cols):
            if c + 1 < n // cols:
                copy(c + 1).start()
            copy(c).wait()
            sl = slice(c * cols, (c + 1) * cols)
            halve = 0.5 if sl.start >= n_mix else 1.0
            w_ref[:, sl] = (halve * stage[c % 2]).astype(BF16)
            project(hb, sl)

    @pl.when(jnp.logical_not(first))
    def _():
        hb = normalised_input()
        for j in range(n // n_chunk):
            project(hb, slice(j * n_chunk, (j + 1) * n_chunk))


def _in_proj(x, mod3, norm_g, w_in, b_in, *, tm):
    batch, seq, d = x.shape
    n = w_in.shape[1]
    tt = tm // batch
    rows = batch * seq
    assert (n // 2) % W_IN_STAGE_COLS == 0 and (n // 2) % (n // N_Z_BLOCKS) == 0
    return pl.pallas_call(
        functools.partial(_in_proj_kernel, n_chunk=n // N_Z_BLOCKS),
        grid=(seq // tt,),
        in_specs=[pl.BlockSpec((batch, tt, d), lambda i: (0, i, 0)),
                  pl.BlockSpec(mod3.shape, lambda i: (0, 0, 0)),
                  pl.BlockSpec((1, d), lambda i: (0, 0)),
                  pl.BlockSpec(memory_space=pl.ANY),
                  pl.BlockSpec((1, n), lambda i: (0, 0))],
        out_specs=[pl.BlockSpec((tm, n // 2), lambda i: (i, 0)),
                   pl.BlockSpec((tm, n // 2), lambda i: (i, 0))],
        out_shape=[jax.ShapeDtypeStruct((rows, n // 2), F32), jax.ShapeDtypeStruct((rows, n // 2), BF16)],
        scratch_shapes=[pltpu.VMEM(w_in.shape, BF16),
                        pltpu.VMEM((2, d, W_IN_STAGE_COLS), F32),
                        pltpu.SemaphoreType.DMA((2,)),
                        pltpu.VMEM((d // LANES, tm, LANES), F32)],
        compiler_params=_params(("arbitrary",)),
        name="in_proj",
    )(x, mod3, norm_g, w_in, b_in)


def _halo_specs(width, *, halo, ts, total_rows):
    per_tile = ts // halo
    last = total_rows // halo - 1
    return (pl.BlockSpec((halo, width), lambda i: (jnp.maximum(i * per_tile - 1, 0), 0)),
            pl.BlockSpec((halo, width), lambda i: (jnp.minimum((i + 1) * per_tile, last), 0)))


def _with_halo(prev, cur, nxt, tile, n_tiles):
    prev = jnp.where(tile > 0, prev, 0.0)
    nxt = jnp.where(tile < n_tiles - 1, nxt, 0.0)
    return jnp.concatenate([prev, cur, nxt], axis=0)


def _pool_mix(ue, gate_ref, w_ref, b_ref, scale_ref, o_ref, *, tile, batch, seq, halo, at_sequence_end):
    ts = o_ref.shape[0]
    grp = ue.shape[1] // len(POOL_WINDOWS)
    assert all(w % 2 == 0 for w in POOL_WINDOWS), "the window sums pair steps of opposite parity"
    row = lax.broadcasted_iota(jnp.int32, (ts, 1), 0)
    t = tile * (ts // batch) + row // batch
    mixed = []
    for g, w in enumerate(POOL_WINDOWS):
        cols = slice(g * grp, (g + 1) * grp)
        e = ue[:, cols]
        lo = w // 2
        hi = w - lo - 1
        s, span = e, 2
        while span < w:
            s = s + _shift_rows(s, span * batch)
            span *= 2
        if hi % 2:
            s = _shift_rows(s, -(hi - 1) * batch)
            total = s + _shift_rows(s, -batch)
        else:
            s = _shift_rows(s, -hi * batch)
            total = s + _shift_rows(s, batch)
        total = total[halo:halo + ts]
        if at_sequence_end:
            cnt = (jnp.minimum(t + hi, seq - 1) - jnp.maximum(t - lo, 0) + 1).astype(F32)
        else:
            cnt = float(w)
        pooled = total / cnt - e[halo:halo + ts]
        scale = scale_ref[:, cols]
        m = jnp.dot(pooled.astype(BF16), (w_ref[g] * scale).astype(BF16), preferred_element_type=F32)
        mixed.append(m + b_ref[g:g + 1, :] * scale)
    y = jnp.concatenate(mixed, axis=-1)
    r = lax.rsqrt(jnp.mean(y * y, axis=-1, keepdims=True) + EPS)
    o_ref[...] = (y * r).astype(o_ref.dtype) * gate_ref[...]


def _softplus(v):
    return jnp.maximum(v, 0.0) + jnp.log1p(jnp.exp(-jnp.abs(v)))


def _lru_direction(xc_half, wg_ref, bg_half, rate2, carry_ref, h_ref, *, batch, reverse, after_gates=None):
    tc, width = xc_half.shape
    head = width // N_LRU_HEADS
    groups = tc // SUBLANES
    steps = SUBLANES // batch
    sub = lax.broadcasted_iota(jnp.int32, (SUBLANES, head), 0)
    step_of_row = (SUBLANES - 1 - sub if reverse else sub) // batch
    shift = SUBLANES - batch if reverse else batch

    for h in range(N_LRU_HEADS):
        cols = slice(h * head, (h + 1) * head)
        xh = xc_half[:, cols]
        g = jnp.dot(xh.astype(BF16), wg_ref[h], preferred_element_type=F32) + bg_half[h:h + 1, :]
        if after_gates is not None:
            after_gates(h)
        tr = jnp.tanh(g[:, :head])
        ti = jnp.tanh(g[:, head:])
        log2a = tr * rate2[:, cols] + rate2[:, cols]
        a = jnp.exp2(log2a)
        one_minus_a2 = (a * a + 1.0) * jnp.tanh(log2a * (-LN2))
        mult = one_minus_a2 * lax.rsqrt(jnp.maximum(one_minus_a2, F32_MIN_NORMAL))
        b = mult * (ti + 1.0) * xh

        a3 = a.reshape(groups, SUBLANES, head)
        b3 = b.reshape(groups, SUBLANES, head)
        state = carry_ref[:, cols]
        order = range(groups - 1, -1, -1) if reverse else range(groups)
        for gi in order:
            passes = []
            for _ in range(steps):
                state = b3[gi] + a3[gi] * pltpu.roll(state, shift, 0)
                passes.append(state)
            out = passes[0]
            for k in range(1, steps):
                out = jnp.where(step_of_row == k, passes[k], out)
            h_ref[gi * SUBLANES:(gi + 1) * SUBLANES, cols] = out
        carry_ref[:, cols] = state


def _lru_rates(lam_row, bg_ref):
    rate2 = (-0.5 * LRU_C * LOG2E) * _softplus(-lam_row)
    return rate2, 0.5 * bg_ref[0]


def _cast_gate_weights(wg_ref, wgb_ref):
    for h in range(N_LRU_HEADS):
        wgb_ref[h] = wg_ref[0, h].astype(BF16)


def _mixers_fwd_kernel(prev_ref, z_ref, next_ref, gate_ref, wp_ref, bp_ref, scale_ref, cw_ref, cb_ref, wg_ref,
                       bg_ref, lam_ref, yp_ref, xc_ref, hf_ref, carry_ref, wgb_ref, *, batch, seq, n_tiles):
    i = pl.program_id(0)
    tc = z_ref.shape[0]
    halo = prev_ref.shape[0]
    width = yp_ref.shape[1]

    @pl.when(i == 0)
    def _():
        _cast_gate_weights(wg_ref, wgb_ref)
        carry_ref[...] = jnp.zeros_like(carry_ref)

    def pool(at_sequence_end):
        ue = _with_halo(prev_ref[:, :width], z_ref[:, :width], next_ref[:, :width], i, n_tiles)
        _pool_mix(ue, gate_ref, wp_ref, bp_ref, scale_ref, yp_ref, tile=i, batch=batch, seq=seq, halo=halo,
                  at_sequence_end=at_sequence_end)

    interior = (i > 0) & (i < n_tiles - 1)
    pl.when(jnp.logical_not(interior))(functools.partial(pool, True))
    pl.when(interior)(functools.partial(pool, False))

    rate2, bg_half = _lru_rates(lam_ref[0:1, :], bg_ref)
    ch = _halo_rows(CONV_WIDTH // 2, batch)
    ue = _with_halo(prev_ref[halo - ch:, width:], z_ref[:, width:], next_ref[:ch, width:], i, n_tiles)
    left = CONV_WIDTH // 2
    xc_half = 0.5 * cb_ref[...]
    for k in range(CONV_WIDTH):
        tap = ue if k == left else _shift_rows(ue, (left - k) * batch)
        xc_half = xc_half + tap[ch:ch + tc] * (0.5 * cw_ref[k:k + 1, :])
    xc_ref[...] = xc_half
    _lru_direction(xc_half, wgb_ref, bg_half, rate2, carry_ref, hf_ref, batch=batch, reverse=False)


def _mixers_fwd(z, sg, w_pool, b_pool, pool_scale, conv_w, conv_b, w_gate, b_gate, lru_lambda, *, batch, seq, tc):
    width = conv_b.shape[1]
    n_tiles = batch * seq // tc
    assert SUBLANES % batch == 0, "a group of 8 rows must hold whole time steps"
    assert tc // batch >= max(POOL_WINDOWS)
    prev_spec, next_spec = _halo_specs(z.shape[1], halo=_halo_rows(max(POOL_WINDOWS) // 2, batch), ts=tc,
                                       total_rows=batch * seq)
    n_grp, grp, _ = w_pool.shape
    _, n_head, head, two_head = w_gate.shape
    tile = pl.BlockSpec((tc, width), lambda i: (i, 0))
    return pl.pallas_call(
        functools.partial(_mixers_fwd_kernel, batch=batch, seq=seq, n_tiles=n_tiles),
        grid=(n_tiles,),
        in_specs=[prev_spec,
                  pl.BlockSpec((tc, z.shape[1]), lambda i: (i, 0)),
                  next_spec,
                  tile,
                  pl.BlockSpec((n_grp, grp, grp), lambda i: (0, 0, 0)),
                  pl.BlockSpec((n_grp, grp), lambda i: (0, 0)),
                  pl.BlockSpec((1, width), lambda i: (0, 0)),
                  pl.BlockSpec((CONV_WIDTH, width), lambda i: (0, 0)),
                  pl.BlockSpec((1, width), lambda i: (0, 0)),
                  pl.BlockSpec((1, n_head, head, two_head), lambda i: (0, 0, 0, 0)),
                  pl.BlockSpec((1, n_head, two_head), lambda i: (0, 0, 0)),
                  pl.BlockSpec(lru_lambda.shape, lambda i: (0, 0))],
        out_specs=[tile, tile, tile],
        out_shape=[jax.ShapeDtypeStruct((batch * seq, width), BF16)] + [jax.ShapeDtypeStruct((batch * seq, width), F32)] * 2,
        scratch_shapes=[pltpu.VMEM((SUBLANES, width), F32),
                        pltpu.VMEM((n_head, head, two_head), BF16)],
        compiler_params=_params(("arbitrary",)),
        name="mixers_fwd",
    )(z, z, z, sg, w_pool, b_pool, pool_scale, conv_w, conv_b, w_gate, b_gate, lru_lambda)


def _lru_bwd_out_kernel(xc_ref, hf_ref, gate_ref, wg_ref, bg_ref, lam_ref, ngp_ref, ng_ref,
                        yp_ref, x_ref, mod_ref, w_hbm, b_ref, fg_ref,
                        o_ref, hb_ref, yl_ref, y_ref, carry_ref, wgb_ref, w_ref, stage, sem, *, n_tiles):
    s = pl.program_id(0)
    kp = yp_ref.shape[1]
    batch, tt, d = x_ref.shape
    n_chunk = d // N_LRU_HEADS

    def project_chunk(h):
        cols = slice(h * n_chunk, (h + 1) * n_chunk)
        y = jnp.dot(yp_ref[...], w_ref[:kp, cols], preferred_element_type=F32)
        y = y + jnp.dot(yl_ref[...], w_ref[kp:, cols], preferred_element_type=F32)
        y = y + b_ref[:, cols]
        for c in range(n_chunk // LANES):
            y_ref[h * (n_chunk // LANES) + c] = y[:, c * LANES:(c + 1) * LANES]

    def scan(after_gates):
        rate2, bg_half = _lru_rates(lam_ref[1:2, :], bg_ref)
        _lru_direction(xc_ref[...], wgb_ref, bg_half, rate2, carry_ref, hb_ref, batch=batch, reverse=True,
                       after_gates=after_gates)

    def finish_scan():
        yl = hf_ref[...] + hb_ref[...]
        rl = lax.rsqrt(jnp.mean(yl * yl, axis=-1, keepdims=True) + EPS)
        yl_ref[...] = (yl * rl).astype(yl_ref.dtype) * gate_ref[...]

    def finish_projection():
        for b in range(batch):
            y = jnp.concatenate([y_ref[c, pl.ds(b, tt, stride=batch), :] for c in range(d // LANES)], axis=1)
            xn = x_ref[b] + mod_ref[2, b:b + 1, :] * y
            r = lax.rsqrt(jnp.mean(xn * xn, axis=-1, keepdims=True) + EPS)
            o_ref[b] = (xn * r) * fg_ref[...]

    @pl.when(s == 0)
    def _():
        _cast_gate_weights(wg_ref, wgb_ref)
        carry_ref[...] = jnp.zeros_like(carry_ref)
        rows = stage.shape[1]

        def row_gain(c):
            g, lo = (ngp_ref, c * rows) if c * rows < kp else (ng_ref, c * rows - kp)
            return jnp.transpose(jnp.broadcast_to(g[:, lo:lo + rows], (LANES, rows)))[:, :1]

        begin, advance, n_chunks = _weight_stream(w_hbm, w_ref, stage, sem, row_gain)
        per_head = n_chunks // N_LRU_HEADS
        begin()
        scan(lambda h: advance(h * per_head, (h + 1) * per_head))
        advance(N_LRU_HEADS * per_head, n_chunks)
        finish_scan()

    @pl.when((s > 0) & (s < n_tiles))
    def _():
        scan(project_chunk)
        finish_projection()
        finish_scan()

    @pl.when(s == n_tiles)
    def _():
        for h in range(N_LRU_HEADS):
            project_chunk(h)
        finish_projection()


def _lru_bwd_out_proj(xc, hf, sg, w_gate, b_gate, lru_lambda, out_norm_pool_g, out_norm_lru_g, yp, x, mod3, w_out,
                      b_out, final_g, *, tm):
    batch, seq, d = x.shape
    width = xc.shape[1]
    tt = tm // batch
    n_tiles = seq // tt
    _, n_head, head, two_head = w_gate.shape

    def scan_tile(s):
        return jnp.maximum(n_tiles - 1 - s, 0)

    def proj_tile(s):
        return jnp.minimum(n_tiles - s, n_tiles - 1)

    return pl.pallas_call(
        functools.partial(_lru_bwd_out_kernel, n_tiles=n_tiles),
        grid=(n_tiles + 1,),
        in_specs=[pl.BlockSpec((tm, width), lambda s: (scan_tile(s), 0)),
                  pl.BlockSpec((tm, width), lambda s: (scan_tile(s), 0)),
                  pl.BlockSpec((tm, width), lambda s: (scan_tile(s), 1)),
                  pl.BlockSpec((1, n_head, head, two_head), lambda s: (1, 0, 0, 0)),
                  pl.BlockSpec((1, n_head, two_head), lambda s: (1, 0, 0)),
                  pl.BlockSpec(lru_lambda.shape, lambda s: (0, 0)),
                  pl.BlockSpec((1, yp.shape[1]), lambda s: (0, 0)),
                  pl.BlockSpec((1, width), lambda s: (0, 0)),
                  pl.BlockSpec((tm, yp.shape[1]), lambda s: (proj_tile(s), 0)),
                  pl.BlockSpec((batch, tt, d), lambda s: (0, proj_tile(s), 0)),
                  pl.BlockSpec(mod3.shape, lambda s: (0, 0, 0)),
                  pl.BlockSpec(memory_space=pl.ANY),
                  pl.BlockSpec((1, d), lambda s: (0, 0)),
                  pl.BlockSpec((1, d), lambda s: (0, 0))],
        out_specs=pl.BlockSpec((batch, tt, d), lambda s: (0, proj_tile(s), 0)),
        out_shape=jax.ShapeDtypeStruct((batch, seq, d), F32),
        scratch_shapes=[pltpu.VMEM((tm, width), F32),
                        pltpu.VMEM((tm, width), BF16),
                        pltpu.VMEM((d // LANES, tm, LANES), F32),
                        pltpu.VMEM((SUBLANES, width), F32),
                        pltpu.VMEM((n_head, head, two_head), BF16),
                        *_weight_scratch(w_out)],
        compiler_params=_params(("arbitrary",)),
        name="lru_bwd_out_proj",
    )(xc, hf, sg, w_gate, b_gate, lru_lambda, out_norm_pool_g, out_norm_lru_g, yp, x, mod3, w_out, b_out, final_g)


def kernel(x, c, norm_g, w_ada, b_ada, w_in, b_in, w_pool, b_pool, pool_scale, conv_w, conv_b, w_gate,
           b_gate, lru_lambda, out_norm_pool_g, out_norm_lru_g, w_out, b_out, final_norm_g):
    batch, seq, d = x.shape
    assert w_in.shape[0] == 1, "single-layer block only"
    mod3 = _adaln_mod(c, w_ada, b_ada)
    tile = TIME_TILE_ROWS
    z, sg = _in_proj(x, mod3, norm_g, w_in[0], b_in, tm=tile)
    yp, xc, hf = _mixers_fwd(z, sg, w_pool[0], b_pool[0], pool_scale, conv_w[0], conv_b, w_gate[0], b_gate[0],
                             lru_lambda[0], batch=batch, seq=seq, tc=tile)
    return _lru_bwd_out_proj(xc, hf, sg, w_gate[0], b_gate[0], lru_lambda[0], out_norm_pool_g, out_norm_lru_g, yp,
                             x, mod3, w_out[0], b_out, final_norm_g[None, :], tm=tile)
```

```python
import functools
import math

import jax
import jax.numpy as jnp
from jax import lax
from jax.experimental import pallas as pl
from jax.experimental.pallas import tpu as pltpu

EPS = 1e-6
LRU_C = 8.0
POOL_WINDOWS = (2, 4, 8, 16)
N_LRU_HEADS = 4
CONV_WIDTH = 4
LANES = 128
SUBLANES = 8
BF16_ROW_TILE = 16
V7X_VMEM_LIMIT_BYTES = 56 * 1024 * 1024
TIME_TILE_ROWS = 512
N_Z_BLOCKS = 4
W_IN_STAGE_COLS = 512
Z_RING_SLOTS = 3
WEIGHT_STAGE_ROWS = 256
LOG2E = math.log2(math.e)
LN2 = math.log(2.0)
F32_MIN_NORMAL = float(jnp.finfo(jnp.float32).tiny)

F32 = jnp.float32
BF16 = jnp.bfloat16


def _params(semantics):
    return pltpu.CompilerParams(dimension_semantics=semantics,
                                vmem_limit_bytes=V7X_VMEM_LIMIT_BYTES)


def _sigmoid(v):
    return 0.5 * jnp.tanh(0.5 * v) + 0.5


def _silu(v):
    return v * _sigmoid(v)


def _halo_rows(steps, batch):
    return -(-steps * batch // SUBLANES) * SUBLANES


def _shift_rows(v, k):
    rows, width = v.shape
    v3 = v.reshape(rows // SUBLANES, SUBLANES, width)
    q, r = divmod(k, SUBLANES)
    if q % v3.shape[0]:
        g = q % v3.shape[0]
        v3 = jnp.concatenate([v3[-g:], v3[:-g]], axis=0)
    if r:
        sub = lax.broadcasted_iota(jnp.int32, v3.shape, 1)
        rot = pltpu.roll(v3, r, 1)
        other = jnp.concatenate([rot[-1:], rot[:-1]], axis=0)
        v3 = jnp.where(sub >= r, rot, other)
    return v3.reshape(rows, width)


def _weight_stream(w_hbm, w_vmem, stage, sem, row_gain=None):
    slots, rows, _ = stage.shape
    n_chunks = w_hbm.shape[0] // rows

    def copy(c):
        return pltpu.make_async_copy(w_hbm.at[pl.ds(c * rows, rows), :], stage.at[c % slots], sem.at[c % slots])

    def begin():
        for c in range(min(slots, n_chunks)):
            copy(c).start()

    def advance(lo, hi):
        for c in range(lo, hi):
            copy(c).wait()
            chunk = stage[c % slots]
            if row_gain is not None:
                chunk = chunk * row_gain(c)
            w_vmem[c * rows:(c + 1) * rows, :] = chunk.astype(BF16)
            if c + slots < n_chunks:
                copy(c + slots).start()

    return begin, advance, n_chunks


def _weight_scratch(w):
    k, n = w.shape
    assert k % WEIGHT_STAGE_ROWS == 0
    return [pltpu.VMEM((k, n), BF16), pltpu.VMEM((2, WEIGHT_STAGE_ROWS, n), F32), pltpu.SemaphoreType.DMA((2,))]


ADALN_STAGE_SLOTS = 4


def _adaln_kernel(c_ref, w_hbm, b_ref, o_ref, stage, sem):
    batch, d = c_ref.shape
    slots, rows, n = stage.shape
    n_chunks = d // rows
    ca = _silu(c_ref[...]).astype(BF16)
    ca = jnp.concatenate([ca, jnp.zeros((-batch % BF16_ROW_TILE, d), BF16)], axis=0)

    def copy(k):
        return pltpu.make_async_copy(w_hbm.at[0, pl.ds(k * rows, rows), :], stage.at[k % slots], sem.at[k % slots])

    for k in range(min(slots - 1, n_chunks)):
        copy(k).start()
    acc = jnp.zeros((ca.shape[0], n), F32)
    for k in range(n_chunks):
        if k + slots - 1 < n_chunks:
            copy(k + slots - 1).start()
        copy(k).wait()
        acc = acc + jnp.dot(ca[:, k * rows:(k + 1) * rows], stage[k % slots].astype(BF16),
                            preferred_element_type=F32)
    res = acc[:batch] + b_ref[...]
    for part in range(o_ref.shape[0]):
        o_ref[part] = res[:, part * d:(part + 1) * d]


def _adaln_mod(c, w_ada, b_ada):
    batch, d = c.shape
    n = w_ada.shape[2]
    assert d % WEIGHT_STAGE_ROWS == 0 and n == 3 * d
    return pl.pallas_call(
        _adaln_kernel,
        in_specs=[pl.BlockSpec(memory_space=pltpu.VMEM),
                  pl.BlockSpec(memory_space=pl.ANY),
                  pl.BlockSpec(memory_space=pltpu.VMEM)],
        out_specs=pl.BlockSpec(memory_space=pltpu.VMEM),
        out_shape=jax.ShapeDtypeStruct((3, batch, d), F32),
        scratch_shapes=[pltpu.VMEM((ADALN_STAGE_SLOTS, WEIGHT_STAGE_ROWS, n), F32),
                        pltpu.SemaphoreType.DMA((ADALN_STAGE_SLOTS,))],
        compiler_params=pltpu.CompilerParams(vmem_limit_bytes=V7X_VMEM_LIMIT_BYTES),
        name="adaln_mod",
    )(c, w_ada, b_ada)


def _in_proj_kernel(x_ref, mod_ref, g_ref, w_hbm, b_ref, u_ref, sg_ref, w_ref, stage, sem, hs_ref, *, n_chunk):
    first = pl.program_id(0) == 0
    n = w_ref.shape[1]
    n_mix = u_ref.shape[1]
    batch, tt, d = x_ref.shape

    def normalised_input():
        for b in range(batch):
            x = x_ref[b]
            r = lax.rsqrt(jnp.mean(x * x, axis=-1, keepdims=True) + EPS)
            h = (x * r) * (g_ref[...] * (1.0 + mod_ref[1, b:b + 1, :])) + mod_ref[0, b:b + 1, :]
            for c in range(d // LANES):
                hs_ref[c, pl.ds(b, tt, stride=batch), :] = h[:, c * LANES:(c + 1) * LANES]
        return jnp.concatenate([hs_ref[c] for c in range(d // LANES)], axis=1).astype(BF16)

    def project(hb, sl):
        zj = jnp.dot(hb, w_ref[:, sl], preferred_element_type=F32)
        if sl.start < n_mix:
            u_ref[:, sl] = zj + b_ref[:, sl]
        else:
            half = zj + 0.5 * b_ref[:, sl]
            sg_ref[:, sl.start - n_mix:sl.stop - n_mix] = (half * jnp.tanh(half) + half).astype(sg_ref.dtype)

    @pl.when(first)
    def _():
        cols = stage.shape[2]

        def copy(c):
            return pltpu.make_async_copy(w_hbm.at[:, pl.ds(c * cols, cols)], stage.at[c % 2], sem.at[c % 2])

        copy(0).start()
        hb = normalised_input()
        for c in range(n // cols):
            if c + 1 < n // cols:
                copy(c + 1).start()
            copy(c).wait()
            sl = slice(c * cols, (c + 1) * cols)
            halve = 0.5 if sl.start >= n_mix else 1.0
            w_ref[:, sl] = (halve * stage[c % 2]).astype(BF16)
            project(hb, sl)

    @pl.when(jnp.logical_not(first))
    def _():
        hb = normalised_input()
        for j in range(n // n_chunk):
            project(hb, slice(j * n_chunk, (j + 1) * n_chunk))


def _in_proj(x, mod3, norm_g, w_in, b_in, *, tm):
    batch, seq, d = x.shape
    n = w_in.shape[1]
    tt = tm // batch
    rows = batch * seq
    assert (n // 2) % W_IN_STAGE_COLS == 0 and (n // 2) % (n // N_Z_BLOCKS) == 0
    return pl.pallas_call(
        functools.partial(_in_proj_kernel, n_chunk=n // N_Z_BLOCKS),
        grid=(seq // tt,),
        in_specs=[pl.BlockSpec((batch, tt, d), lambda i: (0, i, 0)),
                  pl.BlockSpec(mod3.shape, lambda i: (0, 0, 0)),
                  pl.BlockSpec((1, d), lambda i: (0, 0)),
                  pl.BlockSpec(memory_space=pl.ANY),
                  pl.BlockSpec((1, n), lambda i: (0, 0))],
        out_specs=[pl.BlockSpec((tm, n // 2), lambda i: (i, 0)),
                   pl.BlockSpec((tm, n // 2), lambda i: (i, 0))],
        out_shape=[jax.ShapeDtypeStruct((rows, n // 2), F32), jax.ShapeDtypeStruct((rows, n // 2), BF16)],
        scratch_shapes=[pltpu.VMEM(w_in.shape, BF16),
                        pltpu.VMEM((2, d, W_IN_STAGE_COLS), F32),
                        pltpu.SemaphoreType.DMA((2,)),
                        pltpu.VMEM((d // LANES, tm, LANES), F32)],
        compiler_params=_params(("arbitrary",)),
        name="in_proj",
    )(x, mod3, norm_g, w_in, b_in)


def _halo_specs(width, *, halo, ts, total_rows):
    per_tile = ts // halo
    last = total_rows // halo - 1
    return (pl.BlockSpec((halo, width), lambda i: (jnp.maximum(i * per_tile - 1, 0), 0)),
            pl.BlockSpec((halo, width), lambda i: (jnp.minimum((i + 1) * per_tile, last), 0)))


def _with_halo(prev, cur, nxt, tile, n_tiles):
    prev = jnp.where(tile > 0, prev, 0.0)
    nxt = jnp.where(tile < n_tiles - 1, nxt, 0.0)
    return jnp.concatenate([prev, cur, nxt], axis=0)


def _pool_mix(ue, gate_ref, w_ref, b_ref, scale_ref, o_ref, *, tile, batch, seq, halo, at_sequence_end):
    ts = o_ref.shape[0]
    grp = ue.shape[1] // len(POOL_WINDOWS)
    assert all(w % 2 == 0 for w in POOL_WINDOWS), "the window sums pair steps of opposite parity"
    row = lax.broadcasted_iota(jnp.int32, (ts, 1), 0)
    t = tile * (ts // batch) + row // batch
    mixed = []
    for g, w in enumerate(POOL_WINDOWS):
        cols = slice(g * grp, (g + 1) * grp)
        e = ue[:, cols]
        lo = w // 2
        hi = w - lo - 1
        s, span = e, 2
        while span < w:
            s = s + _shift_rows(s, span * batch)
            span *= 2
        if hi % 2:
            s = _shift_rows(s, -(hi - 1) * batch)
            total = s + _shift_rows(s, -batch)
        else:
            s = _shift_rows(s, -hi * batch)
            total = s + _shift_rows(s, batch)
        total = total[halo:halo + ts]
        if at_sequence_end:
            cnt = (jnp.minimum(t + hi, seq - 1) - jnp.maximum(t - lo, 0) + 1).astype(F32)
        else:
            cnt = float(w)
        pooled = total / cnt - e[halo:halo + ts]
        scale = scale_ref[:, cols]
        m = jnp.dot(pooled.astype(BF16), (w_ref[g] * scale).astype(BF16), preferred_element_type=F32)
        mixed.append(m + b_ref[g:g + 1, :] * scale)
    y = jnp.concatenate(mixed, axis=-1)
    r = lax.rsqrt(jnp.mean(y * y, axis=-1, keepdims=True) + EPS)
    o_ref[...] = (y * r).astype(o_ref.dtype) * gate_ref[...]


def _softplus(v):
    return jnp.maximum(v, 0.0) + jnp.log1p(jnp.exp(-jnp.abs(v)))


def _lru_direction(xc_half, wg_ref, bg_half, rate2, carry_ref, h_ref, *, batch, reverse, after_gates=None):
    tc, width = xc_half.shape
    head = width // N_LRU_HEADS
    groups = tc // SUBLANES
    steps = SUBLANES // batch
    sub = lax.broadcasted_iota(jnp.int32, (SUBLANES, head), 0)
    step_of_row = (SUBLANES - 1 - sub if reverse else sub) // batch
    shift = SUBLANES - batch if reverse else batch

    for h in range(N_LRU_HEADS):
        cols = slice(h * head, (h + 1) * head)
        xh = xc_half[:, cols]
        g = jnp.dot(xh.astype(BF16), wg_ref[h], preferred_element_type=F32) + bg_half[h:h + 1, :]
        if after_gates is not None:
            after_gates(h)
        tr = jnp.tanh(g[:, :head])
        ti = jnp.tanh(g[:, head:])
        log2a = tr * rate2[:, cols] + rate2[:, cols]
        a = jnp.exp2(log2a)
        one_minus_a2 = (a * a + 1.0) * jnp.tanh(log2a * (-LN2))
        mult = one_minus_a2 * lax.rsqrt(jnp.maximum(one_minus_a2, F32_MIN_NORMAL))
        b = mult * (ti + 1.0) * xh

        a3 = a.reshape(groups, SUBLANES, head)
        b3 = b.reshape(groups, SUBLANES, head)
        state = carry_ref[:, cols]
        order = range(groups - 1, -1, -1) if reverse else range(groups)
        for gi in order:
            passes = []
            for _ in range(steps):
                state = b3[gi] + a3[gi] * pltpu.roll(state, shift, 0)
                passes.append(state)
            out = passes[0]
            for k in range(1, steps):
                out = jnp.where(step_of_row == k, passes[k], out)
            h_ref[gi * SUBLANES:(gi + 1) * SUBLANES, cols] = out
        carry_ref[:, cols] = state


def _lru_rates(lam_row, bg_ref):
    rate2 = (-0.5 * LRU_C * LOG2E) * _softplus(-lam_row)
    return rate2, 0.5 * bg_ref[0]


def _cast_gate_weights(wg_ref, wgb_ref):
    for h in range(N_LRU_HEADS):
        wgb_ref[h] = wg_ref[0, h].astype(BF16)


def _mixers_fwd_kernel(prev_ref, z_hbm, next_ref, gate_ref, wp_ref, bp_ref, scale_ref, cw_ref, cb_ref, wg_ref,
                       bg_ref, lam_ref, yp_ref, xc_ref, hf_ref, carry_ref, wgb_ref, z_ring, z_sem,
                       *, batch, seq, n_tiles):
    i = pl.program_id(0)
    tc = xc_ref.shape[0]
    halo = prev_ref.shape[0]
    width = yp_ref.shape[1]

    def z_copy(t):
        rows = pl.ds(pl.multiple_of(t * tc, tc), tc)
        return pltpu.make_async_copy(z_hbm.at[rows, :], z_ring.at[t % Z_RING_SLOTS], z_sem.at[t % Z_RING_SLOTS])

    @pl.when(i == 0)
    def _():
        for t in range(Z_RING_SLOTS - 1):
            z_copy(t).start()
        _cast_gate_weights(wg_ref, wgb_ref)
        carry_ref[...] = jnp.zeros_like(carry_ref)

    @pl.when(i + Z_RING_SLOTS - 1 < n_tiles)
    def _():
        z_copy(i + Z_RING_SLOTS - 1).start()

    z_copy(i).wait()
    z_ref = z_ring.at[i % Z_RING_SLOTS]

    def pool(at_sequence_end):
        ue = _with_halo(prev_ref[:, :width], z_ref[:, :width], next_ref[:, :width], i, n_tiles)
        _pool_mix(ue, gate_ref, wp_ref, bp_ref, scale_ref, yp_ref, tile=i, batch=batch, seq=seq, halo=halo,
                  at_sequence_end=at_sequence_end)

    interior = (i > 0) & (i < n_tiles - 1)
    pl.when(jnp.logical_not(interior))(functools.partial(pool, True))
    pl.when(interior)(functools.partial(pool, False))

    rate2, bg_half = _lru_rates(lam_ref[0:1, :], bg_ref)
    ch = _halo_rows(CONV_WIDTH // 2, batch)
    ue = _with_halo(prev_ref[halo - ch:, width:], z_ref[:, width:], next_ref[:ch, width:], i, n_tiles)
    left = CONV_WIDTH // 2
    xc_half = 0.5 * cb_ref[...]
    for k in range(CONV_WIDTH):
        tap = ue if k == left else _shift_rows(ue, (left - k) * batch)
        xc_half = xc_half + tap[ch:ch + tc] * (0.5 * cw_ref[k:k + 1, :])
    xc_ref[...] = xc_half
    _lru_direction(xc_half, wgb_ref, bg_half, rate2, carry_ref, hf_ref, batch=batch, reverse=False)


def _mixers_fwd(z, sg, w_pool, b_pool, pool_scale, conv_w, conv_b, w_gate, b_gate, lru_lambda, *, batch, seq, tc):
    width = conv_b.shape[1]
    n_tiles = batch * seq // tc
    assert SUBLANES % batch == 0, "a group of 8 rows must hold whole time steps"
    assert tc // batch >= max(POOL_WINDOWS) and n_tiles >= Z_RING_SLOTS - 1
    prev_spec, next_spec = _halo_specs(z.shape[1], halo=_halo_rows(max(POOL_WINDOWS) // 2, batch), ts=tc,
                                       total_rows=batch * seq)
    n_grp, grp, _ = w_pool.shape
    _, n_head, head, two_head = w_gate.shape
    tile = pl.BlockSpec((tc, width), lambda i: (i, 0))
    return pl.pallas_call(
        functools.partial(_mixers_fwd_kernel, batch=batch, seq=seq, n_tiles=n_tiles),
        grid=(n_tiles,),
        in_specs=[prev_spec,
                  pl.BlockSpec(memory_space=pl.ANY),
                  next_spec,
                  tile,
                  pl.BlockSpec((n_grp, grp, grp), lambda i: (0, 0, 0)),
                  pl.BlockSpec((n_grp, grp), lambda i: (0, 0)),
                  pl.BlockSpec((1, width), lambda i: (0, 0)),
                  pl.BlockSpec((CONV_WIDTH, width), lambda i: (0, 0)),
                  pl.BlockSpec((1, width), lambda i: (0, 0)),
                  pl.BlockSpec((1, n_head, head, two_head), lambda i: (0, 0, 0, 0)),
                  pl.BlockSpec((1, n_head, two_head), lambda i: (0, 0, 0)),
                  pl.BlockSpec(lru_lambda.shape, lambda i: (0, 0))],
        out_specs=[tile, tile, tile],
        out_shape=[jax.ShapeDtypeStruct((batch * seq, width), BF16)] + [jax.ShapeDtypeStruct((batch * seq, width), F32)] * 2,
        scratch_shapes=[pltpu.VMEM((SUBLANES, width), F32),
                        pltpu.VMEM((n_head, head, two_head), BF16),
                        pltpu.VMEM((Z_RING_SLOTS, tc, z.shape[1]), F32),
                        pltpu.SemaphoreType.DMA((Z_RING_SLOTS,))],
        compiler_params=_params(("arbitrary",)),
        name="mixers_fwd",
    )(z, z, z, sg, w_pool, b_pool, pool_scale, conv_w, conv_b, w_gate, b_gate, lru_lambda)


def _lru_bwd_out_kernel(xc_ref, hf_ref, gate_ref, wg_ref, bg_ref, lam_ref, ngp_ref, ng_ref,
                        yp_ref, x_ref, mod_ref, w_hbm, b_ref, fg_ref,
                        o_ref, hb_ref, yl_ref, y_ref, carry_ref, wgb_ref, w_ref, stage, sem, *, n_tiles):
    s = pl.program_id(0)
    kp = yp_ref.shape[1]
    batch, tt, d = x_ref.shape
    n_chunk = d // N_LRU_HEADS

    def project_chunk(h):
        cols = slice(h * n_chunk, (h + 1) * n_chunk)
        y = jnp.dot(yp_ref[...], w_ref[:kp, cols], preferred_element_type=F32)
        y = y + jnp.dot(yl_ref[...], w_ref[kp:, cols], preferred_element_type=F32)
        y = y + b_ref[:, cols]
        for c in range(n_chunk // LANES):
            y_ref[h * (n_chunk // LANES) + c] = y[:, c * LANES:(c + 1) * LANES]

    def scan(after_gates):
        rate2, bg_half = _lru_rates(lam_ref[1:2, :], bg_ref)
        _lru_direction(xc_ref[...], wgb_ref, bg_half, rate2, carry_ref, hb_ref, batch=batch, reverse=True,
                       after_gates=after_gates)

    def finish_scan():
        yl = hf_ref[...] + hb_ref[...]
        rl = lax.rsqrt(jnp.mean(yl * yl, axis=-1, keepdims=True) + EPS)
        yl_ref[...] = (yl * rl).astype(yl_ref.dtype) * gate_ref[...]

    def finish_projection():
        for b in range(batch):
            y = jnp.concatenate([y_ref[c, pl.ds(b, tt, stride=batch), :] for c in range(d // LANES)], axis=1)
            xn = x_ref[b] + mod_ref[2, b:b + 1, :] * y
            r = lax.rsqrt(jnp.mean(xn * xn, axis=-1, keepdims=True) + EPS)
            o_ref[b] = (xn * r) * fg_ref[...]

    @pl.when(s == 0)
    def _():
        _cast_gate_weights(wg_ref, wgb_ref)
        carry_ref[...] = jnp.zeros_like(carry_ref)
        rows = stage.shape[1]

        def row_gain(c):
            g, lo = (ngp_ref, c * rows) if c * rows < kp else (ng_ref, c * rows - kp)
            return jnp.transpose(jnp.broadcast_to(g[:, lo:lo + rows], (LANES, rows)))[:, :1]

        begin, advance, n_chunks = _weight_stream(w_hbm, w_ref, stage, sem, row_gain)
        per_head = n_chunks // N_LRU_HEADS
        begin()
        scan(lambda h: advance(h * per_head, (h + 1) * per_head))
        advance(N_LRU_HEADS * per_head, n_chunks)
        finish_scan()

    @pl.when((s > 0) & (s < n_tiles))
    def _():
        scan(project_chunk)
        finish_projection()
        finish_scan()

    @pl.when(s == n_tiles)
    def _():
        for h in range(N_LRU_HEADS):
            project_chunk(h)
        finish_projection()


def _lru_bwd_out_proj(xc, hf, sg, w_gate, b_gate, lru_lambda, out_norm_pool_g, out_norm_lru_g, yp, x, mod3, w_out,
                      b_out, final_g, *, tm):
    batch, seq, d = x.shape
    width = xc.shape[1]
    tt = tm // batch
    n_tiles = seq // tt
    _, n_head, head, two_head = w_gate.shape

    def scan_tile(s):
        return jnp.maximum(n_tiles - 1 - s, 0)

    def proj_tile(s):
        return jnp.minimum(n_tiles - s, n_tiles - 1)

    return pl.pallas_call(
        functools.partial(_lru_bwd_out_kernel, n_tiles=n_tiles),
        grid=(n_tiles + 1,),
        in_specs=[pl.BlockSpec((tm, width), lambda s: (scan_tile(s), 0)),
                  pl.BlockSpec((tm, width), lambda s: (scan_tile(s), 0)),
                  pl.BlockSpec((tm, width), lambda s: (scan_tile(s), 1)),
                  pl.BlockSpec((1, n_head, head, two_head), lambda s: (1, 0, 0, 0)),
                  pl.BlockSpec((1, n_head, two_head), lambda s: (1, 0, 0)),
                  pl.BlockSpec(lru_lambda.shape, lambda s: (0, 0)),
                  pl.BlockSpec((1, yp.shape[1]), lambda s: (0, 0)),
                  pl.BlockSpec((1, width), lambda s: (0, 0)),
                  pl.BlockSpec((tm, yp.shape[1]), lambda s: (proj_tile(s), 0)),
                  pl.BlockSpec((batch, tt, d), lambda s: (0, proj_tile(s), 0)),
                  pl.BlockSpec(mod3.shape, lambda s: (0, 0, 0)),
                  pl.BlockSpec(memory_space=pl.ANY),
                  pl.BlockSpec((1, d), lambda s: (0, 0)),
                  pl.BlockSpec((1, d), lambda s: (0, 0))],
        out_specs=pl.BlockSpec((batch, tt, d), lambda s: (0, proj_tile(s), 0)),
        out_shape=jax.ShapeDtypeStruct((batch, seq, d), F32),
        scratch_shapes=[pltpu.VMEM((tm, width), F32),
                        pltpu.VMEM((tm, width), BF16),
                        pltpu.VMEM((d // LANES, tm, LANES), F32),
                        pltpu.VMEM((SUBLANES, width), F32),
                        pltpu.VMEM((n_head, head, two_head), BF16),
                        *_weight_scratch(w_out)],
        compiler_params=_params(("arbitrary",)),
        name="lru_bwd_out_proj",
    )(xc, hf, sg, w_gate, b_gate, lru_lambda, out_norm_pool_g, out_norm_lru_g, yp, x, mod3, w_out, b_out, final_g)


def kernel(x, c, norm_g, w_ada, b_ada, w_in, b_in, w_pool, b_pool, pool_scale, conv_w, conv_b, w_gate,
           b_gate, lru_lambda, out_norm_pool_g, out_norm_lru_g, w_out, b_out, final_norm_g):
    batch, seq, d = x.shape
    assert w_in.shape[0] == 1, "single-layer block only"
    mod3 = _adaln_mod(c, w_ada, b_ada)
    tile = TIME_TILE_ROWS
    z, sg = _in_proj(x, mod3, norm_g, w_in[0], b_in, tm=tile)
    yp, xc, hf = _mixers_fwd(z, sg, w_pool[0], b_pool[0], pool_scale, conv_w[0], conv_b, w_gate[0], b_gate[0],
                             lru_lambda[0], batch=batch, seq=seq, tc=tile)
    return _lru_bwd_out_proj(xc, hf, sg, w_gate[0], b_gate[0], lru_lambda[0], out_norm_pool_g, out_norm_lru_g, yp,
                             x, mod3, w_out[0], b_out, final_norm_g[None, :], tm=tile)
```

```python
import functools
import math

import jax
import jax.numpy as jnp
from jax import lax
from jax.experimental import pallas as pl
from jax.experimental.pallas import tpu as pltpu

EPS = 1e-6
LRU_C = 8.0
POOL_WINDOWS = (2, 4, 8, 16)
N_LRU_HEADS = 4
CONV_WIDTH = 4
LANES = 128
SUBLANES = 8
BF16_ROW_TILE = 16
V7X_VMEM_LIMIT_BYTES = 60 * 1024 * 1024
TIME_TILE_ROWS = 512
N_Z_BLOCKS = 4
W_IN_STAGE_COLS = 512
Z_RING_SLOTS = 3
WEIGHT_STAGE_ROWS = 256
LOG2E = math.log2(math.e)
LN2 = math.log(2.0)
F32_MIN_NORMAL = float(jnp.finfo(jnp.float32).tiny)

F32 = jnp.float32
BF16 = jnp.bfloat16


def _params(semantics):
    return pltpu.CompilerParams(dimension_semantics=semantics,
                                vmem_limit_bytes=V7X_VMEM_LIMIT_BYTES)


def _sigmoid(v):
    return 0.5 * jnp.tanh(0.5 * v) + 0.5


def _silu(v):
    return v * _sigmoid(v)


def _halo_rows(steps, batch):
    return -(-steps * batch // SUBLANES) * SUBLANES


def _shift_rows(v, k):
    rows, width = v.shape
    v3 = v.reshape(rows // SUBLANES, SUBLANES, width)
    q, r = divmod(k, SUBLANES)
    if q % v3.shape[0]:
        g = q % v3.shape[0]
        v3 = jnp.concatenate([v3[-g:], v3[:-g]], axis=0)
    if r:
        sub = lax.broadcasted_iota(jnp.int32, v3.shape, 1)
        rot = pltpu.roll(v3, r, 1)
        other = jnp.concatenate([rot[-1:], rot[:-1]], axis=0)
        v3 = jnp.where(sub >= r, rot, other)
    return v3.reshape(rows, width)


def _weight_stream(w_hbm, w_vmem, stage, sem, row_gain=None):
    slots, rows, _ = stage.shape
    n_chunks = w_hbm.shape[0] // rows

    def copy(c):
        return pltpu.make_async_copy(w_hbm.at[pl.ds(c * rows, rows), :], stage.at[c % slots], sem.at[c % slots])

    def begin():
        for c in range(min(slots, n_chunks)):
            copy(c).start()

    def advance(lo, hi):
        for c in range(lo, hi):
            copy(c).wait()
            chunk = stage[c % slots]
            if row_gain is not None:
                chunk = chunk * row_gain(c)
            w_vmem[c * rows:(c + 1) * rows, :] = chunk.astype(BF16)
            if c + slots < n_chunks:
                copy(c + slots).start()

    return begin, advance, n_chunks


def _weight_scratch(w):
    k, n = w.shape
    assert k % WEIGHT_STAGE_ROWS == 0
    return [pltpu.VMEM((k, n), BF16), pltpu.VMEM((2, WEIGHT_STAGE_ROWS, n), F32), pltpu.SemaphoreType.DMA((2,))]


ADALN_STAGE_SLOTS = 4


def _adaln_kernel(c_ref, w_hbm, b_ref, o_ref, stage, sem):
    batch, d = c_ref.shape
    slots, rows, n = stage.shape
    n_chunks = d // rows
    ca = _silu(c_ref[...]).astype(BF16)
    ca = jnp.concatenate([ca, jnp.zeros((-batch % BF16_ROW_TILE, d), BF16)], axis=0)

    def copy(k):
        return pltpu.make_async_copy(w_hbm.at[0, pl.ds(k * rows, rows), :], stage.at[k % slots], sem.at[k % slots])

    for k in range(min(slots - 1, n_chunks)):
        copy(k).start()
    acc = jnp.zeros((ca.shape[0], n), F32)
    for k in range(n_chunks):
        if k + slots - 1 < n_chunks:
            copy(k + slots - 1).start()
        copy(k).wait()
        acc = acc + jnp.dot(ca[:, k * rows:(k + 1) * rows], stage[k % slots].astype(BF16),
                            preferred_element_type=F32)
    res = acc[:batch] + b_ref[...]
    for part in range(o_ref.shape[0]):
        o_ref[part] = res[:, part * d:(part + 1) * d]


def _adaln_mod(c, w_ada, b_ada):
    batch, d = c.shape
    n = w_ada.shape[2]
    assert d % WEIGHT_STAGE_ROWS == 0 and n == 3 * d
    return pl.pallas_call(
        _adaln_kernel,
        in_specs=[pl.BlockSpec(memory_space=pltpu.VMEM),
                  pl.BlockSpec(memory_space=pl.ANY),
                  pl.BlockSpec(memory_space=pltpu.VMEM)],
        out_specs=pl.BlockSpec(memory_space=pltpu.VMEM),
        out_shape=jax.ShapeDtypeStruct((3, batch, d), F32),
        scratch_shapes=[pltpu.VMEM((ADALN_STAGE_SLOTS, WEIGHT_STAGE_ROWS, n), F32),
                        pltpu.SemaphoreType.DMA((ADALN_STAGE_SLOTS,))],
        compiler_params=pltpu.CompilerParams(vmem_limit_bytes=V7X_VMEM_LIMIT_BYTES),
        name="adaln_mod",
    )(c, w_ada, b_ada)


def _in_proj_kernel(x_ref, mod_ref, g_ref, w_hbm, b_ref, u_ref, sg_ref, w_ref, stage, sem, hs_ref, *, n_chunk):
    first = pl.program_id(0) == 0
    n = w_ref.shape[1]
    n_mix = u_ref.shape[1]
    batch, tt, d = x_ref.shape

    def normalised_input():
        for b in range(batch):
            x = x_ref[b]
            r = lax.rsqrt(jnp.mean(x * x, axis=-1, keepdims=True) + EPS)
            h = (x * r) * (g_ref[...] * (1.0 + mod_ref[1, b:b + 1, :])) + mod_ref[0, b:b + 1, :]
            for c in range(d // LANES):
                hs_ref[c, pl.ds(b, tt, stride=batch), :] = h[:, c * LANES:(c + 1) * LANES]
        return jnp.concatenate([hs_ref[c] for c in range(d // LANES)], axis=1).astype(BF16)

    def project(hb, sl):
        zj = jnp.dot(hb, w_ref[:, sl], preferred_element_type=F32)
        if sl.start < n_mix:
            u_ref[:, sl] = zj + b_ref[:, sl]
        else:
            half = zj + 0.5 * b_ref[:, sl]
            sg_ref[:, sl.start - n_mix:sl.stop - n_mix] = (half * jnp.tanh(half) + half).astype(sg_ref.dtype)

    @pl.when(first)
    def _():
        cols = stage.shape[2]

        def copy(c):
            return pltpu.make_async_copy(w_hbm.at[:, pl.ds(c * cols, cols)], stage.at[c % 2], sem.at[c % 2])

        copy(0).start()
        hb = normalised_input()
        for c in range(n // cols):
            if c + 1 < n // cols:
                copy(c + 1).start()
            copy(c).wait()
            sl = slice(c * cols, (c + 1) * cols)
            halve = 0.5 if sl.start >= n_mix else 1.0
            w_ref[:, sl] = (halve * stage[c % 2]).astype(BF16)
            project(hb, sl)

    @pl.when(jnp.logical_not(first))
    def _():
        hb = normalised_input()
        for j in range(n // n_chunk):
            project(hb, slice(j * n_chunk, (j + 1) * n_chunk))


def _in_proj(x, mod3, norm_g, w_in, b_in, *, tm):
    batch, seq, d = x.shape
    n = w_in.shape[1]
    tt = tm // batch
    rows = batch * seq
    assert (n // 2) % W_IN_STAGE_COLS == 0 and (n // 2) % (n // N_Z_BLOCKS) == 0
    return pl.pallas_call(
        functools.partial(_in_proj_kernel, n_chunk=n // N_Z_BLOCKS),
        grid=(seq // tt,),
        in_specs=[pl.BlockSpec((batch, tt, d), lambda i: (0, i, 0)),
                  pl.BlockSpec(mod3.shape, lambda i: (0, 0, 0)),
                  pl.BlockSpec((1, d), lambda i: (0, 0)),
                  pl.BlockSpec(memory_space=pl.ANY),
                  pl.BlockSpec((1, n), lambda i: (0, 0))],
        out_specs=[pl.BlockSpec((tm, n // 2), lambda i: (i, 0)),
                   pl.BlockSpec((tm, n // 2), lambda i: (i, 0))],
        out_shape=[jax.ShapeDtypeStruct((rows, n // 2), F32), jax.ShapeDtypeStruct((rows, n // 2), BF16)],
        scratch_shapes=[pltpu.VMEM(w_in.shape, BF16),
                        pltpu.VMEM((2, d, W_IN_STAGE_COLS), F32),
                        pltpu.SemaphoreType.DMA((2,)),
                        pltpu.VMEM((d // LANES, tm, LANES), F32)],
        compiler_params=_params(("arbitrary",)),
        name="in_proj",
    )(x, mod3, norm_g, w_in, b_in)


def _halo_specs(width, *, halo, ts, total_rows):
    per_tile = ts // halo
    last = total_rows // halo - 1
    return (pl.BlockSpec((halo, width), lambda i: (jnp.maximum(i * per_tile - 1, 0), 0)),
            pl.BlockSpec((halo, width), lambda i: (jnp.minimum((i + 1) * per_tile, last), 0)))


def _with_halo(prev, cur, nxt, tile, n_tiles):
    prev = jnp.where(tile > 0, prev, 0.0)
    nxt = jnp.where(tile < n_tiles - 1, nxt, 0.0)
    return jnp.concatenate([prev, cur, nxt], axis=0)


def _pool_mix(ue, gate_ref, w_ref, b_ref, scale_ref, o_ref, *, tile, batch, seq, halo, at_sequence_end):
    ts = o_ref.shape[0]
    grp = ue.shape[1] // len(POOL_WINDOWS)
    assert all(w % 2 == 0 for w in POOL_WINDOWS), "the window sums pair steps of opposite parity"
    row = lax.broadcasted_iota(jnp.int32, (ts, 1), 0)
    t = tile * (ts // batch) + row // batch
    mixed = []
    for g, w in enumerate(POOL_WINDOWS):
        cols = slice(g * grp, (g + 1) * grp)
        e = ue[:, cols]
        lo = w // 2
        hi = w - lo - 1
        s, span = e, 2
        while span < w:
            s = s + _shift_rows(s, span * batch)
            span *= 2
        if hi % 2:
            s = _shift_rows(s, -(hi - 1) * batch)
            total = s + _shift_rows(s, -batch)
        else:
            s = _shift_rows(s, -hi * batch)
            total = s + _shift_rows(s, batch)
        total = total[halo:halo + ts]
        if at_sequence_end:
            cnt = (jnp.minimum(t + hi, seq - 1) - jnp.maximum(t - lo, 0) + 1).astype(F32)
        else:
            cnt = float(w)
        pooled = total / cnt - e[halo:halo + ts]
        scale = scale_ref[:, cols]
        m = jnp.dot(pooled.astype(BF16), (w_ref[g] * scale).astype(BF16), preferred_element_type=F32)
        mixed.append(m + b_ref[g:g + 1, :] * scale)
    y = jnp.concatenate(mixed, axis=-1)
    r = lax.rsqrt(jnp.mean(y * y, axis=-1, keepdims=True) + EPS)
    o_ref[...] = (y * r).astype(o_ref.dtype) * gate_ref[...]


def _softplus(v):
    return jnp.maximum(v, 0.0) + jnp.log1p(jnp.exp(-jnp.abs(v)))


def _lru_direction(xc_half, wg_ref, bg_half, rate2, carry_ref, h_ref, *, batch, reverse, after_gates=None):
    tc, width = xc_half.shape
    head = width // N_LRU_HEADS
    groups = tc // SUBLANES
    steps = SUBLANES // batch
    sub = lax.broadcasted_iota(jnp.int32, (SUBLANES, head), 0)
    step_of_row = (SUBLANES - 1 - sub if reverse else sub) // batch
    shift = SUBLANES - batch if reverse else batch

    for h in range(N_LRU_HEADS):
        cols = slice(h * head, (h + 1) * head)
        xh = xc_half[:, cols]
        g = jnp.dot(xh.astype(BF16), wg_ref[h], preferred_element_type=F32) + bg_half[h:h + 1, :]
        if after_gates is not None:
            after_gates(h)
        tr = jnp.tanh(g[:, :head])
        ti = jnp.tanh(g[:, head:])
        log2a = tr * rate2[:, cols] + rate2[:, cols]
        a = jnp.exp2(log2a)
        one_minus_a2 = (a * a + 1.0) * jnp.tanh(log2a * (-LN2))
        mult = one_minus_a2 * lax.rsqrt(jnp.maximum(one_minus_a2, F32_MIN_NORMAL))
        b = mult * (ti + 1.0) * xh

        a3 = a.reshape(groups, SUBLANES, head)
        b3 = b.reshape(groups, SUBLANES, head)
        state = carry_ref[:, cols]
        order = range(groups - 1, -1, -1) if reverse else range(groups)
        for gi in order:
            passes = []
            for _ in range(steps):
                state = b3[gi] + a3[gi] * pltpu.roll(state, shift, 0)
                passes.append(state)
            out = passes[0]
            for k in range(1, steps):
                out = jnp.where(step_of_row == k, passes[k], out)
            h_ref[gi * SUBLANES:(gi + 1) * SUBLANES, cols] = out
        carry_ref[:, cols] = state


def _lru_rates(lam_row, bg_ref):
    rate2 = (-0.5 * LRU_C * LOG2E) * _softplus(-lam_row)
    return rate2, 0.5 * bg_ref[0]


def _cast_gate_weights(wg_ref, wgb_ref):
    for h in range(N_LRU_HEADS):
        wgb_ref[h] = wg_ref[0, h].astype(BF16)


def _mixers_fwd_kernel(prev_ref, z_hbm, next_ref, gate_ref, wp_ref, bp_ref, scale_ref, cw_ref, cb_ref, wg_ref,
                       bg_ref, lam_ref, yp_ref, xc_ref, hf_ref, carry_ref, wgb_ref, z_ring, z_sem,
                       *, batch, seq, n_tiles):
    i = pl.program_id(0)
    tc = xc_ref.shape[0]
    halo = prev_ref.shape[0]
    width = yp_ref.shape[1]

    def z_copy(t):
        rows = pl.ds(pl.multiple_of(t * tc, tc), tc)
        return pltpu.make_async_copy(z_hbm.at[rows, :], z_ring.at[t % Z_RING_SLOTS], z_sem.at[t % Z_RING_SLOTS])

    @pl.when(i == 0)
    def _():
        for t in range(Z_RING_SLOTS - 1):
            z_copy(t).start()
        _cast_gate_weights(wg_ref, wgb_ref)
        carry_ref[...] = jnp.zeros_like(carry_ref)

    @pl.when(i + Z_RING_SLOTS - 1 < n_tiles)
    def _():
        z_copy(i + Z_RING_SLOTS - 1).start()

    z_copy(i).wait()
    z_ref = z_ring.at[i % Z_RING_SLOTS]

    def pool(at_sequence_end):
        ue = _with_halo(prev_ref[:, :width], z_ref[:, :width], next_ref[:, :width], i, n_tiles)
        _pool_mix(ue, gate_ref, wp_ref, bp_ref, scale_ref, yp_ref, tile=i, batch=batch, seq=seq, halo=halo,
                  at_sequence_end=at_sequence_end)

    interior = (i > 0) & (i < n_tiles - 1)
    pl.when(jnp.logical_not(interior))(functools.partial(pool, True))
    pl.when(interior)(functools.partial(pool, False))

    rate2, bg_half = _lru_rates(lam_ref[0:1, :], bg_ref)
    ch = _halo_rows(CONV_WIDTH // 2, batch)
    ue = _with_halo(prev_ref[halo - ch:, width:], z_ref[:, width:], next_ref[:ch, width:], i, n_tiles)
    left = CONV_WIDTH // 2
    xc_half = 0.5 * cb_ref[...]
    for k in range(CONV_WIDTH):
        tap = ue if k == left else _shift_rows(ue, (left - k) * batch)
        xc_half = xc_half + tap[ch:ch + tc] * (0.5 * cw_ref[k:k + 1, :])
    xc_ref[...] = xc_half
    _lru_direction(xc_half, wgb_ref, bg_half, rate2, carry_ref, hf_ref, batch=batch, reverse=False)


def _mixers_fwd(z, sg, w_pool, b_pool, pool_scale, conv_w, conv_b, w_gate, b_gate, lru_lambda, *, batch, seq, tc):
    width = conv_b.shape[1]
    n_tiles = batch * seq // tc
    assert SUBLANES % batch == 0, "a group of 8 rows must hold whole time steps"
    assert tc // batch >= max(POOL_WINDOWS) and n_tiles >= Z_RING_SLOTS - 1
    prev_spec, next_spec = _halo_specs(z.shape[1], halo=_halo_rows(max(POOL_WINDOWS) // 2, batch), ts=tc,
                                       total_rows=batch * seq)
    n_grp, grp, _ = w_pool.shape
    _, n_head, head, two_head = w_gate.shape
    tile = pl.BlockSpec((tc, width), lambda i: (i, 0))
    return pl.pallas_call(
        functools.partial(_mixers_fwd_kernel, batch=batch, seq=seq, n_tiles=n_tiles),
        grid=(n_tiles,),
        in_specs=[prev_spec,
                  pl.BlockSpec(memory_space=pl.ANY),
                  next_spec,
                  tile,
                  pl.BlockSpec((n_grp, grp, grp), lambda i: (0, 0, 0)),
                  pl.BlockSpec((n_grp, grp), lambda i: (0, 0)),
                  pl.BlockSpec((1, width), lambda i: (0, 0)),
                  pl.BlockSpec((CONV_WIDTH, width), lambda i: (0, 0)),
                  pl.BlockSpec((1, width), lambda i: (0, 0)),
                  pl.BlockSpec((1, n_head, head, two_head), lambda i: (0, 0, 0, 0)),
                  pl.BlockSpec((1, n_head, two_head), lambda i: (0, 0, 0)),
                  pl.BlockSpec(lru_lambda.shape, lambda i: (0, 0))],
        out_specs=[tile, tile, tile],
        out_shape=[jax.ShapeDtypeStruct((batch * seq, width), BF16)] + [jax.ShapeDtypeStruct((batch * seq, width), F32)] * 2,
        scratch_shapes=[pltpu.VMEM((SUBLANES, width), F32),
                        pltpu.VMEM((n_head, head, two_head), BF16),
                        pltpu.VMEM((Z_RING_SLOTS, tc, z.shape[1]), F32),
                        pltpu.SemaphoreType.DMA((Z_RING_SLOTS,))],
        compiler_params=_params(("arbitrary",)),
        name="mixers_fwd",
    )(z, z, z, sg, w_pool, b_pool, pool_scale, conv_w, conv_b, w_gate, b_gate, lru_lambda)


def _lru_bwd_out_kernel(xc_hbm, hf_hbm, gate_ref, wg_ref, bg_ref, lam_ref, ngp_ref, ng_ref,
                        yp_ref, x_ref, mod_ref, w_hbm, b_ref, fg_ref,
                        o_ref, hb_ref, yl_ref, y_ref, carry_ref, wgb_ref, w_ref, stage, sem, in_ring, in_sem,
                        *, n_tiles):
    s = pl.program_id(0)
    kp = yp_ref.shape[1]
    batch, tt, d = x_ref.shape
    n_chunk = d // N_LRU_HEADS
    tm = hb_ref.shape[0]

    def tile_copies(k):
        rows = pl.ds(pl.multiple_of((n_tiles - 1 - k) * tm, tm), tm)
        slot = k % Z_RING_SLOTS
        return [pltpu.make_async_copy(src.at[rows, :], in_ring.at[which, slot], in_sem.at[which, slot])
                for which, src in enumerate((xc_hbm, hf_hbm))]

    @pl.when(s == 0)
    def _():
        for k in range(Z_RING_SLOTS - 1):
            for c in tile_copies(k):
                c.start()

    @pl.when(s + Z_RING_SLOTS - 1 < n_tiles)
    def _():
        for c in tile_copies(s + Z_RING_SLOTS - 1):
            c.start()

    @pl.when(s < n_tiles)
    def _():
        for c in tile_copies(s):
            c.wait()

    xc_ref = in_ring.at[0, s % Z_RING_SLOTS]
    hf_ref = in_ring.at[1, s % Z_RING_SLOTS]

    def project_chunk(h):
        cols = slice(h * n_chunk, (h + 1) * n_chunk)
        y = jnp.dot(yp_ref[...], w_ref[:kp, cols], preferred_element_type=F32)
        y = y + jnp.dot(yl_ref[...], w_ref[kp:, cols], preferred_element_type=F32)
        y = y + b_ref[:, cols]
        for c in range(n_chunk // LANES):
            y_ref[h * (n_chunk // LANES) + c] = y[:, c * LANES:(c + 1) * LANES]

    def scan(after_gates):
        rate2, bg_half = _lru_rates(lam_ref[1:2, :], bg_ref)
        _lru_direction(xc_ref[...], wgb_ref, bg_half, rate2, carry_ref, hb_ref, batch=batch, reverse=True,
                       after_gates=after_gates)

    def finish_scan():
        yl = hf_ref[...] + hb_ref[...]
        rl = lax.rsqrt(jnp.mean(yl * yl, axis=-1, keepdims=True) + EPS)
        yl_ref[...] = (yl * rl).astype(yl_ref.dtype) * gate_ref[...]

    def finish_projection():
        for b in range(batch):
            y = jnp.concatenate([y_ref[c, pl.ds(b, tt, stride=batch), :] for c in range(d // LANES)], axis=1)
            xn = x_ref[b] + mod_ref[2, b:b + 1, :] * y
            r = lax.rsqrt(jnp.mean(xn * xn, axis=-1, keepdims=True) + EPS)
            o_ref[b] = (xn * r) * fg_ref[...]

    @pl.when(s == 0)
    def _():
        _cast_gate_weights(wg_ref, wgb_ref)
        carry_ref[...] = jnp.zeros_like(carry_ref)
        rows = stage.shape[1]

        def row_gain(c):
            g, lo = (ngp_ref, c * rows) if c * rows < kp else (ng_ref, c * rows - kp)
            return jnp.transpose(jnp.broadcast_to(g[:, lo:lo + rows], (LANES, rows)))[:, :1]

        begin, advance, n_chunks = _weight_stream(w_hbm, w_ref, stage, sem, row_gain)
        per_head = n_chunks // N_LRU_HEADS
        begin()
        scan(lambda h: advance(h * per_head, (h + 1) * per_head))
        advance(N_LRU_HEADS * per_head, n_chunks)
        finish_scan()

    @pl.when((s > 0) & (s < n_tiles))
    def _():
        scan(project_chunk)
        finish_projection()
        finish_scan()

    @pl.when(s == n_tiles)
    def _():
        for h in range(N_LRU_HEADS):
            project_chunk(h)
        finish_projection()


def _lru_bwd_out_proj(xc, hf, sg, w_gate, b_gate, lru_lambda, out_norm_pool_g, out_norm_lru_g, yp, x, mod3, w_out,
                      b_out, final_g, *, tm):
    batch, seq, d = x.shape
    width = xc.shape[1]
    tt = tm // batch
    n_tiles = seq // tt
    _, n_head, head, two_head = w_gate.shape

    def scan_tile(s):
        return jnp.maximum(n_tiles - 1 - s, 0)

    def proj_tile(s):
        return jnp.minimum(n_tiles - s, n_tiles - 1)

    return pl.pallas_call(
        functools.partial(_lru_bwd_out_kernel, n_tiles=n_tiles),
        grid=(n_tiles + 1,),
        in_specs=[pl.BlockSpec(memory_space=pl.ANY),
                  pl.BlockSpec(memory_space=pl.ANY),
                  pl.BlockSpec((tm, width), lambda s: (scan_tile(s), 1)),
                  pl.BlockSpec((1, n_head, head, two_head), lambda s: (1, 0, 0, 0)),
                  pl.BlockSpec((1, n_head, two_head), lambda s: (1, 0, 0)),
                  pl.BlockSpec(lru_lambda.shape, lambda s: (0, 0)),
                  pl.BlockSpec((1, yp.shape[1]), lambda s: (0, 0)),
                  pl.BlockSpec((1, width), lambda s: (0, 0)),
                  pl.BlockSpec((tm, yp.shape[1]), lambda s: (proj_tile(s), 0)),
                  pl.BlockSpec((batch, tt, d), lambda s: (0, proj_tile(s), 0)),
                  pl.BlockSpec(mod3.shape, lambda s: (0, 0, 0)),
                  pl.BlockSpec(memory_space=pl.ANY),
                  pl.BlockSpec((1, d), lambda s: (0, 0)),
                  pl.BlockSpec((1, d), lambda s: (0, 0))],
        out_specs=pl.BlockSpec((batch, tt, d), lambda s: (0, proj_tile(s), 0)),
        out_shape=jax.ShapeDtypeStruct((batch, seq, d), F32),
        scratch_shapes=[pltpu.VMEM((tm, width), F32),
                        pltpu.VMEM((tm, width), BF16),
                        pltpu.VMEM((d // LANES, tm, LANES), F32),
                        pltpu.VMEM((SUBLANES, width), F32),
                        pltpu.VMEM((n_head, head, two_head), BF16),
                        *_weight_scratch(w_out),
                        pltpu.VMEM((2, Z_RING_SLOTS, tm, width), F32),
                        pltpu.SemaphoreType.DMA((2, Z_RING_SLOTS))],
        compiler_params=_params(("arbitrary",)),
        name="lru_bwd_out_proj",
    )(xc, hf, sg, w_gate, b_gate, lru_lambda, out_norm_pool_g, out_norm_lru_g, yp, x, mod3, w_out, b_out, final_g)


def kernel(x, c, norm_g, w_ada, b_ada, w_in, b_in, w_pool, b_pool, pool_scale, conv_w, conv_b, w_gate,
           b_gate, lru_lambda, out_norm_pool_g, out_norm_lru_g, w_out, b_out, final_norm_g):
    batch, seq, d = x.shape
    assert w_in.shape[0] == 1, "single-layer block only"
    mod3 = _adaln_mod(c, w_ada, b_ada)
    tile = TIME_TILE_ROWS
    z, sg = _in_proj(x, mod3, norm_g, w_in[0], b_in, tm=tile)
    yp, xc, hf = _mixers_fwd(z, sg, w_pool[0], b_pool[0], pool_scale, conv_w[0], conv_b, w_gate[0], b_gate[0],
                             lru_lambda[0], batch=batch, seq=seq, tc=tile)
    return _lru_bwd_out_proj(xc, hf, sg, w_gate[0], b_gate[0], lru_lambda[0], out_norm_pool_g, out_norm_lru_g, yp,
                             x, mod3, w_out[0], b_out, final_norm_g[None, :], tm=tile)
```

```python
import functools
import math

import jax
import jax.numpy as jnp
from jax import lax
from jax.experimental import pallas as pl
from jax.experimental.pallas import tpu as pltpu

EPS = 1e-6
LRU_C = 8.0
POOL_WINDOWS = (2, 4, 8, 16)
N_LRU_HEADS = 4
CONV_WIDTH = 4
LANES = 128
SUBLANES = 8
BF16_ROW_TILE = 16
V7X_VMEM_LIMIT_BYTES = 56 * 1024 * 1024
TIME_TILE_ROWS = 512
N_Z_BLOCKS = 4
W_IN_STAGE_COLS = 512
Z_RING_SLOTS = 3
WEIGHT_STAGE_ROWS = 256
LOG2E = math.log2(math.e)
LN2 = math.log(2.0)
F32_MIN_NORMAL = float(jnp.finfo(jnp.float32).tiny)

F32 = jnp.float32
BF16 = jnp.bfloat16


def _params(semantics):
    return pltpu.CompilerParams(dimension_semantics=semantics,
                                vmem_limit_bytes=V7X_VMEM_LIMIT_BYTES)


def _sigmoid(v):
    return 0.5 * jnp.tanh(0.5 * v) + 0.5


def _silu(v):
    return v * _sigmoid(v)


def _halo_rows(steps, batch):
    return -(-steps * batch // SUBLANES) * SUBLANES


def _shift_rows(v, k):
    rows, width = v.shape
    v3 = v.reshape(rows // SUBLANES, SUBLANES, width)
    q, r = divmod(k, SUBLANES)
    if q % v3.shape[0]:
        g = q % v3.shape[0]
        v3 = jnp.concatenate([v3[-g:], v3[:-g]], axis=0)
    if r:
        sub = lax.broadcasted_iota(jnp.int32, v3.shape, 1)
        rot = pltpu.roll(v3, r, 1)
        other = jnp.concatenate([rot[-1:], rot[:-1]], axis=0)
        v3 = jnp.where(sub >= r, rot, other)
    return v3.reshape(rows, width)


def _weight_stream(w_hbm, w_vmem, stage, sem, row_gain=None):
    slots, rows, _ = stage.shape
    n_chunks = w_hbm.shape[0] // rows

    def copy(c):
        return pltpu.make_async_copy(w_hbm.at[pl.ds(c * rows, rows), :], stage.at[c % slots], sem.at[c % slots])

    def begin():
        for c in range(min(slots, n_chunks)):
            copy(c).start()

    def advance(lo, hi):
        for c in range(lo, hi):
            copy(c).wait()
            chunk = stage[c % slots]
            if row_gain is not None:
                chunk = chunk * row_gain(c)
            w_vmem[c * rows:(c + 1) * rows, :] = chunk.astype(BF16)
            if c + slots < n_chunks:
                copy(c + slots).start()

    return begin, advance, n_chunks


def _weight_scratch(w):
    k, n = w.shape
    assert k % WEIGHT_STAGE_ROWS == 0
    return [pltpu.VMEM((k, n), BF16), pltpu.VMEM((2, WEIGHT_STAGE_ROWS, n), F32), pltpu.SemaphoreType.DMA((2,))]


ADALN_STAGE_SLOTS = 4


def _adaln_kernel(c_ref, w_hbm, b_ref, o_ref, stage, sem):
    batch, d = c_ref.shape
    slots, rows, n = stage.shape
    n_chunks = d // rows
    ca = _silu(c_ref[...]).astype(BF16)
    ca = jnp.concatenate([ca, jnp.zeros((-batch % BF16_ROW_TILE, d), BF16)], axis=0)

    def copy(k):
        return pltpu.make_async_copy(w_hbm.at[0, pl.ds(k * rows, rows), :], stage.at[k % slots], sem.at[k % slots])

    for k in range(min(slots - 1, n_chunks)):
        copy(k).start()
    acc = jnp.zeros((ca.shape[0], n), F32)
    for k in range(n_chunks):
        if k + slots - 1 < n_chunks:
            copy(k + slots - 1).start()
        copy(k).wait()
        acc = acc + jnp.dot(ca[:, k * rows:(k + 1) * rows], stage[k % slots].astype(BF16),
                            preferred_element_type=F32)
    res = acc[:batch] + b_ref[...]
    for part in range(o_ref.shape[0]):
        o_ref[part] = res[:, part * d:(part + 1) * d]


def _adaln_mod(c, w_ada, b_ada):
    batch, d = c.shape
    n = w_ada.shape[2]
    assert d % WEIGHT_STAGE_ROWS == 0 and n == 3 * d
    return pl.pallas_call(
        _adaln_kernel,
        in_specs=[pl.BlockSpec(memory_space=pltpu.VMEM),
                  pl.BlockSpec(memory_space=pl.ANY),
                  pl.BlockSpec(memory_space=pltpu.VMEM)],
        out_specs=pl.BlockSpec(memory_space=pltpu.VMEM),
        out_shape=jax.ShapeDtypeStruct((3, batch, d), F32),
        scratch_shapes=[pltpu.VMEM((ADALN_STAGE_SLOTS, WEIGHT_STAGE_ROWS, n), F32),
                        pltpu.SemaphoreType.DMA((ADALN_STAGE_SLOTS,))],
        compiler_params=pltpu.CompilerParams(vmem_limit_bytes=V7X_VMEM_LIMIT_BYTES),
        name="adaln_mod",
    )(c, w_ada, b_ada)


def _in_proj_kernel(x_ref, mod_ref, g_ref, w_hbm, b_ref, u_ref, sg_ref, w_ref, stage, sem, hs_ref, *, n_chunk):
    first = pl.program_id(0) == 0
    n = w_ref.shape[1]
    n_mix = u_ref.shape[1]
    batch, tt, d = x_ref.shape

    def normalised_input():
        for b in range(batch):
            x = x_ref[b]
            r = lax.rsqrt(jnp.mean(x * x, axis=-1, keepdims=True) + EPS)
            h = (x * r) * (g_ref[...] * (1.0 + mod_ref[1, b:b + 1, :])) + mod_ref[0, b:b + 1, :]
            for c in range(d // LANES):
                hs_ref[c, pl.ds(b, tt, stride=batch), :] = h[:, c * LANES:(c + 1) * LANES]
        return jnp.concatenate([hs_ref[c] for c in range(d // LANES)], axis=1).astype(BF16)

    def project(hb, sl):
        zj = jnp.dot(hb, w_ref[:, sl], preferred_element_type=F32)
        if sl.start < n_mix:
            u_ref[:, sl] = zj + b_ref[:, sl]
        else:
            half = zj + 0.5 * b_ref[:, sl]
            sg_ref[:, sl.start - n_mix:sl.stop - n_mix] = (half * jnp.tanh(half) + half).astype(sg_ref.dtype)

    @pl.when(first)
    def _():
        cols = stage.shape[2]

        def copy(c):
            return pltpu.make_async_copy(w_hbm.at[:, pl.ds(c * cols, cols)], stage.at[c % 2], sem.at[c % 2])

        copy(0).start()
        hb = normalised_input()
        for c in range(n // cols):
            if c + 1 < n // cols:
                copy(c + 1).start()
            copy(c).wait()
            sl = slice(c * cols, (c + 1) * cols)
            halve = 0.5 if sl.start >= n_mix else 1.0
            w_ref[:, sl] = (halve * stage[c % 2]).astype(BF16)
            project(hb, sl)

    @pl.when(jnp.logical_not(first))
    def _():
        hb = normalised_input()
        for j in range(n // n_chunk):
            project(hb, slice(j * n_chunk, (j + 1) * n_chunk))


def _in_proj(x, mod3, norm_g, w_in, b_in, *, tm):
    batch, seq, d = x.shape
    n = w_in.shape[1]
    tt = tm // batch
    rows = batch * seq
    assert (n // 2) % W_IN_STAGE_COLS == 0 and (n // 2) % (n // N_Z_BLOCKS) == 0
    return pl.pallas_call(
        functools.partial(_in_proj_kernel, n_chunk=n // N_Z_BLOCKS),
        grid=(seq // tt,),
        in_specs=[pl.BlockSpec((batch, tt, d), lambda i: (0, i, 0)),
                  pl.BlockSpec(mod3.shape, lambda i: (0, 0, 0)),
                  pl.BlockSpec((1, d), lambda i: (0, 0)),
                  pl.BlockSpec(memory_space=pl.ANY),
                  pl.BlockSpec((1, n), lambda i: (0, 0))],
        out_specs=[pl.BlockSpec((tm, n // 2), lambda i: (i, 0)),
                   pl.BlockSpec((tm, n // 2), lambda i: (i, 0))],
        out_shape=[jax.ShapeDtypeStruct((rows, n // 2), F32), jax.ShapeDtypeStruct((rows, n // 2), BF16)],
        scratch_shapes=[pltpu.VMEM(w_in.shape, BF16),
                        pltpu.VMEM((2, d, W_IN_STAGE_COLS), F32),
                        pltpu.SemaphoreType.DMA((2,)),
                        pltpu.VMEM((d // LANES, tm, LANES), F32)],
        compiler_params=_params(("arbitrary",)),
        name="in_proj",
    )(x, mod3, norm_g, w_in, b_in)


def _halo_specs(width, *, halo, ts, total_rows):
    per_tile = ts // halo
    last = total_rows // halo - 1
    return (pl.BlockSpec((halo, width), lambda i: (jnp.maximum(i * per_tile - 1, 0), 0)),
            pl.BlockSpec((halo, width), lambda i: (jnp.minimum((i + 1) * per_tile, last), 0)))


def _with_halo(prev, cur, nxt, tile, n_tiles):
    prev = jnp.where(tile > 0, prev, 0.0)
    nxt = jnp.where(tile < n_tiles - 1, nxt, 0.0)
    return jnp.concatenate([prev, cur, nxt], axis=0)


def _pool_mix(ue, gate_ref, w_ref, b_ref, scale_ref, o_ref, *, tile, batch, seq, halo, at_sequence_end):
    ts = o_ref.shape[0]
    grp = ue.shape[1] // len(POOL_WINDOWS)
    assert all(w % 2 == 0 for w in POOL_WINDOWS), "the window sums pair steps of opposite parity"
    row = lax.broadcasted_iota(jnp.int32, (ts, 1), 0)
    t = tile * (ts // batch) + row // batch
    mixed = []
    for g, w in enumerate(POOL_WINDOWS):
        cols = slice(g * grp, (g + 1) * grp)
        e = ue[:, cols]
        lo = w // 2
        hi = w - lo - 1
        s, span = e, 2
        while span < w:
            s = s + _shift_rows(s, span * batch)
            span *= 2
        if hi % 2:
            s = _shift_rows(s, -(hi - 1) * batch)
            total = s + _shift_rows(s, -batch)
        else:
            s = _shift_rows(s, -hi * batch)
            total = s + _shift_rows(s, batch)
        total = total[halo:halo + ts]
        if at_sequence_end:
            cnt = (jnp.minimum(t + hi, seq - 1) - jnp.maximum(t - lo, 0) + 1).astype(F32)
        else:
            cnt = float(w)
        pooled = total / cnt - e[halo:halo + ts]
        scale = scale_ref[:, cols]
        m = jnp.dot(pooled.astype(BF16), (w_ref[g] * scale).astype(BF16), preferred_element_type=F32)
        mixed.append(m + b_ref[g:g + 1, :] * scale)
    y = jnp.concatenate(mixed, axis=-1)
    r = lax.rsqrt(jnp.mean(y * y, axis=-1, keepdims=True) + EPS)
    o_ref[...] = (y * r).astype(o_ref.dtype) * gate_ref[...]


def _softplus(v):
    return jnp.maximum(v, 0.0) + jnp.log1p(jnp.exp(-jnp.abs(v)))


def _lru_direction(xc_half, wg_ref, bg_half, rate2, carry_ref, h_ref, *, batch, reverse, after_gates=None):
    tc, width = xc_half.shape
    head = width // N_LRU_HEADS
    groups = tc // SUBLANES
    steps = SUBLANES // batch
    sub = lax.broadcasted_iota(jnp.int32, (SUBLANES, head), 0)
    step_of_row = (SUBLANES - 1 - sub if reverse else sub) // batch
    shift = SUBLANES - batch if reverse else batch

    for h in range(N_LRU_HEADS):
        cols = slice(h * head, (h + 1) * head)
        xh = xc_half[:, cols]
        g = jnp.dot(xh.astype(BF16), wg_ref[h], preferred_element_type=F32) + bg_half[h:h + 1, :]
        if after_gates is not None:
            after_gates(h)
        tr = jnp.tanh(g[:, :head])
        ti = jnp.tanh(g[:, head:])
        log2a = tr * rate2[:, cols] + rate2[:, cols]
        a = jnp.exp2(log2a)
        one_minus_a2 = (a * a + 1.0) * jnp.tanh(log2a * (-LN2))
        mult = one_minus_a2 * lax.rsqrt(jnp.maximum(one_minus_a2, F32_MIN_NORMAL))
        b = mult * (ti + 1.0) * xh

        a3 = a.reshape(groups, SUBLANES, head)
        b3 = b.reshape(groups, SUBLANES, head)
        state = carry_ref[:, cols]
        order = range(groups - 1, -1, -1) if reverse else range(groups)
        for gi in order:
            passes = []
            for _ in range(steps):
                state = b3[gi] + a3[gi] * pltpu.roll(state, shift, 0)
                passes.append(state)
            out = passes[0]
            for k in range(1, steps):
                out = jnp.where(step_of_row == k, passes[k], out)
            h_ref[gi * SUBLANES:(gi + 1) * SUBLANES, cols] = out
        carry_ref[:, cols] = state


def _lru_rates(lam_row, bg_ref):
    rate2 = (-0.5 * LRU_C * LOG2E) * _softplus(-lam_row)
    return rate2, 0.5 * bg_ref[0]


def _cast_gate_weights(wg_ref, wgb_ref):
    for h in range(N_LRU_HEADS):
        wgb_ref[h] = wg_ref[0, h].astype(BF16)


def _mixers_fwd_kernel(prev_ref, z_hbm, next_ref, sg_hbm, wp_ref, bp_ref, scale_ref, cw_ref, cb_ref, wg_ref,
                       bg_ref, lam_ref, yp_ref, xc_ref, hf_ref, carry_ref, wgb_ref, z_ring, z_sem, g_ring, g_sem,
                       *, batch, seq, n_tiles):
    i = pl.program_id(0)
    tc = xc_ref.shape[0]
    halo = prev_ref.shape[0]
    width = yp_ref.shape[1]

    def tile_copies(t):
        rows = pl.ds(pl.multiple_of(t * tc, tc), tc)
        slot = t % Z_RING_SLOTS
        return (pltpu.make_async_copy(z_hbm.at[rows, :], z_ring.at[slot], z_sem.at[slot]),
                pltpu.make_async_copy(sg_hbm.at[rows, pl.ds(0, width)], g_ring.at[slot], g_sem.at[slot]))

    @pl.when(i == 0)
    def _():
        for t in range(Z_RING_SLOTS - 1):
            for c in tile_copies(t):
                c.start()
        _cast_gate_weights(wg_ref, wgb_ref)
        carry_ref[...] = jnp.zeros_like(carry_ref)

    @pl.when(i + Z_RING_SLOTS - 1 < n_tiles)
    def _():
        for c in tile_copies(i + Z_RING_SLOTS - 1):
            c.start()

    for c in tile_copies(i):
        c.wait()
    z_ref = z_ring.at[i % Z_RING_SLOTS]
    gate_ref = g_ring.at[i % Z_RING_SLOTS]

    def pool(at_sequence_end):
        ue = _with_halo(prev_ref[:, :width], z_ref[:, :width], next_ref[:, :width], i, n_tiles)
        _pool_mix(ue, gate_ref, wp_ref, bp_ref, scale_ref, yp_ref, tile=i, batch=batch, seq=seq, halo=halo,
                  at_sequence_end=at_sequence_end)

    interior = (i > 0) & (i < n_tiles - 1)
    pl.when(jnp.logical_not(interior))(functools.partial(pool, True))
    pl.when(interior)(functools.partial(pool, False))

    rate2, bg_half = _lru_rates(lam_ref[0:1, :], bg_ref)
    ch = _halo_rows(CONV_WIDTH // 2, batch)
    ue = _with_halo(prev_ref[halo - ch:, width:], z_ref[:, width:], next_ref[:ch, width:], i, n_tiles)
    left = CONV_WIDTH // 2
    xc_half = 0.5 * cb_ref[...]
    for k in range(CONV_WIDTH):
        tap = ue if k == left else _shift_rows(ue, (left - k) * batch)
        xc_half = xc_half + tap[ch:ch + tc] * (0.5 * cw_ref[k:k + 1, :])
    xc_ref[...] = xc_half
    _lru_direction(xc_half, wgb_ref, bg_half, rate2, carry_ref, hf_ref, batch=batch, reverse=False)


def _mixers_fwd(z, sg, w_pool, b_pool, pool_scale, conv_w, conv_b, w_gate, b_gate, lru_lambda, *, batch, seq, tc):
    width = conv_b.shape[1]
    n_tiles = batch * seq // tc
    assert SUBLANES % batch == 0, "a group of 8 rows must hold whole time steps"
    assert tc // batch >= max(POOL_WINDOWS) and n_tiles >= Z_RING_SLOTS - 1
    prev_spec, next_spec = _halo_specs(z.shape[1], halo=_halo_rows(max(POOL_WINDOWS) // 2, batch), ts=tc,
                                       total_rows=batch * seq)
    n_grp, grp, _ = w_pool.shape
    _, n_head, head, two_head = w_gate.shape
    tile = pl.BlockSpec((tc, width), lambda i: (i, 0))
    return pl.pallas_call(
        functools.partial(_mixers_fwd_kernel, batch=batch, seq=seq, n_tiles=n_tiles),
        grid=(n_tiles,),
        in_specs=[prev_spec,
                  pl.BlockSpec(memory_space=pl.ANY),
                  next_spec,
                  pl.BlockSpec(memory_space=pl.ANY),
                  pl.BlockSpec((n_grp, grp, grp), lambda i: (0, 0, 0)),
                  pl.BlockSpec((n_grp, grp), lambda i: (0, 0)),
                  pl.BlockSpec((1, width), lambda i: (0, 0)),
                  pl.BlockSpec((CONV_WIDTH, width), lambda i: (0, 0)),
                  pl.BlockSpec((1, width), lambda i: (0, 0)),
                  pl.BlockSpec((1, n_head, head, two_head), lambda i: (0, 0, 0, 0)),
                  pl.BlockSpec((1, n_head, two_head), lambda i: (0, 0, 0)),
                  pl.BlockSpec(lru_lambda.shape, lambda i: (0, 0))],
        out_specs=[tile, tile, tile],
        out_shape=[jax.ShapeDtypeStruct((batch * seq, width), BF16)] + [jax.ShapeDtypeStruct((batch * seq, width), F32)] * 2,
        scratch_shapes=[pltpu.VMEM((SUBLANES, width), F32),
                        pltpu.VMEM((n_head, head, two_head), BF16),
                        pltpu.VMEM((Z_RING_SLOTS, tc, z.shape[1]), F32),
                        pltpu.SemaphoreType.DMA((Z_RING_SLOTS,)),
                        pltpu.VMEM((Z_RING_SLOTS, tc, width), BF16),
                        pltpu.SemaphoreType.DMA((Z_RING_SLOTS,))],
        compiler_params=_params(("arbitrary",)),
        name="mixers_fwd",
    )(z, z, z, sg, w_pool, b_pool, pool_scale, conv_w, conv_b, w_gate, b_gate, lru_lambda)


def _lru_bwd_out_kernel(xc_ref, hf_ref, gate_ref, wg_ref, bg_ref, lam_ref, ngp_ref, ng_ref,
                        yp_ref, x_ref, mod_ref, w_hbm, b_ref, fg_ref,
                        o_ref, hb_ref, yl_ref, y_ref, carry_ref, wgb_ref, w_ref, stage, sem, *, n_tiles):
    s = pl.program_id(0)
    kp = yp_ref.shape[1]
    batch, tt, d = x_ref.shape
    n_chunk = d // N_LRU_HEADS

    def project_chunk(h):
        cols = slice(h * n_chunk, (h + 1) * n_chunk)
        y = jnp.dot(yp_ref[...], w_ref[:kp, cols], preferred_element_type=F32)
        y = y + jnp.dot(yl_ref[...], w_ref[kp:, cols], preferred_element_type=F32)
        y = y + b_ref[:, cols]
        for c in range(n_chunk // LANES):
            y_ref[h * (n_chunk // LANES) + c] = y[:, c * LANES:(c + 1) * LANES]

    def scan(after_gates):
        rate2, bg_half = _lru_rates(lam_ref[1:2, :], bg_ref)
        _lru_direction(xc_ref[...], wgb_ref, bg_half, rate2, carry_ref, hb_ref, batch=batch, reverse=True,
                       after_gates=after_gates)

    def finish_scan():
        yl = hf_ref[...] + hb_ref[...]
        rl = lax.rsqrt(jnp.mean(yl * yl, axis=-1, keepdims=True) + EPS)
        yl_ref[...] = (yl * rl).astype(yl_ref.dtype) * gate_ref[...]

    def finish_projection():
        for b in range(batch):
            y = jnp.concatenate([y_ref[c, pl.ds(b, tt, stride=batch), :] for c in range(d // LANES)], axis=1)
            xn = x_ref[b] + mod_ref[2, b:b + 1, :] * y
            r = lax.rsqrt(jnp.mean(xn * xn, axis=-1, keepdims=True) + EPS)
            o_ref[b] = (xn * r) * fg_ref[...]

    @pl.when(s == 0)
    def _():
        _cast_gate_weights(wg_ref, wgb_ref)
        carry_ref[...] = jnp.zeros_like(carry_ref)
        rows = stage.shape[1]

        def row_gain(c):
            g, lo = (ngp_ref, c * rows) if c * rows < kp else (ng_ref, c * rows - kp)
            return jnp.transpose(jnp.broadcast_to(g[:, lo:lo + rows], (LANES, rows)))[:, :1]

        begin, advance, n_chunks = _weight_stream(w_hbm, w_ref, stage, sem, row_gain)
        per_head = n_chunks // N_LRU_HEADS
        begin()
        scan(lambda h: advance(h * per_head, (h + 1) * per_head))
        advance(N_LRU_HEADS * per_head, n_chunks)
        finish_scan()

    @pl.when((s > 0) & (s < n_tiles))
    def _():
        scan(project_chunk)
        finish_projection()
        finish_scan()

    @pl.when(s == n_tiles)
    def _():
        for h in range(N_LRU_HEADS):
            project_chunk(h)
        finish_projection()


def _lru_bwd_out_proj(xc, hf, sg, w_gate, b_gate, lru_lambda, out_norm_pool_g, out_norm_lru_g, yp, x, mod3, w_out,
                      b_out, final_g, *, tm):
    batch, seq, d = x.shape
    width = xc.shape[1]
    tt = tm // batch
    n_tiles = seq // tt
    _, n_head, head, two_head = w_gate.shape

    def scan_tile(s):
        return jnp.maximum(n_tiles - 1 - s, 0)

    def proj_tile(s):
        return jnp.minimum(n_tiles - s, n_tiles - 1)

    return pl.pallas_call(
        functools.partial(_lru_bwd_out_kernel, n_tiles=n_tiles),
        grid=(n_tiles + 1,),
        in_specs=[pl.BlockSpec((tm, width), lambda s: (scan_tile(s), 0)),
                  pl.BlockSpec((tm, width), lambda s: (scan_tile(s), 0)),
                  pl.BlockSpec((tm, width), lambda s: (scan_tile(s), 1)),
                  pl.BlockSpec((1, n_head, head, two_head), lambda s: (1, 0, 0, 0)),
                  pl.BlockSpec((1, n_head, two_head), lambda s: (1, 0, 0)),
                  pl.BlockSpec(lru_lambda.shape, lambda s: (0, 0)),
                  pl.BlockSpec((1, yp.shape[1]), lambda s: (0, 0)),
                  pl.BlockSpec((1, width), lambda s: (0, 0)),
                  pl.BlockSpec((tm, yp.shape[1]), lambda s: (proj_tile(s), 0)),
                  pl.BlockSpec((batch, tt, d), lambda s: (0, proj_tile(s), 0)),
                  pl.BlockSpec(mod3.shape, lambda s: (0, 0, 0)),
                  pl.BlockSpec(memory_space=pl.ANY),
                  pl.BlockSpec((1, d), lambda s: (0, 0)),
                  pl.BlockSpec((1, d), lambda s: (0, 0))],
        out_specs=pl.BlockSpec((batch, tt, d), lambda s: (0, proj_tile(s), 0)),
        out_shape=jax.ShapeDtypeStruct((batch, seq, d), F32),
        scratch_shapes=[pltpu.VMEM((tm, width), F32),
                        pltpu.VMEM((tm, width), BF16),
                        pltpu.VMEM((d // LANES, tm, LANES), F32),
                        pltpu.VMEM((SUBLANES, width), F32),
                        pltpu.VMEM((n_head, head, two_head), BF16),
                        *_weight_scratch(w_out)],
        compiler_params=_params(("arbitrary",)),
        name="lru_bwd_out_proj",
    )(xc, hf, sg, w_gate, b_gate, lru_lambda, out_norm_pool_g, out_norm_lru_g, yp, x, mod3, w_out, b_out, final_g)


def kernel(x, c, norm_g, w_ada, b_ada, w_in, b_in, w_pool, b_pool, pool_scale, conv_w, conv_b, w_gate,
           b_gate, lru_lambda, out_norm_pool_g, out_norm_lru_g, w_out, b_out, final_norm_g):
    batch, seq, d = x.shape
    assert w_in.shape[0] == 1, "single-layer block only"
    mod3 = _adaln_mod(c, w_ada, b_ada)
    tile = TIME_TILE_ROWS
    z, sg = _in_proj(x, mod3, norm_g, w_in[0], b_in, tm=tile)
    yp, xc, hf = _mixers_fwd(z, sg, w_pool[0], b_pool[0], pool_scale, conv_w[0], conv_b, w_gate[0], b_gate[0],
                             lru_lambda[0], batch=batch, seq=seq, tc=tile)
    return _lru_bwd_out_proj(xc, hf, sg, w_gate[0], b_gate[0], lru_lambda[0], out_norm_pool_g, out_norm_lru_g, yp,
                             x, mod3, w_out[0], b_out, final_norm_g[None, :], tm=tile)
```
